```python
import math
import jax, jax.numpy as jnp
from jax import lax
import numpy as np

D_MODEL = 1024
BATCH = 2
SEQ = 16384
DEPTH = 4

HEAD_DIM = 64
BRANCH_WIDTH = 8 * HEAD_DIM
RWKV_HEADS = BRANCH_WIDTH // HEAD_DIM
RWKV_WIDTH = BRANCH_WIDTH
RWKV_W_LORA = 64
RWKV_A_LORA = 64
RWKV_G_LORA = 128
RWKV_LN_EPS = 64e-5
NSA_HEADS = BRANCH_WIDTH // HEAD_DIM
NSA_KV_GROUPS = 2
NSA_WIDTH = BRANCH_WIDTH
NSA_KV_WIDTH = NSA_KV_GROUPS * HEAD_DIM
CMP_STRIDE = 16
CMP_LEN = 2 * CMP_STRIDE
CMP_HIDDEN = 256
SEL_BLOCK = 64
SEL_TOPK = 16
SEL_OVERLAP_W = (1.0, 2.0, 2.0, 2.0, 1.0)
SEL_FORCE_SCORE = 1e4
NSA_WINDOW = 512
Q_BLOCK = 128
SWA_HEADS = BRANCH_WIDTH // HEAD_DIM
SWA_KV_HEADS = 2
SWA_WIDTH = BRANCH_WIDTH
SWA_KV_WIDTH = SWA_KV_HEADS * HEAD_DIM
SWA_WINDOW = 128
REL_BUCKETS = 32
REL_MAX_DIST = 1024
FFN_HIDDEN = ((8 * D_MODEL + 3 * 256 - 1) // (3 * 256)) * 256
N_BRANCHES = 3
NORM_EPS = 1e-6
NEG_INF = -1e30
RWKV_COLS = 3 * RWKV_WIDTH + RWKV_W_LORA + RWKV_A_LORA + RWKV_G_LORA
NSA_COLS = NSA_WIDTH + 6 * NSA_KV_WIDTH + 3 * NSA_HEADS
SWA_COLS = SWA_WIDTH + 2 * SWA_KV_WIDTH
GATE_COLS = N_BRANCHES * D_MODEL
IN_COLS = RWKV_COLS + NSA_COLS + SWA_COLS + GATE_COLS

kernel_name = "hybrid_rwkv7_nsa_swa_gated_trunk"


def _split(x, sizes):
    idx = [int(v) for v in np.cumsum(sizes)[:-1]]
    return jnp.split(x, idx, axis=-1)


def rms_norm(x, g):
    xf = x.astype(jnp.float32)
    y = xf * lax.rsqrt(jnp.mean(xf * xf, axis=-1, keepdims=True) + NORM_EPS)
    return (y * g.astype(jnp.float32)).astype(x.dtype)


def t5_bucket(dist):
    n = jnp.maximum(dist, 0)
    max_exact = REL_BUCKETS // 2
    nf = jnp.maximum(n, 1).astype(jnp.float32)
    large = max_exact + (jnp.log(nf / max_exact) / math.log(REL_MAX_DIST / max_exact)
                         * (REL_BUCKETS - max_exact)).astype(jnp.int32)
    large = jnp.minimum(large, REL_BUCKETS - 1)
    return jnp.where(n < max_exact, n, large)


def rel_bias_2d(table, dist, groups):
    b = table[t5_bucket(dist)].astype(jnp.float32)
    return b.reshape(dist.shape + (groups, -1)).transpose(2, 3, 0, 1)


def masked_softmax(s, mask, sink=None):
    s = jnp.where(mask, s, NEG_INF)
    m = jnp.max(s, axis=-1, keepdims=True)
    if sink is not None:
        m = jnp.maximum(m, sink)
    p = jnp.where(mask, jnp.exp(s - m), 0.0)
    den = jnp.sum(p, axis=-1, keepdims=True)
    if sink is not None:
        den = den + jnp.exp(sink - m)
    return p / jnp.where(den > 0, den, 1.0)


def rwkv7_time_mix(p, mu, w0, w2, a0, a2, g2, k_k, k_a, r_k, ln_w, ln_b):
    B, T, _ = p.shape
    H, N = RWKV_HEADS, HEAD_DIM
    f32 = jnp.float32
    prev = jnp.pad(p, ((0, 0), (1, 0), (0, 0)))[:, :-1]
    p = p + (prev - p) * mu
    r, k, v, zw, za, zg = _split(p, (RWKV_WIDTH,) * 3 + (RWKV_W_LORA, RWKV_A_LORA, RWKV_G_LORA))
    w = -jax.nn.softplus(-(w0 + jnp.tanh(zw) @ w2).astype(f32)) - 0.5
    decay = jnp.exp(-jnp.exp(w))
    a = jax.nn.sigmoid((a0 + za @ a2).astype(f32))
    g = jax.nn.sigmoid(zg) @ g2
    kk = (k * k_k).astype(f32).reshape(B, T, H, N)
    kk = kk / jnp.maximum(jnp.sqrt(jnp.sum(kk * kk, axis=-1, keepdims=True)), 1e-12)
    k = k.astype(f32) * (1.0 + (a - 1.0) * k_a.astype(f32))

    def heads(z):
        return z.astype(f32).reshape(B, T, H, N)
    r_h, w_h, k_h, v_h, a_h = heads(r), heads(decay), heads(k), heads(v), heads(a)
    xs = tuple(jnp.swapaxes(z, 0, 1) for z in (r_h, w_h, k_h, v_h, -kk, kk * a_h))

    def step(S, inp):
        r_t, w_t, k_t, v_t, a_t, b_t = inp
        sa = jnp.einsum('bhij,bhj->bhi', S, a_t)
        S = S * w_t[:, :, None, :] + sa[..., None] * b_t[:, :, None, :] + v_t[..., None] * k_t[:, :, None, :]
        y = jnp.einsum('bhij,bhj->bhi', S, r_t)
        return S, y

    S0 = jnp.zeros((B, H, N, N), f32)
    _, y = lax.scan(step, S0, xs)
    y = jnp.swapaxes(y, 0, 1)
    mean = jnp.mean(y, axis=-1, keepdims=True)
    var = jnp.mean(jnp.square(y - mean), axis=-1, keepdims=True)
    y = ((y - mean) * lax.rsqrt(var + RWKV_LN_EPS)).reshape(B, T, RWKV_WIDTH)
    y = y * ln_w.astype(f32) + ln_b.astype(f32)
    bonus = jnp.sum(r_h * k_h * r_k.astype(f32), axis=-1, keepdims=True) * v_h
    y = y + bonus.reshape(B, T, RWKV_WIDTH)
    return (y * g.astype(f32)).astype(p.dtype)


def nsa_mix(p, pe_k, pe_v, ck_w1, ck_w2, cv_w1, cv_w2, rel_tbl):
    B, T, _ = p.shape
    G, R, N = NSA_KV_GROUPS, NSA_HEADS // NSA_KV_GROUPS, HEAD_DIM
    f32 = jnp.float32
    q, kc, vc, ks, vs, kw, vw, gl = _split(p, (NSA_WIDTH,) + (NSA_KV_WIDTH,) * 6 + (3 * NSA_HEADS,))
    nq = T // Q_BLOCK
    n_cmp = T // CMP_STRIDE - 1
    n_sel = T // SEL_BLOCK
    topk = min(SEL_TOPK, n_sel)
    sel_per_cmp = SEL_BLOCK // CMP_STRIDE

    def compress(z, pe, w1, w2):
        c = z.reshape(B, T // CMP_STRIDE, CMP_STRIDE, G, N)
        blk = jnp.concatenate([c[:, :-1], c[:, 1:]], axis=2) + pe[None, None, :, None, :]
        blk = blk.transpose(0, 3, 1, 2, 4).reshape(B, G, n_cmp, CMP_LEN * N)
        return jax.nn.gelu(blk @ w1) @ w2

    k_cmp = compress(kc, pe_k, ck_w1, ck_w2)
    v_cmp = compress(vc, pe_v, cv_w1, cv_w2)
    cmp_end = jnp.arange(n_cmp) * CMP_STRIDE + (CMP_LEN - 1)

    k_sel = ks.reshape(B, n_sel, SEL_BLOCK, G, N).transpose(0, 3, 1, 2, 4)
    v_sel = vs.reshape(B, n_sel, SEL_BLOCK, G, N).transpose(0, 3, 1, 2, 4)
    pad = ((0, 0), (NSA_WINDOW, 0), (0, 0), (0, 0))
    k_win = jnp.pad(kw.reshape(B, T, G, N), pad)
    v_win = jnp.pad(vw.reshape(B, T, G, N), pad)

    q_blk = (q * HEAD_DIM ** -0.5).reshape(B, nq, Q_BLOCK, G, R, N).transpose(1, 0, 3, 4, 2, 5)
    gates = jax.nn.sigmoid(gl.astype(f32)).reshape(B, nq, Q_BLOCK, 3, G, R).transpose(1, 0, 3, 4, 5, 2)
    tbl_g = rel_tbl.reshape(REL_BUCKETS, G, R).transpose(1, 0, 2).astype(f32)
    b_idx = jnp.arange(B)[:, None, None, None]
    g_idx = jnp.arange(G)[None, :, None, None]
    sel_ids = jnp.arange(n_sel)

    def one_block(args):
        qb, gb, blk = args
        t = blk * Q_BLOCK + jnp.arange(Q_BLOCK)
        dist_c = t[:, None] - cmp_end[None, :]
        s_c = jnp.einsum('bgrqd,bgcd->bgrqc', qb, k_cmp).astype(f32) + rel_bias_2d(rel_tbl, dist_c, G)
        p_c = masked_softmax(s_c, dist_c >= 0)
        o_c = jnp.einsum('bgrqc,bgcd->bgrqd', p_c.astype(v_cmp.dtype), v_cmp)
        imp = jnp.pad(jnp.sum(p_c, axis=2), ((0, 0), (0, 0), (0, 0), (1, sel_per_cmp)))
        imp = sum(wt * lax.slice_in_dim(imp, o, o + sel_per_cmp * (n_sel - 1) + 1, stride=sel_per_cmp, axis=3)
                  for o, wt in enumerate(SEL_OVERLAP_W))
        cur = (t // SEL_BLOCK)[:, None]
        sid = sel_ids[None, :]
        forced = (sid == 0) | (sid == cur) | (sid == cur - 1)
        score = jnp.where(forced, SEL_FORCE_SCORE, jnp.where(sid <= cur, imp, -1.0))
        _, sel = lax.top_k(score, topk)
        k_g = k_sel[b_idx, g_idx, sel].reshape(B, G, Q_BLOCK, topk * SEL_BLOCK, N)
        v_g = v_sel[b_idx, g_idx, sel].reshape(B, G, Q_BLOCK, topk * SEL_BLOCK, N)
        pos = (sel[..., None] * SEL_BLOCK + jnp.arange(SEL_BLOCK)).reshape(B, G, Q_BLOCK, topk * SEL_BLOCK)
        dist_s = t[None, None, :, None] - pos
        bias_s = jnp.moveaxis(tbl_g[g_idx, t5_bucket(dist_s)], -1, 2)
        s_s = jnp.einsum('bgrqd,bgqmd->bgrqm', qb, k_g).astype(f32) + bias_s
        p_s = masked_softmax(s_s, (dist_s >= 0)[:, :, None])
        o_s = jnp.einsum('bgrqm,bgqmd->bgrqd', p_s.astype(v_g.dtype), v_g)
        start = blk * Q_BLOCK
        k_w = lax.dynamic_slice_in_dim(k_win, start, Q_BLOCK + NSA_WINDOW, axis=1)
        v_w = lax.dynamic_slice_in_dim(v_win, start, Q_BLOCK + NSA_WINDOW, axis=1)
        spos = start - NSA_WINDOW + jnp.arange(Q_BLOCK + NSA_WINDOW)
        dist_w = t[:, None] - spos[None, :]
        mask_w = (dist_w >= 0) & (dist_w < NSA_WINDOW) & (spos[None, :] >= 0)
        s_w = jnp.einsum('bgrqd,bsgd->bgrqs', qb, k_w).astype(f32) + rel_bias_2d(rel_tbl, dist_w, G)
        p_w = masked_softmax(s_w, mask_w)
        o_w = jnp.einsum('bgrqs,bsgd->bgrqd', p_w.astype(v_w.dtype), v_w)
        gb = gb[..., None]
        out = gb[:, 0] * o_c + gb[:, 1] * o_s + gb[:, 2] * o_w
        return out.astype(qb.dtype)

    o = lax.map(one_block, (q_blk, gates, jnp.arange(nq)))
    return o.transpose(1, 0, 4, 2, 3, 5).reshape(B, T, NSA_WIDTH)


def swa_sink_mix(p, sinks, rel_tbl):
    B, T, _ = p.shape
    G, R, N, W = SWA_KV_HEADS, SWA_HEADS // SWA_KV_HEADS, HEAD_DIM, SWA_WINDOW
    f32 = jnp.float32
    q, k, v = _split(p, (SWA_WIDTH, SWA_KV_WIDTH, SWA_KV_WIDTH))
    nb = T // W
    q = (q * HEAD_DIM ** -0.5).reshape(B, nb, W, G, R, N)
    k = k.reshape(B, nb, W, G, N)
    v = v.reshape(B, nb, W, G, N)
    shift = ((0, 0), (1, 0), (0, 0), (0, 0), (0, 0))
    kb = jnp.concatenate([jnp.pad(k, shift)[:, :-1], k], axis=2)
    vb = jnp.concatenate([jnp.pad(v, shift)[:, :-1], v], axis=2)
    i = jnp.arange(W)
    j = jnp.arange(2 * W)
    dist = W + i[:, None] - j[None, :]
    valid_key = (jnp.arange(nb)[:, None] > 0) | (j[None, :] >= W)
    mask = ((dist >= 0) & (dist < W))[None] & valid_key[:, None, :]
    s = jnp.einsum('bnqgrd,bnkgd->bngrqk', q, kb).astype(f32) + rel_bias_2d(rel_tbl, dist, G)[None, None]
    sink = sinks.astype(f32).reshape(G, R)[None, None, :, :, None, None]
    pr = masked_softmax(s, mask[None, :, None, None], sink)
    o = jnp.einsum('bngrqk,bnkgd->bnqgrd', pr.astype(vb.dtype), vb)
    return o.reshape(B, T, SWA_WIDTH)


def setup_inputs(seed: int = 0) -> dict:
    key = jax.random.key(seed)
    ks = jax.random.split(key, 32)
    f32 = jnp.float32
    L, D = DEPTH, D_MODEL

    def nrm(k, shape, scale):
        return jax.random.normal(k, shape, f32) * scale

    def gain(k, shape):
        return 1.0 + nrm(k, shape, 0.1)

    return {
        "x": nrm(ks[0], (BATCH, SEQ, D), 1.0),
        "norm_pre_mix": gain(ks[1], (L, D)),
        "norm_post_mix": gain(ks[2], (L, D)),
        "norm_pre_ffn": gain(ks[3], (L, D)),
        "norm_post_ffn": gain(ks[4], (L, D)),
        "w_in": nrm(ks[5], (L, D, IN_COLS), D ** -0.5),
        "rwkv_mu": jax.random.uniform(ks[6], (L, RWKV_COLS), f32),
        "rwkv_w0": jax.random.uniform(ks[7], (L, RWKV_WIDTH), f32, -6.0, 0.0),
        "rwkv_w2": nrm(ks[8], (L, RWKV_W_LORA, RWKV_WIDTH), 0.5 * RWKV_W_LORA ** -0.5),
        "rwkv_a0": nrm(ks[9], (L, RWKV_WIDTH), 0.5),
        "rwkv_a2": nrm(ks[10], (L, RWKV_A_LORA, RWKV_WIDTH), RWKV_A_LORA ** -0.5),
        "rwkv_g2": nrm(ks[11], (L, RWKV_G_LORA, RWKV_WIDTH), RWKV_G_LORA ** -0.5),
        "rwkv_k_k": 0.85 + nrm(ks[12], (L, RWKV_WIDTH), 0.1),
        "rwkv_k_a": 1.0 + nrm(ks[13], (L, RWKV_WIDTH), 0.1),
        "rwkv_r_k": nrm(ks[14], (L, RWKV_HEADS, HEAD_DIM), 0.1),
        "rwkv_ln_w": gain(ks[15], (L, RWKV_WIDTH)),
        "rwkv_ln_b": nrm(ks[16], (L, RWKV_WIDTH), 0.02),
        "nsa_pe_k": nrm(ks[17], (L, CMP_LEN, HEAD_DIM), 0.1),
        "nsa_pe_v": nrm(ks[18], (L, CMP_LEN, HEAD_DIM), 0.1),
        "nsa_ck_w1": nrm(ks[19], (L, CMP_LEN * HEAD_DIM, CMP_HIDDEN), (CMP_LEN * HEAD_DIM) ** -0.5),
        "nsa_ck_w2": nrm(ks[20], (L, CMP_HIDDEN, HEAD_DIM), CMP_HIDDEN ** -0.5),
        "nsa_cv_w1": nrm(ks[21], (L, CMP_LEN * HEAD_DIM, CMP_HIDDEN), (CMP_LEN * HEAD_DIM) ** -0.5),
        "nsa_cv_w2": nrm(ks[22], (L, CMP_HIDDEN, HEAD_DIM), CMP_HIDDEN ** -0.5),
        "swa_sinks": nrm(ks[23], (L, SWA_HEADS), 1.0),
        "rel_bias": nrm(ks[24], (REL_BUCKETS, NSA_HEADS + SWA_HEADS), 0.5),
        "w_branch": nrm(ks[25], (L, N_BRANCHES, BRANCH_WIDTH, D), BRANCH_WIDTH ** -0.5),
        "w_out": nrm(ks[26], (L, D, D), D ** -0.5),
        "ffn_w_gate": nrm(ks[27], (L, D, FFN_HIDDEN), D ** -0.5),
        "ffn_w_up": nrm(ks[28], (L, D, FFN_HIDDEN), D ** -0.5),
        "ffn_w_down": nrm(ks[29], (L, FFN_HIDDEN, D), FFN_HIDDEN ** -0.5),
    }


def reference(x, norm_pre_mix, norm_post_mix, norm_pre_ffn, norm_post_ffn, w_in, rwkv_mu, rwkv_w0, rwkv_w2,
              rwkv_a0, rwkv_a2, rwkv_g2, rwkv_k_k, rwkv_k_a, rwkv_r_k, rwkv_ln_w, rwkv_ln_b, nsa_pe_k, nsa_pe_v,
              nsa_ck_w1, nsa_ck_w2, nsa_cv_w1, nsa_cv_w2, swa_sinks, rel_bias, w_branch, w_out,
              ffn_w_gate, ffn_w_up, ffn_w_down):
    B, T, D = x.shape
    rel_nsa = rel_bias[:, :NSA_HEADS]
    rel_swa = rel_bias[:, NSA_HEADS:]
    for l in range(DEPTH):
        h = rms_norm(x, norm_pre_mix[l])
        proj = h @ w_in[l]
        p_rwkv, p_nsa, p_swa, p_gate = _split(proj, (RWKV_COLS, NSA_COLS, SWA_COLS, GATE_COLS))
        o_rwkv = rwkv7_time_mix(p_rwkv, rwkv_mu[l], rwkv_w0[l], rwkv_w2[l], rwkv_a0[l], rwkv_a2[l],
                                rwkv_g2[l], rwkv_k_k[l], rwkv_k_a[l], rwkv_r_k[l], rwkv_ln_w[l], rwkv_ln_b[l])
        o_nsa = nsa_mix(p_nsa, nsa_pe_k[l], nsa_pe_v[l], nsa_ck_w1[l], nsa_ck_w2[l], nsa_cv_w1[l],
                        nsa_cv_w2[l], rel_nsa)
        o_swa = swa_sink_mix(p_swa, swa_sinks[l], rel_swa)
        gates = jax.nn.sigmoid(p_gate.astype(jnp.float32)).reshape(B, T, N_BRANCHES, D).astype(x.dtype)
        merged = (gates[:, :, 0] * (o_rwkv @ w_branch[l, 0])
                  + gates[:, :, 1] * (o_nsa @ w_branch[l, 1])
                  + gates[:, :, 2] * (o_swa @ w_branch[l, 2]))
        x = x + rms_norm(merged @ w_out[l], norm_post_mix[l])
        h = rms_norm(x, norm_pre_ffn[l])
        f = (jax.nn.silu(h @ ffn_w_gate[l]) * (h @ ffn_w_up[l])) @ ffn_w_down[l]
        x = x + rms_norm(f, norm_post_ffn[l])
    return x
```

```python
import functools
import math

import numpy as np
import jax
import jax.numpy as jnp
from jax import lax
from jax.experimental import pallas as pl
from jax.experimental.pallas import tpu as pltpu

F32 = jnp.float32
BF16 = jnp.bfloat16
HI = lax.Precision.HIGHEST

D_MODEL = 1024
HEAD_DIM = 64
N_HEADS = 8
KV_GROUPS = 2
REP = N_HEADS // KV_GROUPS
WIDTH = N_HEADS * HEAD_DIM
KV_WIDTH = KV_GROUPS * HEAD_DIM
W_LORA, A_LORA, G_LORA = 64, 64, 128
RWKV_LN_EPS = 64e-5
CMP_STRIDE = 16
CMP_LEN = 32
CMP_HIDDEN = 256
SEL_BLOCK = 64
SEL_TOPK = 16
SEL_FORCE_SCORE = 1e4
NSA_WINDOW = 512
Q_BLOCK = 128
SWA_WINDOW = 128
REL_BUCKETS = 32
REL_MAX_DIST = 1024
NORM_EPS = 1e-6
NEG_INF = -1e30
LANE = 128
RWKV_CHUNK = 64
SEL_NEAR = 1024
SEL_PAD = SEL_NEAR - Q_BLOCK
SEL_FAR_TILE = 512
CMP_NEAR = 64
CMP_PAD = CMP_NEAR - Q_BLOCK // CMP_STRIDE
VMEM_LIMIT = 56 * 1024 * 1024


def _cparams(sem):
    return pltpu.CompilerParams(dimension_semantics=sem, vmem_limit_bytes=VMEM_LIMIT)


def _dot(a, b, precision=None):
    return jnp.dot(a, b, preferred_element_type=F32, precision=precision)


def _dot_nt(a, b, precision=None):
    return lax.dot_general(a, b, (((1,), (1,)), ((), ())), preferred_element_type=F32, precision=precision)


def _dot_tn(a, b, precision=None):
    return lax.dot_general(a, b, (((0,), (0,)), ((), ())), preferred_element_type=F32, precision=precision)


def _rms(x, g):
    return x * lax.rsqrt(jnp.mean(x * x, axis=-1, keepdims=True) + NORM_EPS) * g


def _iota(shape, dim):
    return lax.broadcasted_iota(jnp.int32, shape, dim)


def _norm_mm_body(x_ref, g_ref, w_ref, o_ref, h_ref):
    @pl.when(pl.program_id(1) == 0)
    def _():
        h_ref[...] = _rms(x_ref[...], g_ref[...]).astype(BF16)

    o_ref[...] = _dot(h_ref[...], w_ref[...]).astype(o_ref.dtype)


def norm_matmul(x, g, w, out_dtype, tm, tn):
    M, D = x.shape
    N = w.shape[1]
    return pl.pallas_call(
        _norm_mm_body,
        grid=(M // tm, N // tn),
        in_specs=[pl.BlockSpec((tm, D), lambda i, j: (i, 0)),
                  pl.BlockSpec((1, D), lambda i, j: (0, 0)),
                  pl.BlockSpec((D, tn), lambda i, j: (0, j))],
        out_specs=pl.BlockSpec((tm, tn), lambda i, j: (i, j)),
        out_shape=jax.ShapeDtypeStruct((M, N), out_dtype),
        scratch_shapes=[pltpu.VMEM((tm, D), BF16)],
        compiler_params=_cparams(("parallel", "arbitrary")),
        name="norm_matmul",
    )(x, g, w)


RW_COLS = 3 * WIDTH + 3 * LANE


def _rwkv_prep_body(p_ref, mu_ref, w0_ref, w2_ref, a0_ref, a2_ref, g2_ref,
                    r_o, k_o, v_o, a_o, lw_o, g_o, carry_ref):
    tm = p_ref.shape[0]

    @pl.when(pl.program_id(1) == 0)
    def _():
        carry_ref[...] = jnp.zeros_like(carry_ref)

    p = p_ref[...]
    row = _iota(p.shape, 0)
    prev = jnp.where(row == 0, carry_ref[0:1, :], pltpu.roll(p, 1, axis=0))
    carry_ref[0:1, :] = p[tm - 1:tm, :]
    ps = p + (prev - p) * mu_ref[...]
    r_o[...] = ps[:, 0:WIDTH]
    k_o[...] = ps[:, WIDTH:2 * WIDTH]
    v_o[...] = ps[:, 2 * WIDTH:3 * WIDTH]
    zw = ps[:, 3 * WIDTH:3 * WIDTH + LANE]
    za = ps[:, 3 * WIDTH + LANE:3 * WIDTH + 2 * LANE]
    zg = ps[:, 3 * WIDTH + 2 * LANE:3 * WIDTH + 3 * LANE]
    z = -(w0_ref[...] + _dot(jnp.tanh(zw), w2_ref[...], HI))
    softplus = jnp.maximum(z, 0.0) + jnp.log(1.0 + jnp.exp(-jnp.abs(z)))
    lw_o[...] = -jnp.exp(-softplus - 0.5)
    a_o[...] = jax.nn.sigmoid(a0_ref[...] + _dot(za, a2_ref[...], HI))
    g_o[...] = _dot(jax.nn.sigmoid(zg), g2_ref[...], HI)


def rwkv_prep(p, mu, w0, w2, a0, a2, g2, tm):
    B, T, _ = p.shape
    row = lambda b, i: (b, i, 0)
    fixed = lambda b, i: (0, 0)
    out = jax.ShapeDtypeStruct((B, T, WIDTH), F32)
    return pl.pallas_call(
        _rwkv_prep_body,
        grid=(B, T // tm),
        in_specs=[pl.BlockSpec((None, tm, RW_COLS), row),
                  pl.BlockSpec((1, RW_COLS), fixed),
                  pl.BlockSpec((1, WIDTH), fixed), pl.BlockSpec((LANE, WIDTH), fixed),
                  pl.BlockSpec((1, WIDTH), fixed), pl.BlockSpec((LANE, WIDTH), fixed),
                  pl.BlockSpec((LANE, WIDTH), fixed)],
        out_specs=[pl.BlockSpec((None, tm, WIDTH), row)] * 6,
        out_shape=[out] * 6,
        scratch_shapes=[pltpu.VMEM((8, RW_COLS), F32)],
        compiler_params=_cparams(("parallel", "arbitrary")),
        name="rwkv_prep",
    )(p, mu, w0, w2, a0, a2, g2)


def _rwkv_rec_body(r_ref, k_ref, v_ref, a_ref, lw_ref, g_ref, kk_ref, ka_ref, rk_ref, lnw_ref, lnb_ref,
                   o_ref, s_ref):
    C = RWKV_CHUNK
    n_chunks = r_ref.shape[0] // C

    @pl.when(pl.program_id(2) == 0)
    def _():
        s_ref[...] = jnp.zeros_like(s_ref)

    row = _iota((C, C), 0)
    col = _iota((C, C), 1)
    incl = row >= col
    strict = row > col
    tri = incl.astype(F32)
    k_k, k_a, r_k = kk_ref[...], ka_ref[...], rk_ref[...]
    ln_w, ln_b = lnw_ref[...], lnb_ref[...]
    for c in range(n_chunks):
        sl = pl.ds(c * C, C)
        r, k, v, a, lw = r_ref[sl, :], k_ref[sl, :], v_ref[sl, :], a_ref[sl, :], lw_ref[sl, :]
        kk = k * k_k
        kk = kk / jnp.maximum(jnp.sqrt(jnp.sum(kk * kk, axis=-1, keepdims=True)), 1e-12)
        k2 = k * (1.0 + (a - 1.0) * k_a)
        cum = _dot(tri, lw, HI)
        p_incl = jnp.exp(cum)
        p_inv = jnp.exp(-cum)
        a_t = -kk * jnp.exp(cum - lw)
        r_t = r * p_incl
        b_t = kk * a * p_inv
        k_t = k2 * p_inv
        h0 = s_ref[...]
        l_ab = jnp.where(strict, _dot_nt(a_t, b_t, HI), 0.0)
        l_ak = jnp.where(strict, _dot_nt(a_t, k_t, HI), 0.0)
        m_rb = jnp.where(incl, _dot_nt(r_t, b_t, HI), 0.0)
        m_rk = jnp.where(incl, _dot_nt(r_t, k_t, HI), 0.0)
        u = _dot(a_t, h0, HI) + _dot(l_ak, v, HI)
        lp = l_ab
        for it in range(6):
            u = u + _dot(lp, u, HI)
            if it < 5:
                lp = _dot(lp, lp, HI)
        y = _dot(r_t, h0, HI) + _dot(m_rb, u, HI) + _dot(m_rk, v, HI)
        p_end = p_incl[C - 1:C, :]
        diag = jnp.where(row == col, jnp.broadcast_to(p_end, (C, C)), 0.0)
        s_ref[...] = _dot(diag, h0, HI) + _dot_tn(b_t * p_end, u, HI) + _dot_tn(k_t * p_end, v, HI)
        mean = jnp.mean(y, axis=-1, keepdims=True)
        var = jnp.mean(jnp.square(y - mean), axis=-1, keepdims=True)
        yn = (y - mean) * lax.rsqrt(var + RWKV_LN_EPS) * ln_w + ln_b
        bonus = jnp.sum(r * k2 * r_k, axis=-1, keepdims=True) * v
        o_ref[sl, :] = (yn + bonus) * g_ref[sl, :]


def rwkv_recurrence(r, k, v, a, lw, g, k_k, k_a, r_k, ln_w, ln_b, tc):
    B, H, T, N = r.shape
    seq = pl.BlockSpec((None, None, tc, N), lambda b, h, i: (b, h, i, 0))
    par = pl.BlockSpec((None, 1, N), lambda b, h, i: (h, 0, 0))
    return pl.pallas_call(
        _rwkv_rec_body,
        grid=(B, H, T // tc),
        in_specs=[seq] * 6 + [par] * 5,
        out_specs=seq,
        out_shape=jax.ShapeDtypeStruct((B, H, T, N), F32),
        scratch_shapes=[pltpu.VMEM((N, N), F32)],
        compiler_params=_cparams(("parallel", "parallel", "arbitrary")),
        name="rwkv_recurrence",
    )(r, k, v, a, lw, g, k_k, k_a, r_k, ln_w, ln_b)


def _compress_body(z_ref, pe_ref, w1_ref, w2_ref, o_ref):
    nb = z_ref.shape[0]
    half = z_ref.shape[1]
    z = z_ref[...]
    first = _dot(z, w1_ref[0:half, :])
    second = _dot(z, w1_ref[half:2 * half, :])
    pe_term = _dot(pe_ref[...], w1_ref[...])[0:1, :]
    hidden = first + pltpu.roll(second, nb - 1, axis=0) + pe_term
    out = _dot(jax.nn.gelu(hidden).astype(BF16), w2_ref[...])
    rows = _iota(out.shape, 0)
    o_ref[...] = jnp.where(rows < nb - 1, out, 0.0)


def nsa_compress(z, pe, w1, w2):
    B, G, NB, HALF = z.shape
    return pl.pallas_call(
        _compress_body,
        grid=(B, G),
        in_specs=[pl.BlockSpec((None, None, NB, HALF), lambda b, g: (b, g, 0, 0)),
                  pl.BlockSpec((8, 2 * HALF), lambda b, g: (0, 0)),
                  pl.BlockSpec((2 * HALF, CMP_HIDDEN), lambda b, g: (0, 0)),
                  pl.BlockSpec((CMP_HIDDEN, HEAD_DIM), lambda b, g: (0, 0))],
        out_specs=pl.BlockSpec((None, None, NB, HEAD_DIM), lambda b, g: (b, g, 0, 0)),
        out_shape=jax.ShapeDtypeStruct((B, G, NB, HEAD_DIM), F32),
        compiler_params=_cparams(("parallel", "parallel")),
        name="nsa_compress",
    )(z, pe, w1, w2)


def _t5_bucket_np(dist):
    n = np.maximum(dist, 0)
    max_exact = REL_BUCKETS // 2
    nf = np.maximum(n, 1).astype(np.float64)
    large = max_exact + (np.log(nf / max_exact) / math.log(REL_MAX_DIST / max_exact)
                         * (REL_BUCKETS - max_exact)).astype(np.int32)
    large = np.minimum(large, REL_BUCKETS - 1)
    return np.where(n < max_exact, n, large).astype(np.int32)


def _bias_body(tbl_ref, bk_ref, o_ref):
    h = pl.program_id(0)
    bk = bk_ref[...]
    acc = jnp.zeros(bk.shape, F32)
    for b in range(REL_BUCKETS):
        acc = jnp.where(bk == b, tbl_ref[b, h], acc)
    o_ref[...] = acc


def bias_tile(table, dist_np):
    P, Q = dist_np.shape
    H = table.shape[1]
    buckets = jnp.asarray(_t5_bucket_np(dist_np))
    return pl.pallas_call(
        _bias_body,
        grid=(H,),
        in_specs=[pl.BlockSpec(memory_space=pltpu.SMEM),
                  pl.BlockSpec((P, Q), lambda h: (0, 0))],
        out_specs=pl.BlockSpec((None, P, Q), lambda h: (h, 0, 0)),
        out_shape=jax.ShapeDtypeStruct((H, P, Q), F32),
        compiler_params=_cparams(("arbitrary",)),
        name="bias_tile",
    )(table, buckets)


def _nsa_cmp_body(q_ref, kp_ref, vp_ref, kn_ref, vn_ref, bias_ref, bfar_ref, o_ref, sel_ref):
    QB = Q_BLOCK
    NB = kp_ref.shape[0]
    NS = NB // 4
    ns_shift = NS.bit_length() - 1
    blk = pl.program_id(2)
    kp, vp = kp_ref[...], vp_ref[...]
    start = pl.multiple_of(blk * (QB // CMP_STRIDE), 8)
    kn = kn_ref[pl.ds(start, CMP_NEAR), :].astype(BF16)
    vn = vn_ref[pl.ds(start, CMP_NEAR), :].astype(BF16)
    first_near = blk * (QB // CMP_STRIDE) - CMP_PAD
    lane = _iota((1, NB), 1)
    c_far = 4 * (lane & (NS - 1)) + (lane >> ns_shift)
    mask_far = c_far < first_near
    i_n = _iota((QB, CMP_NEAR), 0)
    c_n = _iota((QB, CMP_NEAR), 1)
    dist_n = i_n + (CMP_PAD * CMP_STRIDE - CMP_LEN + 1) - CMP_STRIDE * c_n
    mask_near = (dist_n >= 0) & (first_near + c_n >= 0)
    imp_far = jnp.zeros((QB, NB), F32)
    imp_near = jnp.zeros((QB, CMP_NEAR), F32)
    for r in range(REP):
        rows = slice(r * QB, (r + 1) * QB)
        q = q_ref[rows, :] * (HEAD_DIM ** -0.5)
        s_f = jnp.where(mask_far, _dot_nt(q, kp) + bfar_ref[r], NEG_INF)
        s_n = jnp.where(mask_near, _dot_nt(q, kn) + bias_ref[r], NEG_INF)
        m = jnp.maximum(jnp.max(s_f, axis=-1, keepdims=True), jnp.max(s_n, axis=-1, keepdims=True))
        p_f = jnp.where(mask_far, jnp.exp(s_f - m), 0.0)
        p_n = jnp.where(mask_near, jnp.exp(s_n - m), 0.0)
        den = jnp.sum(p_f, axis=-1, keepdims=True) + jnp.sum(p_n, axis=-1, keepdims=True)
        inv = 1.0 / jnp.where(den > 0, den, 1.0)
        p_f = p_f * inv
        p_n = p_n * inv
        o_ref[rows, :] = _dot(p_f.astype(BF16), vp) + _dot(p_n.astype(BF16), vn)
        imp_far = imp_far + p_f
        imp_near = imp_near + p_n
    panel = [imp_far[:, m * NS:(m + 1) * NS] for m in range(4)]
    j = _iota((QB, NS), 1)
    prev3 = jnp.where(j == 0, 0.0, pltpu.roll(panel[3], 1, axis=1))
    imp = prev3 + 2.0 * panel[0] + 2.0 * panel[1] + 2.0 * panel[2] + panel[3]
    c_abs = first_near + _iota((CMP_NEAR, NS), 0)
    off = c_abs + 1 - 4 * _iota((CMP_NEAR, NS), 1)
    w_near = jnp.where((off == 0) | (off == 4), 1.0, jnp.where((off >= 1) & (off <= 3), 2.0, 0.0))
    imp = imp + _dot(imp_near, w_near, HI)
    cur = 2 * blk + (_iota((QB, NS), 0) >= SEL_BLOCK).astype(jnp.int32)
    forced = (j == 0) | (j == cur) | (j == cur - 1)
    score = jnp.where(forced, SEL_FORCE_SCORE, jnp.where(j <= cur, imp, -1.0))
    sel = jnp.zeros((QB, NS), F32)
    for _ in range(min(SEL_TOPK, NS)):
        m = jnp.max(score, axis=-1, keepdims=True)
        idx = jnp.min(jnp.where(score == m, j, NS), axis=-1, keepdims=True)
        hit = j == idx
        sel = jnp.where(hit, 1.0, sel)
        score = jnp.where(hit, -3.0, score)
    sel_ref[...] = sel.astype(BF16)


def nsa_compressed(q, kperm, vperm, knear, vnear, bias, bias_far):
    B, G, NQ, QR, N = q.shape
    NB = kperm.shape[2]
    NS = NB // 4
    blk4 = lambda b, g, i: (b, g, 0, 0)
    return pl.pallas_call(
        _nsa_cmp_body,
        grid=(B, G, NQ),
        in_specs=[pl.BlockSpec((None, None, None, QR, N), lambda b, g, i: (b, g, i, 0, 0)),
                  pl.BlockSpec((None, None, NB, N), blk4), pl.BlockSpec((None, None, NB, N), blk4),
                  pl.BlockSpec((None, None, knear.shape[2], N), blk4),
                  pl.BlockSpec((None, None, knear.shape[2], N), blk4),
                  pl.BlockSpec((None, REP, Q_BLOCK, CMP_NEAR), lambda b, g, i: (g, 0, 0, 0)),
                  pl.BlockSpec((None, REP, 1, NB), lambda b, g, i: (g, 0, 0, 0))],
        out_specs=[pl.BlockSpec((None, None, None, QR, N), lambda b, g, i: (b, g, i, 0, 0)),
                   pl.BlockSpec((None, None, None, Q_BLOCK, NS), lambda b, g, i: (b, g, i, 0, 0))],
        out_shape=[jax.ShapeDtypeStruct((B, G, NQ, QR, N), F32),
                   jax.ShapeDtypeStruct((B, G, NQ, Q_BLOCK, NS), BF16)],
        compiler_params=_cparams(("parallel", "parallel", "parallel")),
        name="nsa_compressed",
    )(q, kperm, vperm, knear, vnear, bias, bias_far)


def _nsa_sel_body(q_ref, sel_ref, k_ref, v_ref, bias_ref, bfar_ref, o_ref):
    QB = Q_BLOCK
    NS = sel_ref.shape[1]
    TK = SEL_FAR_TILE
    blk = pl.program_id(2)
    selm = sel_ref[...]
    blocks_per_q = QB // SEL_BLOCK
    pad_blocks = SEL_PAD // SEL_BLOCK

    def expand(first_block, width):
        jj = _iota((NS, width), 0)
        cc = _iota((NS, width), 1)
        onehot = (jj == first_block + (cc >> 6)).astype(BF16)
        return _dot(selm, onehot) > 0.5

    row0 = pl.multiple_of(blk * QB, QB)
    kn = k_ref[pl.ds(row0, SEL_NEAR), :]
    vn = v_ref[pl.ds(row0, SEL_NEAR), :]
    i_q = _iota((QB, SEL_NEAR), 0)
    c_k = _iota((QB, SEL_NEAR), 1)
    mask = expand(blocks_per_q * blk - pad_blocks, SEL_NEAR) & (c_k <= i_q + SEL_PAD)
    qs, ms, ls, accs = [], [], [], []
    for r in range(REP):
        q = q_ref[r * QB:(r + 1) * QB, :] * (HEAD_DIM ** -0.5)
        s = jnp.where(mask, _dot_nt(q, kn) + bias_ref[r], NEG_INF)
        m = jnp.max(s, axis=-1, keepdims=True)
        p = jnp.where(mask, jnp.exp(s - m), 0.0)
        qs.append(q)
        ms.append(m)
        ls.append(jnp.sum(p, axis=-1, keepdims=True))
        accs.append(_dot(p.astype(BF16), vn))

    c_t = _iota((QB, TK), 1)

    def far_tile(kt, carry):
        ms, ls, accs = carry
        r0 = pl.multiple_of(kt * TK, TK)
        kf = k_ref[pl.ds(r0, TK), :]
        vf = v_ref[pl.ds(r0, TK), :]
        mask = expand(kt * (TK // SEL_BLOCK) - pad_blocks, TK) & (r0 + c_t < blk * QB)
        new_m, new_l, new_acc = [], [], []
        for r in range(REP):
            s = jnp.where(mask, _dot_nt(qs[r], kf) + bfar_ref[r], NEG_INF)
            m_new = jnp.maximum(ms[r], jnp.max(s, axis=-1, keepdims=True))
            alpha = jnp.exp(ms[r] - m_new)
            p = jnp.where(mask, jnp.exp(s - m_new), 0.0)
            new_m.append(m_new)
            new_l.append(alpha * ls[r] + jnp.sum(p, axis=-1, keepdims=True))
            new_acc.append(alpha * accs[r] + _dot(p.astype(BF16), vf))
        return tuple(new_m), tuple(new_l), tuple(new_acc)

    n_far = (blk * QB + TK - 1) // TK
    ms, ls, accs = lax.fori_loop(0, n_far, far_tile, (tuple(ms), tuple(ls), tuple(accs)))
    for r in range(REP):
        o_ref[r * QB:(r + 1) * QB, :] = accs[r] / ls[r]


def nsa_selected(q, sel, k, v, bias, bias_far):
    B, G, NQ, QR, N = q.shape
    NS = sel.shape[-1]
    rows = k.shape[2]
    kv = pl.BlockSpec((None, None, rows, N), lambda b, g, i: (b, g, 0, 0))
    return pl.pallas_call(
        _nsa_sel_body,
        grid=(B, G, NQ),
        in_specs=[pl.BlockSpec((None, None, None, QR, N), lambda b, g, i: (b, g, i, 0, 0)),
                  pl.BlockSpec((None, None, None, Q_BLOCK, NS), lambda b, g, i: (b, g, i, 0, 0)),
                  kv, kv,
                  pl.BlockSpec((None, REP, Q_BLOCK, SEL_NEAR), lambda b, g, i: (g, 0, 0, 0)),
                  pl.BlockSpec((None, REP, 1, SEL_FAR_TILE), lambda b, g, i: (g, 0, 0, 0))],
        out_specs=pl.BlockSpec((None, None, None, QR, N), lambda b, g, i: (b, g, i, 0, 0)),
        out_shape=jax.ShapeDtypeStruct((B, G, NQ, QR, N), F32),
        compiler_params=_cparams(("parallel", "parallel", "parallel")),
        name="nsa_selected",
    )(q, sel, k, v, bias, bias_far)


def _band_body(*refs, window, pad, has_sink):
    if has_sink:
        q_ref, k_ref, v_ref, bias_ref, sink_ref, o_ref = refs
    else:
        q_ref, k_ref, v_ref, bias_ref, o_ref = refs
    QB = Q_BLOCK
    width = pad + QB
    blk = pl.program_id(2)
    row0 = pl.multiple_of(blk * QB, QB)
    kb = k_ref[pl.ds(row0, width), :]
    vb = v_ref[pl.ds(row0, width), :]
    i_q = _iota((QB, width), 0)
    c_k = _iota((QB, width), 1)
    dist = i_q + pad - c_k
    mask = (dist >= 0) & (dist < window) & (row0 + c_k >= pad)
    for r in range(REP):
        rows = slice(r * QB, (r + 1) * QB)
        q = q_ref[rows, :] * (HEAD_DIM ** -0.5)
        s = jnp.where(mask, _dot_nt(q, kb) + bias_ref[r], NEG_INF)
        m = jnp.max(s, axis=-1, keepdims=True)
        if has_sink:
            sink = sink_ref[r][:, 0:1]
            m = jnp.maximum(m, sink)
        p = jnp.where(mask, jnp.exp(s - m), 0.0)
        den = jnp.sum(p, axis=-1, keepdims=True)
        if has_sink:
            den = den + jnp.exp(sink - m)
        p = p * (1.0 / jnp.where(den > 0, den, 1.0))
        o_ref[rows, :] = _dot(p.astype(BF16), vb)


def band_attention(q, k, v, bias, sink, window, pad):
    B, G, NQ, QR, N = q.shape
    rows = k.shape[2]
    width = pad + Q_BLOCK
    kv = pl.BlockSpec((None, None, rows, N), lambda b, g, i: (b, g, 0, 0))
    in_specs = [pl.BlockSpec((None, None, None, QR, N), lambda b, g, i: (b, g, i, 0, 0)), kv, kv,
                pl.BlockSpec((None, REP, Q_BLOCK, width), lambda b, g, i: (g, 0, 0, 0))]
    args = [q, k, v, bias]
    if sink is not None:
        in_specs.append(pl.BlockSpec((None, REP, 1, LANE), lambda b, g, i: (g, 0, 0, 0)))
        args.append(sink)
    return pl.pallas_call(
        functools.partial(_band_body, window=window, pad=pad, has_sink=sink is not None),
        grid=(B, G, NQ),
        in_specs=in_specs,
        out_specs=pl.BlockSpec((None, None, None, QR, N), lambda b, g, i: (b, g, i, 0, 0)),
        out_shape=jax.ShapeDtypeStruct((B, G, NQ, QR, N), F32),
        compiler_params=_cparams(("parallel", "parallel", "parallel")),
        name="band_attention",
    )(*args)


GATE_COLS = 3 * D_MODEL + LANE


def _merge_body(x_ref, pg_ref, orw_ref, oc_ref, os_ref, ow_ref, osw_ref, wb_ref, wo_ref, gn_ref, ex_ref, o_ref):
    pg = pg_ref[...]
    head_gates = jax.nn.sigmoid(pg[:, 3 * D_MODEL:])
    ge = _dot(head_gates, ex_ref[...], HI)
    o_nsa = (ge[:, 0:WIDTH] * oc_ref[...] + ge[:, WIDTH:2 * WIDTH] * os_ref[...]
             + ge[:, 2 * WIDTH:3 * WIDTH] * ow_ref[...])
    merged = (jax.nn.sigmoid(pg[:, 0:D_MODEL]) * _dot(orw_ref[...].astype(BF16), wb_ref[0])
              + jax.nn.sigmoid(pg[:, D_MODEL:2 * D_MODEL]) * _dot(o_nsa.astype(BF16), wb_ref[1])
              + jax.nn.sigmoid(pg[:, 2 * D_MODEL:3 * D_MODEL]) * _dot(osw_ref[...].astype(BF16), wb_ref[2]))
    y = _dot(merged.astype(BF16), wo_ref[...])
    o_ref[...] = x_ref[...] + _rms(y, gn_ref[...])


def merge_out(x, pg, o_rwkv, o_c, o_s, o_w, o_swa, w_branch, w_out, g_post, expand, tm):
    M, D = x.shape
    row = lambda i: (i, 0)
    wide = pl.BlockSpec((tm, WIDTH), row)
    return pl.pallas_call(
        _merge_body,
        grid=(M // tm,),
        in_specs=[pl.BlockSpec((tm, D), row), pl.BlockSpec((tm, GATE_COLS), row),
                  wide, wide, wide, wide, wide,
                  pl.BlockSpec((3, WIDTH, D), lambda i: (0, 0, 0)),
                  pl.BlockSpec((D, D), lambda i: (0, 0)),
                  pl.BlockSpec((1, D), lambda i: (0, 0)),
                  pl.BlockSpec((LANE, 3 * WIDTH), lambda i: (0, 0))],
        out_specs=pl.BlockSpec((tm, D), row),
        out_shape=jax.ShapeDtypeStruct((M, D), F32),
        compiler_params=_cparams(("parallel",)),
        name="merge_out",
    )(x, pg, o_rwkv, o_c, o_s, o_w, o_swa, w_branch, w_out, g_post, expand)


def _ffn_body(x_ref, gpre_ref, wg_ref, wu_ref, wd_ref, gpost_ref, o_ref, h_ref, acc_ref):
    j = pl.program_id(1)

    @pl.when(j == 0)
    def _():
        h_ref[...] = _rms(x_ref[...], gpre_ref[...]).astype(BF16)
        acc_ref[...] = jnp.zeros_like(acc_ref)

    h = h_ref[...]
    gate = _dot(h, wg_ref[...])
    act = gate * jax.nn.sigmoid(gate) * _dot(h, wu_ref[...])
    acc_ref[...] += _dot(act.astype(BF16), wd_ref[...])

    @pl.when(j == pl.num_programs(1) - 1)
    def _():
        o_ref[...] = x_ref[...] + _rms(acc_ref[...], gpost_ref[...])


def ffn(x, g_pre, w_gate, w_up, w_down, g_post, tm, th):
    M, D = x.shape
    Hd = w_gate.shape[1]
    return pl.pallas_call(
        _ffn_body,
        grid=(M // tm, Hd // th),
        in_specs=[pl.BlockSpec((tm, D), lambda i, j: (i, 0)),
                  pl.BlockSpec((1, D), lambda i, j: (0, 0)),
                  pl.BlockSpec((D, th), lambda i, j: (0, j)),
                  pl.BlockSpec((D, th), lambda i, j: (0, j)),
                  pl.BlockSpec((th, D), lambda i, j: (j, 0)),
                  pl.BlockSpec((1, D), lambda i, j: (0, 0))],
        out_specs=pl.BlockSpec((tm, D), lambda i, j: (i, 0)),
        out_shape=jax.ShapeDtypeStruct((M, D), F32),
        scratch_shapes=[pltpu.VMEM((tm, D), BF16), pltpu.VMEM((tm, D), F32)],
        compiler_params=_cparams(("parallel", "arbitrary")),
        name="ffn",
    )(x, g_pre, w_gate, w_up, w_down, g_post)


def _pad_cols(w, n):
    return jnp.pad(w, ((0, 0), (0, n - w.shape[1])))


def _pad_rows(w, n):
    return jnp.pad(w, ((0, n - w.shape[0]), (0, 0)))


def _heads_major(z, B, T):
    return z.reshape(B, T, N_HEADS, HEAD_DIM).transpose(0, 2, 1, 3)


def _q_blocks(q, B, T):
    nq = T // Q_BLOCK
    q = q.reshape(B, nq, Q_BLOCK, KV_GROUPS, REP, HEAD_DIM).transpose(0, 3, 1, 4, 2, 5)
    return q.reshape(B, KV_GROUPS, nq, REP * Q_BLOCK, HEAD_DIM)


def _unblock(o, B, T):
    nq = T // Q_BLOCK
    o = o.reshape(B, KV_GROUPS, nq, REP, Q_BLOCK, HEAD_DIM).transpose(0, 2, 4, 1, 3, 5)
    return o.reshape(B * T, WIDTH)


def _kv_groups(z, B, T, pad):
    z = z.reshape(B, T, KV_GROUPS, HEAD_DIM).transpose(0, 2, 1, 3)
    return jnp.pad(z, ((0, 0), (0, 0), (pad, 0), (0, 0)))


def _group_bias(b):
    return b.reshape(KV_GROUPS, REP, b.shape[1], b.shape[2])


def kernel(x, norm_pre_mix, norm_post_mix, norm_pre_ffn, norm_post_ffn, w_in, rwkv_mu, rwkv_w0, rwkv_w2, rwkv_a0, rwkv_a2, rwkv_g2, rwkv_k_k, rwkv_k_a, rwkv_r_k, rwkv_ln_w, rwkv_ln_b, nsa_pe_k, nsa_pe_v, nsa_ck_w1, nsa_ck_w2, nsa_cv_w1, nsa_cv_w2, swa_sinks, rel_bias, w_branch, w_out, ffn_w_gate, ffn_w_up, ffn_w_down):
    B, T, D = x.shape
    depth = w_in.shape[0]
    M = B * T
    NB = T // CMP_STRIDE
    NS = T // SEL_BLOCK
    ffn_hidden = ffn_w_gate.shape[2]

    rw_cols = 3 * WIDTH + W_LORA + A_LORA + G_LORA
    nsa_cols = WIDTH + 6 * KV_WIDTH + 3 * N_HEADS
    swa_cols = WIDTH + 2 * KV_WIDTH
    c_nsa = rw_cols
    c_swa = c_nsa + nsa_cols
    c_gate = c_swa + swa_cols
    c_zw = 3 * WIDTH

    ii = np.arange(Q_BLOCK)[:, None]
    bias_swa = bias_tile(rel_bias, SWA_WINDOW + ii - np.arange(SWA_WINDOW + Q_BLOCK)[None, :])
    bias_win = bias_tile(rel_bias, NSA_WINDOW + ii - np.arange(NSA_WINDOW + Q_BLOCK)[None, :])
    bias_sel = bias_tile(rel_bias, SEL_PAD + ii - np.arange(SEL_NEAR)[None, :])
    bias_cmp = bias_tile(rel_bias, ii + (CMP_PAD * CMP_STRIDE - CMP_LEN + 1)
                         - CMP_STRIDE * np.arange(CMP_NEAR)[None, :])
    bias_swa = _group_bias(bias_swa[N_HEADS:])
    bias_win = _group_bias(bias_win[:N_HEADS])
    bias_sel = _group_bias(bias_sel[:N_HEADS])
    bias_cmp = _group_bias(bias_cmp[:N_HEADS])
    far = rel_bias[REL_BUCKETS - 1, :N_HEADS].reshape(KV_GROUPS, REP, 1, 1)
    bfar_sel = jnp.broadcast_to(far, (KV_GROUPS, REP, 1, SEL_FAR_TILE))
    bfar_cmp = jnp.broadcast_to(far, (KV_GROUPS, REP, 1, NB))
    ex = np.zeros((LANE, 3 * WIDTH), np.float32)
    for c in range(3 * N_HEADS):
        ex[c, c * HEAD_DIM:(c + 1) * HEAD_DIM] = 1.0
    expand = jnp.asarray(ex)

    xf = x.reshape(M, D)
    for l in range(depth):
        w = w_in[l]
        w_rw = jnp.concatenate([w[:, :c_zw], _pad_cols(w[:, c_zw:c_zw + W_LORA], LANE),
                                _pad_cols(w[:, c_zw + W_LORA:c_zw + W_LORA + A_LORA], LANE),
                                w[:, c_zw + W_LORA + A_LORA:rw_cols]], axis=1).astype(BF16)
        w_nsa = w[:, c_nsa:c_nsa + WIDTH + 6 * KV_WIDTH].astype(BF16)
        w_swa = w[:, c_swa:c_gate].astype(BF16)
        w_gate = jnp.concatenate([w[:, c_gate:], _pad_cols(w[:, c_swa - 3 * N_HEADS:c_swa], LANE)],
                                 axis=1).astype(BF16)
        g_pre = norm_pre_mix[l][None, :]
        p_rw = norm_matmul(xf, g_pre, w_rw, F32, 1024, 640)
        p_nsa = norm_matmul(xf, g_pre, w_nsa, BF16, 1024, 640)
        p_swa = norm_matmul(xf, g_pre, w_swa, BF16, 1024, 768)
        p_gate = norm_matmul(xf, g_pre, w_gate, F32, 1024, 640)

        mu = rwkv_mu[l]
        mu_p = jnp.concatenate([mu[:c_zw], jnp.pad(mu[c_zw:c_zw + W_LORA], (0, LANE - W_LORA)),
                                jnp.pad(mu[c_zw + W_LORA:c_zw + W_LORA + A_LORA], (0, LANE - A_LORA)),
                                mu[c_zw + W_LORA + A_LORA:]])[None, :]
        r, k, v, a, lw, g = rwkv_prep(p_rw.reshape(B, T, RW_COLS), mu_p, rwkv_w0[l][None, :],
                                      _pad_rows(rwkv_w2[l], LANE), rwkv_a0[l][None, :],
                                      _pad_rows(rwkv_a2[l], LANE), rwkv_g2[l], 512)
        hm = lambda z: _heads_major(z, B, T)
        per_head = lambda z: z.reshape(N_HEADS, 1, HEAD_DIM)
        o_rwkv = rwkv_recurrence(hm(r), hm(k), hm(v), hm(a), hm(lw), hm(g),
                                 per_head(rwkv_k_k[l]), per_head(rwkv_k_a[l]), per_head(rwkv_r_k[l]),
                                 per_head(rwkv_ln_w[l]), per_head(rwkv_ln_b[l]), 512)
        o_rwkv = o_rwkv.transpose(0, 2, 1, 3).reshape(M, WIDTH)

        pn = p_nsa.reshape(B, T, WIDTH + 6 * KV_WIDTH)
        q = _q_blocks(pn[..., :WIDTH], B, T)
        seg = lambda n: pn[..., WIDTH + n * KV_WIDTH:WIDTH + (n + 1) * KV_WIDTH]

        def strides(z):
            z = z.reshape(B, NB, CMP_STRIDE, KV_GROUPS, HEAD_DIM).transpose(0, 3, 1, 2, 4)
            return z.reshape(B, KV_GROUPS, NB, CMP_STRIDE * HEAD_DIM)

        def pe_rows(pe):
            return jnp.broadcast_to(pe.reshape(1, CMP_LEN * HEAD_DIM), (8, CMP_LEN * HEAD_DIM)).astype(BF16)

        k_cmp = nsa_compress(strides(seg(0)), pe_rows(nsa_pe_k[l]), nsa_ck_w1[l].astype(BF16),
                             nsa_ck_w2[l].astype(BF16))
        v_cmp = nsa_compress(strides(seg(1)), pe_rows(nsa_pe_v[l]), nsa_cv_w1[l].astype(BF16),
                             nsa_cv_w2[l].astype(BF16))

        def panels(z):
            z = z.reshape(B, KV_GROUPS, NS, 4, HEAD_DIM).transpose(0, 1, 3, 2, 4)
            return z.reshape(B, KV_GROUPS, NB, HEAD_DIM).astype(BF16)

        near = lambda z: jnp.pad(z, ((0, 0), (0, 0), (CMP_PAD, 8), (0, 0)))
        o_c, sel = nsa_compressed(q, panels(k_cmp), panels(v_cmp), near(k_cmp), near(v_cmp), bias_cmp, bfar_cmp)
        o_s = nsa_selected(q, sel, _kv_groups(seg(2), B, T, SEL_PAD), _kv_groups(seg(3), B, T, SEL_PAD),
                           bias_sel, bfar_sel)
        o_w = band_attention(q, _kv_groups(seg(4), B, T, NSA_WINDOW), _kv_groups(seg(5), B, T, NSA_WINDOW),
                             bias_win, None, NSA_WINDOW, NSA_WINDOW)

        ps = p_swa.reshape(B, T, swa_cols)
        sink = jnp.broadcast_to(swa_sinks[l].reshape(KV_GROUPS, REP, 1, 1), (KV_GROUPS, REP, 1, LANE))
        o_swa = band_attention(_q_blocks(ps[..., :WIDTH], B, T),
                               _kv_groups(ps[..., WIDTH:WIDTH + KV_WIDTH], B, T, SWA_WINDOW),
                               _kv_groups(ps[..., WIDTH + KV_WIDTH:], B, T, SWA_WINDOW),
                               bias_swa, sink, SWA_WINDOW, SWA_WINDOW)

        xf = merge_out(xf, p_gate, o_rwkv, _unblock(o_c, B, T), _unblock(o_s, B, T), _unblock(o_w, B, T),
                       _unblock(o_swa, B, T), w_branch[l].astype(BF16), w_out[l].astype(BF16),
                       norm_post_mix[l][None, :], expand, 256)

        xf = ffn(xf, norm_pre_ffn[l][None, :], ffn_w_gate[l].astype(BF16), ffn_w_up[l].astype(BF16),
                 ffn_w_down[l].astype(BF16), norm_post_ffn[l][None, :], 512, ffn_hidden // 2)
    return xf.reshape(B, T, D)
```

```python
import functools
import math

import numpy as np
import jax
import jax.numpy as jnp
from jax import lax
from jax.experimental import pallas as pl
from jax.experimental.pallas import tpu as pltpu

F32 = jnp.float32
BF16 = jnp.bfloat16
HI = lax.Precision.HIGHEST

D_MODEL = 1024
HEAD_DIM = 64
N_HEADS = 8
KV_GROUPS = 2
REP = N_HEADS // KV_GROUPS
WIDTH = N_HEADS * HEAD_DIM
KV_WIDTH = KV_GROUPS * HEAD_DIM
W_LORA, A_LORA, G_LORA = 64, 64, 128
RWKV_LN_EPS = 64e-5
CMP_STRIDE = 16
CMP_LEN = 32
CMP_HIDDEN = 256
SEL_BLOCK = 64
SEL_TOPK = 16
SEL_FORCE_SCORE = 1e4
NSA_WINDOW = 512
Q_BLOCK = 128
SWA_WINDOW = 128
REL_BUCKETS = 32
REL_MAX_DIST = 1024
NORM_EPS = 1e-6
NEG_INF = -1e30
LANE = 128
RWKV_CHUNK = 64
SEL_NEAR = 1024
SEL_PAD = SEL_NEAR - Q_BLOCK
CMP_NEAR = 64
CMP_PAD = CMP_NEAR - Q_BLOCK // CMP_STRIDE
VMEM_LIMIT = 56 * 1024 * 1024


def _cparams(sem):
    return pltpu.CompilerParams(dimension_semantics=sem, vmem_limit_bytes=VMEM_LIMIT)


def _dot(a, b, precision=None):
    return jnp.dot(a, b, preferred_element_type=F32, precision=precision)


def _dot_nt(a, b, precision=None):
    return lax.dot_general(a, b, (((1,), (1,)), ((), ())), preferred_element_type=F32, precision=precision)


def _dot_tn(a, b, precision=None):
    return lax.dot_general(a, b, (((0,), (0,)), ((), ())), preferred_element_type=F32, precision=precision)


def _bdot(a, b):
    return _dot(a.astype(BF16), b.astype(BF16))


def _bdot_nt(a, b):
    return _dot_nt(a.astype(BF16), b.astype(BF16))


def _bdot_tn(a, b):
    return _dot_tn(a.astype(BF16), b.astype(BF16))


def _rms(x, g):
    return x * lax.rsqrt(jnp.mean(x * x, axis=-1, keepdims=True) + NORM_EPS) * g


def _iota(shape, dim):
    return lax.broadcasted_iota(jnp.int32, shape, dim)


def _norm_mm_body(x_ref, g_ref, w_ref, o_ref, h_ref):
    @pl.when(pl.program_id(1) == 0)
    def _():
        h_ref[...] = _rms(x_ref[...], g_ref[...]).astype(BF16)

    o_ref[...] = _dot(h_ref[...], w_ref[...]).astype(o_ref.dtype)


def norm_matmul(x, g, w, out_dtype, tm, tn):
    M, D = x.shape
    N = w.shape[1]
    return pl.pallas_call(
        _norm_mm_body,
        grid=(M // tm, N // tn),
        in_specs=[pl.BlockSpec((tm, D), lambda i, j: (i, 0)),
                  pl.BlockSpec((1, D), lambda i, j: (0, 0)),
                  pl.BlockSpec((D, tn), lambda i, j: (0, j))],
        out_specs=pl.BlockSpec((tm, tn), lambda i, j: (i, j)),
        out_shape=jax.ShapeDtypeStruct((M, N), out_dtype),
        scratch_shapes=[pltpu.VMEM((tm, D), BF16)],
        compiler_params=_cparams(("parallel", "arbitrary")),
        name="norm_matmul",
    )(x, g, w)


RW_COLS = 3 * WIDTH + 3 * LANE


def _rwkv_prep_body(p_ref, mu_ref, w0_ref, w2_ref, a0_ref, a2_ref, g2_ref,
                    r_o, k_o, v_o, a_o, lw_o, g_o, carry_ref):
    tm = p_ref.shape[0]

    @pl.when(pl.program_id(1) == 0)
    def _():
        carry_ref[...] = jnp.zeros_like(carry_ref)

    p = p_ref[...]
    row = _iota(p.shape, 0)
    prev = jnp.where(row == 0, carry_ref[0:1, :], pltpu.roll(p, 1, axis=0))
    carry_ref[0:1, :] = p[tm - 1:tm, :]
    ps = p + (prev - p) * mu_ref[...]
    r_o[...] = ps[:, 0:WIDTH]
    k_o[...] = ps[:, WIDTH:2 * WIDTH]
    v_o[...] = ps[:, 2 * WIDTH:3 * WIDTH]
    zw = ps[:, 3 * WIDTH:3 * WIDTH + LANE]
    za = ps[:, 3 * WIDTH + LANE:3 * WIDTH + 2 * LANE]
    zg = ps[:, 3 * WIDTH + 2 * LANE:3 * WIDTH + 3 * LANE]
    z = -(w0_ref[...] + _dot(jnp.tanh(zw), w2_ref[...], HI))
    softplus = jnp.maximum(z, 0.0) + jnp.log(1.0 + jnp.exp(-jnp.abs(z)))
    lw_o[...] = -jnp.exp(-softplus - 0.5)
    a_o[...] = jax.nn.sigmoid(a0_ref[...] + _dot(za, a2_ref[...], HI))
    g_o[...] = _dot(jax.nn.sigmoid(zg), g2_ref[...], HI)


def rwkv_prep(p, mu, w0, w2, a0, a2, g2, tm):
    B, T, _ = p.shape
    row = lambda b, i: (b, i, 0)
    fixed = lambda b, i: (0, 0)
    out = jax.ShapeDtypeStruct((B, T, WIDTH), F32)
    return pl.pallas_call(
        _rwkv_prep_body,
        grid=(B, T // tm),
        in_specs=[pl.BlockSpec((None, tm, RW_COLS), row),
                  pl.BlockSpec((1, RW_COLS), fixed),
                  pl.BlockSpec((1, WIDTH), fixed), pl.BlockSpec((LANE, WIDTH), fixed),
                  pl.BlockSpec((1, WIDTH), fixed), pl.BlockSpec((LANE, WIDTH), fixed),
                  pl.BlockSpec((LANE, WIDTH), fixed)],
        out_specs=[pl.BlockSpec((None, tm, WIDTH), row)] * 6,
        out_shape=[out] * 6,
        scratch_shapes=[pltpu.VMEM((8, RW_COLS), F32)],
        compiler_params=_cparams(("parallel", "arbitrary")),
        name="rwkv_prep",
    )(p, mu, w0, w2, a0, a2, g2)


def _rwkv_rec_body(r_ref, k_ref, v_ref, a_ref, lw_ref, g_ref, kk_ref, ka_ref, rk_ref, lnw_ref, lnb_ref,
                   o_ref, s_ref):
    C = RWKV_CHUNK
    assert C == HEAD_DIM
    n_chunks = r_ref.shape[0] // C

    @pl.when(pl.program_id(2) == 0)
    def _():
        s_ref[...] = jnp.zeros_like(s_ref)

    row = _iota((C, C), 0)
    col = _iota((C, C), 1)
    incl = row >= col
    strict = row > col
    eye = row == col
    tri = incl.astype(F32)
    k_k, k_a, r_k = kk_ref[...], ka_ref[...], rk_ref[...]
    ln_w, ln_b = lnw_ref[...], lnb_ref[...]
    maps = []
    for c in range(n_chunks):
        sl = pl.ds(c * C, C)
        r, k, v, a, lw = r_ref[sl, :], k_ref[sl, :], v_ref[sl, :], a_ref[sl, :], lw_ref[sl, :]
        kk = k * k_k
        kk = kk / jnp.maximum(jnp.sqrt(jnp.sum(kk * kk, axis=-1, keepdims=True)), 1e-12)
        k2 = k * (1.0 + (a - 1.0) * k_a)
        cum = _dot(tri, lw, HI)
        p_incl = jnp.exp(cum)
        p_inv = jnp.exp(-cum)
        a_t = -kk * jnp.exp(cum - lw)
        r_t = r * p_incl
        b_t = kk * a * p_inv
        k_t = k2 * p_inv
        ar = jnp.concatenate([a_t, r_t], axis=0)
        gram_b = _bdot_nt(ar, b_t)
        gram_k = _bdot_nt(ar, k_t)
        l_ab = jnp.where(strict, gram_b[0:C], 0.0)
        l_ak = jnp.where(strict, gram_k[0:C], 0.0)
        m_rb = jnp.where(incl, gram_b[C:2 * C], 0.0)
        m_rk = jnp.where(incl, gram_k[C:2 * C], 0.0)
        inv = jnp.where(eye, 1.0, l_ab)
        lp = l_ab
        for _ in range(5):
            lp = _bdot(lp, lp)
            inv = inv + _bdot(lp, inv)
        t_a = _bdot(inv, a_t)
        t_v = _bdot(inv, _bdot(l_ak, v))
        p_end = p_incl[C - 1:C, :]
        bp = b_t * p_end
        kp = k_t * p_end
        maps.append(dict(
            ry=r_t + _bdot(m_rb, t_a),
            yc=_bdot(m_rb, t_v) + _bdot(m_rk, v),
            a=jnp.where(eye, jnp.broadcast_to(p_end, (C, C)), 0.0) + _bdot_tn(bp, t_a),
            g=_bdot_tn(bp, t_v) + _bdot_tn(kp, v),
            bonus=jnp.sum(r * k2 * r_k, axis=-1, keepdims=True) * v))
    h = s_ref[...]
    for c, mp in enumerate(maps):
        y = _bdot(mp["ry"], h) + mp["yc"]
        h = _bdot(mp["a"], h) + mp["g"]
        mean = jnp.mean(y, axis=-1, keepdims=True)
        var = jnp.mean(jnp.square(y - mean), axis=-1, keepdims=True)
        yn = (y - mean) * lax.rsqrt(var + RWKV_LN_EPS) * ln_w + ln_b
        o_ref[pl.ds(c * C, C), :] = (yn + mp["bonus"]) * g_ref[pl.ds(c * C, C), :]
    s_ref[...] = h


def rwkv_recurrence(r, k, v, a, lw, g, k_k, k_a, r_k, ln_w, ln_b, tc):
    B, H, T, N = r.shape
    seq = pl.BlockSpec((None, None, tc, N), lambda b, h, i: (b, h, i, 0))
    par = pl.BlockSpec((None, 1, N), lambda b, h, i: (h, 0, 0))
    return pl.pallas_call(
        _rwkv_rec_body,
        grid=(B, H, T // tc),
        in_specs=[seq] * 6 + [par] * 5,
        out_specs=seq,
        out_shape=jax.ShapeDtypeStruct((B, H, T, N), F32),
        scratch_shapes=[pltpu.VMEM((N, N), F32)],
        compiler_params=_cparams(("parallel", "parallel", "arbitrary")),
        name="rwkv_recurrence",
    )(r, k, v, a, lw, g, k_k, k_a, r_k, ln_w, ln_b)


def _compress_body(z_ref, pe_ref, w1_ref, w2_ref, o_ref):
    nb = z_ref.shape[0]
    half = z_ref.shape[1]
    z = z_ref[...]
    first = _dot(z, w1_ref[0:half, :])
    second = _dot(z, w1_ref[half:2 * half, :])
    pe_term = _dot(pe_ref[...], w1_ref[...])[0:1, :]
    hidden = first + pltpu.roll(second, nb - 1, axis=0) + pe_term
    out = _dot(jax.nn.gelu(hidden).astype(BF16), w2_ref[...])
    rows = _iota(out.shape, 0)
    o_ref[...] = jnp.where(rows < nb - 1, out, 0.0)


def nsa_compress(z, pe, w1, w2):
    B, G, NB, HALF = z.shape
    return pl.pallas_call(
        _compress_body,
        grid=(B, G),
        in_specs=[pl.BlockSpec((None, None, NB, HALF), lambda b, g: (b, g, 0, 0)),
                  pl.BlockSpec((8, 2 * HALF), lambda b, g: (0, 0)),
                  pl.BlockSpec((2 * HALF, CMP_HIDDEN), lambda b, g: (0, 0)),
                  pl.BlockSpec((CMP_HIDDEN, HEAD_DIM), lambda b, g: (0, 0))],
        out_specs=pl.BlockSpec((None, None, NB, HEAD_DIM), lambda b, g: (b, g, 0, 0)),
        out_shape=jax.ShapeDtypeStruct((B, G, NB, HEAD_DIM), F32),
        compiler_params=_cparams(("parallel", "parallel")),
        name="nsa_compress",
    )(z, pe, w1, w2)


def _t5_bucket_np(dist):
    n = np.maximum(dist, 0)
    max_exact = REL_BUCKETS // 2
    nf = np.maximum(n, 1).astype(np.float64)
    large = max_exact + (np.log(nf / max_exact) / math.log(REL_MAX_DIST / max_exact)
                         * (REL_BUCKETS - max_exact)).astype(np.int32)
    large = np.minimum(large, REL_BUCKETS - 1)
    return np.where(n < max_exact, n, large).astype(np.int32)


def _bias_body(tbl_ref, bk_ref, o_ref):
    h = pl.program_id(0)
    bk = bk_ref[...]
    acc = jnp.zeros(bk.shape, F32)
    for b in range(REL_BUCKETS):
        acc = jnp.where(bk == b, tbl_ref[b, h], acc)
    o_ref[...] = acc


def bias_tile(table, dist_np):
    P, Q = dist_np.shape
    H = table.shape[1]
    buckets = jnp.asarray(_t5_bucket_np(dist_np))
    return pl.pallas_call(
        _bias_body,
        grid=(H,),
        in_specs=[pl.BlockSpec(memory_space=pltpu.SMEM),
                  pl.BlockSpec((P, Q), lambda h: (0, 0))],
        out_specs=pl.BlockSpec((None, P, Q), lambda h: (h, 0, 0)),
        out_shape=jax.ShapeDtypeStruct((H, P, Q), F32),
        compiler_params=_cparams(("arbitrary",)),
        name="bias_tile",
    )(table, buckets)


def _nsa_cmp_body(q_ref, kp_ref, vp_ref, kn_ref, vn_ref, bias_ref, bfar_ref, o_ref, sel_ref):
    QB = Q_BLOCK
    NB = kp_ref.shape[0]
    NS = NB // 4
    ns_shift = NS.bit_length() - 1
    blk = pl.program_id(2)
    kp, vp = kp_ref[...], vp_ref[...]
    start = pl.multiple_of(blk * (QB // CMP_STRIDE), 8)
    kn = kn_ref[pl.ds(start, CMP_NEAR), :].astype(BF16)
    vn = vn_ref[pl.ds(start, CMP_NEAR), :].astype(BF16)
    first_near = blk * (QB // CMP_STRIDE) - CMP_PAD
    lane = _iota((1, NB), 1)
    c_far = 4 * (lane & (NS - 1)) + (lane >> ns_shift)
    mask_far = c_far < first_near
    i_n = _iota((QB, CMP_NEAR), 0)
    c_n = _iota((QB, CMP_NEAR), 1)
    dist_n = i_n + (CMP_PAD * CMP_STRIDE - CMP_LEN + 1) - CMP_STRIDE * c_n
    mask_near = (dist_n >= 0) & (first_near + c_n >= 0)
    imp_far = jnp.zeros((QB, NB), F32)
    imp_near = jnp.zeros((QB, CMP_NEAR), F32)
    for r in range(REP):
        rows = slice(r * QB, (r + 1) * QB)
        q = q_ref[rows, :] * (HEAD_DIM ** -0.5)
        s_f = jnp.where(mask_far, _dot_nt(q, kp) + bfar_ref[r], NEG_INF)
        s_n = jnp.where(mask_near, _dot_nt(q, kn) + bias_ref[r], NEG_INF)
        m = jnp.maximum(jnp.max(s_f, axis=-1, keepdims=True), jnp.max(s_n, axis=-1, keepdims=True))
        p_f = jnp.where(mask_far, jnp.exp(s_f - m), 0.0)
        p_n = jnp.where(mask_near, jnp.exp(s_n - m), 0.0)
        den = jnp.sum(p_f, axis=-1, keepdims=True) + jnp.sum(p_n, axis=-1, keepdims=True)
        inv = 1.0 / jnp.where(den > 0, den, 1.0)
        p_f = p_f * inv
        p_n = p_n * inv
        o_ref[rows, :] = _dot(p_f.astype(BF16), vp) + _dot(p_n.astype(BF16), vn)
        imp_far = imp_far + p_f
        imp_near = imp_near + p_n
    panel = [imp_far[:, m * NS:(m + 1) * NS] for m in range(4)]
    j = _iota((QB, NS), 1)
    prev3 = jnp.where(j == 0, 0.0, pltpu.roll(panel[3], 1, axis=1))
    imp = prev3 + 2.0 * panel[0] + 2.0 * panel[1] + 2.0 * panel[2] + panel[3]
    c_abs = first_near + _iota((CMP_NEAR, NS), 0)
    off = c_abs + 1 - 4 * _iota((CMP_NEAR, NS), 1)
    w_near = jnp.where((off == 0) | (off == 4), 1.0, jnp.where((off >= 1) & (off <= 3), 2.0, 0.0))
    imp = imp + _dot(imp_near, w_near, HI)
    cur = 2 * blk + (_iota((QB, NS), 0) >= SEL_BLOCK).astype(jnp.int32)
    forced = (j == 0) | (j == cur) | (j == cur - 1)
    score = jnp.where(forced, SEL_FORCE_SCORE, jnp.where(j <= cur, imp, -1.0))
    sel = jnp.zeros((QB, NS), F32)
    for _ in range(min(SEL_TOPK, NS)):
        m = jnp.max(score, axis=-1, keepdims=True)
        idx = jnp.min(jnp.where(score == m, j, NS), axis=-1, keepdims=True)
        hit = j == idx
        sel = jnp.where(hit, 1.0, sel)
        score = jnp.where(hit, -3.0, score)
    sel_ref[...] = sel.astype(BF16)


def nsa_compressed(q, kperm, vperm, knear, vnear, bias, bias_far):
    B, G, NQ, QR, N = q.shape
    NB = kperm.shape[2]
    NS = NB // 4
    blk4 = lambda b, g, i: (b, g, 0, 0)
    return pl.pallas_call(
        _nsa_cmp_body,
        grid=(B, G, NQ),
        in_specs=[pl.BlockSpec((None, None, None, QR, N), lambda b, g, i: (b, g, i, 0, 0)),
                  pl.BlockSpec((None, None, NB, N), blk4), pl.BlockSpec((None, None, NB, N), blk4),
                  pl.BlockSpec((None, None, knear.shape[2], N), blk4),
                  pl.BlockSpec((None, None, knear.shape[2], N), blk4),
                  pl.BlockSpec((None, REP, Q_BLOCK, CMP_NEAR), lambda b, g, i: (g, 0, 0, 0)),
                  pl.BlockSpec((None, REP, 1, NB), lambda b, g, i: (g, 0, 0, 0))],
        out_specs=[pl.BlockSpec((None, None, None, QR, N), lambda b, g, i: (b, g, i, 0, 0)),
                   pl.BlockSpec((None, None, None, Q_BLOCK, NS), lambda b, g, i: (b, g, i, 0, 0))],
        out_shape=[jax.ShapeDtypeStruct((B, G, NQ, QR, N), F32),
                   jax.ShapeDtypeStruct((B, G, NQ, Q_BLOCK, NS), BF16)],
        compiler_params=_cparams(("parallel", "parallel", "parallel")),
        name="nsa_compressed",
    )(q, kperm, vperm, knear, vnear, bias, bias_far)


SEL_TILE = 1024
SEL_TILE_BLOCKS = SEL_TILE // SEL_BLOCK
SEL_PAD_BLOCKS = SEL_PAD // SEL_BLOCK
AUG = 2 * HEAD_DIM
V_CHUNK = 128


def _nsa_sel_body(qt_ref, selt_ref, k_ref, vt_ref, biast_ref, bfar_ref, o_ref):
    QB = Q_BLOCK
    NQ = REP * QB
    blk = pl.program_id(2)
    qt = qt_ref[...] * (HEAD_DIM ** -0.5)

    row0 = pl.multiple_of(blk * QB, QB)
    kn = k_ref[pl.ds(row0, SEL_NEAR), :]
    q_near = jnp.concatenate([qt, jnp.zeros((AUG - HEAD_DIM, NQ), BF16)], axis=0)
    s = _dot(kn, q_near) + biast_ref[...]
    first = 2 * blk
    base = pl.multiple_of((first // SEL_TILE_BLOCKS) * SEL_TILE_BLOCKS, SEL_TILE_BLOCKS)
    rows = selt_ref[pl.ds(base, 2 * SEL_TILE_BLOCKS), :]
    rows = jnp.concatenate([rows] * REP, axis=1)
    kb = _iota((SEL_NEAR, 2 * SEL_TILE_BLOCKS), 0) >> 6
    uu = _iota((SEL_NEAR, 2 * SEL_TILE_BLOCKS), 1)
    onehot = (uu == kb + (first - base)).astype(BF16)
    chosen = _dot(onehot, rows) > 0.5
    c_k = _iota((SEL_NEAR, NQ), 0)
    i_q = _iota((SEL_NEAR, NQ), 1) & (QB - 1)
    s = jnp.where(chosen & (c_k <= i_q + SEL_PAD), s, NEG_INF)
    m = jnp.max(s, axis=0, keepdims=True)
    p = jnp.exp(s - m).astype(BF16)
    l = jnp.sum(p.astype(F32), axis=0, keepdims=True)
    acc = jnp.zeros((HEAD_DIM, NQ), F32)
    for j in range(SEL_NEAR // V_CHUNK):
        acc = acc + _dot(vt_ref[blk + j], p[j * V_CHUNK:(j + 1) * V_CHUNK])

    pb_u = _iota((SEL_TILE_BLOCKS, QB), 0)
    zeros = jnp.zeros((AUG - HEAD_DIM - 2 * SEL_TILE_BLOCKS, NQ), BF16)
    bfar = bfar_ref[...]

    def far_tile(kt, carry):
        m, l, acc = carry
        b0 = pl.multiple_of(kt * SEL_TILE_BLOCKS, SEL_TILE_BLOCKS)
        picked = selt_ref[pl.ds(b0, SEL_TILE_BLOCKS), :].astype(F32) > 0.5
        neg = jnp.where(picked & (b0 + pb_u < first), 0.0, NEG_INF).astype(BF16)
        q_far = jnp.concatenate([qt, jnp.concatenate([neg] * REP, axis=1), bfar, zeros], axis=0)
        r0 = pl.multiple_of(kt * SEL_TILE, SEL_TILE)
        s = _dot(k_ref[pl.ds(r0, SEL_TILE), :], q_far)
        m_new = jnp.maximum(m, jnp.max(s, axis=0, keepdims=True))
        alpha = jnp.exp(m - m_new)
        p = jnp.exp(s - m_new).astype(BF16)
        l = alpha * l + jnp.sum(p.astype(F32), axis=0, keepdims=True)
        acc = alpha * acc
        for j in range(SEL_TILE // V_CHUNK):
            acc = acc + _dot(vt_ref[kt * (SEL_TILE // V_CHUNK) + j], p[j * V_CHUNK:(j + 1) * V_CHUNK])
        return m_new, l, acc

    n_far = (first + SEL_TILE_BLOCKS - 1) // SEL_TILE_BLOCKS
    m, l, acc = lax.fori_loop(0, n_far, far_tile, (m, l, acc))
    o_ref[...] = acc / l


def nsa_selected(qt, selt, k_aug, vt, bias_t, bias_far):
    B, G, NQ, N, W = qt.shape
    nsp = selt.shape[3]
    rows = k_aug.shape[2]
    return pl.pallas_call(
        _nsa_sel_body,
        grid=(B, G, NQ),
        in_specs=[pl.BlockSpec((None, None, None, N, W), lambda b, g, i: (b, g, i, 0, 0)),
                  pl.BlockSpec((None, None, None, nsp, Q_BLOCK), lambda b, g, i: (b, g, i, 0, 0)),
                  pl.BlockSpec((None, None, rows, AUG), lambda b, g, i: (b, g, 0, 0)),
                  pl.BlockSpec((None, None, rows // V_CHUNK, N, V_CHUNK), lambda b, g, i: (b, g, 0, 0, 0)),
                  pl.BlockSpec((None, SEL_NEAR, W), lambda b, g, i: (g, 0, 0)),
                  pl.BlockSpec((None, SEL_TILE_BLOCKS, W), lambda b, g, i: (g, 0, 0))],
        out_specs=pl.BlockSpec((None, None, None, N, W), lambda b, g, i: (b, g, i, 0, 0)),
        out_shape=jax.ShapeDtypeStruct((B, G, NQ, N, W), F32),
        compiler_params=_cparams(("parallel", "parallel", "parallel")),
        name="nsa_selected",
    )(qt, selt, k_aug, vt, bias_t, bias_far)


def _band_body(*refs, window, pad, has_sink):
    if has_sink:
        q_ref, k_ref, v_ref, bias_ref, sink_ref, o_ref = refs
    else:
        q_ref, k_ref, v_ref, bias_ref, o_ref = refs
    QB = Q_BLOCK
    width = pad + QB
    blk = pl.program_id(2)
    row0 = pl.multiple_of(blk * QB, QB)
    kb = k_ref[pl.ds(row0, width), :]
    vb = v_ref[pl.ds(row0, width), :]
    i_q = _iota((QB, width), 0)
    c_k = _iota((QB, width), 1)
    dist = i_q + pad - c_k
    mask = (dist >= 0) & (dist < window) & (row0 + c_k >= pad)
    for r in range(REP):
        rows = slice(r * QB, (r + 1) * QB)
        q = q_ref[rows, :] * (HEAD_DIM ** -0.5)
        s = jnp.where(mask, _dot_nt(q, kb) + bias_ref[r], NEG_INF)
        m = jnp.max(s, axis=-1, keepdims=True)
        if has_sink:
            sink = sink_ref[r][:, 0:1]
            m = jnp.maximum(m, sink)
        p = jnp.where(mask, jnp.exp(s - m), 0.0)
        den = jnp.sum(p, axis=-1, keepdims=True)
        if has_sink:
            den = den + jnp.exp(sink - m)
        p = p * (1.0 / jnp.where(den > 0, den, 1.0))
        o_ref[rows, :] = _dot(p.astype(BF16), vb)


def band_attention(q, k, v, bias, sink, window, pad):
    B, G, NQ, QR, N = q.shape
    rows = k.shape[2]
    width = pad + Q_BLOCK
    kv = pl.BlockSpec((None, None, rows, N), lambda b, g, i: (b, g, 0, 0))
    in_specs = [pl.BlockSpec((None, None, None, QR, N), lambda b, g, i: (b, g, i, 0, 0)), kv, kv,
                pl.BlockSpec((None, REP, Q_BLOCK, width), lambda b, g, i: (g, 0, 0, 0))]
    args = [q, k, v, bias]
    if sink is not None:
        in_specs.append(pl.BlockSpec((None, REP, 1, LANE), lambda b, g, i: (g, 0, 0, 0)))
        args.append(sink)
    return pl.pallas_call(
        functools.partial(_band_body, window=window, pad=pad, has_sink=sink is not None),
        grid=(B, G, NQ),
        in_specs=in_specs,
        out_specs=pl.BlockSpec((None, None, None, QR, N), lambda b, g, i: (b, g, i, 0, 0)),
        out_shape=jax.ShapeDtypeStruct((B, G, NQ, QR, N), F32),
        compiler_params=_cparams(("parallel", "parallel", "parallel")),
        name="band_attention",
    )(*args)


GATE_COLS = 3 * D_MODEL + LANE


def _merge_body(x_ref, pg_ref, orw_ref, oc_ref, os_ref, ow_ref, osw_ref, wb_ref, wo_ref, gn_ref, ex_ref, o_ref):
    pg = pg_ref[...]
    head_gates = jax.nn.sigmoid(pg[:, 3 * D_MODEL:])
    ge = _dot(head_gates, ex_ref[...], HI)
    o_nsa = (ge[:, 0:WIDTH] * oc_ref[...] + ge[:, WIDTH:2 * WIDTH] * os_ref[...]
             + ge[:, 2 * WIDTH:3 * WIDTH] * ow_ref[...])
    merged = (jax.nn.sigmoid(pg[:, 0:D_MODEL]) * _dot(orw_ref[...].astype(BF16), wb_ref[0])
              + jax.nn.sigmoid(pg[:, D_MODEL:2 * D_MODEL]) * _dot(o_nsa.astype(BF16), wb_ref[1])
              + jax.nn.sigmoid(pg[:, 2 * D_MODEL:3 * D_MODEL]) * _dot(osw_ref[...].astype(BF16), wb_ref[2]))
    y = _dot(merged.astype(BF16), wo_ref[...])
    o_ref[...] = x_ref[...] + _rms(y, gn_ref[...])


def merge_out(x, pg, o_rwkv, o_c, o_s, o_w, o_swa, w_branch, w_out, g_post, expand, tm):
    M, D = x.shape
    row = lambda i: (i, 0)
    wide = pl.BlockSpec((tm, WIDTH), row)
    return pl.pallas_call(
        _merge_body,
        grid=(M // tm,),
        in_specs=[pl.BlockSpec((tm, D), row), pl.BlockSpec((tm, GATE_COLS), row),
                  wide, wide, wide, wide, wide,
                  pl.BlockSpec((3, WIDTH, D), lambda i: (0, 0, 0)),
                  pl.BlockSpec((D, D), lambda i: (0, 0)),
                  pl.BlockSpec((1, D), lambda i: (0, 0)),
                  pl.BlockSpec((LANE, 3 * WIDTH), lambda i: (0, 0))],
        out_specs=pl.BlockSpec((tm, D), row),
        out_shape=jax.ShapeDtypeStruct((M, D), F32),
        compiler_params=_cparams(("parallel",)),
        name="merge_out",
    )(x, pg, o_rwkv, o_c, o_s, o_w, o_swa, w_branch, w_out, g_post, expand)


def _ffn_body(x_ref, gpre_ref, wg_ref, wu_ref, wd_ref, gpost_ref, o_ref, h_ref, acc_ref):
    j = pl.program_id(1)

    @pl.when(j == 0)
    def _():
        h_ref[...] = _rms(x_ref[...], gpre_ref[...]).astype(BF16)
        acc_ref[...] = jnp.zeros_like(acc_ref)

    h = h_ref[...]
    gate = _dot(h, wg_ref[...])
    act = gate * jax.nn.sigmoid(gate) * _dot(h, wu_ref[...])
    acc_ref[...] += _dot(act.astype(BF16), wd_ref[...])

    @pl.when(j == pl.num_programs(1) - 1)
    def _():
        o_ref[...] = x_ref[...] + _rms(acc_ref[...], gpost_ref[...])


def ffn(x, g_pre, w_gate, w_up, w_down, g_post, tm, th):
    M, D = x.shape
    Hd = w_gate.shape[1]
    return pl.pallas_call(
        _ffn_body,
        grid=(M // tm, Hd // th),
        in_specs=[pl.BlockSpec((tm, D), lambda i, j: (i, 0)),
                  pl.BlockSpec((1, D), lambda i, j: (0, 0)),
                  pl.BlockSpec((D, th), lambda i, j: (0, j)),
                  pl.BlockSpec((D, th), lambda i, j: (0, j)),
                  pl.BlockSpec((th, D), lambda i, j: (j, 0)),
                  pl.BlockSpec((1, D), lambda i, j: (0, 0))],
        out_specs=pl.BlockSpec((tm, D), lambda i, j: (i, 0)),
        out_shape=jax.ShapeDtypeStruct((M, D), F32),
        scratch_shapes=[pltpu.VMEM((tm, D), BF16), pltpu.VMEM((tm, D), F32)],
        compiler_params=_cparams(("parallel", "arbitrary")),
        name="ffn",
    )(x, g_pre, w_gate, w_up, w_down, g_post)


def _pad_cols(w, n):
    return jnp.pad(w, ((0, 0), (0, n - w.shape[1])))


def _pad_rows(w, n):
    return jnp.pad(w, ((0, n - w.shape[0]), (0, 0)))


def _heads_major(z, B, T):
    return z.reshape(B, T, N_HEADS, HEAD_DIM).transpose(0, 2, 1, 3)


def _q_blocks(q, B, T):
    nq = T // Q_BLOCK
    q = q.reshape(B, nq, Q_BLOCK, KV_GROUPS, REP, HEAD_DIM).transpose(0, 3, 1, 4, 2, 5)
    return q.reshape(B, KV_GROUPS, nq, REP * Q_BLOCK, HEAD_DIM)


def _q_blocks_t(q, B, T):
    nq = T // Q_BLOCK
    q = q.reshape(B, nq, Q_BLOCK, KV_GROUPS, REP, HEAD_DIM).transpose(0, 3, 1, 5, 4, 2)
    return q.reshape(B, KV_GROUPS, nq, HEAD_DIM, REP * Q_BLOCK)


def _unblock_t(o, B, T):
    nq = T // Q_BLOCK
    o = o.reshape(B, KV_GROUPS, nq, HEAD_DIM, REP, Q_BLOCK).transpose(0, 2, 5, 1, 4, 3)
    return o.reshape(B * T, WIDTH)


def _unblock(o, B, T):
    nq = T // Q_BLOCK
    o = o.reshape(B, KV_GROUPS, nq, REP, Q_BLOCK, HEAD_DIM).transpose(0, 2, 4, 1, 3, 5)
    return o.reshape(B * T, WIDTH)


def _kv_groups(z, B, T, pad):
    z = z.reshape(B, T, KV_GROUPS, HEAD_DIM).transpose(0, 2, 1, 3)
    return jnp.pad(z, ((0, 0), (0, 0), (pad, 0), (0, 0)))


def _group_bias(b):
    return b.reshape(KV_GROUPS, REP, b.shape[1], b.shape[2])


def kernel(x, norm_pre_mix, norm_post_mix, norm_pre_ffn, norm_post_ffn, w_in, rwkv_mu, rwkv_w0, rwkv_w2, rwkv_a0, rwkv_a2, rwkv_g2, rwkv_k_k, rwkv_k_a, rwkv_r_k, rwkv_ln_w, rwkv_ln_b, nsa_pe_k, nsa_pe_v, nsa_ck_w1, nsa_ck_w2, nsa_cv_w1, nsa_cv_w2, swa_sinks, rel_bias, w_branch, w_out, ffn_w_gate, ffn_w_up, ffn_w_down):
    B, T, D = x.shape
    depth = w_in.shape[0]
    M = B * T
    NB = T // CMP_STRIDE
    NS = T // SEL_BLOCK
    ffn_hidden = ffn_w_gate.shape[2]

    rw_cols = 3 * WIDTH + W_LORA + A_LORA + G_LORA
    nsa_cols = WIDTH + 6 * KV_WIDTH + 3 * N_HEADS
    swa_cols = WIDTH + 2 * KV_WIDTH
    c_nsa = rw_cols
    c_swa = c_nsa + nsa_cols
    c_gate = c_swa + swa_cols
    c_zw = 3 * WIDTH

    ii = np.arange(Q_BLOCK)[:, None]
    bias_swa = bias_tile(rel_bias, SWA_WINDOW + ii - np.arange(SWA_WINDOW + Q_BLOCK)[None, :])
    bias_win = bias_tile(rel_bias, NSA_WINDOW + ii - np.arange(NSA_WINDOW + Q_BLOCK)[None, :])
    bias_sel = bias_tile(rel_bias, SEL_PAD + ii - np.arange(SEL_NEAR)[None, :])
    bias_cmp = bias_tile(rel_bias, ii + (CMP_PAD * CMP_STRIDE - CMP_LEN + 1)
                         - CMP_STRIDE * np.arange(CMP_NEAR)[None, :])
    bias_swa = _group_bias(bias_swa[N_HEADS:])
    bias_win = _group_bias(bias_win[:N_HEADS])
    bias_sel = _group_bias(bias_sel[:N_HEADS])
    bias_cmp = _group_bias(bias_cmp[:N_HEADS])
    far = rel_bias[REL_BUCKETS - 1, :N_HEADS].reshape(KV_GROUPS, REP, 1, 1)
    bias_sel_t = bias_sel.transpose(0, 3, 1, 2).reshape(KV_GROUPS, SEL_NEAR, REP * Q_BLOCK)
    far_hi = far.astype(BF16)
    far_lo = (far - far_hi.astype(F32)).astype(BF16)
    far_rows = jnp.concatenate([far_hi, far_lo], axis=2)
    far_rows = jnp.broadcast_to(far_rows, (KV_GROUPS, REP, 2, Q_BLOCK)).transpose(0, 2, 1, 3)
    bfar_sel = jnp.pad(far_rows.reshape(KV_GROUPS, 2, REP * Q_BLOCK), ((0, 0), (0, SEL_TILE_BLOCKS - 2), (0, 0)))
    rows_pad = SEL_PAD + T
    blk_in_tile = (np.arange(rows_pad) // SEL_BLOCK) % SEL_TILE_BLOCKS
    ka = np.zeros((rows_pad, AUG - HEAD_DIM), np.float32)
    ka[np.arange(rows_pad), blk_in_tile] = 1.0
    ka[:, SEL_TILE_BLOCKS:SEL_TILE_BLOCKS + 2] = 1.0
    key_aug = jnp.asarray(ka, BF16)
    nsp = -(-(NS + SEL_PAD_BLOCKS) // SEL_TILE_BLOCKS) * SEL_TILE_BLOCKS + SEL_TILE_BLOCKS
    bfar_cmp = jnp.broadcast_to(far, (KV_GROUPS, REP, 1, NB))
    ex = np.zeros((LANE, 3 * WIDTH), np.float32)
    for c in range(3 * N_HEADS):
        ex[c, c * HEAD_DIM:(c + 1) * HEAD_DIM] = 1.0
    expand = jnp.asarray(ex)

    xf = x.reshape(M, D)
    for l in range(depth):
        w = w_in[l]
        w_rw = jnp.concatenate([w[:, :c_zw], _pad_cols(w[:, c_zw:c_zw + W_LORA], LANE),
                                _pad_cols(w[:, c_zw + W_LORA:c_zw + W_LORA + A_LORA], LANE),
                                w[:, c_zw + W_LORA + A_LORA:rw_cols]], axis=1).astype(BF16)
        w_nsa = w[:, c_nsa:c_nsa + WIDTH + 6 * KV_WIDTH].astype(BF16)
        w_swa = w[:, c_swa:c_gate].astype(BF16)
        w_gate = jnp.concatenate([w[:, c_gate:], _pad_cols(w[:, c_swa - 3 * N_HEADS:c_swa], LANE)],
                                 axis=1).astype(BF16)
        g_pre = norm_pre_mix[l][None, :]
        p_rw = norm_matmul(xf, g_pre, w_rw, F32, 1024, 640)
        p_nsa = norm_matmul(xf, g_pre, w_nsa, BF16, 1024, 640)
        p_swa = norm_matmul(xf, g_pre, w_swa, BF16, 1024, 768)
        p_gate = norm_matmul(xf, g_pre, w_gate, F32, 1024, 640)

        mu = rwkv_mu[l]
        mu_p = jnp.concatenate([mu[:c_zw], jnp.pad(mu[c_zw:c_zw + W_LORA], (0, LANE - W_LORA)),
                                jnp.pad(mu[c_zw + W_LORA:c_zw + W_LORA + A_LORA], (0, LANE - A_LORA)),
                                mu[c_zw + W_LORA + A_LORA:]])[None, :]
        r, k, v, a, lw, g = rwkv_prep(p_rw.reshape(B, T, RW_COLS), mu_p, rwkv_w0[l][None, :],
                                      _pad_rows(rwkv_w2[l], LANE), rwkv_a0[l][None, :],
                                      _pad_rows(rwkv_a2[l], LANE), rwkv_g2[l], 512)
        hm = lambda z: _heads_major(z, B, T)
        per_head = lambda z: z.reshape(N_HEADS, 1, HEAD_DIM)
        o_rwkv = rwkv_recurrence(hm(r), hm(k), hm(v), hm(a), hm(lw), hm(g),
                                 per_head(rwkv_k_k[l]), per_head(rwkv_k_a[l]), per_head(rwkv_r_k[l]),
                                 per_head(rwkv_ln_w[l]), per_head(rwkv_ln_b[l]), 512)
        o_rwkv = o_rwkv.transpose(0, 2, 1, 3).reshape(M, WIDTH)

        pn = p_nsa.reshape(B, T, WIDTH + 6 * KV_WIDTH)
        q = _q_blocks(pn[..., :WIDTH], B, T)
        seg = lambda n: pn[..., WIDTH + n * KV_WIDTH:WIDTH + (n + 1) * KV_WIDTH]

        def strides(z):
            z = z.reshape(B, NB, CMP_STRIDE, KV_GROUPS, HEAD_DIM).transpose(0, 3, 1, 2, 4)
            return z.reshape(B, KV_GROUPS, NB, CMP_STRIDE * HEAD_DIM)

        def pe_rows(pe):
            return jnp.broadcast_to(pe.reshape(1, CMP_LEN * HEAD_DIM), (8, CMP_LEN * HEAD_DIM)).astype(BF16)

        k_cmp = nsa_compress(strides(seg(0)), pe_rows(nsa_pe_k[l]), nsa_ck_w1[l].astype(BF16),
                             nsa_ck_w2[l].astype(BF16))
        v_cmp = nsa_compress(strides(seg(1)), pe_rows(nsa_pe_v[l]), nsa_cv_w1[l].astype(BF16),
                             nsa_cv_w2[l].astype(BF16))

        def panels(z):
            z = z.reshape(B, KV_GROUPS, NS, 4, HEAD_DIM).transpose(0, 1, 3, 2, 4)
            return z.reshape(B, KV_GROUPS, NB, HEAD_DIM).astype(BF16)

        near = lambda z: jnp.pad(z, ((0, 0), (0, 0), (CMP_PAD, 8), (0, 0)))
        o_c, sel = nsa_compressed(q, panels(k_cmp), panels(v_cmp), near(k_cmp), near(v_cmp), bias_cmp, bfar_cmp)
        k_aug = jnp.concatenate([_kv_groups(seg(2), B, T, SEL_PAD),
                                 jnp.broadcast_to(key_aug, (B, KV_GROUPS) + key_aug.shape)], axis=-1)
        v_t = _kv_groups(seg(3), B, T, SEL_PAD).reshape(B, KV_GROUPS, -1, V_CHUNK, HEAD_DIM).swapaxes(-1, -2)
        sel_t = jnp.pad(sel.swapaxes(-1, -2), ((0, 0), (0, 0), (0, 0), (SEL_PAD_BLOCKS, nsp - NS - SEL_PAD_BLOCKS), (0, 0)))
        o_s = nsa_selected(_q_blocks_t(pn[..., :WIDTH], B, T), sel_t, k_aug, v_t, bias_sel_t, bfar_sel)
        o_w = band_attention(q, _kv_groups(seg(4), B, T, NSA_WINDOW), _kv_groups(seg(5), B, T, NSA_WINDOW),
                             bias_win, None, NSA_WINDOW, NSA_WINDOW)

        ps = p_swa.reshape(B, T, swa_cols)
        sink = jnp.broadcast_to(swa_sinks[l].reshape(KV_GROUPS, REP, 1, 1), (KV_GROUPS, REP, 1, LANE))
        o_swa = band_attention(_q_blocks(ps[..., :WIDTH], B, T),
                               _kv_groups(ps[..., WIDTH:WIDTH + KV_WIDTH], B, T, SWA_WINDOW),
                               _kv_groups(ps[..., WIDTH + KV_WIDTH:], B, T, SWA_WINDOW),
                               bias_swa, sink, SWA_WINDOW, SWA_WINDOW)

        xf = merge_out(xf, p_gate, o_rwkv, _unblock(o_c, B, T), _unblock_t(o_s, B, T), _unblock(o_w, B, T),
                       _unblock(o_swa, B, T), w_branch[l].astype(BF16), w_out[l].astype(BF16),
                       norm_post_mix[l][None, :], expand, 256)

        xf = ffn(xf, norm_pre_ffn[l][None, :], ffn_w_gate[l].astype(BF16), ffn_w_up[l].astype(BF16),
                 ffn_w_down[l].astype(BF16), norm_post_ffn[l][None, :], 512, ffn_hidden // 2)
    return xf.reshape(B, T, D)
```

```python
import functools
import math

import numpy as np
import jax
import jax.numpy as jnp
from jax import lax
from jax.experimental import pallas as pl
from jax.experimental.pallas import tpu as pltpu

F32 = jnp.float32
BF16 = jnp.bfloat16
HI = lax.Precision.HIGHEST

D_MODEL = 1024
HEAD_DIM = 64
N_HEADS = 8
KV_GROUPS = 2
REP = N_HEADS // KV_GROUPS
WIDTH = N_HEADS * HEAD_DIM
KV_WIDTH = KV_GROUPS * HEAD_DIM
W_LORA, A_LORA, G_LORA = 64, 64, 128
RWKV_LN_EPS = 64e-5
CMP_STRIDE = 16
CMP_LEN = 32
CMP_HIDDEN = 256
SEL_BLOCK = 64
SEL_TOPK = 16
SEL_FORCE_SCORE = 1e4
NSA_WINDOW = 512
Q_BLOCK = 128
SWA_WINDOW = 128
REL_BUCKETS = 32
REL_MAX_DIST = 1024
NORM_EPS = 1e-6
NEG_INF = -1e30
LANE = 128
RWKV_CHUNK = 64
SEL_NEAR = 1024
SEL_SPAN = SEL_NEAR - Q_BLOCK
SEL_PAD = SEL_NEAR
SEL_TILE = 1024
SEL_TILE_BLOCKS = SEL_TILE // SEL_BLOCK
SEL_PAD_BLOCKS = SEL_PAD // SEL_BLOCK
AUG = 2 * HEAD_DIM
V_CHUNK = 128
CMP_NEAR = 64
CMP_PAD = CMP_NEAR - Q_BLOCK // CMP_STRIDE
VMEM_LIMIT = 56 * 1024 * 1024


def _cparams(sem):
    return pltpu.CompilerParams(dimension_semantics=sem, vmem_limit_bytes=VMEM_LIMIT)


def _dot(a, b, precision=None):
    return jnp.dot(a, b, preferred_element_type=F32, precision=precision)


def _dot_nt(a, b, precision=None):
    return lax.dot_general(a, b, (((1,), (1,)), ((), ())), preferred_element_type=F32, precision=precision)


def _dot_tn(a, b, precision=None):
    return lax.dot_general(a, b, (((0,), (0,)), ((), ())), preferred_element_type=F32, precision=precision)


def _bdot(a, b):
    return _dot(a.astype(BF16), b.astype(BF16))


def _bdot_nt(a, b):
    return _dot_nt(a.astype(BF16), b.astype(BF16))


def _bdot_tn(a, b):
    return _dot_tn(a.astype(BF16), b.astype(BF16))


def _rms(x, g):
    return x * lax.rsqrt(jnp.mean(x * x, axis=-1, keepdims=True) + NORM_EPS) * g


def _iota(shape, dim):
    return lax.broadcasted_iota(jnp.int32, shape, dim)


def _norm_mm_body(x_ref, g_ref, w_ref, o_ref, h_ref):
    @pl.when(pl.program_id(1) == 0)
    def _():
        h_ref[...] = _rms(x_ref[...], g_ref[...]).astype(BF16)

    o_ref[...] = _dot(h_ref[...], w_ref[...]).astype(o_ref.dtype)


def norm_matmul(x, g, w, out_dtype, tm, tn):
    M, D = x.shape
    N = w.shape[1]
    return pl.pallas_call(
        _norm_mm_body,
        grid=(M // tm, N // tn),
        in_specs=[pl.BlockSpec((tm, D), lambda i, j: (i, 0)),
                  pl.BlockSpec((1, D), lambda i, j: (0, 0)),
                  pl.BlockSpec((D, tn), lambda i, j: (0, j))],
        out_specs=pl.BlockSpec((tm, tn), lambda i, j: (i, j)),
        out_shape=jax.ShapeDtypeStruct((M, N), out_dtype),
        scratch_shapes=[pltpu.VMEM((tm, D), BF16)],
        compiler_params=_cparams(("parallel", "arbitrary")),
        name="norm_matmul",
    )(x, g, w)


RW_COLS = 3 * WIDTH + 3 * LANE


def _rwkv_prep_body(p_ref, mu_ref, w0_ref, w2_ref, a0_ref, a2_ref, g2_ref,
                    r_o, k_o, v_o, a_o, lw_o, g_o, carry_ref):
    tm = p_ref.shape[0]

    @pl.when(pl.program_id(1) == 0)
    def _():
        carry_ref[...] = jnp.zeros_like(carry_ref)

    p = p_ref[...]
    row = _iota(p.shape, 0)
    prev = jnp.where(row == 0, carry_ref[0:1, :], pltpu.roll(p, 1, axis=0))
    carry_ref[0:1, :] = p[tm - 1:tm, :]
    ps = p + (prev - p) * mu_ref[...]
    r_o[...] = ps[:, 0:WIDTH]
    k_o[...] = ps[:, WIDTH:2 * WIDTH]
    v_o[...] = ps[:, 2 * WIDTH:3 * WIDTH]
    zw = ps[:, 3 * WIDTH:3 * WIDTH + LANE]
    za = ps[:, 3 * WIDTH + LANE:3 * WIDTH + 2 * LANE]
    zg = ps[:, 3 * WIDTH + 2 * LANE:3 * WIDTH + 3 * LANE]
    z = -(w0_ref[...] + _dot(jnp.tanh(zw), w2_ref[...], HI))
    softplus = jnp.maximum(z, 0.0) + jnp.log(1.0 + jnp.exp(-jnp.abs(z)))
    lw_o[...] = -jnp.exp(-softplus - 0.5)
    a_o[...] = jax.nn.sigmoid(a0_ref[...] + _dot(za, a2_ref[...], HI))
    g_o[...] = _dot(jax.nn.sigmoid(zg), g2_ref[...], HI)


def rwkv_prep(p, mu, w0, w2, a0, a2, g2, tm):
    B, T, _ = p.shape
    row = lambda b, i: (b, i, 0)
    fixed = lambda b, i: (0, 0)
    out = jax.ShapeDtypeStruct((B, T, WIDTH), F32)
    return pl.pallas_call(
        _rwkv_prep_body,
        grid=(B, T // tm),
        in_specs=[pl.BlockSpec((None, tm, RW_COLS), row),
                  pl.BlockSpec((1, RW_COLS), fixed),
                  pl.BlockSpec((1, WIDTH), fixed), pl.BlockSpec((LANE, WIDTH), fixed),
                  pl.BlockSpec((1, WIDTH), fixed), pl.BlockSpec((LANE, WIDTH), fixed),
                  pl.BlockSpec((LANE, WIDTH), fixed)],
        out_specs=[pl.BlockSpec((None, tm, WIDTH), row)] * 6,
        out_shape=[out] * 6,
        scratch_shapes=[pltpu.VMEM((8, RW_COLS), F32)],
        compiler_params=_cparams(("parallel", "arbitrary")),
        name="rwkv_prep",
    )(p, mu, w0, w2, a0, a2, g2)


RWKV_HEADS_PER_STEP = 4


def _bmm(a, b, precision=None):
    return lax.dot_general(a, b, (((2,), (1,)), ((0,), (0,))), preferred_element_type=F32, precision=precision)


def _bbmm(a, b):
    return _bmm(a.astype(BF16), b.astype(BF16))


def _bbmm_nt(a, b):
    return lax.dot_general(a.astype(BF16), b.astype(BF16), (((2,), (2,)), ((0,), (0,))),
                           preferred_element_type=F32)


def _bbmm_tn(a, b):
    return lax.dot_general(a.astype(BF16), b.astype(BF16), (((1,), (1,)), ((0,), (0,))),
                           preferred_element_type=F32)


def _rwkv_rec_body(r_ref, k_ref, v_ref, a_ref, lw_ref, g_ref, kk_ref, ka_ref, rk_ref, lnw_ref, lnb_ref,
                   o_ref, s_ref):
    C = RWKV_CHUNK
    N = HEAD_DIM
    assert C == N
    tc = r_ref.shape[0]
    nc = tc // C
    hb = r_ref.shape[1] // N

    @pl.when(pl.program_id(2) == 0)
    def _():
        s_ref[...] = jnp.zeros_like(s_ref)

    def chunks(ref):
        x = ref[...]
        return jnp.concatenate([x[:, h * N:(h + 1) * N].reshape(nc, C, N) for h in range(hb)], axis=0)

    def per_head(ref):
        x = ref[...]
        return jnp.concatenate([jnp.broadcast_to(x[:, h * N:(h + 1) * N][None], (nc, 1, N))
                                for h in range(hb)], axis=0)

    row = _iota((C, C), 0)
    col = _iota((C, C), 1)
    incl = (row >= col)[None]
    strict = (row > col)[None]
    eye = (row == col)[None]
    nb = hb * nc
    r, k, v, a, lw = chunks(r_ref), chunks(k_ref), chunks(v_ref), chunks(a_ref), chunks(lw_ref)
    kk = k * per_head(kk_ref)
    kk = kk / jnp.maximum(jnp.sqrt(jnp.sum(kk * kk, axis=-1, keepdims=True)), 1e-12)
    k2 = k * (1.0 + (a - 1.0) * per_head(ka_ref))
    tri = jnp.broadcast_to(incl.astype(F32), (nb, C, C))
    cum = _bmm(tri, lw, HI)
    p_incl = jnp.exp(cum)
    p_inv = jnp.exp(-cum)
    a_t = -kk * jnp.exp(cum - lw)
    r_t = r * p_incl
    b_t = kk * a * p_inv
    k_t = k2 * p_inv
    l_ab = jnp.where(strict, _bbmm_nt(a_t, b_t), 0.0)
    l_ak = jnp.where(strict, _bbmm_nt(a_t, k_t), 0.0)
    m_rb = jnp.where(incl, _bbmm_nt(r_t, b_t), 0.0)
    m_rk = jnp.where(incl, _bbmm_nt(r_t, k_t), 0.0)
    inv = jnp.where(eye, 1.0, l_ab)
    lp = l_ab
    for _ in range(5):
        lp = _bbmm(lp, lp)
        inv = inv + _bbmm(lp, inv)
    t_a = _bbmm(inv, a_t)
    t_v = _bbmm(inv, _bbmm(l_ak, v))
    p_end = p_incl[:, C - 1:C, :]
    bp = b_t * p_end
    kp = k_t * p_end
    ry = (r_t + _bbmm(m_rb, t_a)).reshape(hb, nc, C, N)
    yc = (_bbmm(m_rb, t_v) + _bbmm(m_rk, v)).reshape(hb, nc, C, N)
    am = (jnp.where(eye, jnp.broadcast_to(p_end, (nb, C, C)), 0.0) + _bbmm_tn(bp, t_a)).reshape(hb, nc, C, C)
    gm = (_bbmm_tn(bp, t_v) + _bbmm_tn(kp, v)).reshape(hb, nc, C, N)
    h = s_ref[...]
    ys = []
    for c in range(nc):
        ys.append(_bbmm(ry[:, c], h) + yc[:, c])
        h = _bbmm(am[:, c], h) + gm[:, c]
    s_ref[...] = h
    y = jnp.stack(ys, axis=1).reshape(nb, C, N)
    mean = jnp.mean(y, axis=-1, keepdims=True)
    var = jnp.mean(jnp.square(y - mean), axis=-1, keepdims=True)
    yn = (y - mean) * lax.rsqrt(var + RWKV_LN_EPS) * per_head(lnw_ref) + per_head(lnb_ref)
    bonus = jnp.sum(r * k2 * per_head(rk_ref), axis=-1, keepdims=True) * v
    out = (yn + bonus) * chunks(g_ref)
    o_ref[...] = jnp.concatenate([out[h * nc:(h + 1) * nc].reshape(tc, N) for h in range(hb)], axis=1)


def rwkv_recurrence(r, k, v, a, lw, g, k_k, k_a, r_k, ln_w, ln_b, tc):
    B, T, W = r.shape
    slab = RWKV_HEADS_PER_STEP * HEAD_DIM
    seq = pl.BlockSpec((None, tc, slab), lambda b, h, i: (b, i, h))
    par = pl.BlockSpec((1, slab), lambda b, h, i: (0, h))
    return pl.pallas_call(
        _rwkv_rec_body,
        grid=(B, W // slab, T // tc),
        in_specs=[seq] * 6 + [par] * 5,
        out_specs=seq,
        out_shape=jax.ShapeDtypeStruct((B, T, W), F32),
        scratch_shapes=[pltpu.VMEM((RWKV_HEADS_PER_STEP, HEAD_DIM, HEAD_DIM), F32)],
        compiler_params=_cparams(("parallel", "parallel", "arbitrary")),
        name="rwkv_recurrence",
    )(r, k, v, a, lw, g, k_k, k_a, r_k, ln_w, ln_b)


def _compress_body(z_ref, pe_ref, w1_ref, w2_ref, o_ref):
    nb = z_ref.shape[0]
    half = z_ref.shape[1]
    z = z_ref[...]
    first = _dot(z, w1_ref[0:half, :])
    second = _dot(z, w1_ref[half:2 * half, :])
    pe_term = _dot(pe_ref[...], w1_ref[...])[0:1, :]
    hidden = first + pltpu.roll(second, nb - 1, axis=0) + pe_term
    out = _dot(jax.nn.gelu(hidden).astype(BF16), w2_ref[...])
    rows = _iota(out.shape, 0)
    o_ref[...] = jnp.where(rows < nb - 1, out, 0.0)


def nsa_compress(z, pe, w1, w2):
    B, G, NB, HALF = z.shape
    return pl.pallas_call(
        _compress_body,
        grid=(B, G),
        in_specs=[pl.BlockSpec((None, None, NB, HALF), lambda b, g: (b, g, 0, 0)),
                  pl.BlockSpec((8, 2 * HALF), lambda b, g: (0, 0)),
                  pl.BlockSpec((2 * HALF, CMP_HIDDEN), lambda b, g: (0, 0)),
                  pl.BlockSpec((CMP_HIDDEN, HEAD_DIM), lambda b, g: (0, 0))],
        out_specs=pl.BlockSpec((None, None, NB, HEAD_DIM), lambda b, g: (b, g, 0, 0)),
        out_shape=jax.ShapeDtypeStruct((B, G, NB, HEAD_DIM), F32),
        compiler_params=_cparams(("parallel", "parallel")),
        name="nsa_compress",
    )(z, pe, w1, w2)


def _t5_bucket_np(dist):
    n = np.maximum(dist, 0)
    max_exact = REL_BUCKETS // 2
    nf = np.maximum(n, 1).astype(np.float64)
    large = max_exact + (np.log(nf / max_exact) / math.log(REL_MAX_DIST / max_exact)
                         * (REL_BUCKETS - max_exact)).astype(np.int32)
    large = np.minimum(large, REL_BUCKETS - 1)
    return np.where(n < max_exact, n, large).astype(np.int32)


def _bias_body(tbl_ref, bk_ref, o_ref):
    h = pl.program_id(0)
    bk = bk_ref[...]
    acc = jnp.zeros(bk.shape, F32)
    for b in range(REL_BUCKETS):
        acc = jnp.where(bk == b, tbl_ref[b, h], acc)
    o_ref[...] = acc


def bias_tile(table, dist_np):
    P, Q = dist_np.shape
    H = table.shape[1]
    buckets = jnp.asarray(_t5_bucket_np(dist_np))
    return pl.pallas_call(
        _bias_body,
        grid=(H,),
        in_specs=[pl.BlockSpec(memory_space=pltpu.SMEM),
                  pl.BlockSpec((P, Q), lambda h: (0, 0))],
        out_specs=pl.BlockSpec((None, P, Q), lambda h: (h, 0, 0)),
        out_shape=jax.ShapeDtypeStruct((H, P, Q), F32),
        compiler_params=_cparams(("arbitrary",)),
        name="bias_tile",
    )(table, buckets)


def _nsa_cmp_body(q_ref, kp_ref, vp_ref, kn_ref, vn_ref, bias_ref, bfar_ref, o_ref, sel_ref):
    QB = Q_BLOCK
    NB = kp_ref.shape[0]
    NS = NB // 4
    ns_shift = NS.bit_length() - 1
    blk = pl.program_id(2)
    kp, vp = kp_ref[...], vp_ref[...]
    start = pl.multiple_of(blk * (QB // CMP_STRIDE), 8)
    kn = kn_ref[pl.ds(start, CMP_NEAR), :].astype(BF16)
    vn = vn_ref[pl.ds(start, CMP_NEAR), :].astype(BF16)
    first_near = blk * (QB // CMP_STRIDE) - CMP_PAD
    lane = _iota((1, NB), 1)
    c_far = 4 * (lane & (NS - 1)) + (lane >> ns_shift)
    mask_far = c_far < first_near
    i_n = _iota((QB, CMP_NEAR), 0)
    c_n = _iota((QB, CMP_NEAR), 1)
    dist_n = i_n + (CMP_PAD * CMP_STRIDE - CMP_LEN + 1) - CMP_STRIDE * c_n
    mask_near = (dist_n >= 0) & (first_near + c_n >= 0)
    imp_far = jnp.zeros((QB, NB), F32)
    imp_near = jnp.zeros((QB, CMP_NEAR), F32)
    for r in range(REP):
        rows = slice(r * QB, (r + 1) * QB)
        q = q_ref[rows, :] * (HEAD_DIM ** -0.5)
        s_f = jnp.where(mask_far, _dot_nt(q, kp) + bfar_ref[r], NEG_INF)
        s_n = jnp.where(mask_near, _dot_nt(q, kn) + bias_ref[r], NEG_INF)
        m = jnp.maximum(jnp.max(s_f, axis=-1, keepdims=True), jnp.max(s_n, axis=-1, keepdims=True))
        p_f = jnp.where(mask_far, jnp.exp(s_f - m), 0.0)
        p_n = jnp.where(mask_near, jnp.exp(s_n - m), 0.0)
        den = jnp.sum(p_f, axis=-1, keepdims=True) + jnp.sum(p_n, axis=-1, keepdims=True)
        inv = 1.0 / jnp.where(den > 0, den, 1.0)
        p_f = p_f * inv
        p_n = p_n * inv
        o_ref[rows, :] = _dot(p_f.astype(BF16), vp) + _dot(p_n.astype(BF16), vn)
        imp_far = imp_far + p_f
        imp_near = imp_near + p_n
    panel = [imp_far[:, m * NS:(m + 1) * NS] for m in range(4)]
    j = _iota((QB, NS), 1)
    prev3 = jnp.where(j == 0, 0.0, pltpu.roll(panel[3], 1, axis=1))
    imp = prev3 + 2.0 * panel[0] + 2.0 * panel[1] + 2.0 * panel[2] + panel[3]
    c_abs = first_near + _iota((CMP_NEAR, NS), 0)
    off = c_abs + 1 - 4 * _iota((CMP_NEAR, NS), 1)
    w_near = jnp.where((off == 0) | (off == 4), 1.0, jnp.where((off >= 1) & (off <= 3), 2.0, 0.0))
    imp = imp + _dot(imp_near, w_near, HI)
    cur = 2 * blk + (_iota((QB, NS), 0) >= SEL_BLOCK).astype(jnp.int32)
    forced = (j == 0) | (j == cur) | (j == cur - 1)
    score = jnp.where(forced, SEL_FORCE_SCORE, jnp.where(j <= cur, imp, -1.0))
    score = score.T
    jt = _iota((NS, QB), 0)
    sel = jnp.zeros((NS, QB), F32)
    for _ in range(min(SEL_TOPK, NS)):
        m = jnp.max(score, axis=0, keepdims=True)
        idx = jnp.min(jnp.where(score == m, jt, NS), axis=0, keepdims=True)
        hit = jt == idx
        sel = jnp.where(hit, 1.0, sel)
        score = jnp.where(hit, -3.0, score)
    pad = jnp.zeros((SEL_PAD_BLOCKS, QB), BF16)
    sel_ref[...] = jnp.concatenate([pad, sel.astype(BF16), pad], axis=0)


def nsa_compressed(q, kperm, vperm, knear, vnear, bias, bias_far):
    B, G, NQ, QR, N = q.shape
    NB = kperm.shape[2]
    nsp = NB // 4 + 2 * SEL_PAD_BLOCKS
    blk4 = lambda b, g, i: (b, g, 0, 0)
    return pl.pallas_call(
        _nsa_cmp_body,
        grid=(B, G, NQ),
        in_specs=[pl.BlockSpec((None, None, None, QR, N), lambda b, g, i: (b, g, i, 0, 0)),
                  pl.BlockSpec((None, None, NB, N), blk4), pl.BlockSpec((None, None, NB, N), blk4),
                  pl.BlockSpec((None, None, knear.shape[2], N), blk4),
                  pl.BlockSpec((None, None, knear.shape[2], N), blk4),
                  pl.BlockSpec((None, REP, Q_BLOCK, CMP_NEAR), lambda b, g, i: (g, 0, 0, 0)),
                  pl.BlockSpec((None, REP, 1, NB), lambda b, g, i: (g, 0, 0, 0))],
        out_specs=[pl.BlockSpec((None, None, None, QR, N), lambda b, g, i: (b, g, i, 0, 0)),
                   pl.BlockSpec((None, None, None, nsp, Q_BLOCK), lambda b, g, i: (b, g, i, 0, 0))],
        out_shape=[jax.ShapeDtypeStruct((B, G, NQ, QR, N), F32),
                   jax.ShapeDtypeStruct((B, G, NQ, nsp, Q_BLOCK), BF16)],
        compiler_params=_cparams(("parallel", "parallel", "parallel")),
        name="nsa_compressed",
    )(q, kperm, vperm, knear, vnear, bias, bias_far)


def _nsa_sel_body(qt_ref, selt_ref, k_ref, vt_ref, biast_ref, bfar_ref, o_ref):
    QB = Q_BLOCK
    NQ = REP * QB
    blk = pl.program_id(2)
    qt = qt_ref[...] * (HEAD_DIM ** -0.5)

    near_chunk = blk + (SEL_PAD - SEL_SPAN) // V_CHUNK
    row0 = pl.multiple_of(near_chunk * V_CHUNK, V_CHUNK)
    kn = k_ref[pl.ds(row0, SEL_NEAR), :]
    q_near = jnp.concatenate([qt, jnp.zeros((AUG - HEAD_DIM, NQ), BF16)], axis=0)
    s = _dot(kn, q_near) + biast_ref[...]
    first = near_chunk * (V_CHUNK // SEL_BLOCK)
    base = pl.multiple_of((first // SEL_TILE_BLOCKS) * SEL_TILE_BLOCKS, SEL_TILE_BLOCKS)
    rows = selt_ref[pl.ds(base, 2 * SEL_TILE_BLOCKS), :]
    rows = jnp.concatenate([rows] * REP, axis=1)
    kb = _iota((SEL_NEAR, 2 * SEL_TILE_BLOCKS), 0) >> 6
    uu = _iota((SEL_NEAR, 2 * SEL_TILE_BLOCKS), 1)
    onehot = (uu == kb + (first - base)).astype(BF16)
    chosen = _dot(onehot, rows) > 0.5
    c_k = _iota((SEL_NEAR, NQ), 0)
    i_q = _iota((SEL_NEAR, NQ), 1) & (QB - 1)
    s = jnp.where(chosen & (c_k <= i_q + SEL_SPAN), s, NEG_INF)
    m = jnp.max(s, axis=0, keepdims=True)
    p = jnp.exp(s - m)
    l = jnp.sum(p, axis=0, keepdims=True)
    p = p.astype(BF16)
    acc = jnp.zeros((HEAD_DIM, NQ), F32)
    for j in range(SEL_NEAR // V_CHUNK):
        acc = acc + _dot(vt_ref[near_chunk + j], p[j * V_CHUNK:(j + 1) * V_CHUNK])

    pb_u = _iota((SEL_TILE_BLOCKS, QB), 0)
    zeros = jnp.zeros((AUG - HEAD_DIM - 2 * SEL_TILE_BLOCKS, NQ), BF16)
    bfar = bfar_ref[...]

    def far_tile(kt, carry):
        m, l, acc = carry
        b0 = pl.multiple_of(kt * SEL_TILE_BLOCKS, SEL_TILE_BLOCKS)
        picked = selt_ref[pl.ds(b0, SEL_TILE_BLOCKS), :].astype(F32) > 0.5
        neg = jnp.where(picked & (b0 + pb_u < first), 0.0, NEG_INF).astype(BF16)
        q_far = jnp.concatenate([qt, jnp.concatenate([neg] * REP, axis=1), bfar, zeros], axis=0)
        r0 = pl.multiple_of(kt * SEL_TILE, SEL_TILE)
        s = _dot(k_ref[pl.ds(r0, SEL_TILE), :], q_far)
        m_new = jnp.maximum(m, jnp.max(s, axis=0, keepdims=True))
        alpha = jnp.exp(m - m_new)
        p = jnp.exp(s - m_new)
        l = alpha * l + jnp.sum(p, axis=0, keepdims=True)
        p = p.astype(BF16)
        acc = alpha * acc
        for j in range(SEL_TILE // V_CHUNK):
            acc = acc + _dot(vt_ref[kt * (SEL_TILE // V_CHUNK) + j], p[j * V_CHUNK:(j + 1) * V_CHUNK])
        return m_new, l, acc

    n_far = (first + SEL_TILE_BLOCKS - 1) // SEL_TILE_BLOCKS
    m, l, acc = lax.fori_loop(0, n_far, far_tile, (m, l, acc))
    o_ref[...] = acc / l


def nsa_selected(qt, selt, k_aug, vt, bias_t, bias_far):
    B, G, NQ, N, W = qt.shape
    nsp = selt.shape[3]
    rows = k_aug.shape[2]
    return pl.pallas_call(
        _nsa_sel_body,
        grid=(B, G, NQ),
        in_specs=[pl.BlockSpec((None, None, None, N, W), lambda b, g, i: (b, g, i, 0, 0)),
                  pl.BlockSpec((None, None, None, nsp, Q_BLOCK), lambda b, g, i: (b, g, i, 0, 0)),
                  pl.BlockSpec((None, None, rows, AUG), lambda b, g, i: (b, g, 0, 0)),
                  pl.BlockSpec((None, None, rows // V_CHUNK, N, V_CHUNK), lambda b, g, i: (b, g, 0, 0, 0)),
                  pl.BlockSpec((None, SEL_NEAR, W), lambda b, g, i: (g, 0, 0)),
                  pl.BlockSpec((None, SEL_TILE_BLOCKS, W), lambda b, g, i: (g, 0, 0))],
        out_specs=pl.BlockSpec((None, None, None, N, W), lambda b, g, i: (b, g, i, 0, 0)),
        out_shape=jax.ShapeDtypeStruct((B, G, NQ, N, W), F32),
        compiler_params=_cparams(("parallel", "parallel", "parallel")),
        name="nsa_selected",
    )(qt, selt, k_aug, vt, bias_t, bias_far)


def _band_body(*refs, window, pad, has_sink):
    if has_sink:
        q_ref, k_ref, v_ref, bias_ref, sink_ref, o_ref = refs
    else:
        q_ref, k_ref, v_ref, bias_ref, o_ref = refs
    QB = Q_BLOCK
    width = pad + QB
    blk = pl.program_id(2)
    row0 = pl.multiple_of(blk * QB, QB)
    kb = k_ref[pl.ds(row0, width), :]
    vb = v_ref[pl.ds(row0, width), :]
    i_q = _iota((QB, width), 0)
    c_k = _iota((QB, width), 1)
    dist = i_q + pad - c_k
    mask = (dist >= 0) & (dist < window) & (row0 + c_k >= pad)
    for r in range(REP):
        rows = slice(r * QB, (r + 1) * QB)
        q = q_ref[rows, :] * (HEAD_DIM ** -0.5)
        s = jnp.where(mask, _dot_nt(q, kb) + bias_ref[r], NEG_INF)
        m = jnp.max(s, axis=-1, keepdims=True)
        if has_sink:
            sink = sink_ref[r][:, 0:1]
            m = jnp.maximum(m, sink)
        p = jnp.where(mask, jnp.exp(s - m), 0.0)
        den = jnp.sum(p, axis=-1, keepdims=True)
        if has_sink:
            den = den + jnp.exp(sink - m)
        p = p * (1.0 / jnp.where(den > 0, den, 1.0))
        o_ref[rows, :] = _dot(p.astype(BF16), vb)


def band_attention(q, k, v, bias, sink, window, pad):
    B, G, NQ, QR, N = q.shape
    rows = k.shape[2]
    width = pad + Q_BLOCK
    kv = pl.BlockSpec((None, None, rows, N), lambda b, g, i: (b, g, 0, 0))
    in_specs = [pl.BlockSpec((None, None, None, QR, N), lambda b, g, i: (b, g, i, 0, 0)), kv, kv,
                pl.BlockSpec((None, REP, Q_BLOCK, width), lambda b, g, i: (g, 0, 0, 0))]
    args = [q, k, v, bias]
    if sink is not None:
        in_specs.append(pl.BlockSpec((None, REP, 1, LANE), lambda b, g, i: (g, 0, 0, 0)))
        args.append(sink)
    return pl.pallas_call(
        functools.partial(_band_body, window=window, pad=pad, has_sink=sink is not None),
        grid=(B, G, NQ),
        in_specs=in_specs,
        out_specs=pl.BlockSpec((None, None, None, QR, N), lambda b, g, i: (b, g, i, 0, 0)),
        out_shape=jax.ShapeDtypeStruct((B, G, NQ, QR, N), F32),
        compiler_params=_cparams(("parallel", "parallel", "parallel")),
        name="band_attention",
    )(*args)


GATE_COLS = 3 * D_MODEL + LANE


def _merge_body(x_ref, pg_ref, orw_ref, oc_ref, os_ref, ow_ref, osw_ref, wb_ref, wo_ref, gn_ref, ex_ref, o_ref):
    pg = pg_ref[...]
    head_gates = jax.nn.sigmoid(pg[:, 3 * D_MODEL:])
    ge = _dot(head_gates, ex_ref[...], HI)
    o_nsa = (ge[:, 0:WIDTH] * oc_ref[...] + ge[:, WIDTH:2 * WIDTH] * os_ref[...]
             + ge[:, 2 * WIDTH:3 * WIDTH] * ow_ref[...])
    merged = (jax.nn.sigmoid(pg[:, 0:D_MODEL]) * _dot(orw_ref[...].astype(BF16), wb_ref[0])
              + jax.nn.sigmoid(pg[:, D_MODEL:2 * D_MODEL]) * _dot(o_nsa.astype(BF16), wb_ref[1])
              + jax.nn.sigmoid(pg[:, 2 * D_MODEL:3 * D_MODEL]) * _dot(osw_ref[...].astype(BF16), wb_ref[2]))
    y = _dot(merged.astype(BF16), wo_ref[...])
    o_ref[...] = x_ref[...] + _rms(y, gn_ref[...])


def merge_out(x, pg, o_rwkv, o_c, o_s, o_w, o_swa, w_branch, w_out, g_post, expand, tm):
    M, D = x.shape
    row = lambda i: (i, 0)
    wide = pl.BlockSpec((tm, WIDTH), row)
    return pl.pallas_call(
        _merge_body,
        grid=(M // tm,),
        in_specs=[pl.BlockSpec((tm, D), row), pl.BlockSpec((tm, GATE_COLS), row),
                  wide, wide, wide, wide, wide,
                  pl.BlockSpec((3, WIDTH, D), lambda i: (0, 0, 0)),
                  pl.BlockSpec((D, D), lambda i: (0, 0)),
                  pl.BlockSpec((1, D), lambda i: (0, 0)),
                  pl.BlockSpec((LANE, 3 * WIDTH), lambda i: (0, 0))],
        out_specs=pl.BlockSpec((tm, D), row),
        out_shape=jax.ShapeDtypeStruct((M, D), F32),
        compiler_params=_cparams(("parallel",)),
        name="merge_out",
    )(x, pg, o_rwkv, o_c, o_s, o_w, o_swa, w_branch, w_out, g_post, expand)


def _ffn_body(x_ref, gpre_ref, wg_ref, wu_ref, wd_ref, gpost_ref, o_ref, h_ref, acc_ref):
    j = pl.program_id(1)

    @pl.when(j == 0)
    def _():
        h_ref[...] = _rms(x_ref[...], gpre_ref[...]).astype(BF16)
        acc_ref[...] = jnp.zeros_like(acc_ref)

    h = h_ref[...]
    gate = _dot(h, wg_ref[...])
    act = gate * jax.nn.sigmoid(gate) * _dot(h, wu_ref[...])
    acc_ref[...] += _dot(act.astype(BF16), wd_ref[...])

    @pl.when(j == pl.num_programs(1) - 1)
    def _():
        o_ref[...] = x_ref[...] + _rms(acc_ref[...], gpost_ref[...])


def ffn(x, g_pre, w_gate, w_up, w_down, g_post, tm, th):
    M, D = x.shape
    Hd = w_gate.shape[1]
    return pl.pallas_call(
        _ffn_body,
        grid=(M // tm, Hd // th),
        in_specs=[pl.BlockSpec((tm, D), lambda i, j: (i, 0)),
                  pl.BlockSpec((1, D), lambda i, j: (0, 0)),
                  pl.BlockSpec((D, th), lambda i, j: (0, j)),
                  pl.BlockSpec((D, th), lambda i, j: (0, j)),
                  pl.BlockSpec((th, D), lambda i, j: (j, 0)),
                  pl.BlockSpec((1, D), lambda i, j: (0, 0))],
        out_specs=pl.BlockSpec((tm, D), lambda i, j: (i, 0)),
        out_shape=jax.ShapeDtypeStruct((M, D), F32),
        scratch_shapes=[pltpu.VMEM((tm, D), BF16), pltpu.VMEM((tm, D), F32)],
        compiler_params=_cparams(("parallel", "arbitrary")),
        name="ffn",
    )(x, g_pre, w_gate, w_up, w_down, g_post)


def _pad_cols(w, n):
    return jnp.pad(w, ((0, 0), (0, n - w.shape[1])))


def _pad_rows(w, n):
    return jnp.pad(w, ((0, n - w.shape[0]), (0, 0)))


def _q_blocks(q, B, T):
    nq = T // Q_BLOCK
    q = q.reshape(B, nq, Q_BLOCK, KV_GROUPS, REP, HEAD_DIM).transpose(0, 3, 1, 4, 2, 5)
    return q.reshape(B, KV_GROUPS, nq, REP * Q_BLOCK, HEAD_DIM)


def _q_blocks_t(q, B, T):
    nq = T // Q_BLOCK
    q = q.reshape(B, nq, Q_BLOCK, KV_GROUPS, REP, HEAD_DIM).transpose(0, 3, 1, 5, 4, 2)
    return q.reshape(B, KV_GROUPS, nq, HEAD_DIM, REP * Q_BLOCK)


def _unblock_t(o, B, T):
    nq = T // Q_BLOCK
    o = o.reshape(B, KV_GROUPS, nq, HEAD_DIM, REP, Q_BLOCK).transpose(0, 2, 5, 1, 4, 3)
    return o.reshape(B * T, WIDTH)


def _unblock(o, B, T):
    nq = T // Q_BLOCK
    o = o.reshape(B, KV_GROUPS, nq, REP, Q_BLOCK, HEAD_DIM).transpose(0, 2, 4, 1, 3, 5)
    return o.reshape(B * T, WIDTH)


def _kv_groups(z, B, T, pad):
    z = z.reshape(B, T, KV_GROUPS, HEAD_DIM).transpose(0, 2, 1, 3)
    return jnp.pad(z, ((0, 0), (0, 0), (pad, 0), (0, 0)))


def _group_bias(b):
    return b.reshape(KV_GROUPS, REP, b.shape[1], b.shape[2])


def kernel(x, norm_pre_mix, norm_post_mix, norm_pre_ffn, norm_post_ffn, w_in, rwkv_mu, rwkv_w0, rwkv_w2, rwkv_a0, rwkv_a2, rwkv_g2, rwkv_k_k, rwkv_k_a, rwkv_r_k, rwkv_ln_w, rwkv_ln_b, nsa_pe_k, nsa_pe_v, nsa_ck_w1, nsa_ck_w2, nsa_cv_w1, nsa_cv_w2, swa_sinks, rel_bias, w_branch, w_out, ffn_w_gate, ffn_w_up, ffn_w_down):
    B, T, D = x.shape
    depth = w_in.shape[0]
    M = B * T
    NB = T // CMP_STRIDE
    NS = T // SEL_BLOCK
    ffn_hidden = ffn_w_gate.shape[2]

    rw_cols = 3 * WIDTH + W_LORA + A_LORA + G_LORA
    nsa_cols = WIDTH + 6 * KV_WIDTH + 3 * N_HEADS
    swa_cols = WIDTH + 2 * KV_WIDTH
    c_nsa = rw_cols
    c_swa = c_nsa + nsa_cols
    c_gate = c_swa + swa_cols
    c_zw = 3 * WIDTH

    ii = np.arange(Q_BLOCK)[:, None]
    bias_swa = bias_tile(rel_bias, SWA_WINDOW + ii - np.arange(SWA_WINDOW + Q_BLOCK)[None, :])
    bias_win = bias_tile(rel_bias, NSA_WINDOW + ii - np.arange(NSA_WINDOW + Q_BLOCK)[None, :])
    bias_sel = bias_tile(rel_bias, SEL_SPAN + ii - np.arange(SEL_NEAR)[None, :])
    bias_cmp = bias_tile(rel_bias, ii + (CMP_PAD * CMP_STRIDE - CMP_LEN + 1)
                         - CMP_STRIDE * np.arange(CMP_NEAR)[None, :])
    bias_swa = _group_bias(bias_swa[N_HEADS:])
    bias_win = _group_bias(bias_win[:N_HEADS])
    bias_sel = _group_bias(bias_sel[:N_HEADS])
    bias_cmp = _group_bias(bias_cmp[:N_HEADS])
    far = rel_bias[REL_BUCKETS - 1, :N_HEADS].reshape(KV_GROUPS, REP, 1, 1)
    bias_sel_t = bias_sel.transpose(0, 3, 1, 2).reshape(KV_GROUPS, SEL_NEAR, REP * Q_BLOCK)
    far_hi = far.astype(BF16)
    far_lo = (far - far_hi.astype(F32)).astype(BF16)
    far_rows = jnp.concatenate([far_hi, far_lo], axis=2)
    far_rows = jnp.broadcast_to(far_rows, (KV_GROUPS, REP, 2, Q_BLOCK)).transpose(0, 2, 1, 3)
    bfar_sel = jnp.pad(far_rows.reshape(KV_GROUPS, 2, REP * Q_BLOCK), ((0, 0), (0, SEL_TILE_BLOCKS - 2), (0, 0)))
    rows_pad = SEL_PAD + T
    blk_in_tile = (np.arange(rows_pad) // SEL_BLOCK) % SEL_TILE_BLOCKS
    ka = np.zeros((rows_pad, AUG - HEAD_DIM), np.float32)
    ka[np.arange(rows_pad), blk_in_tile] = 1.0
    ka[:, SEL_TILE_BLOCKS:SEL_TILE_BLOCKS + 2] = 1.0
    key_aug = jnp.asarray(ka, BF16)
    bfar_cmp = jnp.broadcast_to(far, (KV_GROUPS, REP, 1, NB))
    ex = np.zeros((LANE, 3 * WIDTH), np.float32)
    for c in range(3 * N_HEADS):
        ex[c, c * HEAD_DIM:(c + 1) * HEAD_DIM] = 1.0
    expand = jnp.asarray(ex)

    xf = x.reshape(M, D)
    for l in range(depth):
        w = w_in[l]
        w_rw = jnp.concatenate([w[:, :c_zw], _pad_cols(w[:, c_zw:c_zw + W_LORA], LANE),
                                _pad_cols(w[:, c_zw + W_LORA:c_zw + W_LORA + A_LORA], LANE),
                                w[:, c_zw + W_LORA + A_LORA:rw_cols]], axis=1).astype(BF16)
        w_nsa = w[:, c_nsa:c_nsa + WIDTH + 6 * KV_WIDTH].astype(BF16)
        w_swa = w[:, c_swa:c_gate].astype(BF16)
        w_gate = jnp.concatenate([w[:, c_gate:], _pad_cols(w[:, c_swa - 3 * N_HEADS:c_swa], LANE)],
                                 axis=1).astype(BF16)
        g_pre = norm_pre_mix[l][None, :]
        p_rw = norm_matmul(xf, g_pre, w_rw, F32, 1024, 640)
        p_nsa = norm_matmul(xf, g_pre, w_nsa, BF16, 1024, 640)
        p_swa = norm_matmul(xf, g_pre, w_swa, BF16, 1024, 768)
        p_gate = norm_matmul(xf, g_pre, w_gate, F32, 1024, 640)

        mu = rwkv_mu[l]
        mu_p = jnp.concatenate([mu[:c_zw], jnp.pad(mu[c_zw:c_zw + W_LORA], (0, LANE - W_LORA)),
                                jnp.pad(mu[c_zw + W_LORA:c_zw + W_LORA + A_LORA], (0, LANE - A_LORA)),
                                mu[c_zw + W_LORA + A_LORA:]])[None, :]
        r, k, v, a, lw, g = rwkv_prep(p_rw.reshape(B, T, RW_COLS), mu_p, rwkv_w0[l][None, :],
                                      _pad_rows(rwkv_w2[l], LANE), rwkv_a0[l][None, :],
                                      _pad_rows(rwkv_a2[l], LANE), rwkv_g2[l], 512)
        row = lambda z: z.reshape(1, WIDTH)
        o_rwkv = rwkv_recurrence(r, k, v, a, lw, g, row(rwkv_k_k[l]), row(rwkv_k_a[l]), row(rwkv_r_k[l]),
                                 row(rwkv_ln_w[l]), row(rwkv_ln_b[l]), 512).reshape(M, WIDTH)

        pn = p_nsa.reshape(B, T, WIDTH + 6 * KV_WIDTH)
        q = _q_blocks(pn[..., :WIDTH], B, T)
        seg = lambda n: pn[..., WIDTH + n * KV_WIDTH:WIDTH + (n + 1) * KV_WIDTH]

        def strides(z):
            z = z.reshape(B, NB, CMP_STRIDE, KV_GROUPS, HEAD_DIM).transpose(0, 3, 1, 2, 4)
            return z.reshape(B, KV_GROUPS, NB, CMP_STRIDE * HEAD_DIM)

        def pe_rows(pe):
            return jnp.broadcast_to(pe.reshape(1, CMP_LEN * HEAD_DIM), (8, CMP_LEN * HEAD_DIM)).astype(BF16)

        k_cmp = nsa_compress(strides(seg(0)), pe_rows(nsa_pe_k[l]), nsa_ck_w1[l].astype(BF16),
                             nsa_ck_w2[l].astype(BF16))
        v_cmp = nsa_compress(strides(seg(1)), pe_rows(nsa_pe_v[l]), nsa_cv_w1[l].astype(BF16),
                             nsa_cv_w2[l].astype(BF16))

        def panels(z):
            z = z.reshape(B, KV_GROUPS, NS, 4, HEAD_DIM).transpose(0, 1, 3, 2, 4)
            return z.reshape(B, KV_GROUPS, NB, HEAD_DIM).astype(BF16)

        near = lambda z: jnp.pad(z, ((0, 0), (0, 0), (CMP_PAD, 8), (0, 0)))
        o_c, sel_t = nsa_compressed(q, panels(k_cmp), panels(v_cmp), near(k_cmp), near(v_cmp), bias_cmp, bfar_cmp)
        k_aug = jnp.concatenate([_kv_groups(seg(2), B, T, SEL_PAD),
                                 jnp.broadcast_to(key_aug, (B, KV_GROUPS) + key_aug.shape)], axis=-1)
        v_t = _kv_groups(seg(3), B, T, SEL_PAD).reshape(B, KV_GROUPS, -1, V_CHUNK, HEAD_DIM).swapaxes(-1, -2)
        o_s = nsa_selected(_q_blocks_t(pn[..., :WIDTH], B, T), sel_t, k_aug, v_t, bias_sel_t, bfar_sel)
        o_w = band_attention(q, _kv_groups(seg(4), B, T, NSA_WINDOW), _kv_groups(seg(5), B, T, NSA_WINDOW),
                             bias_win, None, NSA_WINDOW, NSA_WINDOW)

        ps = p_swa.reshape(B, T, swa_cols)
        sink = jnp.broadcast_to(swa_sinks[l].reshape(KV_GROUPS, REP, 1, 1), (KV_GROUPS, REP, 1, LANE))
        o_swa = band_attention(_q_blocks(ps[..., :WIDTH], B, T),
                               _kv_groups(ps[..., WIDTH:WIDTH + KV_WIDTH], B, T, SWA_WINDOW),
                               _kv_groups(ps[..., WIDTH + KV_WIDTH:], B, T, SWA_WINDOW),
                               bias_swa, sink, SWA_WINDOW, SWA_WINDOW)

        xf = merge_out(xf, p_gate, o_rwkv, _unblock(o_c, B, T), _unblock_t(o_s, B, T), _unblock(o_w, B, T),
                       _unblock(o_swa, B, T), w_branch[l].astype(BF16), w_out[l].astype(BF16),
                       norm_post_mix[l][None, :], expand, 256)

        xf = ffn(xf, norm_pre_ffn[l][None, :], ffn_w_gate[l].astype(BF16), ffn_w_up[l].astype(BF16),
                 ffn_w_down[l].astype(BF16), norm_post_ffn[l][None, :], 512, ffn_hidden // 2)
    return xf.reshape(B, T, D)
```

```python
import functools
import math

import numpy as np
import jax
import jax.numpy as jnp
from jax import lax
from jax.experimental import pallas as pl
from jax.experimental.pallas import tpu as pltpu

F32 = jnp.float32
BF16 = jnp.bfloat16
HI = lax.Precision.HIGHEST

D_MODEL = 1024
HEAD_DIM = 64
N_HEADS = 8
KV_GROUPS = 2
REP = N_HEADS // KV_GROUPS
WIDTH = N_HEADS * HEAD_DIM
KV_WIDTH = KV_GROUPS * HEAD_DIM
W_LORA, A_LORA, G_LORA = 64, 64, 128
RWKV_LN_EPS = 64e-5
CMP_STRIDE = 16
CMP_LEN = 32
CMP_HIDDEN = 256
SEL_BLOCK = 64
SEL_TOPK = 16
SEL_FORCE_SCORE = 1e4
NSA_WINDOW = 512
Q_BLOCK = 128
SWA_WINDOW = 128
REL_BUCKETS = 32
REL_MAX_DIST = 1024
NORM_EPS = 1e-6
NEG_INF = -1e30
LOG2E = math.log2(math.e)
LANE = 128
RWKV_CHUNK = 64
SEL_NEAR = 1024
SEL_SPAN = SEL_NEAR - Q_BLOCK
SEL_PAD = SEL_NEAR
SEL_TILE = 1024
SEL_TILE_BLOCKS = SEL_TILE // SEL_BLOCK
SEL_PAD_BLOCKS = SEL_PAD // SEL_BLOCK
AUG = 2 * HEAD_DIM
V_CHUNK = 128
CMP_NEAR = 64
CMP_PAD = CMP_NEAR - Q_BLOCK // CMP_STRIDE
VMEM_LIMIT = 56 * 1024 * 1024


def _cparams(sem):
    return pltpu.CompilerParams(dimension_semantics=sem, vmem_limit_bytes=VMEM_LIMIT)


def _dot(a, b, precision=None):
    return jnp.dot(a, b, preferred_element_type=F32, precision=precision)


def _dot_nt(a, b, precision=None):
    return lax.dot_general(a, b, (((1,), (1,)), ((), ())), preferred_element_type=F32, precision=precision)


def _dot_tn(a, b, precision=None):
    return lax.dot_general(a, b, (((0,), (0,)), ((), ())), preferred_element_type=F32, precision=precision)


def _bdot(a, b):
    return _dot(a.astype(BF16), b.astype(BF16))


def _bdot_nt(a, b):
    return _dot_nt(a.astype(BF16), b.astype(BF16))


def _bdot_tn(a, b):
    return _dot_tn(a.astype(BF16), b.astype(BF16))


def _rms(x, g):
    return x * lax.rsqrt(jnp.mean(x * x, axis=-1, keepdims=True) + NORM_EPS) * g


def _iota(shape, dim):
    return lax.broadcasted_iota(jnp.int32, shape, dim)


def _norm_mm_body(x_ref, g_ref, w_ref, o_ref, h_ref):
    @pl.when(pl.program_id(1) == 0)
    def _():
        h_ref[...] = _rms(x_ref[...], g_ref[...]).astype(BF16)

    o_ref[...] = _dot(h_ref[...], w_ref[...]).astype(o_ref.dtype)


def norm_matmul(x, g, w, out_dtype, tm, tn):
    M, D = x.shape
    N = w.shape[1]
    return pl.pallas_call(
        _norm_mm_body,
        grid=(M // tm, N // tn),
        in_specs=[pl.BlockSpec((tm, D), lambda i, j: (i, 0)),
                  pl.BlockSpec((1, D), lambda i, j: (0, 0)),
                  pl.BlockSpec((D, tn), lambda i, j: (0, j))],
        out_specs=pl.BlockSpec((tm, tn), lambda i, j: (i, j)),
        out_shape=jax.ShapeDtypeStruct((M, N), out_dtype),
        scratch_shapes=[pltpu.VMEM((tm, D), BF16)],
        compiler_params=_cparams(("parallel", "arbitrary")),
        name="norm_matmul",
    )(x, g, w)


RW_COLS = 3 * WIDTH + 3 * LANE


def _rwkv_prep_body(p_ref, mu_ref, w0_ref, w2_ref, a0_ref, a2_ref, g2_ref,
                    r_o, k_o, v_o, a_o, lw_o, g_o, carry_ref):
    tm = p_ref.shape[0]

    @pl.when(pl.program_id(1) == 0)
    def _():
        carry_ref[...] = jnp.zeros_like(carry_ref)

    p = p_ref[...]
    row = _iota(p.shape, 0)
    prev = jnp.where(row == 0, carry_ref[0:1, :], pltpu.roll(p, 1, axis=0))
    carry_ref[0:1, :] = p[tm - 1:tm, :]
    ps = p + (prev - p) * mu_ref[...]
    r_o[...] = ps[:, 0:WIDTH]
    k_o[...] = ps[:, WIDTH:2 * WIDTH]
    v_o[...] = ps[:, 2 * WIDTH:3 * WIDTH]
    zw = ps[:, 3 * WIDTH:3 * WIDTH + LANE]
    za = ps[:, 3 * WIDTH + LANE:3 * WIDTH + 2 * LANE]
    zg = ps[:, 3 * WIDTH + 2 * LANE:3 * WIDTH + 3 * LANE]
    z = -(w0_ref[...] + _dot(jnp.tanh(zw), w2_ref[...], HI))
    softplus = jnp.maximum(z, 0.0) + jnp.log(1.0 + jnp.exp(-jnp.abs(z)))
    lw_o[...] = -jnp.exp(-softplus - 0.5)
    a_o[...] = jax.nn.sigmoid(a0_ref[...] + _dot(za, a2_ref[...], HI))
    g_o[...] = _dot(jax.nn.sigmoid(zg), g2_ref[...], HI)


def rwkv_prep(p, mu, w0, w2, a0, a2, g2, tm):
    B, T, _ = p.shape
    row = lambda b, i: (b, i, 0)
    fixed = lambda b, i: (0, 0)
    out = jax.ShapeDtypeStruct((B, T, WIDTH), F32)
    return pl.pallas_call(
        _rwkv_prep_body,
        grid=(B, T // tm),
        in_specs=[pl.BlockSpec((None, tm, RW_COLS), row),
                  pl.BlockSpec((1, RW_COLS), fixed),
                  pl.BlockSpec((1, WIDTH), fixed), pl.BlockSpec((LANE, WIDTH), fixed),
                  pl.BlockSpec((1, WIDTH), fixed), pl.BlockSpec((LANE, WIDTH), fixed),
                  pl.BlockSpec((LANE, WIDTH), fixed)],
        out_specs=[pl.BlockSpec((None, tm, WIDTH), row)] * 6,
        out_shape=[out] * 6,
        scratch_shapes=[pltpu.VMEM((8, RW_COLS), F32)],
        compiler_params=_cparams(("parallel", "arbitrary")),
        name="rwkv_prep",
    )(p, mu, w0, w2, a0, a2, g2)


RWKV_HEADS_PER_STEP = 4


def _bmm(a, b, precision=None):
    return lax.dot_general(a, b, (((2,), (1,)), ((0,), (0,))), preferred_element_type=F32, precision=precision)


def _bbmm(a, b):
    return _bmm(a.astype(BF16), b.astype(BF16))


def _bbmm_nt(a, b):
    return lax.dot_general(a.astype(BF16), b.astype(BF16), (((2,), (2,)), ((0,), (0,))),
                           preferred_element_type=F32)


def _bbmm_tn(a, b):
    return lax.dot_general(a.astype(BF16), b.astype(BF16), (((1,), (1,)), ((0,), (0,))),
                           preferred_element_type=F32)


def _rwkv_rec_body(r_ref, k_ref, v_ref, a_ref, lw_ref, g_ref, kk_ref, ka_ref, rk_ref, lnw_ref, lnb_ref,
                   o_ref, s_ref):
    C = RWKV_CHUNK
    N = HEAD_DIM
    assert C == N
    tc = r_ref.shape[0]
    nc = tc // C
    hb = r_ref.shape[1] // N

    @pl.when(pl.program_id(2) == 0)
    def _():
        s_ref[...] = jnp.zeros_like(s_ref)

    def chunks(ref):
        x = ref[...]
        return jnp.concatenate([x[:, h * N:(h + 1) * N].reshape(nc, C, N) for h in range(hb)], axis=0)

    def per_head(ref):
        x = ref[...]
        return jnp.concatenate([jnp.broadcast_to(x[:, h * N:(h + 1) * N][None], (nc, 1, N))
                                for h in range(hb)], axis=0)

    row = _iota((C, C), 0)
    col = _iota((C, C), 1)
    incl = (row >= col)[None]
    strict = (row > col)[None]
    eye = (row == col)[None]
    nb = hb * nc
    r, k, v, a, lw = chunks(r_ref), chunks(k_ref), chunks(v_ref), chunks(a_ref), chunks(lw_ref)
    kk = k * per_head(kk_ref)
    kk = kk / jnp.maximum(jnp.sqrt(jnp.sum(kk * kk, axis=-1, keepdims=True)), 1e-12)
    k2 = k * (1.0 + (a - 1.0) * per_head(ka_ref))
    tri = jnp.broadcast_to(incl.astype(F32), (nb, C, C))
    cum = _bmm(tri, lw, HI)
    p_incl = jnp.exp(cum)
    p_inv = jnp.exp(-cum)
    a_t = -kk * jnp.exp(cum - lw)
    r_t = r * p_incl
    b_t = kk * a * p_inv
    k_t = k2 * p_inv
    l_ab = jnp.where(strict, _bbmm_nt(a_t, b_t), 0.0)
    l_ak = jnp.where(strict, _bbmm_nt(a_t, k_t), 0.0)
    m_rb = jnp.where(incl, _bbmm_nt(r_t, b_t), 0.0)
    m_rk = jnp.where(incl, _bbmm_nt(r_t, k_t), 0.0)
    inv = jnp.where(eye, 1.0, l_ab)
    lp = l_ab
    for _ in range(5):
        lp = _bbmm(lp, lp)
        inv = inv + _bbmm(lp, inv)
    t_a = _bbmm(inv, a_t)
    t_v = _bbmm(inv, _bbmm(l_ak, v))
    p_end = p_incl[:, C - 1:C, :]
    bp = b_t * p_end
    kp = k_t * p_end
    ry = (r_t + _bbmm(m_rb, t_a)).reshape(hb, nc, C, N)
    yc = (_bbmm(m_rb, t_v) + _bbmm(m_rk, v)).reshape(hb, nc, C, N)
    am = (jnp.where(eye, jnp.broadcast_to(p_end, (nb, C, C)), 0.0) + _bbmm_tn(bp, t_a)).reshape(hb, nc, C, C)
    gm = (_bbmm_tn(bp, t_v) + _bbmm_tn(kp, v)).reshape(hb, nc, C, N)
    h = s_ref[...]
    ys = []
    for c in range(nc):
        ys.append(_bbmm(ry[:, c], h) + yc[:, c])
        h = _bbmm(am[:, c], h) + gm[:, c]
    s_ref[...] = h
    y = jnp.stack(ys, axis=1).reshape(nb, C, N)
    mean = jnp.mean(y, axis=-1, keepdims=True)
    var = jnp.mean(jnp.square(y - mean), axis=-1, keepdims=True)
    yn = (y - mean) * lax.rsqrt(var + RWKV_LN_EPS) * per_head(lnw_ref) + per_head(lnb_ref)
    bonus = jnp.sum(r * k2 * per_head(rk_ref), axis=-1, keepdims=True) * v
    out = (yn + bonus) * chunks(g_ref)
    o_ref[...] = jnp.concatenate([out[h * nc:(h + 1) * nc].reshape(tc, N) for h in range(hb)], axis=1)


def rwkv_recurrence(r, k, v, a, lw, g, k_k, k_a, r_k, ln_w, ln_b, tc):
    B, T, W = r.shape
    slab = RWKV_HEADS_PER_STEP * HEAD_DIM
    seq = pl.BlockSpec((None, tc, slab), lambda b, h, i: (b, i, h))
    par = pl.BlockSpec((1, slab), lambda b, h, i: (0, h))
    return pl.pallas_call(
        _rwkv_rec_body,
        grid=(B, W // slab, T // tc),
        in_specs=[seq] * 6 + [par] * 5,
        out_specs=seq,
        out_shape=jax.ShapeDtypeStruct((B, T, W), F32),
        scratch_shapes=[pltpu.VMEM((RWKV_HEADS_PER_STEP, HEAD_DIM, HEAD_DIM), F32)],
        compiler_params=_cparams(("parallel", "parallel", "arbitrary")),
        name="rwkv_recurrence",
    )(r, k, v, a, lw, g, k_k, k_a, r_k, ln_w, ln_b)


def _compress_body(z_ref, pe_ref, w1_ref, w2_ref, o_ref):
    nb = z_ref.shape[0]
    half = z_ref.shape[1]
    z = z_ref[...]
    first = _dot(z, w1_ref[0:half, :])
    second = _dot(z, w1_ref[half:2 * half, :])
    pe_term = _dot(pe_ref[...], w1_ref[...])[0:1, :]
    hidden = first + pltpu.roll(second, nb - 1, axis=0) + pe_term
    out = _dot(jax.nn.gelu(hidden).astype(BF16), w2_ref[...])
    rows = _iota(out.shape, 0)
    o_ref[...] = jnp.where(rows < nb - 1, out, 0.0)


def nsa_compress(z, pe, w1, w2):
    B, G, NB, HALF = z.shape
    return pl.pallas_call(
        _compress_body,
        grid=(B, G),
        in_specs=[pl.BlockSpec((None, None, NB, HALF), lambda b, g: (b, g, 0, 0)),
                  pl.BlockSpec((8, 2 * HALF), lambda b, g: (0, 0)),
                  pl.BlockSpec((2 * HALF, CMP_HIDDEN), lambda b, g: (0, 0)),
                  pl.BlockSpec((CMP_HIDDEN, HEAD_DIM), lambda b, g: (0, 0))],
        out_specs=pl.BlockSpec((None, None, NB, HEAD_DIM), lambda b, g: (b, g, 0, 0)),
        out_shape=jax.ShapeDtypeStruct((B, G, NB, HEAD_DIM), F32),
        compiler_params=_cparams(("parallel", "parallel")),
        name="nsa_compress",
    )(z, pe, w1, w2)


def _t5_bucket_np(dist):
    n = np.maximum(dist, 0)
    max_exact = REL_BUCKETS // 2
    nf = np.maximum(n, 1).astype(np.float64)
    large = max_exact + (np.log(nf / max_exact) / math.log(REL_MAX_DIST / max_exact)
                         * (REL_BUCKETS - max_exact)).astype(np.int32)
    large = np.minimum(large, REL_BUCKETS - 1)
    return np.where(n < max_exact, n, large).astype(np.int32)


def _bias_body(tbl_ref, bk_ref, o_ref):
    h = pl.program_id(0)
    bk = bk_ref[...]
    acc = jnp.zeros(bk.shape, F32)
    for b in range(REL_BUCKETS):
        acc = jnp.where(bk == b, tbl_ref[b, h], acc)
    o_ref[...] = acc


def bias_tile(table, dist_np):
    P, Q = dist_np.shape
    H = table.shape[1]
    buckets = jnp.asarray(_t5_bucket_np(dist_np))
    return pl.pallas_call(
        _bias_body,
        grid=(H,),
        in_specs=[pl.BlockSpec(memory_space=pltpu.SMEM),
                  pl.BlockSpec((P, Q), lambda h: (0, 0))],
        out_specs=pl.BlockSpec((None, P, Q), lambda h: (h, 0, 0)),
        out_shape=jax.ShapeDtypeStruct((H, P, Q), F32),
        compiler_params=_cparams(("arbitrary",)),
        name="bias_tile",
    )(table, buckets)


def _nsa_cmp_body(q_ref, kp_ref, vp_ref, kn_ref, vn_ref, bias_ref, bfar_ref, o_ref, sel_ref):
    QB = Q_BLOCK
    NB = kp_ref.shape[0]
    NS = NB // 4
    ns_shift = NS.bit_length() - 1
    blk = pl.program_id(2)
    kp, vp = kp_ref[...], vp_ref[...]
    start = pl.multiple_of(blk * (QB // CMP_STRIDE), 8)
    kn = kn_ref[pl.ds(start, CMP_NEAR), :].astype(BF16)
    vn = vn_ref[pl.ds(start, CMP_NEAR), :].astype(BF16)
    first_near = blk * (QB // CMP_STRIDE) - CMP_PAD
    lane = _iota((1, NB), 1)
    c_far = 4 * (lane & (NS - 1)) + (lane >> ns_shift)
    mask_far = c_far < first_near
    i_n = _iota((QB, CMP_NEAR), 0)
    c_n = _iota((QB, CMP_NEAR), 1)
    dist_n = i_n + (CMP_PAD * CMP_STRIDE - CMP_LEN + 1) - CMP_STRIDE * c_n
    mask_near = (dist_n >= 0) & (first_near + c_n >= 0)
    imp_far = jnp.zeros((QB, NB), F32)
    imp_near = jnp.zeros((QB, CMP_NEAR), F32)
    for r in range(REP):
        rows = slice(r * QB, (r + 1) * QB)
        q = q_ref[rows, :] * (HEAD_DIM ** -0.5)
        s_f = jnp.where(mask_far, _dot_nt(q, kp) + bfar_ref[r], NEG_INF)
        s_n = jnp.where(mask_near, _dot_nt(q, kn) + bias_ref[r], NEG_INF)
        m = jnp.maximum(jnp.max(s_f, axis=-1, keepdims=True), jnp.max(s_n, axis=-1, keepdims=True))
        p_f = jnp.where(mask_far, jnp.exp(s_f - m), 0.0)
        p_n = jnp.where(mask_near, jnp.exp(s_n - m), 0.0)
        den = jnp.sum(p_f, axis=-1, keepdims=True) + jnp.sum(p_n, axis=-1, keepdims=True)
        inv = 1.0 / jnp.where(den > 0, den, 1.0)
        p_f = p_f * inv
        p_n = p_n * inv
        o_ref[rows, :] = _dot(p_f.astype(BF16), vp) + _dot(p_n.astype(BF16), vn)
        imp_far = imp_far + p_f
        imp_near = imp_near + p_n
    panel = [imp_far[:, m * NS:(m + 1) * NS] for m in range(4)]
    j = _iota((QB, NS), 1)
    prev3 = jnp.where(j == 0, 0.0, pltpu.roll(panel[3], 1, axis=1))
    imp = prev3 + 2.0 * panel[0] + 2.0 * panel[1] + 2.0 * panel[2] + panel[3]
    c_abs = first_near + _iota((CMP_NEAR, NS), 0)
    off = c_abs + 1 - 4 * _iota((CMP_NEAR, NS), 1)
    w_near = jnp.where((off == 0) | (off == 4), 1.0, jnp.where((off >= 1) & (off <= 3), 2.0, 0.0))
    imp = imp + _dot(imp_near, w_near, HI)
    cur = 2 * blk + (_iota((QB, NS), 0) >= SEL_BLOCK).astype(jnp.int32)
    forced = (j == 0) | (j == cur) | (j == cur - 1)
    score = jnp.where(forced, SEL_FORCE_SCORE, jnp.where(j <= cur, imp, -1.0))
    score = score.T
    jt = _iota((NS, QB), 0)
    sel = jnp.zeros((NS, QB), F32)
    for _ in range(min(SEL_TOPK, NS)):
        m = jnp.max(score, axis=0, keepdims=True)
        idx = jnp.min(jnp.where(score == m, jt, NS), axis=0, keepdims=True)
        hit = jt == idx
        sel = jnp.where(hit, 1.0, sel)
        score = jnp.where(hit, -3.0, score)
    pad = jnp.zeros((SEL_PAD_BLOCKS, QB), BF16)
    sel_ref[...] = jnp.concatenate([pad, sel.astype(BF16), pad], axis=0)


def nsa_compressed(q, kperm, vperm, knear, vnear, bias, bias_far):
    B, G, NQ, QR, N = q.shape
    NB = kperm.shape[2]
    nsp = NB // 4 + 2 * SEL_PAD_BLOCKS
    blk4 = lambda b, g, i: (b, g, 0, 0)
    return pl.pallas_call(
        _nsa_cmp_body,
        grid=(B, G, NQ),
        in_specs=[pl.BlockSpec((None, None, None, QR, N), lambda b, g, i: (b, g, i, 0, 0)),
                  pl.BlockSpec((None, None, NB, N), blk4), pl.BlockSpec((None, None, NB, N), blk4),
                  pl.BlockSpec((None, None, knear.shape[2], N), blk4),
                  pl.BlockSpec((None, None, knear.shape[2], N), blk4),
                  pl.BlockSpec((None, REP, Q_BLOCK, CMP_NEAR), lambda b, g, i: (g, 0, 0, 0)),
                  pl.BlockSpec((None, REP, 1, NB), lambda b, g, i: (g, 0, 0, 0))],
        out_specs=[pl.BlockSpec((None, None, None, QR, N), lambda b, g, i: (b, g, i, 0, 0)),
                   pl.BlockSpec((None, None, None, nsp, Q_BLOCK), lambda b, g, i: (b, g, i, 0, 0))],
        out_shape=[jax.ShapeDtypeStruct((B, G, NQ, QR, N), F32),
                   jax.ShapeDtypeStruct((B, G, NQ, nsp, Q_BLOCK), BF16)],
        compiler_params=_cparams(("parallel", "parallel", "parallel")),
        name="nsa_compressed",
    )(q, kperm, vperm, knear, vnear, bias, bias_far)


def _nsa_sel_body(qt_ref, selt_ref, k_ref, vt_ref, biast_ref, bfar_ref, o_ref, sa_ref, sb_ref):
    QB = Q_BLOCK
    NQ = REP * QB
    blk = pl.program_id(2)
    qt = (qt_ref[...].astype(F32) * (HEAD_DIM ** -0.5 * LOG2E)).astype(BF16)
    near_chunk = blk + (SEL_PAD - SEL_SPAN) // V_CHUNK
    first = near_chunk * (V_CHUNK // SEL_BLOCK)
    pb_u = _iota((SEL_TILE_BLOCKS, QB), 0)
    zeros = jnp.zeros((AUG - HEAD_DIM - 2 * SEL_TILE_BLOCKS, NQ), BF16)
    bfar = bfar_ref[...]

    def far_scores(kt, s_ref):
        b0 = pl.multiple_of(kt * SEL_TILE_BLOCKS, SEL_TILE_BLOCKS)
        picked = selt_ref[pl.ds(b0, SEL_TILE_BLOCKS), :].astype(F32) > 0.5
        neg = jnp.where(picked & (b0 + pb_u < first), 0.0, NEG_INF).astype(BF16)
        q_far = jnp.concatenate([qt, jnp.concatenate([neg] * REP, axis=1), bfar, zeros], axis=0)
        r0 = pl.multiple_of(kt * SEL_TILE, SEL_TILE)
        s_ref[...] = _dot(k_ref[pl.ds(r0, SEL_TILE), :], q_far)

    def far_update(kt, s_ref, carry):
        m, l, acc = carry
        s = s_ref[...]
        m_new = jnp.maximum(m, jnp.max(s, axis=0, keepdims=True))
        alpha = jnp.exp2(m - m_new)
        p = jnp.exp2(s - m_new)
        l = alpha * l + jnp.sum(p, axis=0, keepdims=True)
        p = p.astype(BF16)
        acc = alpha * acc
        for j in range(SEL_TILE // V_CHUNK):
            acc = acc + _dot(vt_ref[kt * (SEL_TILE // V_CHUNK) + j], p[j * V_CHUNK:(j + 1) * V_CHUNK])
        return m_new, l, acc

    far_scores(0, sa_ref)

    row0 = pl.multiple_of(near_chunk * V_CHUNK, V_CHUNK)
    kn = k_ref[pl.ds(row0, SEL_NEAR), :]
    q_near = jnp.concatenate([qt, jnp.zeros((AUG - HEAD_DIM, NQ), BF16)], axis=0)
    s = _dot(kn, q_near) + biast_ref[...]
    base = pl.multiple_of((first // SEL_TILE_BLOCKS) * SEL_TILE_BLOCKS, SEL_TILE_BLOCKS)
    rows = selt_ref[pl.ds(base, 2 * SEL_TILE_BLOCKS), :]
    rows = jnp.concatenate([rows] * REP, axis=1)
    kb = _iota((SEL_NEAR, 2 * SEL_TILE_BLOCKS), 0) >> 6
    uu = _iota((SEL_NEAR, 2 * SEL_TILE_BLOCKS), 1)
    onehot = (uu == kb + (first - base)).astype(BF16)
    chosen = _dot(onehot, rows) > 0.5
    c_k = _iota((SEL_NEAR, NQ), 0)
    i_q = _iota((SEL_NEAR, NQ), 1) & (QB - 1)
    s = jnp.where(chosen & (c_k <= i_q + SEL_SPAN), s, NEG_INF)
    m = jnp.max(s, axis=0, keepdims=True)
    p = jnp.exp2(s - m)
    l = jnp.sum(p, axis=0, keepdims=True)
    p = p.astype(BF16)
    acc = jnp.zeros((HEAD_DIM, NQ), F32)
    for j in range(SEL_NEAR // V_CHUNK):
        acc = acc + _dot(vt_ref[near_chunk + j], p[j * V_CHUNK:(j + 1) * V_CHUNK])

    def far_pair(j, carry):
        far_scores(2 * j + 1, sb_ref)
        carry = far_update(2 * j, sa_ref, carry)
        far_scores(2 * j + 2, sa_ref)
        return far_update(2 * j + 1, sb_ref, carry)

    n_far = (first + SEL_TILE_BLOCKS - 1) // SEL_TILE_BLOCKS
    m, l, acc = lax.fori_loop(0, (n_far + 1) // 2, far_pair, (m, l, acc))
    o_ref[...] = acc / l


def nsa_selected(qt, selt, k_aug, vt, bias_t, bias_far):
    B, G, NQ, N, W = qt.shape
    nsp = selt.shape[3]
    rows = k_aug.shape[2]
    return pl.pallas_call(
        _nsa_sel_body,
        grid=(B, G, NQ),
        in_specs=[pl.BlockSpec((None, None, None, N, W), lambda b, g, i: (b, g, i, 0, 0)),
                  pl.BlockSpec((None, None, None, nsp, Q_BLOCK), lambda b, g, i: (b, g, i, 0, 0)),
                  pl.BlockSpec((None, None, rows, AUG), lambda b, g, i: (b, g, 0, 0)),
                  pl.BlockSpec((None, None, rows // V_CHUNK, N, V_CHUNK), lambda b, g, i: (b, g, 0, 0, 0)),
                  pl.BlockSpec((None, SEL_NEAR, W), lambda b, g, i: (g, 0, 0)),
                  pl.BlockSpec((None, SEL_TILE_BLOCKS, W), lambda b, g, i: (g, 0, 0))],
        out_specs=pl.BlockSpec((None, None, None, N, W), lambda b, g, i: (b, g, i, 0, 0)),
        out_shape=jax.ShapeDtypeStruct((B, G, NQ, N, W), F32),
        scratch_shapes=[pltpu.VMEM((SEL_TILE, W), F32)] * 2,
        compiler_params=_cparams(("parallel", "parallel", "parallel")),
        name="nsa_selected",
    )(qt, selt, k_aug, vt, bias_t, bias_far)


def _band_body(*refs, window, pad, has_sink):
    if has_sink:
        q_ref, k_ref, v_ref, bias_ref, sink_ref, o_ref = refs
    else:
        q_ref, k_ref, v_ref, bias_ref, o_ref = refs
    QB = Q_BLOCK
    NQ = REP * QB
    width = pad + QB
    blk = pl.program_id(2)
    row0 = pl.multiple_of(blk * QB, QB)
    kb = k_ref[pl.ds(row0, width), :]
    vb = v_ref[pl.ds(row0, width), :]
    i_q = _iota((NQ, width), 0) & (QB - 1)
    c_k = _iota((NQ, width), 1)
    dist = i_q + pad - c_k
    mask = (dist >= 0) & (dist < window) & (row0 + c_k >= pad)
    q = q_ref[...] * (HEAD_DIM ** -0.5)
    s = jnp.where(mask, _dot_nt(q, kb) + bias_ref[...].reshape(NQ, width), NEG_INF)
    m = jnp.max(s, axis=-1, keepdims=True)
    if has_sink:
        sink = jnp.concatenate([jnp.broadcast_to(sink_ref[r][:, 0:1], (QB, 1)) for r in range(REP)], axis=0)
        m = jnp.maximum(m, sink)
    p = jnp.exp(s - m)
    den = jnp.sum(p, axis=-1, keepdims=True)
    if has_sink:
        den = den + jnp.exp(sink - m)
    p = p * (1.0 / den)
    o_ref[...] = _dot(p.astype(BF16), vb)


def band_attention(q, k, v, bias, sink, window, pad):
    B, G, NQ, QR, N = q.shape
    rows = k.shape[2]
    width = pad + Q_BLOCK
    kv = pl.BlockSpec((None, None, rows, N), lambda b, g, i: (b, g, 0, 0))
    in_specs = [pl.BlockSpec((None, None, None, QR, N), lambda b, g, i: (b, g, i, 0, 0)), kv, kv,
                pl.BlockSpec((None, REP, Q_BLOCK, width), lambda b, g, i: (g, 0, 0, 0))]
    args = [q, k, v, bias]
    if sink is not None:
        in_specs.append(pl.BlockSpec((None, REP, 1, LANE), lambda b, g, i: (g, 0, 0, 0)))
        args.append(sink)
    return pl.pallas_call(
        functools.partial(_band_body, window=window, pad=pad, has_sink=sink is not None),
        grid=(B, G, NQ),
        in_specs=in_specs,
        out_specs=pl.BlockSpec((None, None, None, QR, N), lambda b, g, i: (b, g, i, 0, 0)),
        out_shape=jax.ShapeDtypeStruct((B, G, NQ, QR, N), F32),
        compiler_params=_cparams(("parallel", "parallel", "parallel")),
        name="band_attention",
    )(*args)


GATE_COLS = 3 * D_MODEL + LANE


def _merge_body(x_ref, pg_ref, orw_ref, oc_ref, os_ref, ow_ref, osw_ref, wb_ref, wo_ref, gn_ref, ex_ref, o_ref):
    pg = pg_ref[...]
    head_gates = jax.nn.sigmoid(pg[:, 3 * D_MODEL:])
    ge = _dot(head_gates, ex_ref[...], HI)
    o_nsa = (ge[:, 0:WIDTH] * oc_ref[...] + ge[:, WIDTH:2 * WIDTH] * os_ref[...]
             + ge[:, 2 * WIDTH:3 * WIDTH] * ow_ref[...])
    merged = (jax.nn.sigmoid(pg[:, 0:D_MODEL]) * _dot(orw_ref[...].astype(BF16), wb_ref[0])
              + jax.nn.sigmoid(pg[:, D_MODEL:2 * D_MODEL]) * _dot(o_nsa.astype(BF16), wb_ref[1])
              + jax.nn.sigmoid(pg[:, 2 * D_MODEL:3 * D_MODEL]) * _dot(osw_ref[...].astype(BF16), wb_ref[2]))
    y = _dot(merged.astype(BF16), wo_ref[...])
    o_ref[...] = x_ref[...] + _rms(y, gn_ref[...])


def merge_out(x, pg, o_rwkv, o_c, o_s, o_w, o_swa, w_branch, w_out, g_post, expand, tm):
    M, D = x.shape
    row = lambda i: (i, 0)
    wide = pl.BlockSpec((tm, WIDTH), row)
    return pl.pallas_call(
        _merge_body,
        grid=(M // tm,),
        in_specs=[pl.BlockSpec((tm, D), row), pl.BlockSpec((tm, GATE_COLS), row),
                  wide, wide, wide, wide, wide,
                  pl.BlockSpec((3, WIDTH, D), lambda i: (0, 0, 0)),
                  pl.BlockSpec((D, D), lambda i: (0, 0)),
                  pl.BlockSpec((1, D), lambda i: (0, 0)),
                  pl.BlockSpec((LANE, 3 * WIDTH), lambda i: (0, 0))],
        out_specs=pl.BlockSpec((tm, D), row),
        out_shape=jax.ShapeDtypeStruct((M, D), F32),
        compiler_params=_cparams(("parallel",)),
        name="merge_out",
    )(x, pg, o_rwkv, o_c, o_s, o_w, o_swa, w_branch, w_out, g_post, expand)


def _ffn_body(x_ref, gpre_ref, wg_ref, wu_ref, wd_ref, gpost_ref, o_ref, h_ref, acc_ref):
    j = pl.program_id(1)

    @pl.when(j == 0)
    def _():
        h_ref[...] = _rms(x_ref[...], gpre_ref[...]).astype(BF16)
        acc_ref[...] = jnp.zeros_like(acc_ref)

    h = h_ref[...]
    gate = _dot(h, wg_ref[...])
    act = gate * jax.nn.sigmoid(gate) * _dot(h, wu_ref[...])
    acc_ref[...] += _dot(act.astype(BF16), wd_ref[...])

    @pl.when(j == pl.num_programs(1) - 1)
    def _():
        o_ref[...] = x_ref[...] + _rms(acc_ref[...], gpost_ref[...])


def ffn(x, g_pre, w_gate, w_up, w_down, g_post, tm, th):
    M, D = x.shape
    Hd = w_gate.shape[1]
    return pl.pallas_call(
        _ffn_body,
        grid=(M // tm, Hd // th),
        in_specs=[pl.BlockSpec((tm, D), lambda i, j: (i, 0)),
                  pl.BlockSpec((1, D), lambda i, j: (0, 0)),
                  pl.BlockSpec((D, th), lambda i, j: (0, j)),
                  pl.BlockSpec((D, th), lambda i, j: (0, j)),
                  pl.BlockSpec((th, D), lambda i, j: (j, 0)),
                  pl.BlockSpec((1, D), lambda i, j: (0, 0))],
        out_specs=pl.BlockSpec((tm, D), lambda i, j: (i, 0)),
        out_shape=jax.ShapeDtypeStruct((M, D), F32),
        scratch_shapes=[pltpu.VMEM((tm, D), BF16), pltpu.VMEM((tm, D), F32)],
        compiler_params=_cparams(("parallel", "arbitrary")),
        name="ffn",
    )(x, g_pre, w_gate, w_up, w_down, g_post)


def _pad_cols(w, n):
    return jnp.pad(w, ((0, 0), (0, n - w.shape[1])))


def _pad_rows(w, n):
    return jnp.pad(w, ((0, n - w.shape[0]), (0, 0)))


def _q_blocks(q, B, T):
    nq = T // Q_BLOCK
    q = q.reshape(B, nq, Q_BLOCK, KV_GROUPS, REP, HEAD_DIM).transpose(0, 3, 1, 4, 2, 5)
    return q.reshape(B, KV_GROUPS, nq, REP * Q_BLOCK, HEAD_DIM)


def _q_blocks_t(q, B, T):
    nq = T // Q_BLOCK
    q = q.reshape(B, nq, Q_BLOCK, KV_GROUPS, REP, HEAD_DIM).transpose(0, 3, 1, 5, 4, 2)
    return q.reshape(B, KV_GROUPS, nq, HEAD_DIM, REP * Q_BLOCK)


def _unblock_t(o, B, T):
    nq = T // Q_BLOCK
    o = o.reshape(B, KV_GROUPS, nq, HEAD_DIM, REP, Q_BLOCK).transpose(0, 2, 5, 1, 4, 3)
    return o.reshape(B * T, WIDTH)


def _unblock(o, B, T):
    nq = T // Q_BLOCK
    o = o.reshape(B, KV_GROUPS, nq, REP, Q_BLOCK, HEAD_DIM).transpose(0, 2, 4, 1, 3, 5)
    return o.reshape(B * T, WIDTH)


def _kv_groups(z, B, T, pad):
    z = z.reshape(B, T, KV_GROUPS, HEAD_DIM).transpose(0, 2, 1, 3)
    return jnp.pad(z, ((0, 0), (0, 0), (pad, 0), (0, 0)))


def _group_bias(b):
    return b.reshape(KV_GROUPS, REP, b.shape[1], b.shape[2])


def kernel(x, norm_pre_mix, norm_post_mix, norm_pre_ffn, norm_post_ffn, w_in, rwkv_mu, rwkv_w0, rwkv_w2, rwkv_a0, rwkv_a2, rwkv_g2, rwkv_k_k, rwkv_k_a, rwkv_r_k, rwkv_ln_w, rwkv_ln_b, nsa_pe_k, nsa_pe_v, nsa_ck_w1, nsa_ck_w2, nsa_cv_w1, nsa_cv_w2, swa_sinks, rel_bias, w_branch, w_out, ffn_w_gate, ffn_w_up, ffn_w_down):
    B, T, D = x.shape
    depth = w_in.shape[0]
    M = B * T
    NB = T // CMP_STRIDE
    NS = T // SEL_BLOCK
    ffn_hidden = ffn_w_gate.shape[2]

    rw_cols = 3 * WIDTH + W_LORA + A_LORA + G_LORA
    nsa_cols = WIDTH + 6 * KV_WIDTH + 3 * N_HEADS
    swa_cols = WIDTH + 2 * KV_WIDTH
    c_nsa = rw_cols
    c_swa = c_nsa + nsa_cols
    c_gate = c_swa + swa_cols
    c_zw = 3 * WIDTH

    ii = np.arange(Q_BLOCK)[:, None]
    bias_swa = bias_tile(rel_bias, SWA_WINDOW + ii - np.arange(SWA_WINDOW + Q_BLOCK)[None, :])
    bias_win = bias_tile(rel_bias, NSA_WINDOW + ii - np.arange(NSA_WINDOW + Q_BLOCK)[None, :])
    bias_sel = bias_tile(rel_bias, SEL_SPAN + ii - np.arange(SEL_NEAR)[None, :])
    bias_cmp = bias_tile(rel_bias, ii + (CMP_PAD * CMP_STRIDE - CMP_LEN + 1)
                         - CMP_STRIDE * np.arange(CMP_NEAR)[None, :])
    bias_swa = _group_bias(bias_swa[N_HEADS:])
    bias_win = _group_bias(bias_win[:N_HEADS])
    bias_sel = _group_bias(bias_sel[:N_HEADS])
    bias_cmp = _group_bias(bias_cmp[:N_HEADS])
    far = rel_bias[REL_BUCKETS - 1, :N_HEADS].reshape(KV_GROUPS, REP, 1, 1)
    bias_sel_t = bias_sel.transpose(0, 3, 1, 2).reshape(KV_GROUPS, SEL_NEAR, REP * Q_BLOCK) * LOG2E
    far2 = far * LOG2E
    far_hi = far2.astype(BF16)
    far_lo = (far2 - far_hi.astype(F32)).astype(BF16)
    far_rows = jnp.concatenate([far_hi, far_lo], axis=2)
    far_rows = jnp.broadcast_to(far_rows, (KV_GROUPS, REP, 2, Q_BLOCK)).transpose(0, 2, 1, 3)
    bfar_sel = jnp.pad(far_rows.reshape(KV_GROUPS, 2, REP * Q_BLOCK), ((0, 0), (0, SEL_TILE_BLOCKS - 2), (0, 0)))
    rows_pad = SEL_PAD + T
    blk_in_tile = (np.arange(rows_pad) // SEL_BLOCK) % SEL_TILE_BLOCKS
    ka = np.zeros((rows_pad, AUG - HEAD_DIM), np.float32)
    ka[np.arange(rows_pad), blk_in_tile] = 1.0
    ka[:, SEL_TILE_BLOCKS:SEL_TILE_BLOCKS + 2] = 1.0
    key_aug = jnp.asarray(ka, BF16)
    bfar_cmp = jnp.broadcast_to(far, (KV_GROUPS, REP, 1, NB))
    ex = np.zeros((LANE, 3 * WIDTH), np.float32)
    for c in range(3 * N_HEADS):
        ex[c, c * HEAD_DIM:(c + 1) * HEAD_DIM] = 1.0
    expand = jnp.asarray(ex)

    xf = x.reshape(M, D)
    for l in range(depth):
        w = w_in[l]
        w_rw = jnp.concatenate([w[:, :c_zw], _pad_cols(w[:, c_zw:c_zw + W_LORA], LANE),
                                _pad_cols(w[:, c_zw + W_LORA:c_zw + W_LORA + A_LORA], LANE),
                                w[:, c_zw + W_LORA + A_LORA:rw_cols]], axis=1).astype(BF16)
        w_nsa = w[:, c_nsa:c_nsa + WIDTH + 6 * KV_WIDTH].astype(BF16)
        w_swa = w[:, c_swa:c_gate].astype(BF16)
        w_gate = jnp.concatenate([w[:, c_gate:], _pad_cols(w[:, c_swa - 3 * N_HEADS:c_swa], LANE)],
                                 axis=1).astype(BF16)
        g_pre = norm_pre_mix[l][None, :]
        p_rw = norm_matmul(xf, g_pre, w_rw, F32, 1024, 640)
        p_nsa = norm_matmul(xf, g_pre, w_nsa, BF16, 1024, 640)
        p_swa = norm_matmul(xf, g_pre, w_swa, BF16, 1024, 768)
        p_gate = norm_matmul(xf, g_pre, w_gate, F32, 1024, 640)

        mu = rwkv_mu[l]
        mu_p = jnp.concatenate([mu[:c_zw], jnp.pad(mu[c_zw:c_zw + W_LORA], (0, LANE - W_LORA)),
                                jnp.pad(mu[c_zw + W_LORA:c_zw + W_LORA + A_LORA], (0, LANE - A_LORA)),
                                mu[c_zw + W_LORA + A_LORA:]])[None, :]
        r, k, v, a, lw, g = rwkv_prep(p_rw.reshape(B, T, RW_COLS), mu_p, rwkv_w0[l][None, :],
                                      _pad_rows(rwkv_w2[l], LANE), rwkv_a0[l][None, :],
                                      _pad_rows(rwkv_a2[l], LANE), rwkv_g2[l], 512)
        row = lambda z: z.reshape(1, WIDTH)
        o_rwkv = rwkv_recurrence(r, k, v, a, lw, g, row(rwkv_k_k[l]), row(rwkv_k_a[l]), row(rwkv_r_k[l]),
                                 row(rwkv_ln_w[l]), row(rwkv_ln_b[l]), 512).reshape(M, WIDTH)

        pn = p_nsa.reshape(B, T, WIDTH + 6 * KV_WIDTH)
        q = _q_blocks(pn[..., :WIDTH], B, T)
        seg = lambda n: pn[..., WIDTH + n * KV_WIDTH:WIDTH + (n + 1) * KV_WIDTH]

        def strides(z):
            z = z.reshape(B, NB, CMP_STRIDE, KV_GROUPS, HEAD_DIM).transpose(0, 3, 1, 2, 4)
            return z.reshape(B, KV_GROUPS, NB, CMP_STRIDE * HEAD_DIM)

        def pe_rows(pe):
            return jnp.broadcast_to(pe.reshape(1, CMP_LEN * HEAD_DIM), (8, CMP_LEN * HEAD_DIM)).astype(BF16)

        k_cmp = nsa_compress(strides(seg(0)), pe_rows(nsa_pe_k[l]), nsa_ck_w1[l].astype(BF16),
                             nsa_ck_w2[l].astype(BF16))
        v_cmp = nsa_compress(strides(seg(1)), pe_rows(nsa_pe_v[l]), nsa_cv_w1[l].astype(BF16),
                             nsa_cv_w2[l].astype(BF16))

        def panels(z):
            z = z.reshape(B, KV_GROUPS, NS, 4, HEAD_DIM).transpose(0, 1, 3, 2, 4)
            return z.reshape(B, KV_GROUPS, NB, HEAD_DIM).astype(BF16)

        near = lambda z: jnp.pad(z, ((0, 0), (0, 0), (CMP_PAD, 8), (0, 0)))
        o_c, sel_t = nsa_compressed(q, panels(k_cmp), panels(v_cmp), near(k_cmp), near(v_cmp), bias_cmp, bfar_cmp)
        k_aug = jnp.concatenate([_kv_groups(seg(2), B, T, SEL_PAD),
                                 jnp.broadcast_to(key_aug, (B, KV_GROUPS) + key_aug.shape)], axis=-1)
        v_t = _kv_groups(seg(3), B, T, SEL_PAD).reshape(B, KV_GROUPS, -1, V_CHUNK, HEAD_DIM).swapaxes(-1, -2)
        o_s = nsa_selected(_q_blocks_t(pn[..., :WIDTH], B, T), sel_t, k_aug, v_t, bias_sel_t, bfar_sel)
        o_w = band_attention(q, _kv_groups(seg(4), B, T, NSA_WINDOW), _kv_groups(seg(5), B, T, NSA_WINDOW),
                             bias_win, None, NSA_WINDOW, NSA_WINDOW)

        ps = p_swa.reshape(B, T, swa_cols)
        sink = jnp.broadcast_to(swa_sinks[l].reshape(KV_GROUPS, REP, 1, 1), (KV_GROUPS, REP, 1, LANE))
        o_swa = band_attention(_q_blocks(ps[..., :WIDTH], B, T),
                               _kv_groups(ps[..., WIDTH:WIDTH + KV_WIDTH], B, T, SWA_WINDOW),
                               _kv_groups(ps[..., WIDTH + KV_WIDTH:], B, T, SWA_WINDOW),
                               bias_swa, sink, SWA_WINDOW, SWA_WINDOW)

        xf = merge_out(xf, p_gate, o_rwkv, _unblock(o_c, B, T), _unblock_t(o_s, B, T), _unblock(o_w, B, T),
                       _unblock(o_swa, B, T), w_branch[l].astype(BF16), w_out[l].astype(BF16),
                       norm_post_mix[l][None, :], expand, 256)

        xf = ffn(xf, norm_pre_ffn[l][None, :], ffn_w_gate[l].astype(BF16), ffn_w_up[l].astype(BF16),
                 ffn_w_down[l].astype(BF16), norm_post_ffn[l][None, :], 512, ffn_hidden // 2)
    return xf.reshape(B, T, D)
```

```python
import functools
import math

import numpy as np
import jax
import jax.numpy as jnp
from jax import lax
from jax.experimental import pallas as pl
from jax.experimental.pallas import tpu as pltpu

F32 = jnp.float32
BF16 = jnp.bfloat16
HI = lax.Precision.HIGHEST

D_MODEL = 1024
HEAD_DIM = 64
N_HEADS = 8
KV_GROUPS = 2
REP = N_HEADS // KV_GROUPS
WIDTH = N_HEADS * HEAD_DIM
KV_WIDTH = KV_GROUPS * HEAD_DIM
W_LORA, A_LORA, G_LORA = 64, 64, 128
RWKV_LN_EPS = 64e-5
CMP_STRIDE = 16
CMP_LEN = 32
CMP_HIDDEN = 256
SEL_BLOCK = 64
SEL_TOPK = 16
SEL_FORCE_SCORE = 1e4
NSA_WINDOW = 512
Q_BLOCK = 128
SWA_WINDOW = 128
REL_BUCKETS = 32
REL_MAX_DIST = 1024
NORM_EPS = 1e-6
NEG_INF = -1e30
LOG2E = math.log2(math.e)
LANE = 128
RWKV_CHUNK = 64
SEL_NEAR = 1024
SEL_SPAN = SEL_NEAR - Q_BLOCK
SEL_PAD = SEL_NEAR
SEL_TILE = 1024
SEL_TILE_BLOCKS = SEL_TILE // SEL_BLOCK
SEL_PAD_BLOCKS = SEL_PAD // SEL_BLOCK
AUG = 2 * HEAD_DIM
V_CHUNK = 128
CMP_NEAR = 64
CMP_PAD = CMP_NEAR - Q_BLOCK // CMP_STRIDE
VMEM_LIMIT = 56 * 1024 * 1024


def _cparams(sem):
    return pltpu.CompilerParams(dimension_semantics=sem, vmem_limit_bytes=VMEM_LIMIT)


def _dot(a, b, precision=None):
    return jnp.dot(a, b, preferred_element_type=F32, precision=precision)


def _dot_nt(a, b, precision=None):
    return lax.dot_general(a, b, (((1,), (1,)), ((), ())), preferred_element_type=F32, precision=precision)


def _dot_tn(a, b, precision=None):
    return lax.dot_general(a, b, (((0,), (0,)), ((), ())), preferred_element_type=F32, precision=precision)


def _bdot(a, b):
    return _dot(a.astype(BF16), b.astype(BF16))


def _bdot_nt(a, b):
    return _dot_nt(a.astype(BF16), b.astype(BF16))


def _bdot_tn(a, b):
    return _dot_tn(a.astype(BF16), b.astype(BF16))


def _rms(x, g):
    return x * lax.rsqrt(jnp.mean(x * x, axis=-1, keepdims=True) + NORM_EPS) * g


def _iota(shape, dim):
    return lax.broadcasted_iota(jnp.int32, shape, dim)


def _norm_mm_body(x_ref, g_ref, w_ref, o_ref):
    h = _rms(x_ref[...], g_ref[...]).astype(BF16)
    o_ref[...] = _dot(h, w_ref[...]).astype(o_ref.dtype)


def norm_matmul(x, g, w, out_dtype, tm):
    M, D = x.shape
    N = w.shape[1]
    return pl.pallas_call(
        _norm_mm_body,
        grid=(M // tm,),
        in_specs=[pl.BlockSpec((tm, D), lambda i: (i, 0)),
                  pl.BlockSpec((1, D), lambda i: (0, 0)),
                  pl.BlockSpec((D, N), lambda i: (0, 0))],
        out_specs=pl.BlockSpec((tm, N), lambda i: (i, 0)),
        out_shape=jax.ShapeDtypeStruct((M, N), out_dtype),
        compiler_params=_cparams(("parallel",)),
        name="norm_matmul",
    )(x, g, w)


RW_COLS = 3 * WIDTH + 3 * LANE


def _rwkv_prep_body(p_ref, mu_ref, w0_ref, w2_ref, a0_ref, a2_ref, g2_ref,
                    r_o, k_o, v_o, a_o, lw_o, g_o, carry_ref):
    tm = p_ref.shape[0]

    @pl.when(pl.program_id(1) == 0)
    def _():
        carry_ref[...] = jnp.zeros_like(carry_ref)

    p = p_ref[...]
    row = _iota(p.shape, 0)
    prev = jnp.where(row == 0, carry_ref[0:1, :], pltpu.roll(p, 1, axis=0))
    carry_ref[0:1, :] = p[tm - 1:tm, :]
    ps = p + (prev - p) * mu_ref[...]
    r_o[...] = ps[:, 0:WIDTH]
    k_o[...] = ps[:, WIDTH:2 * WIDTH]
    v_o[...] = ps[:, 2 * WIDTH:3 * WIDTH]
    zw = ps[:, 3 * WIDTH:3 * WIDTH + LANE]
    za = ps[:, 3 * WIDTH + LANE:3 * WIDTH + 2 * LANE]
    zg = ps[:, 3 * WIDTH + 2 * LANE:3 * WIDTH + 3 * LANE]
    z = -(w0_ref[...] + _dot(jnp.tanh(zw), w2_ref[...], HI))
    softplus = jnp.maximum(z, 0.0) + jnp.log(1.0 + jnp.exp(-jnp.abs(z)))
    lw_o[...] = -jnp.exp(-softplus - 0.5)
    a_o[...] = jax.nn.sigmoid(a0_ref[...] + _dot(za, a2_ref[...], HI))
    g_o[...] = _dot(jax.nn.sigmoid(zg), g2_ref[...], HI)


def rwkv_prep(p, mu, w0, w2, a0, a2, g2, tm):
    B, T, _ = p.shape
    row = lambda b, i: (b, i, 0)
    fixed = lambda b, i: (0, 0)
    out = jax.ShapeDtypeStruct((B, T, WIDTH), F32)
    return pl.pallas_call(
        _rwkv_prep_body,
        grid=(B, T // tm),
        in_specs=[pl.BlockSpec((None, tm, RW_COLS), row),
                  pl.BlockSpec((1, RW_COLS), fixed),
                  pl.BlockSpec((1, WIDTH), fixed), pl.BlockSpec((LANE, WIDTH), fixed),
                  pl.BlockSpec((1, WIDTH), fixed), pl.BlockSpec((LANE, WIDTH), fixed),
                  pl.BlockSpec((LANE, WIDTH), fixed)],
        out_specs=[pl.BlockSpec((None, tm, WIDTH), row)] * 6,
        out_shape=[out] * 6,
        scratch_shapes=[pltpu.VMEM((8, RW_COLS), F32)],
        compiler_params=_cparams(("parallel", "arbitrary")),
        name="rwkv_prep",
    )(p, mu, w0, w2, a0, a2, g2)


RWKV_HEADS_PER_STEP = 4


def _bmm(a, b, precision=None):
    return lax.dot_general(a, b, (((2,), (1,)), ((0,), (0,))), preferred_element_type=F32, precision=precision)


def _bbmm(a, b):
    return _bmm(a.astype(BF16), b.astype(BF16))


def _bbmm_nt(a, b):
    return lax.dot_general(a.astype(BF16), b.astype(BF16), (((2,), (2,)), ((0,), (0,))),
                           preferred_element_type=F32)


def _bbmm_tn(a, b):
    return lax.dot_general(a.astype(BF16), b.astype(BF16), (((1,), (1,)), ((0,), (0,))),
                           preferred_element_type=F32)


def _rwkv_rec_body(r_ref, k_ref, v_ref, a_ref, lw_ref, g_ref, kk_ref, ka_ref, rk_ref, lnw_ref, lnb_ref,
                   o_ref, s_ref):
    C = RWKV_CHUNK
    N = HEAD_DIM
    assert C == N
    tc = r_ref.shape[0]
    nc = tc // C
    hb = r_ref.shape[1] // N

    @pl.when(pl.program_id(2) == 0)
    def _():
        s_ref[...] = jnp.zeros_like(s_ref)

    def chunks(ref):
        x = ref[...]
        return jnp.concatenate([x[:, h * N:(h + 1) * N].reshape(nc, C, N) for h in range(hb)], axis=0)

    def per_head(ref):
        x = ref[...]
        return jnp.concatenate([jnp.broadcast_to(x[:, h * N:(h + 1) * N][None], (nc, 1, N))
                                for h in range(hb)], axis=0)

    row = _iota((C, C), 0)
    col = _iota((C, C), 1)
    incl = (row >= col)[None]
    strict = (row > col)[None]
    eye = (row == col)[None]
    nb = hb * nc
    r, k, v, a, lw = chunks(r_ref), chunks(k_ref), chunks(v_ref), chunks(a_ref), chunks(lw_ref)
    kk = k * per_head(kk_ref)
    kk = kk / jnp.maximum(jnp.sqrt(jnp.sum(kk * kk, axis=-1, keepdims=True)), 1e-12)
    k2 = k * (1.0 + (a - 1.0) * per_head(ka_ref))
    tri = jnp.broadcast_to(incl.astype(F32), (nb, C, C))
    cum = _bmm(tri, lw, HI)
    p_incl = jnp.exp(cum)
    p_inv = jnp.exp(-cum)
    a_t = -kk * jnp.exp(cum - lw)
    r_t = r * p_incl
    b_t = kk * a * p_inv
    k_t = k2 * p_inv
    l_ab = jnp.where(strict, _bbmm_nt(a_t, b_t), 0.0)
    l_ak = jnp.where(strict, _bbmm_nt(a_t, k_t), 0.0)
    m_rb = jnp.where(incl, _bbmm_nt(r_t, b_t), 0.0)
    m_rk = jnp.where(incl, _bbmm_nt(r_t, k_t), 0.0)
    inv = jnp.where(eye, 1.0, l_ab)
    lp = l_ab
    for _ in range(5):
        lp = _bbmm(lp, lp)
        inv = inv + _bbmm(lp, inv)
    t_a = _bbmm(inv, a_t)
    t_v = _bbmm(inv, _bbmm(l_ak, v))
    p_end = p_incl[:, C - 1:C, :]
    bp = b_t * p_end
    kp = k_t * p_end
    ry = (r_t + _bbmm(m_rb, t_a)).reshape(hb, nc, C, N)
    yc = (_bbmm(m_rb, t_v) + _bbmm(m_rk, v)).reshape(hb, nc, C, N)
    am = (jnp.where(eye, jnp.broadcast_to(p_end, (nb, C, C)), 0.0) + _bbmm_tn(bp, t_a)).reshape(hb, nc, C, C)
    gm = (_bbmm_tn(bp, t_v) + _bbmm_tn(kp, v)).reshape(hb, nc, C, N)
    h = s_ref[...]
    ys = []
    for c in range(nc):
        ys.append(_bbmm(ry[:, c], h) + yc[:, c])
        h = _bbmm(am[:, c], h) + gm[:, c]
    s_ref[...] = h
    y = jnp.stack(ys, axis=1).reshape(nb, C, N)
    mean = jnp.mean(y, axis=-1, keepdims=True)
    var = jnp.mean(jnp.square(y - mean), axis=-1, keepdims=True)
    yn = (y - mean) * lax.rsqrt(var + RWKV_LN_EPS) * per_head(lnw_ref) + per_head(lnb_ref)
    bonus = jnp.sum(r * k2 * per_head(rk_ref), axis=-1, keepdims=True) * v
    out = (yn + bonus) * chunks(g_ref)
    o_ref[...] = jnp.concatenate([out[h * nc:(h + 1) * nc].reshape(tc, N) for h in range(hb)], axis=1)


def rwkv_recurrence(r, k, v, a, lw, g, k_k, k_a, r_k, ln_w, ln_b, tc):
    B, T, W = r.shape
    slab = RWKV_HEADS_PER_STEP * HEAD_DIM
    seq = pl.BlockSpec((None, tc, slab), lambda b, h, i: (b, i, h))
    par = pl.BlockSpec((1, slab), lambda b, h, i: (0, h))
    return pl.pallas_call(
        _rwkv_rec_body,
        grid=(B, W // slab, T // tc),
        in_specs=[seq] * 6 + [par] * 5,
        out_specs=seq,
        out_shape=jax.ShapeDtypeStruct((B, T, W), F32),
        scratch_shapes=[pltpu.VMEM((RWKV_HEADS_PER_STEP, HEAD_DIM, HEAD_DIM), F32)],
        compiler_params=_cparams(("parallel", "parallel", "arbitrary")),
        name="rwkv_recurrence",
    )(r, k, v, a, lw, g, k_k, k_a, r_k, ln_w, ln_b)


def _compress_body(z_ref, pe_ref, w1_ref, w2_ref, o_ref):
    nb = z_ref.shape[0]
    half = z_ref.shape[1]
    z = z_ref[...]
    first = _dot(z, w1_ref[0:half, :])
    second = _dot(z, w1_ref[half:2 * half, :])
    pe_term = _dot(pe_ref[...], w1_ref[...])[0:1, :]
    hidden = first + pltpu.roll(second, nb - 1, axis=0) + pe_term
    out = _dot(jax.nn.gelu(hidden).astype(BF16), w2_ref[...])
    rows = _iota(out.shape, 0)
    o_ref[...] = jnp.where(rows < nb - 1, out, 0.0)


def nsa_compress(z, pe, w1, w2):
    B, G, NB, HALF = z.shape
    return pl.pallas_call(
        _compress_body,
        grid=(B, G),
        in_specs=[pl.BlockSpec((None, None, NB, HALF), lambda b, g: (b, g, 0, 0)),
                  pl.BlockSpec((8, 2 * HALF), lambda b, g: (0, 0)),
                  pl.BlockSpec((2 * HALF, CMP_HIDDEN), lambda b, g: (0, 0)),
                  pl.BlockSpec((CMP_HIDDEN, HEAD_DIM), lambda b, g: (0, 0))],
        out_specs=pl.BlockSpec((None, None, NB, HEAD_DIM), lambda b, g: (b, g, 0, 0)),
        out_shape=jax.ShapeDtypeStruct((B, G, NB, HEAD_DIM), F32),
        compiler_params=_cparams(("parallel", "parallel")),
        name="nsa_compress",
    )(z, pe, w1, w2)


def _t5_bucket_np(dist):
    n = np.maximum(dist, 0)
    max_exact = REL_BUCKETS // 2
    nf = np.maximum(n, 1).astype(np.float64)
    large = max_exact + (np.log(nf / max_exact) / math.log(REL_MAX_DIST / max_exact)
                         * (REL_BUCKETS - max_exact)).astype(np.int32)
    large = np.minimum(large, REL_BUCKETS - 1)
    return np.where(n < max_exact, n, large).astype(np.int32)


def _bias_body(tbl_ref, bk_ref, o_ref):
    h = pl.program_id(0)
    bk = bk_ref[...]
    acc = jnp.full(bk.shape, NEG_INF, F32)
    for b in range(REL_BUCKETS):
        acc = jnp.where(bk == b, tbl_ref[b, h], acc)
    o_ref[...] = acc


def bias_tile(table, dist_np, valid_np):
    P, Q = dist_np.shape
    H = table.shape[1]
    buckets = jnp.asarray(np.where(valid_np, _t5_bucket_np(dist_np), -1).astype(np.int32))
    return pl.pallas_call(
        _bias_body,
        grid=(H,),
        in_specs=[pl.BlockSpec(memory_space=pltpu.SMEM),
                  pl.BlockSpec((P, Q), lambda h: (0, 0))],
        out_specs=pl.BlockSpec((None, P, Q), lambda h: (h, 0, 0)),
        out_shape=jax.ShapeDtypeStruct((H, P, Q), F32),
        compiler_params=_cparams(("arbitrary",)),
        name="bias_tile",
    )(table, buckets)


def _nsa_cmp_body(q_ref, kp_ref, vp_ref, kn_ref, vn_ref, bias_ref, bfar_ref, o_ref, sel_ref):
    QB = Q_BLOCK
    NB = kp_ref.shape[0]
    NS = NB // 4
    ns_shift = NS.bit_length() - 1
    blk = pl.program_id(2)
    kp, vp = kp_ref[...], vp_ref[...]
    start = pl.multiple_of(blk * (QB // CMP_STRIDE), 8)
    kn = kn_ref[pl.ds(start, CMP_NEAR), :].astype(BF16)
    vn = vn_ref[pl.ds(start, CMP_NEAR), :].astype(BF16)
    first_near = blk * (QB // CMP_STRIDE) - CMP_PAD
    lane = _iota((1, NB), 1)
    c_far = 4 * (lane & (NS - 1)) + (lane >> ns_shift)
    mask_far = c_far < first_near
    i_n = _iota((QB, CMP_NEAR), 0)
    c_n = _iota((QB, CMP_NEAR), 1)
    dist_n = i_n + (CMP_PAD * CMP_STRIDE - CMP_LEN + 1) - CMP_STRIDE * c_n
    mask_near = (dist_n >= 0) & (first_near + c_n >= 0)
    imp_far = jnp.zeros((QB, NB), F32)
    imp_near = jnp.zeros((QB, CMP_NEAR), F32)
    for r in range(REP):
        rows = slice(r * QB, (r + 1) * QB)
        q = q_ref[rows, :] * (HEAD_DIM ** -0.5)
        s_f = jnp.where(mask_far, _dot_nt(q, kp) + bfar_ref[r], NEG_INF)
        s_n = jnp.where(mask_near, _dot_nt(q, kn) + bias_ref[r], NEG_INF)
        m = jnp.maximum(jnp.max(s_f, axis=-1, keepdims=True), jnp.max(s_n, axis=-1, keepdims=True))
        p_f = jnp.where(mask_far, jnp.exp(s_f - m), 0.0)
        p_n = jnp.where(mask_near, jnp.exp(s_n - m), 0.0)
        den = jnp.sum(p_f, axis=-1, keepdims=True) + jnp.sum(p_n, axis=-1, keepdims=True)
        inv = 1.0 / jnp.where(den > 0, den, 1.0)
        p_f = p_f * inv
        p_n = p_n * inv
        o_ref[rows, :] = _dot(p_f.astype(BF16), vp) + _dot(p_n.astype(BF16), vn)
        imp_far = imp_far + p_f
        imp_near = imp_near + p_n
    panel = [imp_far[:, m * NS:(m + 1) * NS] for m in range(4)]
    j = _iota((QB, NS), 1)
    prev3 = jnp.where(j == 0, 0.0, pltpu.roll(panel[3], 1, axis=1))
    imp = prev3 + 2.0 * panel[0] + 2.0 * panel[1] + 2.0 * panel[2] + panel[3]
    c_abs = first_near + _iota((CMP_NEAR, NS), 0)
    off = c_abs + 1 - 4 * _iota((CMP_NEAR, NS), 1)
    w_near = jnp.where((off == 0) | (off == 4), 1.0, jnp.where((off >= 1) & (off <= 3), 2.0, 0.0))
    imp = imp + _dot(imp_near, w_near, HI)
    cur = 2 * blk + (_iota((QB, NS), 0) >= SEL_BLOCK).astype(jnp.int32)
    forced = (j == 0) | (j == cur) | (j == cur - 1)
    score = jnp.where(forced, SEL_FORCE_SCORE, jnp.where(j <= cur, imp, -1.0))
    score = score.T
    jt = _iota((NS, QB), 0)
    sel = jnp.zeros((NS, QB), F32)
    for _ in range(min(SEL_TOPK, NS)):
        m = jnp.max(score, axis=0, keepdims=True)
        idx = jnp.min(jnp.where(score == m, jt, NS), axis=0, keepdims=True)
        hit = jt == idx
        sel = jnp.where(hit, 1.0, sel)
        score = jnp.where(hit, -3.0, score)
    pad = jnp.zeros((SEL_PAD_BLOCKS, QB), BF16)
    sel_ref[...] = jnp.concatenate([pad, sel.astype(BF16), pad], axis=0)


def nsa_compressed(q, kperm, vperm, knear, vnear, bias, bias_far):
    B, G, NQ, QR, N = q.shape
    NB = kperm.shape[2]
    nsp = NB // 4 + 2 * SEL_PAD_BLOCKS
    blk4 = lambda b, g, i: (b, g, 0, 0)
    return pl.pallas_call(
        _nsa_cmp_body,
        grid=(B, G, NQ),
        in_specs=[pl.BlockSpec((None, None, None, QR, N), lambda b, g, i: (b, g, i, 0, 0)),
                  pl.BlockSpec((None, None, NB, N), blk4), pl.BlockSpec((None, None, NB, N), blk4),
                  pl.BlockSpec((None, None, knear.shape[2], N), blk4),
                  pl.BlockSpec((None, None, knear.shape[2], N), blk4),
                  pl.BlockSpec((None, REP, Q_BLOCK, CMP_NEAR), lambda b, g, i: (g, 0, 0, 0)),
                  pl.BlockSpec((None, REP, 1, NB), lambda b, g, i: (g, 0, 0, 0))],
        out_specs=[pl.BlockSpec((None, None, None, QR, N), lambda b, g, i: (b, g, i, 0, 0)),
                   pl.BlockSpec((None, None, None, nsp, Q_BLOCK), lambda b, g, i: (b, g, i, 0, 0))],
        out_shape=[jax.ShapeDtypeStruct((B, G, NQ, QR, N), F32),
                   jax.ShapeDtypeStruct((B, G, NQ, nsp, Q_BLOCK), BF16)],
        compiler_params=_cparams(("parallel", "parallel", "parallel")),
        name="nsa_compressed",
    )(q, kperm, vperm, knear, vnear, bias, bias_far)


def _nsa_sel_body(qt_ref, selt_ref, k_ref, vt_ref, biast_ref, bfar_ref, o_ref, sa_ref, sb_ref):
    QB = Q_BLOCK
    NQ = REP * QB
    blk = pl.program_id(2)
    qt = (qt_ref[...].astype(F32) * (HEAD_DIM ** -0.5 * LOG2E)).astype(BF16)
    near_chunk = blk + (SEL_PAD - SEL_SPAN) // V_CHUNK
    first = near_chunk * (V_CHUNK // SEL_BLOCK)
    pb_u = _iota((SEL_TILE_BLOCKS, QB), 0)
    zeros = jnp.zeros((AUG - HEAD_DIM - 2 * SEL_TILE_BLOCKS, NQ), BF16)
    bfar = bfar_ref[...]

    def far_scores(kt, s_ref):
        b0 = pl.multiple_of(kt * SEL_TILE_BLOCKS, SEL_TILE_BLOCKS)
        picked = selt_ref[pl.ds(b0, SEL_TILE_BLOCKS), :].astype(F32) > 0.5
        neg = jnp.where(picked & (b0 + pb_u < first), 0.0, NEG_INF).astype(BF16)
        q_far = jnp.concatenate([qt, jnp.concatenate([neg] * REP, axis=1), bfar, zeros], axis=0)
        r0 = pl.multiple_of(kt * SEL_TILE, SEL_TILE)
        s_ref[...] = _dot(k_ref[pl.ds(r0, SEL_TILE), :], q_far)

    def far_update(kt, s_ref, carry):
        m, l, acc = carry
        s = s_ref[...]
        m_new = jnp.maximum(m, jnp.max(s, axis=0, keepdims=True))
        alpha = jnp.exp2(m - m_new)
        p = jnp.exp2(s - m_new)
        l = alpha * l + jnp.sum(p, axis=0, keepdims=True)
        p = p.astype(BF16)
        acc = alpha * acc
        for j in range(SEL_TILE // V_CHUNK):
            acc = acc + _dot(vt_ref[kt * (SEL_TILE // V_CHUNK) + j], p[j * V_CHUNK:(j + 1) * V_CHUNK])
        return m_new, l, acc

    far_scores(0, sa_ref)

    row0 = pl.multiple_of(near_chunk * V_CHUNK, V_CHUNK)
    base = pl.multiple_of((first // SEL_TILE_BLOCKS) * SEL_TILE_BLOCKS, SEL_TILE_BLOCKS)
    rows = selt_ref[pl.ds(base, 2 * SEL_TILE_BLOCKS), :].astype(F32)
    picked = jnp.where(pb_u >= first - base, rows[0:SEL_TILE_BLOCKS], rows[SEL_TILE_BLOCKS:]) > 0.5
    neg = jnp.where(picked, 0.0, NEG_INF).astype(BF16)
    q_near = jnp.concatenate([qt, jnp.concatenate([neg] * REP, axis=1), jnp.zeros_like(bfar), zeros], axis=0)
    s = _dot(k_ref[pl.ds(row0, SEL_NEAR), :], q_near) + biast_ref[...]
    m = jnp.max(s, axis=0, keepdims=True)
    p = jnp.exp2(s - m)
    l = jnp.sum(p, axis=0, keepdims=True)
    p = p.astype(BF16)
    acc = jnp.zeros((HEAD_DIM, NQ), F32)
    for j in range(SEL_NEAR // V_CHUNK):
        acc = acc + _dot(vt_ref[near_chunk + j], p[j * V_CHUNK:(j + 1) * V_CHUNK])

    def far_pair(j, carry):
        far_scores(2 * j + 1, sb_ref)
        carry = far_update(2 * j, sa_ref, carry)
        far_scores(2 * j + 2, sa_ref)
        return far_update(2 * j + 1, sb_ref, carry)

    n_far = (first + SEL_TILE_BLOCKS - 1) // SEL_TILE_BLOCKS
    m, l, acc = lax.fori_loop(0, (n_far + 1) // 2, far_pair, (m, l, acc))
    o_ref[...] = acc / l


def nsa_selected(qt, selt, k_aug, vt, bias_t, bias_far):
    B, G, NQ, N, W = qt.shape
    nsp = selt.shape[3]
    rows = k_aug.shape[2]
    return pl.pallas_call(
        _nsa_sel_body,
        grid=(B, G, NQ),
        in_specs=[pl.BlockSpec((None, None, None, N, W), lambda b, g, i: (b, g, i, 0, 0)),
                  pl.BlockSpec((None, None, None, nsp, Q_BLOCK), lambda b, g, i: (b, g, i, 0, 0)),
                  pl.BlockSpec((None, None, rows, AUG), lambda b, g, i: (b, g, 0, 0)),
                  pl.BlockSpec((None, None, rows // V_CHUNK, N, V_CHUNK), lambda b, g, i: (b, g, 0, 0, 0)),
                  pl.BlockSpec((None, SEL_NEAR, W), lambda b, g, i: (g, 0, 0)),
                  pl.BlockSpec((None, SEL_TILE_BLOCKS, W), lambda b, g, i: (g, 0, 0))],
        out_specs=pl.BlockSpec((None, None, None, N, W), lambda b, g, i: (b, g, i, 0, 0)),
        out_shape=jax.ShapeDtypeStruct((B, G, NQ, N, W), F32),
        scratch_shapes=[pltpu.VMEM((SEL_TILE, W), F32)] * 2,
        compiler_params=_cparams(("parallel", "parallel", "parallel")),
        name="nsa_selected",
    )(qt, selt, k_aug, vt, bias_t, bias_far)


def _band_body(*refs, pad, has_sink):
    if has_sink:
        q_ref, k_ref, v_ref, bias_ref, sink_ref, o_ref = refs
    else:
        q_ref, k_ref, v_ref, bias_ref, o_ref = refs
    QB = Q_BLOCK
    NQ = REP * QB
    width = pad + QB
    blk = pl.program_id(2)
    row0 = pl.multiple_of(blk * QB, QB)
    kb = k_ref[pl.ds(row0, width), :]
    vb = v_ref[pl.ds(row0, width), :]
    before_start = jnp.where(row0 + _iota((1, width), 1) >= pad, 0.0, NEG_INF)
    q = q_ref[...] * (HEAD_DIM ** -0.5)
    s = _dot_nt(q, kb) + bias_ref[...].reshape(NQ, width) + before_start
    m = jnp.max(s, axis=-1, keepdims=True)
    if has_sink:
        sink = jnp.concatenate([jnp.broadcast_to(sink_ref[r][:, 0:1], (QB, 1)) for r in range(REP)], axis=0)
        m = jnp.maximum(m, sink)
    p = jnp.exp(s - m)
    den = jnp.sum(p, axis=-1, keepdims=True)
    if has_sink:
        den = den + jnp.exp(sink - m)
    p = p * (1.0 / den)
    o_ref[...] = _dot(p.astype(BF16), vb)


def band_attention(q, k, v, bias, sink, pad):
    B, G, NQ, QR, N = q.shape
    rows = k.shape[2]
    width = pad + Q_BLOCK
    kv = pl.BlockSpec((None, None, rows, N), lambda b, g, i: (b, g, 0, 0))
    in_specs = [pl.BlockSpec((None, None, None, QR, N), lambda b, g, i: (b, g, i, 0, 0)), kv, kv,
                pl.BlockSpec((None, REP, Q_BLOCK, width), lambda b, g, i: (g, 0, 0, 0))]
    args = [q, k, v, bias]
    if sink is not None:
        in_specs.append(pl.BlockSpec((None, REP, 1, LANE), lambda b, g, i: (g, 0, 0, 0)))
        args.append(sink)
    return pl.pallas_call(
        functools.partial(_band_body, pad=pad, has_sink=sink is not None),
        grid=(B, G, NQ),
        in_specs=in_specs,
        out_specs=pl.BlockSpec((None, None, None, QR, N), lambda b, g, i: (b, g, i, 0, 0)),
        out_shape=jax.ShapeDtypeStruct((B, G, NQ, QR, N), F32),
        compiler_params=_cparams(("parallel", "parallel", "parallel")),
        name="band_attention",
    )(*args)


def _merge_body(x_ref, pg_ref, hg_ref, orw_ref, oc_ref, os_ref, ow_ref, osw_ref, wb_ref, wo_ref, gn_ref, ex_ref,
                o_ref):
    pg = pg_ref[...]
    head_gates = jax.nn.sigmoid(hg_ref[...])
    ge = _dot(head_gates, ex_ref[...], HI)
    o_nsa = (ge[:, 0:WIDTH] * oc_ref[...] + ge[:, WIDTH:2 * WIDTH] * os_ref[...]
             + ge[:, 2 * WIDTH:3 * WIDTH] * ow_ref[...])
    merged = (jax.nn.sigmoid(pg[:, 0:D_MODEL]) * _dot(orw_ref[...].astype(BF16), wb_ref[0])
              + jax.nn.sigmoid(pg[:, D_MODEL:2 * D_MODEL]) * _dot(o_nsa.astype(BF16), wb_ref[1])
              + jax.nn.sigmoid(pg[:, 2 * D_MODEL:3 * D_MODEL]) * _dot(osw_ref[...].astype(BF16), wb_ref[2]))
    y = _dot(merged.astype(BF16), wo_ref[...])
    o_ref[...] = x_ref[...] + _rms(y, gn_ref[...])


def merge_out(x, pg, p_rw, o_rwkv, o_c, o_s, o_w, o_swa, w_branch, w_out, g_post, expand, tm):
    M, D = x.shape
    row = lambda i: (i, 0)
    wide = pl.BlockSpec((tm, WIDTH), row)
    return pl.pallas_call(
        _merge_body,
        grid=(M // tm,),
        in_specs=[pl.BlockSpec((tm, D), row), pl.BlockSpec((tm, 3 * D), row),
                  pl.BlockSpec((tm, LANE), lambda i: (i, RW_COLS // LANE)),
                  wide, wide, wide, wide, wide,
                  pl.BlockSpec((3, WIDTH, D), lambda i: (0, 0, 0)),
                  pl.BlockSpec((D, D), lambda i: (0, 0)),
                  pl.BlockSpec((1, D), lambda i: (0, 0)),
                  pl.BlockSpec((LANE, 3 * WIDTH), lambda i: (0, 0))],
        out_specs=pl.BlockSpec((tm, D), row),
        out_shape=jax.ShapeDtypeStruct((M, D), F32),
        compiler_params=_cparams(("parallel",)),
        name="merge_out",
    )(x, pg, p_rw, o_rwkv, o_c, o_s, o_w, o_swa, w_branch, w_out, g_post, expand)


def _ffn_body(x_ref, gpre_ref, wg_ref, wu_ref, wd_ref, gpost_ref, o_ref, h_ref, acc_ref):
    j = pl.program_id(1)

    @pl.when(j == 0)
    def _():
        h_ref[...] = _rms(x_ref[...], gpre_ref[...]).astype(BF16)
        acc_ref[...] = jnp.zeros_like(acc_ref)

    h = h_ref[...]
    gate = _dot(h, wg_ref[...])
    act = gate * jax.nn.sigmoid(gate) * _dot(h, wu_ref[...])
    acc_ref[...] += _dot(act.astype(BF16), wd_ref[...])

    @pl.when(j == pl.num_programs(1) - 1)
    def _():
        o_ref[...] = x_ref[...] + _rms(acc_ref[...], gpost_ref[...])


def ffn(x, g_pre, w_gate, w_up, w_down, g_post, tm, th):
    M, D = x.shape
    Hd = w_gate.shape[1]
    return pl.pallas_call(
        _ffn_body,
        grid=(M // tm, Hd // th),
        in_specs=[pl.BlockSpec((tm, D), lambda i, j: (i, 0)),
                  pl.BlockSpec((1, D), lambda i, j: (0, 0)),
                  pl.BlockSpec((D, th), lambda i, j: (0, j)),
                  pl.BlockSpec((D, th), lambda i, j: (0, j)),
                  pl.BlockSpec((th, D), lambda i, j: (j, 0)),
                  pl.BlockSpec((1, D), lambda i, j: (0, 0))],
        out_specs=pl.BlockSpec((tm, D), lambda i, j: (i, 0)),
        out_shape=jax.ShapeDtypeStruct((M, D), F32),
        scratch_shapes=[pltpu.VMEM((tm, D), BF16), pltpu.VMEM((tm, D), F32)],
        compiler_params=_cparams(("parallel", "arbitrary")),
        name="ffn",
    )(x, g_pre, w_gate, w_up, w_down, g_post)


def _pad_cols(w, n):
    return jnp.pad(w, ((0, 0), (0, n - w.shape[1])))


def _pad_rows(w, n):
    return jnp.pad(w, ((0, n - w.shape[0]), (0, 0)))


def _q_blocks(q, B, T):
    nq = T // Q_BLOCK
    q = q.reshape(B, nq, Q_BLOCK, KV_GROUPS, REP, HEAD_DIM).transpose(0, 3, 1, 4, 2, 5)
    return q.reshape(B, KV_GROUPS, nq, REP * Q_BLOCK, HEAD_DIM)


def _q_blocks_t(q, B, T):
    nq = T // Q_BLOCK
    q = q.reshape(B, nq, Q_BLOCK, KV_GROUPS, REP, HEAD_DIM).transpose(0, 3, 1, 5, 4, 2)
    return q.reshape(B, KV_GROUPS, nq, HEAD_DIM, REP * Q_BLOCK)


def _unblock_t(o, B, T):
    nq = T // Q_BLOCK
    o = o.reshape(B, KV_GROUPS, nq, HEAD_DIM, REP, Q_BLOCK).transpose(0, 2, 5, 1, 4, 3)
    return o.reshape(B * T, WIDTH)


def _unblock(o, B, T):
    nq = T // Q_BLOCK
    o = o.reshape(B, KV_GROUPS, nq, REP, Q_BLOCK, HEAD_DIM).transpose(0, 2, 4, 1, 3, 5)
    return o.reshape(B * T, WIDTH)


def _kv_groups(z, B, T, pad):
    z = z.reshape(B, T, KV_GROUPS, HEAD_DIM).transpose(0, 2, 1, 3)
    return jnp.pad(z, ((0, 0), (0, 0), (pad, 0), (0, 0)))


def _group_bias(b):
    return b.reshape(KV_GROUPS, REP, b.shape[1], b.shape[2])


def kernel(x, norm_pre_mix, norm_post_mix, norm_pre_ffn, norm_post_ffn, w_in, rwkv_mu, rwkv_w0, rwkv_w2, rwkv_a0, rwkv_a2, rwkv_g2, rwkv_k_k, rwkv_k_a, rwkv_r_k, rwkv_ln_w, rwkv_ln_b, nsa_pe_k, nsa_pe_v, nsa_ck_w1, nsa_ck_w2, nsa_cv_w1, nsa_cv_w2, swa_sinks, rel_bias, w_branch, w_out, ffn_w_gate, ffn_w_up, ffn_w_down):
    B, T, D = x.shape
    depth = w_in.shape[0]
    M = B * T
    NB = T // CMP_STRIDE
    NS = T // SEL_BLOCK
    ffn_hidden = ffn_w_gate.shape[2]

    rw_cols = 3 * WIDTH + W_LORA + A_LORA + G_LORA
    nsa_cols = WIDTH + 6 * KV_WIDTH + 3 * N_HEADS
    swa_cols = WIDTH + 2 * KV_WIDTH
    c_nsa = rw_cols
    c_swa = c_nsa + nsa_cols
    c_gate = c_swa + swa_cols
    c_zw = 3 * WIDTH

    ii = np.arange(Q_BLOCK)[:, None]
    d_swa = SWA_WINDOW + ii - np.arange(SWA_WINDOW + Q_BLOCK)[None, :]
    d_win = NSA_WINDOW + ii - np.arange(NSA_WINDOW + Q_BLOCK)[None, :]
    d_sel = SEL_SPAN + ii - np.arange(SEL_NEAR)[None, :]
    d_cmp = ii + (CMP_PAD * CMP_STRIDE - CMP_LEN + 1) - CMP_STRIDE * np.arange(CMP_NEAR)[None, :]
    bias_swa = bias_tile(rel_bias, d_swa, (d_swa >= 0) & (d_swa < SWA_WINDOW))
    bias_win = bias_tile(rel_bias, d_win, (d_win >= 0) & (d_win < NSA_WINDOW))
    bias_sel = bias_tile(rel_bias, d_sel, d_sel >= 0)
    bias_cmp = bias_tile(rel_bias, d_cmp, d_cmp >= 0)
    bias_swa = _group_bias(bias_swa[N_HEADS:])
    bias_win = _group_bias(bias_win[:N_HEADS])
    bias_sel = _group_bias(bias_sel[:N_HEADS])
    bias_cmp = _group_bias(bias_cmp[:N_HEADS])
    far = rel_bias[REL_BUCKETS - 1, :N_HEADS].reshape(KV_GROUPS, REP, 1, 1)
    bias_sel_t = bias_sel.transpose(0, 3, 1, 2).reshape(KV_GROUPS, SEL_NEAR, REP * Q_BLOCK) * LOG2E
    far2 = far * LOG2E
    far_hi = far2.astype(BF16)
    far_lo = (far2 - far_hi.astype(F32)).astype(BF16)
    far_rows = jnp.concatenate([far_hi, far_lo], axis=2)
    far_rows = jnp.broadcast_to(far_rows, (KV_GROUPS, REP, 2, Q_BLOCK)).transpose(0, 2, 1, 3)
    bfar_sel = jnp.pad(far_rows.reshape(KV_GROUPS, 2, REP * Q_BLOCK), ((0, 0), (0, SEL_TILE_BLOCKS - 2), (0, 0)))
    rows_pad = SEL_PAD + T
    blk_in_tile = (np.arange(rows_pad) // SEL_BLOCK) % SEL_TILE_BLOCKS
    ka = np.zeros((rows_pad, AUG - HEAD_DIM), np.float32)
    ka[np.arange(rows_pad), blk_in_tile] = 1.0
    ka[:, SEL_TILE_BLOCKS:SEL_TILE_BLOCKS + 2] = 1.0
    key_aug = jnp.asarray(ka, BF16)
    bfar_cmp = jnp.broadcast_to(far, (KV_GROUPS, REP, 1, NB))
    ex = np.zeros((LANE, 3 * WIDTH), np.float32)
    for c in range(3 * N_HEADS):
        ex[c, c * HEAD_DIM:(c + 1) * HEAD_DIM] = 1.0
    expand = jnp.asarray(ex)

    xf = x.reshape(M, D)
    for l in range(depth):
        w = w_in[l]
        w_rw = jnp.concatenate([w[:, :c_zw], _pad_cols(w[:, c_zw:c_zw + W_LORA], LANE),
                                _pad_cols(w[:, c_zw + W_LORA:c_zw + W_LORA + A_LORA], LANE),
                                w[:, c_zw + W_LORA + A_LORA:rw_cols],
                                _pad_cols(w[:, c_swa - 3 * N_HEADS:c_swa], LANE)], axis=1).astype(BF16)
        w_nsa = w[:, c_nsa:c_nsa + WIDTH + 6 * KV_WIDTH].astype(BF16)
        w_swa = w[:, c_swa:c_gate].astype(BF16)
        w_gate = w[:, c_gate:].astype(BF16)
        g_pre = norm_pre_mix[l][None, :]
        p_rw = norm_matmul(xf, g_pre, w_rw, F32, 1024)
        p_nsa = norm_matmul(xf, g_pre, w_nsa, BF16, 1024)
        p_swa = norm_matmul(xf, g_pre, w_swa, BF16, 1024)
        p_gate = norm_matmul(xf, g_pre, w_gate, F32, 512)

        mu = rwkv_mu[l]
        mu_p = jnp.concatenate([mu[:c_zw], jnp.pad(mu[c_zw:c_zw + W_LORA], (0, LANE - W_LORA)),
                                jnp.pad(mu[c_zw + W_LORA:c_zw + W_LORA + A_LORA], (0, LANE - A_LORA)),
                                mu[c_zw + W_LORA + A_LORA:]])[None, :]
        r, k, v, a, lw, g = rwkv_prep(p_rw.reshape(B, T, RW_COLS + LANE), mu_p, rwkv_w0[l][None, :],
                                      _pad_rows(rwkv_w2[l], LANE), rwkv_a0[l][None, :],
                                      _pad_rows(rwkv_a2[l], LANE), rwkv_g2[l], 512)
        row = lambda z: z.reshape(1, WIDTH)
        o_rwkv = rwkv_recurrence(r, k, v, a, lw, g, row(rwkv_k_k[l]), row(rwkv_k_a[l]), row(rwkv_r_k[l]),
                                 row(rwkv_ln_w[l]), row(rwkv_ln_b[l]), 512).reshape(M, WIDTH)

        pn = p_nsa.reshape(B, T, WIDTH + 6 * KV_WIDTH)
        q = _q_blocks(pn[..., :WIDTH], B, T)
        seg = lambda n: pn[..., WIDTH + n * KV_WIDTH:WIDTH + (n + 1) * KV_WIDTH]

        def strides(z):
            z = z.reshape(B, NB, CMP_STRIDE, KV_GROUPS, HEAD_DIM).transpose(0, 3, 1, 2, 4)
            return z.reshape(B, KV_GROUPS, NB, CMP_STRIDE * HEAD_DIM)

        def pe_rows(pe):
            return jnp.broadcast_to(pe.reshape(1, CMP_LEN * HEAD_DIM), (8, CMP_LEN * HEAD_DIM)).astype(BF16)

        k_cmp = nsa_compress(strides(seg(0)), pe_rows(nsa_pe_k[l]), nsa_ck_w1[l].astype(BF16),
                             nsa_ck_w2[l].astype(BF16))
        v_cmp = nsa_compress(strides(seg(1)), pe_rows(nsa_pe_v[l]), nsa_cv_w1[l].astype(BF16),
                             nsa_cv_w2[l].astype(BF16))

        def panels(z):
            z = z.reshape(B, KV_GROUPS, NS, 4, HEAD_DIM).transpose(0, 1, 3, 2, 4)
            return z.reshape(B, KV_GROUPS, NB, HEAD_DIM).astype(BF16)

        near = lambda z: jnp.pad(z, ((0, 0), (0, 0), (CMP_PAD, 8), (0, 0)))
        o_c, sel_t = nsa_compressed(q, panels(k_cmp), panels(v_cmp), near(k_cmp), near(v_cmp), bias_cmp, bfar_cmp)
        k_aug = jnp.concatenate([_kv_groups(seg(2), B, T, SEL_PAD),
                                 jnp.broadcast_to(key_aug, (B, KV_GROUPS) + key_aug.shape)], axis=-1)
        v_t = _kv_groups(seg(3), B, T, SEL_PAD).reshape(B, KV_GROUPS, -1, V_CHUNK, HEAD_DIM).swapaxes(-1, -2)
        o_s = nsa_selected(_q_blocks_t(pn[..., :WIDTH], B, T), sel_t, k_aug, v_t, bias_sel_t, bfar_sel)
        o_w = band_attention(q, _kv_groups(seg(4), B, T, NSA_WINDOW), _kv_groups(seg(5), B, T, NSA_WINDOW),
                             bias_win, None, NSA_WINDOW)

        ps = p_swa.reshape(B, T, swa_cols)
        sink = jnp.broadcast_to(swa_sinks[l].reshape(KV_GROUPS, REP, 1, 1), (KV_GROUPS, REP, 1, LANE))
        o_swa = band_attention(_q_blocks(ps[..., :WIDTH], B, T),
                               _kv_groups(ps[..., WIDTH:WIDTH + KV_WIDTH], B, T, SWA_WINDOW),
                               _kv_groups(ps[..., WIDTH + KV_WIDTH:], B, T, SWA_WINDOW),
                               bias_swa, sink, SWA_WINDOW)

        xf = merge_out(xf, p_gate, p_rw, o_rwkv, _unblock(o_c, B, T), _unblock_t(o_s, B, T), _unblock(o_w, B, T),
                       _unblock(o_swa, B, T), w_branch[l].astype(BF16), w_out[l].astype(BF16),
                       norm_post_mix[l][None, :], expand, 256)

        xf = ffn(xf, norm_pre_ffn[l][None, :], ffn_w_gate[l].astype(BF16), ffn_w_up[l].astype(BF16),
                 ffn_w_down[l].astype(BF16), norm_post_ffn[l][None, :], 512, ffn_hidden // 2)
    return xf.reshape(B, T, D)
```

```python
import functools
import math

import numpy as np
import jax
import jax.numpy as jnp
from jax import lax
from jax.experimental import pallas as pl
from jax.experimental.pallas import tpu as pltpu

F32 = jnp.float32
BF16 = jnp.bfloat16
HI = lax.Precision.HIGHEST

D_MODEL = 1024
HEAD_DIM = 64
N_HEADS = 8
KV_GROUPS = 2
REP = N_HEADS // KV_GROUPS
WIDTH = N_HEADS * HEAD_DIM
KV_WIDTH = KV_GROUPS * HEAD_DIM
W_LORA, A_LORA, G_LORA = 64, 64, 128
RWKV_LN_EPS = 64e-5
CMP_STRIDE = 16
CMP_LEN = 32
CMP_HIDDEN = 256
SEL_BLOCK = 64
SEL_TOPK = 16
SEL_FORCE_SCORE = 1e4
NSA_WINDOW = 512
Q_BLOCK = 128
SWA_WINDOW = 128
REL_BUCKETS = 32
REL_MAX_DIST = 1024
NORM_EPS = 1e-6
NEG_INF = -1e30
LOG2E = math.log2(math.e)
LANE = 128
RWKV_CHUNK = 64
SEL_NEAR = 1024
SEL_SPAN = SEL_NEAR - Q_BLOCK
SEL_PAD = SEL_NEAR
SEL_TILE = 1024
SEL_TILE_BLOCKS = SEL_TILE // SEL_BLOCK
SEL_PAD_BLOCKS = SEL_PAD // SEL_BLOCK
AUG = 2 * HEAD_DIM
V_CHUNK = 128
CMP_NEAR = 64
CMP_PAD = CMP_NEAR - Q_BLOCK // CMP_STRIDE
VMEM_LIMIT = 56 * 1024 * 1024


def _cparams(sem):
    return pltpu.CompilerParams(dimension_semantics=sem, vmem_limit_bytes=VMEM_LIMIT)


def _dot(a, b, precision=None):
    return jnp.dot(a, b, preferred_element_type=F32, precision=precision)


def _dot_nt(a, b, precision=None):
    return lax.dot_general(a, b, (((1,), (1,)), ((), ())), preferred_element_type=F32, precision=precision)


def _dot_tn(a, b, precision=None):
    return lax.dot_general(a, b, (((0,), (0,)), ((), ())), preferred_element_type=F32, precision=precision)


def _bdot(a, b):
    return _dot(a.astype(BF16), b.astype(BF16))


def _bdot_nt(a, b):
    return _dot_nt(a.astype(BF16), b.astype(BF16))


def _bdot_tn(a, b):
    return _dot_tn(a.astype(BF16), b.astype(BF16))


def _rms(x, g):
    return x * lax.rsqrt(jnp.mean(x * x, axis=-1, keepdims=True) + NORM_EPS) * g


def _iota(shape, dim):
    return lax.broadcasted_iota(jnp.int32, shape, dim)


def _norm_mm_body(x_ref, g_ref, w_ref, o_ref):
    h = _rms(x_ref[...], g_ref[...]).astype(BF16)
    o_ref[...] = _dot(h, w_ref[...]).astype(o_ref.dtype)


def norm_matmul(x, g, w, out_dtype, tm):
    M, D = x.shape
    N = w.shape[1]
    return pl.pallas_call(
        _norm_mm_body,
        grid=(M // tm,),
        in_specs=[pl.BlockSpec((tm, D), lambda i: (i, 0)),
                  pl.BlockSpec((1, D), lambda i: (0, 0)),
                  pl.BlockSpec((D, N), lambda i: (0, 0))],
        out_specs=pl.BlockSpec((tm, N), lambda i: (i, 0)),
        out_shape=jax.ShapeDtypeStruct((M, N), out_dtype),
        compiler_params=_cparams(("parallel",)),
        name="norm_matmul",
    )(x, g, w)


RW_COLS = 3 * WIDTH + 3 * LANE


def _rwkv_prep_body(p_ref, mu_ref, w0_ref, w2_ref, a0_ref, a2_ref, g2_ref,
                    r_o, k_o, v_o, a_o, lw_o, g_o, carry_ref):
    tm = p_ref.shape[0]

    @pl.when(pl.program_id(1) == 0)
    def _():
        carry_ref[...] = jnp.zeros_like(carry_ref)

    p = p_ref[...]
    row = _iota(p.shape, 0)
    prev = jnp.where(row == 0, carry_ref[0:1, :], pltpu.roll(p, 1, axis=0))
    carry_ref[0:1, :] = p[tm - 1:tm, :]
    ps = p + (prev - p) * mu_ref[...]
    r_o[...] = ps[:, 0:WIDTH]
    k_o[...] = ps[:, WIDTH:2 * WIDTH]
    v_o[...] = ps[:, 2 * WIDTH:3 * WIDTH]
    zw = ps[:, 3 * WIDTH:3 * WIDTH + LANE]
    za = ps[:, 3 * WIDTH + LANE:3 * WIDTH + 2 * LANE]
    zg = ps[:, 3 * WIDTH + 2 * LANE:3 * WIDTH + 3 * LANE]
    z = -(w0_ref[...] + _dot(jnp.tanh(zw), w2_ref[...], HI))
    softplus = jnp.maximum(z, 0.0) + jnp.log(1.0 + jnp.exp(-jnp.abs(z)))
    lw_o[...] = -jnp.exp(-softplus - 0.5)
    a_o[...] = jax.nn.sigmoid(a0_ref[...] + _dot(za, a2_ref[...], HI))
    g_o[...] = _dot(jax.nn.sigmoid(zg), g2_ref[...], HI)


def rwkv_prep(p, mu, w0, w2, a0, a2, g2, tm):
    B, T, _ = p.shape
    row = lambda b, i: (b, i, 0)
    fixed = lambda b, i: (0, 0)
    out = jax.ShapeDtypeStruct((B, T, WIDTH), F32)
    return pl.pallas_call(
        _rwkv_prep_body,
        grid=(B, T // tm),
        in_specs=[pl.BlockSpec((None, tm, RW_COLS), row),
                  pl.BlockSpec((1, RW_COLS), fixed),
                  pl.BlockSpec((1, WIDTH), fixed), pl.BlockSpec((LANE, WIDTH), fixed),
                  pl.BlockSpec((1, WIDTH), fixed), pl.BlockSpec((LANE, WIDTH), fixed),
                  pl.BlockSpec((LANE, WIDTH), fixed)],
        out_specs=[pl.BlockSpec((None, tm, WIDTH), row)] * 6,
        out_shape=[out] * 6,
        scratch_shapes=[pltpu.VMEM((8, RW_COLS), F32)],
        compiler_params=_cparams(("parallel", "arbitrary")),
        name="rwkv_prep",
    )(p, mu, w0, w2, a0, a2, g2)


RWKV_HEADS_PER_STEP = 4


def _bmm(a, b, precision=None):
    return lax.dot_general(a, b, (((2,), (1,)), ((0,), (0,))), preferred_element_type=F32, precision=precision)


def _bbmm(a, b):
    return _bmm(a.astype(BF16), b.astype(BF16))


def _bbmm_nt(a, b):
    return lax.dot_general(a.astype(BF16), b.astype(BF16), (((2,), (2,)), ((0,), (0,))),
                           preferred_element_type=F32)


def _bbmm_tn(a, b):
    return lax.dot_general(a.astype(BF16), b.astype(BF16), (((1,), (1,)), ((0,), (0,))),
                           preferred_element_type=F32)


def _rwkv_rec_body(r_ref, k_ref, v_ref, a_ref, lw_ref, g_ref, kk_ref, ka_ref, rk_ref, lnw_ref, lnb_ref,
                   o_ref, s_ref):
    C = RWKV_CHUNK
    N = HEAD_DIM
    assert C == N
    tc = r_ref.shape[0]
    nc = tc // C
    hb = r_ref.shape[1] // N

    @pl.when(pl.program_id(2) == 0)
    def _():
        s_ref[...] = jnp.zeros_like(s_ref)

    def chunks(ref):
        x = ref[...]
        return jnp.concatenate([x[:, h * N:(h + 1) * N].reshape(nc, C, N) for h in range(hb)], axis=0)

    def per_head(ref):
        x = ref[...]
        return jnp.concatenate([jnp.broadcast_to(x[:, h * N:(h + 1) * N][None], (nc, 1, N))
                                for h in range(hb)], axis=0)

    row = _iota((C, C), 0)
    col = _iota((C, C), 1)
    incl = (row >= col)[None]
    strict = (row > col)[None]
    eye = (row == col)[None]
    nb = hb * nc
    r, k, v, a, lw = chunks(r_ref), chunks(k_ref), chunks(v_ref), chunks(a_ref), chunks(lw_ref)
    kk = k * per_head(kk_ref)
    kk = kk / jnp.maximum(jnp.sqrt(jnp.sum(kk * kk, axis=-1, keepdims=True)), 1e-12)
    k2 = k * (1.0 + (a - 1.0) * per_head(ka_ref))
    tri = jnp.broadcast_to(incl.astype(F32), (nb, C, C))
    cum = _bmm(tri, lw, HI)
    p_incl = jnp.exp(cum)
    p_inv = jnp.exp(-cum)
    a_t = -kk * jnp.exp(cum - lw)
    r_t = r * p_incl
    b_t = kk * a * p_inv
    k_t = k2 * p_inv
    l_ab = jnp.where(strict, _bbmm_nt(a_t, b_t), 0.0)
    l_ak = jnp.where(strict, _bbmm_nt(a_t, k_t), 0.0)
    m_rb = jnp.where(incl, _bbmm_nt(r_t, b_t), 0.0)
    m_rk = jnp.where(incl, _bbmm_nt(r_t, k_t), 0.0)
    inv = jnp.where(eye, 1.0, l_ab)
    lp = l_ab
    for _ in range(5):
        lp = _bbmm(lp, lp)
        inv = inv + _bbmm(lp, inv)
    t_a = _bbmm(inv, a_t)
    t_v = _bbmm(inv, _bbmm(l_ak, v))
    p_end = p_incl[:, C - 1:C, :]
    bp = b_t * p_end
    kp = k_t * p_end
    ry = (r_t + _bbmm(m_rb, t_a)).reshape(hb, nc, C, N)
    yc = (_bbmm(m_rb, t_v) + _bbmm(m_rk, v)).reshape(hb, nc, C, N)
    am = (jnp.where(eye, jnp.broadcast_to(p_end, (nb, C, C)), 0.0) + _bbmm_tn(bp, t_a)).reshape(hb, nc, C, C)
    gm = (_bbmm_tn(bp, t_v) + _bbmm_tn(kp, v)).reshape(hb, nc, C, N)
    h = s_ref[...]
    ys = []
    for c in range(nc):
        ys.append(_bbmm(ry[:, c], h) + yc[:, c])
        h = _bbmm(am[:, c], h) + gm[:, c]
    s_ref[...] = h
    y = jnp.stack(ys, axis=1).reshape(nb, C, N)
    mean = jnp.mean(y, axis=-1, keepdims=True)
    var = jnp.mean(jnp.square(y - mean), axis=-1, keepdims=True)
    yn = (y - mean) * lax.rsqrt(var + RWKV_LN_EPS) * per_head(lnw_ref) + per_head(lnb_ref)
    bonus = jnp.sum(r * k2 * per_head(rk_ref), axis=-1, keepdims=True) * v
    out = (yn + bonus) * chunks(g_ref)
    o_ref[...] = jnp.concatenate([out[h * nc:(h + 1) * nc].reshape(tc, N) for h in range(hb)], axis=1)


def rwkv_recurrence(r, k, v, a, lw, g, k_k, k_a, r_k, ln_w, ln_b, tc):
    B, T, W = r.shape
    slab = RWKV_HEADS_PER_STEP * HEAD_DIM
    seq = pl.BlockSpec((None, tc, slab), lambda b, h, i: (b, i, h))
    par = pl.BlockSpec((1, slab), lambda b, h, i: (0, h))
    return pl.pallas_call(
        _rwkv_rec_body,
        grid=(B, W // slab, T // tc),
        in_specs=[seq] * 6 + [par] * 5,
        out_specs=seq,
        out_shape=jax.ShapeDtypeStruct((B, T, W), F32),
        scratch_shapes=[pltpu.VMEM((RWKV_HEADS_PER_STEP, HEAD_DIM, HEAD_DIM), F32)],
        compiler_params=_cparams(("parallel", "parallel", "arbitrary")),
        name="rwkv_recurrence",
    )(r, k, v, a, lw, g, k_k, k_a, r_k, ln_w, ln_b)


def _compress_body(z_ref, pe_ref, w1_ref, w2_ref, o_ref):
    nb = z_ref.shape[0]
    half = z_ref.shape[1]
    z = z_ref[...]
    first = _dot(z, w1_ref[0:half, :])
    second = _dot(z, w1_ref[half:2 * half, :])
    pe_term = _dot(pe_ref[...], w1_ref[...])[0:1, :]
    hidden = first + pltpu.roll(second, nb - 1, axis=0) + pe_term
    out = _dot(jax.nn.gelu(hidden).astype(BF16), w2_ref[...])
    rows = _iota(out.shape, 0)
    o_ref[...] = jnp.where(rows < nb - 1, out, 0.0)


def nsa_compress(z, pe, w1, w2):
    B, G, NB, HALF = z.shape
    return pl.pallas_call(
        _compress_body,
        grid=(B, G),
        in_specs=[pl.BlockSpec((None, None, NB, HALF), lambda b, g: (b, g, 0, 0)),
                  pl.BlockSpec((8, 2 * HALF), lambda b, g: (0, 0)),
                  pl.BlockSpec((2 * HALF, CMP_HIDDEN), lambda b, g: (0, 0)),
                  pl.BlockSpec((CMP_HIDDEN, HEAD_DIM), lambda b, g: (0, 0))],
        out_specs=pl.BlockSpec((None, None, NB, HEAD_DIM), lambda b, g: (b, g, 0, 0)),
        out_shape=jax.ShapeDtypeStruct((B, G, NB, HEAD_DIM), F32),
        compiler_params=_cparams(("parallel", "parallel")),
        name="nsa_compress",
    )(z, pe, w1, w2)


def _t5_bucket_np(dist):
    n = np.maximum(dist, 0)
    max_exact = REL_BUCKETS // 2
    nf = np.maximum(n, 1).astype(np.float64)
    large = max_exact + (np.log(nf / max_exact) / math.log(REL_MAX_DIST / max_exact)
                         * (REL_BUCKETS - max_exact)).astype(np.int32)
    large = np.minimum(large, REL_BUCKETS - 1)
    return np.where(n < max_exact, n, large).astype(np.int32)


def _bias_body(tbl_ref, bk_ref, o_ref):
    h = pl.program_id(0)
    bk = bk_ref[...]
    acc = jnp.full(bk.shape, NEG_INF, F32)
    for b in range(REL_BUCKETS):
        acc = jnp.where(bk == b, tbl_ref[b, h], acc)
    o_ref[...] = acc


def bias_tile(table, dist_np, valid_np):
    P, Q = dist_np.shape
    H = table.shape[1]
    buckets = jnp.asarray(np.where(valid_np, _t5_bucket_np(dist_np), -1).astype(np.int32))
    return pl.pallas_call(
        _bias_body,
        grid=(H,),
        in_specs=[pl.BlockSpec(memory_space=pltpu.SMEM),
                  pl.BlockSpec((P, Q), lambda h: (0, 0))],
        out_specs=pl.BlockSpec((None, P, Q), lambda h: (h, 0, 0)),
        out_shape=jax.ShapeDtypeStruct((H, P, Q), F32),
        compiler_params=_cparams(("arbitrary",)),
        name="bias_tile",
    )(table, buckets)


def _heads_on_rows(qb):
    return jnp.concatenate([qb[:, r * HEAD_DIM:(r + 1) * HEAD_DIM] for r in range(REP)], axis=0)


def _heads_on_lanes(o):
    return jnp.concatenate([o[r * Q_BLOCK:(r + 1) * Q_BLOCK] for r in range(REP)], axis=1)


def _per_head_column(ref):
    return jnp.concatenate([jnp.broadcast_to(ref[r][:, 0:1], (Q_BLOCK, 1)) for r in range(REP)], axis=0)


def _nsa_cmp_body(q_ref, kp_ref, vp_ref, kn_ref, vn_ref, bias_ref, bfar_ref, o_ref, sel_ref):
    QB = Q_BLOCK
    NQ = REP * QB
    NB = kp_ref.shape[0]
    NS = NB // 4
    ns_shift = NS.bit_length() - 1
    blk = pl.program_id(2)
    start = pl.multiple_of(blk * (QB // CMP_STRIDE), 8)
    kn = kn_ref[pl.ds(start, CMP_NEAR), :].astype(BF16)
    vn = vn_ref[pl.ds(start, CMP_NEAR), :].astype(BF16)
    first_near = blk * (QB // CMP_STRIDE) - CMP_PAD
    lane = _iota((1, NB), 1)
    c_far = 4 * (lane & (NS - 1)) + (lane >> ns_shift)
    far_row = jnp.where(c_far < first_near, 0.0, NEG_INF)
    near_row = jnp.where(first_near + _iota((1, CMP_NEAR), 1) >= 0, 0.0, NEG_INF)
    q = _heads_on_rows(q_ref[...]) * (HEAD_DIM ** -0.5)
    s_f = _dot_nt(q, kp_ref[...]) + _per_head_column(bfar_ref) + far_row
    s_n = _dot_nt(q, kn) + bias_ref[...].reshape(NQ, CMP_NEAR) + near_row
    m = jnp.maximum(jnp.max(s_f, axis=-1, keepdims=True), jnp.max(s_n, axis=-1, keepdims=True))
    m = jnp.maximum(m, 0.1 * NEG_INF)
    p_f = jnp.exp(s_f - m)
    p_n = jnp.exp(s_n - m)
    den = jnp.sum(p_f, axis=-1, keepdims=True) + jnp.sum(p_n, axis=-1, keepdims=True)
    inv = 1.0 / jnp.where(den > 0, den, 1.0)
    p_f = p_f * inv
    p_n = p_n * inv
    o_ref[...] = _heads_on_lanes(_dot(p_f.astype(BF16), vp_ref[...]) + _dot(p_n.astype(BF16), vn))
    imp_far = p_f[0:QB]
    imp_near = p_n[0:QB]
    for r in range(1, REP):
        imp_far = imp_far + p_f[r * QB:(r + 1) * QB]
        imp_near = imp_near + p_n[r * QB:(r + 1) * QB]
    panel = [imp_far[:, m * NS:(m + 1) * NS] for m in range(4)]
    j = _iota((QB, NS), 1)
    prev3 = jnp.where(j == 0, 0.0, pltpu.roll(panel[3], 1, axis=1))
    imp = prev3 + 2.0 * panel[0] + 2.0 * panel[1] + 2.0 * panel[2] + panel[3]
    c_abs = first_near + _iota((CMP_NEAR, NS), 0)
    off = c_abs + 1 - 4 * _iota((CMP_NEAR, NS), 1)
    w_near = jnp.where((off == 0) | (off == 4), 1.0, jnp.where((off >= 1) & (off <= 3), 2.0, 0.0))
    imp = imp + _dot(imp_near, w_near, HI)
    cur = 2 * blk + (_iota((QB, NS), 0) >= SEL_BLOCK).astype(jnp.int32)
    forced = (j == 0) | (j == cur) | (j == cur - 1)
    score = jnp.where(forced, SEL_FORCE_SCORE, jnp.where(j <= cur, imp, -1.0))
    score = score.T
    jt = _iota((NS, QB), 0)
    sel = jnp.zeros((NS, QB), F32)
    for _ in range(min(SEL_TOPK, NS)):
        m = jnp.max(score, axis=0, keepdims=True)
        idx = jnp.min(jnp.where(score == m, jt, NS), axis=0, keepdims=True)
        hit = jt == idx
        sel = jnp.where(hit, 1.0, sel)
        score = jnp.where(hit, -3.0, score)
    pad = jnp.zeros((SEL_PAD_BLOCKS, QB), BF16)
    sel_ref[...] = jnp.concatenate([pad, sel.astype(BF16), pad], axis=0)


GROUP_LANES = REP * HEAD_DIM


def _q_spec():
    return pl.BlockSpec((None, Q_BLOCK, GROUP_LANES), lambda b, g, i: (b, i, g))


def nsa_compressed(p_nsa, kperm, vperm, knear, vnear, bias, bias_far):
    B, T, _ = p_nsa.shape
    G, NQ, N = KV_GROUPS, T // Q_BLOCK, HEAD_DIM
    NB = kperm.shape[2]
    nsp = NB // 4 + 2 * SEL_PAD_BLOCKS
    blk4 = lambda b, g, i: (b, g, 0, 0)
    return pl.pallas_call(
        _nsa_cmp_body,
        grid=(B, G, NQ),
        in_specs=[_q_spec(),
                  pl.BlockSpec((None, None, NB, N), blk4), pl.BlockSpec((None, None, NB, N), blk4),
                  pl.BlockSpec((None, None, knear.shape[2], N), blk4),
                  pl.BlockSpec((None, None, knear.shape[2], N), blk4),
                  pl.BlockSpec((None, REP, Q_BLOCK, CMP_NEAR), lambda b, g, i: (g, 0, 0, 0)),
                  pl.BlockSpec((None, REP, 1, LANE), lambda b, g, i: (g, 0, 0, 0))],
        out_specs=[_q_spec(),
                   pl.BlockSpec((None, None, None, nsp, Q_BLOCK), lambda b, g, i: (b, g, i, 0, 0))],
        out_shape=[jax.ShapeDtypeStruct((B, T, WIDTH), F32),
                   jax.ShapeDtypeStruct((B, G, NQ, nsp, Q_BLOCK), BF16)],
        compiler_params=_cparams(("parallel", "parallel", "parallel")),
        name="nsa_compressed",
    )(p_nsa, kperm, vperm, knear, vnear, bias, bias_far)


def _nsa_sel_body(q_ref, selt_ref, k_ref, vt_ref, biast_ref, bfar_ref, o_ref, sa_ref, sb_ref):
    QB = Q_BLOCK
    NQ = REP * QB
    blk = pl.program_id(2)
    qb = q_ref[...].astype(F32) * (HEAD_DIM ** -0.5 * LOG2E)
    qt = jnp.concatenate([qb[:, r * HEAD_DIM:(r + 1) * HEAD_DIM].T for r in range(REP)], axis=1).astype(BF16)
    near_chunk = blk + (SEL_PAD - SEL_SPAN) // V_CHUNK
    first = near_chunk * (V_CHUNK // SEL_BLOCK)
    pb_u = _iota((SEL_TILE_BLOCKS, QB), 0)
    zeros = jnp.zeros((AUG - HEAD_DIM - 2 * SEL_TILE_BLOCKS, NQ), BF16)
    bfar = bfar_ref[...]

    def far_scores(kt, s_ref):
        b0 = pl.multiple_of(kt * SEL_TILE_BLOCKS, SEL_TILE_BLOCKS)
        picked = selt_ref[pl.ds(b0, SEL_TILE_BLOCKS), :].astype(F32) > 0.5
        neg = jnp.where(picked & (b0 + pb_u < first), 0.0, NEG_INF).astype(BF16)
        q_far = jnp.concatenate([qt, jnp.concatenate([neg] * REP, axis=1), bfar, zeros], axis=0)
        r0 = pl.multiple_of(kt * SEL_TILE, SEL_TILE)
        s_ref[...] = _dot(k_ref[pl.ds(r0, SEL_TILE), :], q_far)

    def far_update(kt, s_ref, carry):
        m, l, acc = carry
        s = s_ref[...]
        m_new = jnp.maximum(m, jnp.max(s, axis=0, keepdims=True))
        alpha = jnp.exp2(m - m_new)
        p = jnp.exp2(s - m_new)
        l = alpha * l + jnp.sum(p, axis=0, keepdims=True)
        p = p.astype(BF16)
        acc = alpha * acc
        for j in range(SEL_TILE // V_CHUNK):
            acc = acc + _dot(vt_ref[kt * (SEL_TILE // V_CHUNK) + j], p[j * V_CHUNK:(j + 1) * V_CHUNK])
        return m_new, l, acc

    far_scores(0, sa_ref)

    row0 = pl.multiple_of(near_chunk * V_CHUNK, V_CHUNK)
    base = pl.multiple_of((first // SEL_TILE_BLOCKS) * SEL_TILE_BLOCKS, SEL_TILE_BLOCKS)
    rows = selt_ref[pl.ds(base, 2 * SEL_TILE_BLOCKS), :].astype(F32)
    picked = jnp.where(pb_u >= first - base, rows[0:SEL_TILE_BLOCKS], rows[SEL_TILE_BLOCKS:]) > 0.5
    neg = jnp.where(picked, 0.0, NEG_INF).astype(BF16)
    q_near = jnp.concatenate([qt, jnp.concatenate([neg] * REP, axis=1), jnp.zeros_like(bfar), zeros], axis=0)
    s = _dot(k_ref[pl.ds(row0, SEL_NEAR), :], q_near) + biast_ref[...]
    m = jnp.max(s, axis=0, keepdims=True)
    p = jnp.exp2(s - m)
    l = jnp.sum(p, axis=0, keepdims=True)
    p = p.astype(BF16)
    acc = jnp.zeros((HEAD_DIM, NQ), F32)
    for j in range(SEL_NEAR // V_CHUNK):
        acc = acc + _dot(vt_ref[near_chunk + j], p[j * V_CHUNK:(j + 1) * V_CHUNK])

    def far_pair(j, carry):
        far_scores(2 * j + 1, sb_ref)
        carry = far_update(2 * j, sa_ref, carry)
        far_scores(2 * j + 2, sa_ref)
        return far_update(2 * j + 1, sb_ref, carry)

    n_far = (first + SEL_TILE_BLOCKS - 1) // SEL_TILE_BLOCKS
    m, l, acc = lax.fori_loop(0, (n_far + 1) // 2, far_pair, (m, l, acc))
    out = acc / l
    o_ref[...] = jnp.concatenate([out[:, r * QB:(r + 1) * QB].T for r in range(REP)], axis=1)


def nsa_selected(p_nsa, selt, k_aug, vt, bias_t, bias_far):
    B, T, _ = p_nsa.shape
    G, NQ, N, W = KV_GROUPS, T // Q_BLOCK, HEAD_DIM, REP * Q_BLOCK
    nsp = selt.shape[3]
    rows = k_aug.shape[2]
    return pl.pallas_call(
        _nsa_sel_body,
        grid=(B, G, NQ),
        in_specs=[_q_spec(),
                  pl.BlockSpec((None, None, None, nsp, Q_BLOCK), lambda b, g, i: (b, g, i, 0, 0)),
                  pl.BlockSpec((None, None, rows, AUG), lambda b, g, i: (b, g, 0, 0)),
                  pl.BlockSpec((None, None, rows // V_CHUNK, N, V_CHUNK), lambda b, g, i: (b, g, 0, 0, 0)),
                  pl.BlockSpec((None, SEL_NEAR, W), lambda b, g, i: (g, 0, 0)),
                  pl.BlockSpec((None, SEL_TILE_BLOCKS, W), lambda b, g, i: (g, 0, 0))],
        out_specs=_q_spec(),
        out_shape=jax.ShapeDtypeStruct((B, T, WIDTH), F32),
        scratch_shapes=[pltpu.VMEM((SEL_TILE, W), F32)] * 2,
        compiler_params=_cparams(("parallel", "parallel", "parallel")),
        name="nsa_selected",
    )(p_nsa, selt, k_aug, vt, bias_t, bias_far)


def _band_body(*refs, pad, has_sink):
    if has_sink:
        q_ref, k_ref, v_ref, bias_ref, sink_ref, o_ref = refs
    else:
        q_ref, k_ref, v_ref, bias_ref, o_ref = refs
    QB = Q_BLOCK
    NQ = REP * QB
    width = pad + QB
    blk = pl.program_id(2)
    row0 = pl.multiple_of(blk * QB, QB)
    kb = k_ref[pl.ds(row0, width), :]
    vb = v_ref[pl.ds(row0, width), :]
    before_start = jnp.where(row0 + _iota((1, width), 1) >= pad, 0.0, NEG_INF)
    q = _heads_on_rows(q_ref[...]) * (HEAD_DIM ** -0.5)
    s = _dot_nt(q, kb) + bias_ref[...].reshape(NQ, width) + before_start
    m = jnp.max(s, axis=-1, keepdims=True)
    if has_sink:
        sink = _per_head_column(sink_ref)
        m = jnp.maximum(m, sink)
    p = jnp.exp(s - m)
    den = jnp.sum(p, axis=-1, keepdims=True)
    if has_sink:
        den = den + jnp.exp(sink - m)
    p = p * (1.0 / den)
    o_ref[...] = _heads_on_lanes(_dot(p.astype(BF16), vb))


def band_attention(p, k, v, bias, sink, pad):
    B, T, _ = p.shape
    G, NQ, N = KV_GROUPS, T // Q_BLOCK, HEAD_DIM
    rows = k.shape[2]
    width = pad + Q_BLOCK
    kv = pl.BlockSpec((None, None, rows, N), lambda b, g, i: (b, g, 0, 0))
    in_specs = [_q_spec(), kv, kv,
                pl.BlockSpec((None, REP, Q_BLOCK, width), lambda b, g, i: (g, 0, 0, 0))]
    args = [p, k, v, bias]
    if sink is not None:
        in_specs.append(pl.BlockSpec((None, REP, 1, LANE), lambda b, g, i: (g, 0, 0, 0)))
        args.append(sink)
    return pl.pallas_call(
        functools.partial(_band_body, pad=pad, has_sink=sink is not None),
        grid=(B, G, NQ),
        in_specs=in_specs,
        out_specs=_q_spec(),
        out_shape=jax.ShapeDtypeStruct((B, T, WIDTH), F32),
        compiler_params=_cparams(("parallel", "parallel", "parallel")),
        name="band_attention",
    )(*args)


def _merge_body(x_ref, pg_ref, hg_ref, orw_ref, oc_ref, os_ref, ow_ref, osw_ref, wb_ref, wo_ref, gn_ref, ex_ref,
                o_ref):
    pg = pg_ref[...]
    head_gates = jax.nn.sigmoid(hg_ref[...])
    ge = _dot(head_gates, ex_ref[...], HI)
    o_nsa = (ge[:, 0:WIDTH] * oc_ref[...] + ge[:, WIDTH:2 * WIDTH] * os_ref[...]
             + ge[:, 2 * WIDTH:3 * WIDTH] * ow_ref[...])
    merged = (jax.nn.sigmoid(pg[:, 0:D_MODEL]) * _dot(orw_ref[...].astype(BF16), wb_ref[0])
              + jax.nn.sigmoid(pg[:, D_MODEL:2 * D_MODEL]) * _dot(o_nsa.astype(BF16), wb_ref[1])
              + jax.nn.sigmoid(pg[:, 2 * D_MODEL:3 * D_MODEL]) * _dot(osw_ref[...].astype(BF16), wb_ref[2]))
    y = _dot(merged.astype(BF16), wo_ref[...])
    o_ref[...] = x_ref[...] + _rms(y, gn_ref[...])


def merge_out(x, pg, p_rw, o_rwkv, o_c, o_s, o_w, o_swa, w_branch, w_out, g_post, expand, tm):
    M, D = x.shape
    row = lambda i: (i, 0)
    wide = pl.BlockSpec((tm, WIDTH), row)
    return pl.pallas_call(
        _merge_body,
        grid=(M // tm,),
        in_specs=[pl.BlockSpec((tm, D), row), pl.BlockSpec((tm, 3 * D), row),
                  pl.BlockSpec((tm, LANE), lambda i: (i, RW_COLS // LANE)),
                  wide, wide, wide, wide, wide,
                  pl.BlockSpec((3, WIDTH, D), lambda i: (0, 0, 0)),
                  pl.BlockSpec((D, D), lambda i: (0, 0)),
                  pl.BlockSpec((1, D), lambda i: (0, 0)),
                  pl.BlockSpec((LANE, 3 * WIDTH), lambda i: (0, 0))],
        out_specs=pl.BlockSpec((tm, D), row),
        out_shape=jax.ShapeDtypeStruct((M, D), F32),
        compiler_params=_cparams(("parallel",)),
        name="merge_out",
    )(x, pg, p_rw, o_rwkv, o_c, o_s, o_w, o_swa, w_branch, w_out, g_post, expand)


def _ffn_body(x_ref, gpre_ref, wg_ref, wu_ref, wd_ref, gpost_ref, o_ref, h_ref, acc_ref):
    j = pl.program_id(1)

    @pl.when(j == 0)
    def _():
        h_ref[...] = _rms(x_ref[...], gpre_ref[...]).astype(BF16)
        acc_ref[...] = jnp.zeros_like(acc_ref)

    h = h_ref[...]
    gate = _dot(h, wg_ref[...])
    act = gate * jax.nn.sigmoid(gate) * _dot(h, wu_ref[...])
    acc_ref[...] += _dot(act.astype(BF16), wd_ref[...])

    @pl.when(j == pl.num_programs(1) - 1)
    def _():
        o_ref[...] = x_ref[...] + _rms(acc_ref[...], gpost_ref[...])


def ffn(x, g_pre, w_gate, w_up, w_down, g_post, tm, th):
    M, D = x.shape
    Hd = w_gate.shape[1]
    return pl.pallas_call(
        _ffn_body,
        grid=(M // tm, Hd // th),
        in_specs=[pl.BlockSpec((tm, D), lambda i, j: (i, 0)),
                  pl.BlockSpec((1, D), lambda i, j: (0, 0)),
                  pl.BlockSpec((D, th), lambda i, j: (0, j)),
                  pl.BlockSpec((D, th), lambda i, j: (0, j)),
                  pl.BlockSpec((th, D), lambda i, j: (j, 0)),
                  pl.BlockSpec((1, D), lambda i, j: (0, 0))],
        out_specs=pl.BlockSpec((tm, D), lambda i, j: (i, 0)),
        out_shape=jax.ShapeDtypeStruct((M, D), F32),
        scratch_shapes=[pltpu.VMEM((tm, D), BF16), pltpu.VMEM((tm, D), F32)],
        compiler_params=_cparams(("parallel", "arbitrary")),
        name="ffn",
    )(x, g_pre, w_gate, w_up, w_down, g_post)


def _pad_cols(w, n):
    return jnp.pad(w, ((0, 0), (0, n - w.shape[1])))


def _pad_rows(w, n):
    return jnp.pad(w, ((0, n - w.shape[0]), (0, 0)))


def _kv_groups(z, B, T, pad):
    z = z.reshape(B, T, KV_GROUPS, HEAD_DIM).transpose(0, 2, 1, 3)
    return jnp.pad(z, ((0, 0), (0, 0), (pad, 0), (0, 0)))


def _group_bias(b):
    return b.reshape(KV_GROUPS, REP, b.shape[1], b.shape[2])


def kernel(x, norm_pre_mix, norm_post_mix, norm_pre_ffn, norm_post_ffn, w_in, rwkv_mu, rwkv_w0, rwkv_w2, rwkv_a0, rwkv_a2, rwkv_g2, rwkv_k_k, rwkv_k_a, rwkv_r_k, rwkv_ln_w, rwkv_ln_b, nsa_pe_k, nsa_pe_v, nsa_ck_w1, nsa_ck_w2, nsa_cv_w1, nsa_cv_w2, swa_sinks, rel_bias, w_branch, w_out, ffn_w_gate, ffn_w_up, ffn_w_down):
    B, T, D = x.shape
    depth = w_in.shape[0]
    M = B * T
    NB = T // CMP_STRIDE
    NS = T // SEL_BLOCK
    ffn_hidden = ffn_w_gate.shape[2]

    rw_cols = 3 * WIDTH + W_LORA + A_LORA + G_LORA
    nsa_cols = WIDTH + 6 * KV_WIDTH + 3 * N_HEADS
    swa_cols = WIDTH + 2 * KV_WIDTH
    c_nsa = rw_cols
    c_swa = c_nsa + nsa_cols
    c_gate = c_swa + swa_cols
    c_zw = 3 * WIDTH

    ii = np.arange(Q_BLOCK)[:, None]
    d_swa = SWA_WINDOW + ii - np.arange(SWA_WINDOW + Q_BLOCK)[None, :]
    d_win = NSA_WINDOW + ii - np.arange(NSA_WINDOW + Q_BLOCK)[None, :]
    d_sel = SEL_SPAN + ii - np.arange(SEL_NEAR)[None, :]
    d_cmp = ii + (CMP_PAD * CMP_STRIDE - CMP_LEN + 1) - CMP_STRIDE * np.arange(CMP_NEAR)[None, :]
    bias_swa = bias_tile(rel_bias, d_swa, (d_swa >= 0) & (d_swa < SWA_WINDOW))
    bias_win = bias_tile(rel_bias, d_win, (d_win >= 0) & (d_win < NSA_WINDOW))
    bias_sel = bias_tile(rel_bias, d_sel, d_sel >= 0)
    bias_cmp = bias_tile(rel_bias, d_cmp, d_cmp >= 0)
    bias_swa = _group_bias(bias_swa[N_HEADS:])
    bias_win = _group_bias(bias_win[:N_HEADS])
    bias_sel = _group_bias(bias_sel[:N_HEADS])
    bias_cmp = _group_bias(bias_cmp[:N_HEADS])
    far = rel_bias[REL_BUCKETS - 1, :N_HEADS].reshape(KV_GROUPS, REP, 1, 1)
    bias_sel_t = bias_sel.transpose(0, 3, 1, 2).reshape(KV_GROUPS, SEL_NEAR, REP * Q_BLOCK) * LOG2E
    far2 = far * LOG2E
    far_hi = far2.astype(BF16)
    far_lo = (far2 - far_hi.astype(F32)).astype(BF16)
    far_rows = jnp.concatenate([far_hi, far_lo], axis=2)
    far_rows = jnp.broadcast_to(far_rows, (KV_GROUPS, REP, 2, Q_BLOCK)).transpose(0, 2, 1, 3)
    bfar_sel = jnp.pad(far_rows.reshape(KV_GROUPS, 2, REP * Q_BLOCK), ((0, 0), (0, SEL_TILE_BLOCKS - 2), (0, 0)))
    rows_pad = SEL_PAD + T
    blk_in_tile = (np.arange(rows_pad) // SEL_BLOCK) % SEL_TILE_BLOCKS
    ka = np.zeros((rows_pad, AUG - HEAD_DIM), np.float32)
    ka[np.arange(rows_pad), blk_in_tile] = 1.0
    ka[:, SEL_TILE_BLOCKS:SEL_TILE_BLOCKS + 2] = 1.0
    key_aug = jnp.asarray(ka, BF16)
    bfar_cmp = jnp.broadcast_to(far, (KV_GROUPS, REP, 1, LANE))
    ex = np.zeros((LANE, 3 * WIDTH), np.float32)
    for c in range(3 * N_HEADS):
        ex[c, c * HEAD_DIM:(c + 1) * HEAD_DIM] = 1.0
    expand = jnp.asarray(ex)

    xf = x.reshape(M, D)
    for l in range(depth):
        w = w_in[l]
        w_rw = jnp.concatenate([w[:, :c_zw], _pad_cols(w[:, c_zw:c_zw + W_LORA], LANE),
                                _pad_cols(w[:, c_zw + W_LORA:c_zw + W_LORA + A_LORA], LANE),
                                w[:, c_zw + W_LORA + A_LORA:rw_cols],
                                _pad_cols(w[:, c_swa - 3 * N_HEADS:c_swa], LANE)], axis=1).astype(BF16)
        w_nsa = w[:, c_nsa:c_nsa + WIDTH + 6 * KV_WIDTH].astype(BF16)
        w_swa = w[:, c_swa:c_gate].astype(BF16)
        w_gate = w[:, c_gate:].astype(BF16)
        g_pre = norm_pre_mix[l][None, :]
        p_rw = norm_matmul(xf, g_pre, w_rw, F32, 1024)
        p_nsa = norm_matmul(xf, g_pre, w_nsa, BF16, 1024)
        p_swa = norm_matmul(xf, g_pre, w_swa, BF16, 1024)
        p_gate = norm_matmul(xf, g_pre, w_gate, F32, 512)

        mu = rwkv_mu[l]
        mu_p = jnp.concatenate([mu[:c_zw], jnp.pad(mu[c_zw:c_zw + W_LORA], (0, LANE - W_LORA)),
                                jnp.pad(mu[c_zw + W_LORA:c_zw + W_LORA + A_LORA], (0, LANE - A_LORA)),
                                mu[c_zw + W_LORA + A_LORA:]])[None, :]
        r, k, v, a, lw, g = rwkv_prep(p_rw.reshape(B, T, RW_COLS + LANE), mu_p, rwkv_w0[l][None, :],
                                      _pad_rows(rwkv_w2[l], LANE), rwkv_a0[l][None, :],
                                      _pad_rows(rwkv_a2[l], LANE), rwkv_g2[l], 512)
        row = lambda z: z.reshape(1, WIDTH)
        o_rwkv = rwkv_recurrence(r, k, v, a, lw, g, row(rwkv_k_k[l]), row(rwkv_k_a[l]), row(rwkv_r_k[l]),
                                 row(rwkv_ln_w[l]), row(rwkv_ln_b[l]), 512).reshape(M, WIDTH)

        pn = p_nsa.reshape(B, T, WIDTH + 6 * KV_WIDTH)
        seg = lambda n: pn[..., WIDTH + n * KV_WIDTH:WIDTH + (n + 1) * KV_WIDTH]

        def strides(z):
            z = z.reshape(B, NB, CMP_STRIDE, KV_GROUPS, HEAD_DIM).transpose(0, 3, 1, 2, 4)
            return z.reshape(B, KV_GROUPS, NB, CMP_STRIDE * HEAD_DIM)

        def pe_rows(pe):
            return jnp.broadcast_to(pe.reshape(1, CMP_LEN * HEAD_DIM), (8, CMP_LEN * HEAD_DIM)).astype(BF16)

        k_cmp = nsa_compress(strides(seg(0)), pe_rows(nsa_pe_k[l]), nsa_ck_w1[l].astype(BF16),
                             nsa_ck_w2[l].astype(BF16))
        v_cmp = nsa_compress(strides(seg(1)), pe_rows(nsa_pe_v[l]), nsa_cv_w1[l].astype(BF16),
                             nsa_cv_w2[l].astype(BF16))

        def panels(z):
            z = z.reshape(B, KV_GROUPS, NS, 4, HEAD_DIM).transpose(0, 1, 3, 2, 4)
            return z.reshape(B, KV_GROUPS, NB, HEAD_DIM).astype(BF16)

        near = lambda z: jnp.pad(z, ((0, 0), (0, 0), (CMP_PAD, 8), (0, 0)))
        o_c, sel_t = nsa_compressed(pn, panels(k_cmp), panels(v_cmp), near(k_cmp), near(v_cmp), bias_cmp, bfar_cmp)
        k_aug = jnp.concatenate([_kv_groups(seg(2), B, T, SEL_PAD),
                                 jnp.broadcast_to(key_aug, (B, KV_GROUPS) + key_aug.shape)], axis=-1)
        v_t = _kv_groups(seg(3), B, T, SEL_PAD).reshape(B, KV_GROUPS, -1, V_CHUNK, HEAD_DIM).swapaxes(-1, -2)
        o_s = nsa_selected(pn, sel_t, k_aug, v_t, bias_sel_t, bfar_sel)
        o_w = band_attention(pn, _kv_groups(seg(4), B, T, NSA_WINDOW), _kv_groups(seg(5), B, T, NSA_WINDOW),
                             bias_win, None, NSA_WINDOW)

        ps = p_swa.reshape(B, T, swa_cols)
        sink = jnp.broadcast_to(swa_sinks[l].reshape(KV_GROUPS, REP, 1, 1), (KV_GROUPS, REP, 1, LANE))
        o_swa = band_attention(ps,
                               _kv_groups(ps[..., WIDTH:WIDTH + KV_WIDTH], B, T, SWA_WINDOW),
                               _kv_groups(ps[..., WIDTH + KV_WIDTH:], B, T, SWA_WINDOW),
                               bias_swa, sink, SWA_WINDOW)

        flat = lambda o: o.reshape(M, WIDTH)
        xf = merge_out(xf, p_gate, p_rw, o_rwkv, flat(o_c), flat(o_s), flat(o_w),
                       flat(o_swa), w_branch[l].astype(BF16), w_out[l].astype(BF16),
                       norm_post_mix[l][None, :], expand, 256)

        xf = ffn(xf, norm_pre_ffn[l][None, :], ffn_w_gate[l].astype(BF16), ffn_w_up[l].astype(BF16),
                 ffn_w_down[l].astype(BF16), norm_post_ffn[l][None, :], 512, ffn_hidden // 2)
    return xf.reshape(B, T, D)
```

```python
import functools
import math

import numpy as np
import jax
import jax.numpy as jnp
from jax import lax
from jax.experimental import pallas as pl
from jax.experimental.pallas import tpu as pltpu

F32 = jnp.float32
BF16 = jnp.bfloat16
HI = lax.Precision.HIGHEST

D_MODEL = 1024
HEAD_DIM = 64
N_HEADS = 8
KV_GROUPS = 2
REP = N_HEADS // KV_GROUPS
WIDTH = N_HEADS * HEAD_DIM
KV_WIDTH = KV_GROUPS * HEAD_DIM
W_LORA, A_LORA, G_LORA = 64, 64, 128
RWKV_LN_EPS = 64e-5
CMP_STRIDE = 16
CMP_LEN = 32
CMP_HIDDEN = 256
SEL_BLOCK = 64
SEL_TOPK = 16
SEL_FORCE_SCORE = 1e4
NSA_WINDOW = 512
Q_BLOCK = 128
SWA_WINDOW = 128
REL_BUCKETS = 32
REL_MAX_DIST = 1024
NORM_EPS = 1e-6
NEG_INF = -1e30
LOG2E = math.log2(math.e)
LANE = 128
RWKV_CHUNK = 64
SEL_NEAR = 1024
SEL_SPAN = SEL_NEAR - Q_BLOCK
SEL_PAD = SEL_NEAR
SEL_TILE = 1024
SEL_TILE_BLOCKS = SEL_TILE // SEL_BLOCK
SEL_PAD_BLOCKS = SEL_PAD // SEL_BLOCK
AUG = 2 * HEAD_DIM
V_CHUNK = 128
BAND_BLOCKS = 2
CMP_NEAR = 64
CMP_PAD = CMP_NEAR - Q_BLOCK // CMP_STRIDE
VMEM_LIMIT = 56 * 1024 * 1024


def _cparams(sem):
    return pltpu.CompilerParams(dimension_semantics=sem, vmem_limit_bytes=VMEM_LIMIT)


def _dot(a, b, precision=None):
    return jnp.dot(a, b, preferred_element_type=F32, precision=precision)


def _dot_nt(a, b, precision=None):
    return lax.dot_general(a, b, (((1,), (1,)), ((), ())), preferred_element_type=F32, precision=precision)


def _rms(x, g):
    return x * lax.rsqrt(jnp.mean(x * x, axis=-1, keepdims=True) + NORM_EPS) * g


def _iota(shape, dim):
    return lax.broadcasted_iota(jnp.int32, shape, dim)


def _norm_mm_body(x_ref, g_ref, w_ref, o_ref):
    h = _rms(x_ref[...], g_ref[...]).astype(BF16)
    o_ref[...] = _dot(h, w_ref[...]).astype(o_ref.dtype)


def norm_matmul(x, g, w, out_dtype, tm):
    M, D = x.shape
    N = w.shape[1]
    return pl.pallas_call(
        _norm_mm_body,
        grid=(M // tm,),
        in_specs=[pl.BlockSpec((tm, D), lambda i: (i, 0)),
                  pl.BlockSpec((1, D), lambda i: (0, 0)),
                  pl.BlockSpec((D, N), lambda i: (0, 0))],
        out_specs=pl.BlockSpec((tm, N), lambda i: (i, 0)),
        out_shape=jax.ShapeDtypeStruct((M, N), out_dtype),
        compiler_params=_cparams(("parallel",)),
        name="norm_matmul",
    )(x, g, w)


RW_COLS = 3 * WIDTH + 3 * LANE


def _rwkv_prep_body(p_ref, mu_ref, w0_ref, w2_ref, a0_ref, a2_ref, g2_ref,
                    r_o, k_o, v_o, a_o, lw_o, cum_o, g_o, carry_ref):
    tm = p_ref.shape[0]

    @pl.when(pl.program_id(1) == 0)
    def _():
        carry_ref[...] = jnp.zeros_like(carry_ref)

    p = p_ref[...]
    row = _iota(p.shape, 0)
    prev = jnp.where(row == 0, carry_ref[0:1, :], pltpu.roll(p, 1, axis=0))
    carry_ref[0:1, :] = p[tm - 1:tm, :]
    ps = p + (prev - p) * mu_ref[...]
    r_o[...] = ps[:, 0:WIDTH]
    k_o[...] = ps[:, WIDTH:2 * WIDTH]
    v_o[...] = ps[:, 2 * WIDTH:3 * WIDTH]
    zw = ps[:, 3 * WIDTH:3 * WIDTH + LANE]
    za = ps[:, 3 * WIDTH + LANE:3 * WIDTH + 2 * LANE]
    zg = ps[:, 3 * WIDTH + 2 * LANE:3 * WIDTH + 3 * LANE]
    z = -(w0_ref[...] + _dot(jnp.tanh(zw), w2_ref[...], HI))
    softplus = jnp.maximum(z, 0.0) + jnp.log(1.0 + jnp.exp(-jnp.abs(z)))
    lw = -jnp.exp(-softplus - 0.5)
    lw_o[...] = lw
    C = RWKV_CHUNK
    tri = (_iota((C, C), 0) >= _iota((C, C), 1)).astype(F32)
    for c in range(tm // C):
        cum_o[c * C:(c + 1) * C, :] = _dot(tri, lw[c * C:(c + 1) * C, :], HI)
    a_o[...] = jax.nn.sigmoid(a0_ref[...] + _dot(za, a2_ref[...], HI))
    g_o[...] = _dot(jax.nn.sigmoid(zg), g2_ref[...], HI)


def rwkv_prep(p, mu, w0, w2, a0, a2, g2, tm):
    B, T, _ = p.shape
    row = lambda b, i: (b, i, 0)
    fixed = lambda b, i: (0, 0)
    out = jax.ShapeDtypeStruct((B, T, WIDTH), F32)
    return pl.pallas_call(
        _rwkv_prep_body,
        grid=(B, T // tm),
        in_specs=[pl.BlockSpec((None, tm, RW_COLS), row),
                  pl.BlockSpec((1, RW_COLS), fixed),
                  pl.BlockSpec((1, WIDTH), fixed), pl.BlockSpec((LANE, WIDTH), fixed),
                  pl.BlockSpec((1, WIDTH), fixed), pl.BlockSpec((LANE, WIDTH), fixed),
                  pl.BlockSpec((LANE, WIDTH), fixed)],
        out_specs=[pl.BlockSpec((None, tm, WIDTH), row)] * 7,
        out_shape=[out] * 7,
        scratch_shapes=[pltpu.VMEM((8, RW_COLS), F32)],
        compiler_params=_cparams(("parallel", "arbitrary")),
        name="rwkv_prep",
    )(p, mu, w0, w2, a0, a2, g2)


RWKV_HEADS_PER_STEP = 4


def _bbmm(a, b):
    return lax.dot_general(a.astype(BF16), b.astype(BF16), (((2,), (1,)), ((0,), (0,))),
                           preferred_element_type=F32)


def _bbmm_nt(a, b):
    return lax.dot_general(a.astype(BF16), b.astype(BF16), (((2,), (2,)), ((0,), (0,))),
                           preferred_element_type=F32)


def _bbmm_tn(a, b):
    return lax.dot_general(a.astype(BF16), b.astype(BF16), (((1,), (1,)), ((0,), (0,))),
                           preferred_element_type=F32)


def _rwkv_rec_body(r_ref, k_ref, v_ref, a_ref, lw_ref, cum_ref, g_ref, kk_ref, ka_ref, rk_ref, lnw_ref, lnb_ref,
                   o_ref, s_ref):
    C = RWKV_CHUNK
    N = HEAD_DIM
    assert C == N
    tc = r_ref.shape[0]
    nc = tc // C
    hb = r_ref.shape[1] // N

    @pl.when(pl.program_id(2) == 0)
    def _():
        s_ref[...] = jnp.zeros_like(s_ref)

    def chunks(ref):
        x = ref[...]
        return jnp.concatenate([x[:, h * N:(h + 1) * N].reshape(nc, C, N) for h in range(hb)], axis=0)

    def per_head(ref):
        x = ref[...]
        return jnp.concatenate([jnp.broadcast_to(x[:, h * N:(h + 1) * N][None], (nc, 1, N))
                                for h in range(hb)], axis=0)

    row = _iota((C, C), 0)
    col = _iota((C, C), 1)
    incl = (row >= col)[None]
    strict = (row > col)[None]
    eye = (row == col)[None]
    nb = hb * nc
    r, k, v, a, lw = chunks(r_ref), chunks(k_ref), chunks(v_ref), chunks(a_ref), chunks(lw_ref)
    kk = k * per_head(kk_ref)
    kk = kk / jnp.maximum(jnp.sqrt(jnp.sum(kk * kk, axis=-1, keepdims=True)), 1e-12)
    k2 = k * (1.0 + (a - 1.0) * per_head(ka_ref))
    cum = chunks(cum_ref)
    p_incl = jnp.exp(cum)
    p_inv = jnp.exp(-cum)
    a_t = -kk * jnp.exp(cum - lw)
    r_t = r * p_incl
    b_t = kk * a * p_inv
    k_t = k2 * p_inv
    gram = _bbmm_nt(jnp.concatenate([a_t, r_t], axis=1), jnp.concatenate([b_t, k_t], axis=1))
    l_ab = jnp.where(strict, gram[:, 0:C, 0:C], 0.0)
    l_ak = jnp.where(strict, gram[:, 0:C, C:2 * C], 0.0)
    m_rb = jnp.where(incl, gram[:, C:2 * C, 0:C], 0.0)
    m_rk = jnp.where(incl, gram[:, C:2 * C, C:2 * C], 0.0)
    inv = jnp.where(eye, 1.0, l_ab)
    lp = l_ab
    for _ in range(5):
        lp = _bbmm(lp, lp)
        inv = inv + _bbmm(lp, inv)
    t_a = _bbmm(inv, a_t)
    t_v = _bbmm(inv, _bbmm(l_ak, v))
    p_end = p_incl[:, C - 1:C, :]
    bp = b_t * p_end
    kp = k_t * p_end
    ry = (r_t + _bbmm(m_rb, t_a)).reshape(hb, nc, C, N)
    yc = (_bbmm(m_rb, t_v) + _bbmm(m_rk, v)).reshape(hb, nc, C, N)
    am = (jnp.where(eye, jnp.broadcast_to(p_end, (nb, C, C)), 0.0) + _bbmm_tn(bp, t_a)).reshape(hb, nc, C, C)
    gm = (_bbmm_tn(bp, t_v) + _bbmm_tn(kp, v)).reshape(hb, nc, C, N)
    h = s_ref[...]
    ry_am = jnp.concatenate([ry, am], axis=2)
    ys = []
    for c in range(nc):
        both = _bbmm(ry_am[:, c], h)
        ys.append(both[:, 0:C] + yc[:, c])
        h = both[:, C:2 * C] + gm[:, c]
    s_ref[...] = h
    y = jnp.stack(ys, axis=1).reshape(nb, C, N)
    mean = jnp.mean(y, axis=-1, keepdims=True)
    var = jnp.mean(jnp.square(y - mean), axis=-1, keepdims=True)
    yn = (y - mean) * lax.rsqrt(var + RWKV_LN_EPS) * per_head(lnw_ref) + per_head(lnb_ref)
    bonus = jnp.sum(r * k2 * per_head(rk_ref), axis=-1, keepdims=True) * v
    out = (yn + bonus) * chunks(g_ref)
    o_ref[...] = jnp.concatenate([out[h * nc:(h + 1) * nc].reshape(tc, N) for h in range(hb)], axis=1)


def rwkv_recurrence(r, k, v, a, lw, cum, g, k_k, k_a, r_k, ln_w, ln_b, tc):
    B, T, W = r.shape
    slab = RWKV_HEADS_PER_STEP * HEAD_DIM
    seq = pl.BlockSpec((None, tc, slab), lambda b, h, i: (b, i, h))
    par = pl.BlockSpec((1, slab), lambda b, h, i: (0, h))
    return pl.pallas_call(
        _rwkv_rec_body,
        grid=(B, W // slab, T // tc),
        in_specs=[seq] * 7 + [par] * 5,
        out_specs=seq,
        out_shape=jax.ShapeDtypeStruct((B, T, W), F32),
        scratch_shapes=[pltpu.VMEM((RWKV_HEADS_PER_STEP, HEAD_DIM, HEAD_DIM), F32)],
        compiler_params=_cparams(("parallel", "parallel", "arbitrary")),
        name="rwkv_recurrence",
    )(r, k, v, a, lw, cum, g, k_k, k_a, r_k, ln_w, ln_b)


def _compress_body(z_ref, pe_ref, w1_ref, w2_ref, o_ref):
    nb = z_ref.shape[0]
    half = z_ref.shape[1]
    z = z_ref[...]
    first = _dot(z, w1_ref[0:half, :])
    second = _dot(z, w1_ref[half:2 * half, :])
    pe_term = _dot(pe_ref[...], w1_ref[...])[0:1, :]
    hidden = first + pltpu.roll(second, nb - 1, axis=0) + pe_term
    out = _dot(jax.nn.gelu(hidden).astype(BF16), w2_ref[...])
    rows = _iota(out.shape, 0)
    o_ref[...] = jnp.where(rows < nb - 1, out, 0.0)


def nsa_compress(z, pe, w1, w2):
    B, G, NB, HALF = z.shape
    return pl.pallas_call(
        _compress_body,
        grid=(B, G),
        in_specs=[pl.BlockSpec((None, None, NB, HALF), lambda b, g: (b, g, 0, 0)),
                  pl.BlockSpec((8, 2 * HALF), lambda b, g: (0, 0)),
                  pl.BlockSpec((2 * HALF, CMP_HIDDEN), lambda b, g: (0, 0)),
                  pl.BlockSpec((CMP_HIDDEN, HEAD_DIM), lambda b, g: (0, 0))],
        out_specs=pl.BlockSpec((None, None, NB, HEAD_DIM), lambda b, g: (b, g, 0, 0)),
        out_shape=jax.ShapeDtypeStruct((B, G, NB, HEAD_DIM), F32),
        compiler_params=_cparams(("parallel", "parallel")),
        name="nsa_compress",
    )(z, pe, w1, w2)


def _t5_bucket_np(dist):
    n = np.maximum(dist, 0)
    max_exact = REL_BUCKETS // 2
    nf = np.maximum(n, 1).astype(np.float64)
    large = max_exact + (np.log(nf / max_exact) / math.log(REL_MAX_DIST / max_exact)
                         * (REL_BUCKETS - max_exact)).astype(np.int32)
    large = np.minimum(large, REL_BUCKETS - 1)
    return np.where(n < max_exact, n, large).astype(np.int32)


def _bias_body(tbl_ref, bk_ref, o_ref):
    h = pl.program_id(0)
    bk = bk_ref[...]
    acc = jnp.full(bk.shape, NEG_INF, F32)
    for b in range(REL_BUCKETS):
        acc = jnp.where(bk == b, tbl_ref[b, h], acc)
    o_ref[...] = acc


def bias_tile(table, dist_np, valid_np):
    P, Q = dist_np.shape
    H = table.shape[1]
    buckets = jnp.asarray(np.where(valid_np, _t5_bucket_np(dist_np), -1).astype(np.int32))
    return pl.pallas_call(
        _bias_body,
        grid=(H,),
        in_specs=[pl.BlockSpec(memory_space=pltpu.SMEM),
                  pl.BlockSpec((P, Q), lambda h: (0, 0))],
        out_specs=pl.BlockSpec((None, P, Q), lambda h: (h, 0, 0)),
        out_shape=jax.ShapeDtypeStruct((H, P, Q), F32),
        compiler_params=_cparams(("arbitrary",)),
        name="bias_tile",
    )(table, buckets)


def _heads_on_rows(qb):
    return jnp.concatenate([qb[:, r * HEAD_DIM:(r + 1) * HEAD_DIM] for r in range(REP)], axis=0)


def _heads_on_lanes(o):
    return jnp.concatenate([o[r * Q_BLOCK:(r + 1) * Q_BLOCK] for r in range(REP)], axis=1)


def _per_head_column(ref):
    return jnp.concatenate([jnp.broadcast_to(ref[r][:, 0:1], (Q_BLOCK, 1)) for r in range(REP)], axis=0)


def _nsa_cmp_body(q_ref, kp_ref, vp_ref, kn_ref, vn_ref, bias_ref, bfar_ref, o_ref, sel_ref):
    QB = Q_BLOCK
    NQ = REP * QB
    NB = kp_ref.shape[0]
    NS = NB // 4
    ns_shift = NS.bit_length() - 1
    blk = pl.program_id(2)
    start = pl.multiple_of(blk * (QB // CMP_STRIDE), 8)
    kn = kn_ref[pl.ds(start, CMP_NEAR), :].astype(BF16)
    vn = vn_ref[pl.ds(start, CMP_NEAR), :].astype(BF16)
    first_near = blk * (QB // CMP_STRIDE) - CMP_PAD
    lane = _iota((1, NB), 1)
    c_far = 4 * (lane & (NS - 1)) + (lane >> ns_shift)
    far_row = jnp.where(c_far < first_near, 0.0, NEG_INF)
    near_row = jnp.where(first_near + _iota((1, CMP_NEAR), 1) >= 0, 0.0, NEG_INF)
    q = _heads_on_rows(q_ref[...]) * (HEAD_DIM ** -0.5)
    s_f = _dot_nt(q, kp_ref[...]) + _per_head_column(bfar_ref) + far_row
    s_n = _dot_nt(q, kn) + bias_ref[...].reshape(NQ, CMP_NEAR) + near_row
    m = jnp.maximum(jnp.max(s_f, axis=-1, keepdims=True), jnp.max(s_n, axis=-1, keepdims=True))
    m = jnp.maximum(m, 0.1 * NEG_INF)
    p_f = jnp.exp(s_f - m)
    p_n = jnp.exp(s_n - m)
    den = jnp.sum(p_f, axis=-1, keepdims=True) + jnp.sum(p_n, axis=-1, keepdims=True)
    inv = 1.0 / jnp.where(den > 0, den, 1.0)
    p_f = p_f * inv
    p_n = p_n * inv
    o_ref[...] = _heads_on_lanes(_dot(p_f.astype(BF16), vp_ref[...]) + _dot(p_n.astype(BF16), vn))
    imp_far = p_f[0:QB]
    imp_near = p_n[0:QB]
    for r in range(1, REP):
        imp_far = imp_far + p_f[r * QB:(r + 1) * QB]
        imp_near = imp_near + p_n[r * QB:(r + 1) * QB]
    panel = [imp_far[:, m * NS:(m + 1) * NS] for m in range(4)]
    j = _iota((QB, NS), 1)
    prev3 = jnp.where(j == 0, 0.0, pltpu.roll(panel[3], 1, axis=1))
    imp = prev3 + 2.0 * panel[0] + 2.0 * panel[1] + 2.0 * panel[2] + panel[3]
    c_abs = first_near + _iota((CMP_NEAR, NS), 0)
    off = c_abs + 1 - 4 * _iota((CMP_NEAR, NS), 1)
    w_near = jnp.where((off == 0) | (off == 4), 1.0, jnp.where((off >= 1) & (off <= 3), 2.0, 0.0))
    imp = imp + _dot(imp_near, w_near, HI)
    cur = 2 * blk + (_iota((QB, NS), 0) >= SEL_BLOCK).astype(jnp.int32)
    forced = (j == 0) | (j == cur) | (j == cur - 1)
    score = jnp.where(forced, SEL_FORCE_SCORE, jnp.where(j <= cur, imp, -1.0))
    score = score.T
    jt = _iota((NS, QB), 0)
    sel = jnp.zeros((NS, QB), F32)
    for _ in range(min(SEL_TOPK, NS)):
        m = jnp.max(score, axis=0, keepdims=True)
        idx = jnp.min(jnp.where(score == m, jt, NS), axis=0, keepdims=True)
        hit = jt == idx
        sel = jnp.where(hit, 1.0, sel)
        score = jnp.where(hit, -3.0, score)
    pad = jnp.zeros((SEL_PAD_BLOCKS, QB), BF16)
    sel_ref[...] = jnp.concatenate([pad, sel.astype(BF16), pad], axis=0)


GROUP_LANES = REP * HEAD_DIM


def _q_spec(blocks=1):
    return pl.BlockSpec((None, blocks * Q_BLOCK, GROUP_LANES), lambda b, g, i: (b, i, g))


def nsa_compressed(p_nsa, kperm, vperm, knear, vnear, bias, bias_far):
    B, T, _ = p_nsa.shape
    G, NQ, N = KV_GROUPS, T // Q_BLOCK, HEAD_DIM
    NB = kperm.shape[2]
    nsp = NB // 4 + 2 * SEL_PAD_BLOCKS
    blk4 = lambda b, g, i: (b, g, 0, 0)
    return pl.pallas_call(
        _nsa_cmp_body,
        grid=(B, G, NQ),
        in_specs=[_q_spec(),
                  pl.BlockSpec((None, None, NB, N), blk4), pl.BlockSpec((None, None, NB, N), blk4),
                  pl.BlockSpec((None, None, knear.shape[2], N), blk4),
                  pl.BlockSpec((None, None, knear.shape[2], N), blk4),
                  pl.BlockSpec((None, REP, Q_BLOCK, CMP_NEAR), lambda b, g, i: (g, 0, 0, 0)),
                  pl.BlockSpec((None, REP, 1, LANE), lambda b, g, i: (g, 0, 0, 0))],
        out_specs=[_q_spec(),
                   pl.BlockSpec((None, None, None, nsp, Q_BLOCK), lambda b, g, i: (b, g, i, 0, 0))],
        out_shape=[jax.ShapeDtypeStruct((B, T, WIDTH), F32),
                   jax.ShapeDtypeStruct((B, G, NQ, nsp, Q_BLOCK), BF16)],
        compiler_params=_cparams(("parallel", "parallel", "parallel")),
        name="nsa_compressed",
    )(p_nsa, kperm, vperm, knear, vnear, bias, bias_far)


def _nsa_sel_body(q_ref, selt_ref, k_ref, vt_ref, biast_ref, bfar_ref, o_ref, sa_ref, sb_ref):
    QB = Q_BLOCK
    NQ = REP * QB
    blk = pl.program_id(2)
    qb = q_ref[...].astype(F32) * (HEAD_DIM ** -0.5 * LOG2E)
    qt = jnp.concatenate([qb[:, r * HEAD_DIM:(r + 1) * HEAD_DIM].T for r in range(REP)], axis=1).astype(BF16)
    near_chunk = blk + (SEL_PAD - SEL_SPAN) // V_CHUNK
    first = near_chunk * (V_CHUNK // SEL_BLOCK)
    pb_u = _iota((SEL_TILE_BLOCKS, QB), 0)
    zeros = jnp.zeros((AUG - HEAD_DIM - 2 * SEL_TILE_BLOCKS, NQ), BF16)
    bfar = bfar_ref[...]

    def far_scores(kt, s_ref):
        b0 = pl.multiple_of(kt * SEL_TILE_BLOCKS, SEL_TILE_BLOCKS)
        picked = selt_ref[pl.ds(b0, SEL_TILE_BLOCKS), :].astype(F32) > 0.5
        neg = jnp.where(picked & (b0 + pb_u < first), 0.0, NEG_INF).astype(BF16)
        q_far = jnp.concatenate([qt, jnp.concatenate([neg] * REP, axis=1), bfar, zeros], axis=0)
        r0 = pl.multiple_of(kt * SEL_TILE, SEL_TILE)
        s_ref[...] = _dot(k_ref[pl.ds(r0, SEL_TILE), :], q_far)

    def weighted_values(chunk0, p):
        out = jnp.zeros((HEAD_DIM, NQ), F32)
        for j in range(0, SEL_TILE // V_CHUNK, 2):
            vt = jnp.concatenate([vt_ref[chunk0 + j], vt_ref[chunk0 + j + 1]], axis=1)
            out = out + _dot(vt, p[j * V_CHUNK:(j + 2) * V_CHUNK])
        return out

    def far_update(kt, s_ref, carry):
        m, l, acc = carry
        s = s_ref[...]
        m_new = jnp.maximum(m, jnp.max(s, axis=0, keepdims=True))
        alpha = jnp.exp2(m - m_new)
        p = jnp.exp2(s - m_new)
        l = alpha * l + jnp.sum(p, axis=0, keepdims=True)
        p = p.astype(BF16)
        return m_new, l, alpha * acc + weighted_values(kt * (SEL_TILE // V_CHUNK), p)

    far_scores(0, sa_ref)

    row0 = pl.multiple_of(near_chunk * V_CHUNK, V_CHUNK)
    base = pl.multiple_of((first // SEL_TILE_BLOCKS) * SEL_TILE_BLOCKS, SEL_TILE_BLOCKS)
    rows = selt_ref[pl.ds(base, 2 * SEL_TILE_BLOCKS), :].astype(F32)
    picked = jnp.where(pb_u >= first - base, rows[0:SEL_TILE_BLOCKS], rows[SEL_TILE_BLOCKS:]) > 0.5
    neg = jnp.where(picked, 0.0, NEG_INF).astype(BF16)
    q_near = jnp.concatenate([qt, jnp.concatenate([neg] * REP, axis=1), jnp.zeros_like(bfar), zeros], axis=0)
    s = _dot(k_ref[pl.ds(row0, SEL_NEAR), :], q_near) + biast_ref[...]
    m = jnp.max(s, axis=0, keepdims=True)
    p = jnp.exp2(s - m)
    l = jnp.sum(p, axis=0, keepdims=True)
    acc = weighted_values(near_chunk, p.astype(BF16))

    def far_pair(j, carry):
        far_scores(2 * j + 1, sb_ref)
        carry = far_update(2 * j, sa_ref, carry)
        far_scores(2 * j + 2, sa_ref)
        return far_update(2 * j + 1, sb_ref, carry)

    n_far = (first + SEL_TILE_BLOCKS - 1) // SEL_TILE_BLOCKS
    m, l, acc = lax.fori_loop(0, (n_far + 1) // 2, far_pair, (m, l, acc))
    out = acc / l
    o_ref[...] = jnp.concatenate([out[:, r * QB:(r + 1) * QB].T for r in range(REP)], axis=1)


def nsa_selected(p_nsa, selt, k_aug, vt, bias_t, bias_far):
    B, T, _ = p_nsa.shape
    G, NQ, N, W = KV_GROUPS, T // Q_BLOCK, HEAD_DIM, REP * Q_BLOCK
    nsp = selt.shape[3]
    rows = k_aug.shape[2]
    return pl.pallas_call(
        _nsa_sel_body,
        grid=(B, G, NQ),
        in_specs=[_q_spec(),
                  pl.BlockSpec((None, None, None, nsp, Q_BLOCK), lambda b, g, i: (b, g, i, 0, 0)),
                  pl.BlockSpec((None, None, rows, AUG), lambda b, g, i: (b, g, 0, 0)),
                  pl.BlockSpec((None, None, rows // V_CHUNK, N, V_CHUNK), lambda b, g, i: (b, g, 0, 0, 0)),
                  pl.BlockSpec((None, SEL_NEAR, W), lambda b, g, i: (g, 0, 0)),
                  pl.BlockSpec((None, SEL_TILE_BLOCKS, W), lambda b, g, i: (g, 0, 0))],
        out_specs=_q_spec(),
        out_shape=jax.ShapeDtypeStruct((B, T, WIDTH), F32),
        scratch_shapes=[pltpu.VMEM((SEL_TILE, W), F32)] * 2,
        compiler_params=_cparams(("parallel", "parallel", "parallel")),
        name="nsa_selected",
    )(p_nsa, selt, k_aug, vt, bias_t, bias_far)


def _band_body(*refs, pad, has_sink):
    if has_sink:
        q_ref, k_ref, v_ref, bias_ref, sink_ref, o_ref = refs
    else:
        q_ref, k_ref, v_ref, bias_ref, o_ref = refs
    QB = Q_BLOCK
    NQ = REP * QB
    width = pad + QB
    bias = bias_ref[...].reshape(NQ, width)
    if has_sink:
        sink = sink_ref[...].reshape(NQ, LANE)
    for u in range(BAND_BLOCKS):
        blk = pl.program_id(2) * BAND_BLOCKS + u
        row0 = pl.multiple_of(blk * QB, QB)
        kb = k_ref[pl.ds(row0, width), :]
        vb = v_ref[pl.ds(row0, width), :]
        before_start = jnp.where(row0 + _iota((1, width), 1) >= pad, 0.0, NEG_INF)
        q = _heads_on_rows(q_ref[u * QB:(u + 1) * QB, :]) * (HEAD_DIM ** -0.5)
        s = _dot_nt(q, kb) + bias + before_start
        m = jnp.max(s, axis=-1, keepdims=True)
        if has_sink:
            m = jnp.maximum(m, jnp.max(sink, axis=-1, keepdims=True))
        p = jnp.exp(s - m)
        den = jnp.sum(p, axis=-1, keepdims=True)
        if has_sink:
            den = den + jnp.sum(jnp.exp(sink - m), axis=-1, keepdims=True)
        p = p * (1.0 / den)
        o_ref[u * QB:(u + 1) * QB, :] = _heads_on_lanes(_dot(p.astype(BF16), vb))


def band_attention(p, k, v, bias, sink, pad):
    B, T, _ = p.shape
    G, NQ, N = KV_GROUPS, T // Q_BLOCK, HEAD_DIM
    rows = k.shape[2]
    width = pad + Q_BLOCK
    kv = pl.BlockSpec((None, None, rows, N), lambda b, g, i: (b, g, 0, 0))
    in_specs = [_q_spec(BAND_BLOCKS), kv, kv,
                pl.BlockSpec((None, REP, Q_BLOCK, width), lambda b, g, i: (g, 0, 0, 0))]
    args = [p, k, v, bias]
    if sink is not None:
        in_specs.append(pl.BlockSpec((None, REP, Q_BLOCK, LANE), lambda b, g, i: (g, 0, 0, 0)))
        args.append(sink)
    return pl.pallas_call(
        functools.partial(_band_body, pad=pad, has_sink=sink is not None),
        grid=(B, G, NQ // BAND_BLOCKS),
        in_specs=in_specs,
        out_specs=_q_spec(BAND_BLOCKS),
        out_shape=jax.ShapeDtypeStruct((B, T, WIDTH), F32),
        compiler_params=_cparams(("parallel", "parallel", "parallel")),
        name="band_attention",
    )(*args)


def _merge_body(x_ref, pg_ref, hg_ref, orw_ref, oc_ref, os_ref, ow_ref, osw_ref, wb_ref, wo_ref, gn_ref, ex_ref,
                o_ref):
    pg = pg_ref[...]
    head_gates = jax.nn.sigmoid(hg_ref[...])
    gates_hi = head_gates.astype(BF16)
    gates_lo = (head_gates - gates_hi.astype(F32)).astype(BF16)
    ge = _dot(gates_hi, ex_ref[...]) + _dot(gates_lo, ex_ref[...])
    o_nsa = (ge[:, 0:WIDTH] * oc_ref[...] + ge[:, WIDTH:2 * WIDTH] * os_ref[...]
             + ge[:, 2 * WIDTH:3 * WIDTH] * ow_ref[...])
    merged = (jax.nn.sigmoid(pg[:, 0:D_MODEL]) * _dot(orw_ref[...].astype(BF16), wb_ref[0])
              + jax.nn.sigmoid(pg[:, D_MODEL:2 * D_MODEL]) * _dot(o_nsa.astype(BF16), wb_ref[1])
              + jax.nn.sigmoid(pg[:, 2 * D_MODEL:3 * D_MODEL]) * _dot(osw_ref[...].astype(BF16), wb_ref[2]))
    y = _dot(merged.astype(BF16), wo_ref[...])
    o_ref[...] = x_ref[...] + _rms(y, gn_ref[...])


def merge_out(x, pg, p_rw, o_rwkv, o_c, o_s, o_w, o_swa, w_branch, w_out, g_post, expand, tm):
    M, D = x.shape
    row = lambda i: (i, 0)
    wide = pl.BlockSpec((tm, WIDTH), row)
    return pl.pallas_call(
        _merge_body,
        grid=(M // tm,),
        in_specs=[pl.BlockSpec((tm, D), row), pl.BlockSpec((tm, 3 * D), row),
                  pl.BlockSpec((tm, LANE), lambda i: (i, RW_COLS // LANE)),
                  wide, wide, wide, wide, wide,
                  pl.BlockSpec((3, WIDTH, D), lambda i: (0, 0, 0)),
                  pl.BlockSpec((D, D), lambda i: (0, 0)),
                  pl.BlockSpec((1, D), lambda i: (0, 0)),
                  pl.BlockSpec((LANE, 3 * WIDTH), lambda i: (0, 0))],
        out_specs=pl.BlockSpec((tm, D), row),
        out_shape=jax.ShapeDtypeStruct((M, D), F32),
        compiler_params=_cparams(("parallel",)),
        name="merge_out",
    )(x, pg, p_rw, o_rwkv, o_c, o_s, o_w, o_swa, w_branch, w_out, g_post, expand)


def _ffn_body(x_ref, gpre_ref, wg_ref, wu_ref, wd_ref, gpost_ref, o_ref, h_ref, acc_ref):
    j = pl.program_id(1)

    @pl.when(j == 0)
    def _():
        h_ref[...] = _rms(x_ref[...], gpre_ref[...]).astype(BF16)
        acc_ref[...] = jnp.zeros_like(acc_ref)

    h = h_ref[...]
    gate = _dot(h, wg_ref[...])
    act = gate * jax.nn.sigmoid(gate) * _dot(h, wu_ref[...])
    acc_ref[...] += _dot(act.astype(BF16), wd_ref[...])

    @pl.when(j == pl.num_programs(1) - 1)
    def _():
        o_ref[...] = x_ref[...] + _rms(acc_ref[...], gpost_ref[...])


def ffn(x, g_pre, w_gate, w_up, w_down, g_post, tm, th):
    M, D = x.shape
    Hd = w_gate.shape[1]
    return pl.pallas_call(
        _ffn_body,
        grid=(M // tm, Hd // th),
        in_specs=[pl.BlockSpec((tm, D), lambda i, j: (i, 0)),
                  pl.BlockSpec((1, D), lambda i, j: (0, 0)),
                  pl.BlockSpec((D, th), lambda i, j: (0, j)),
                  pl.BlockSpec((D, th), lambda i, j: (0, j)),
                  pl.BlockSpec((th, D), lambda i, j: (j, 0)),
                  pl.BlockSpec((1, D), lambda i, j: (0, 0))],
        out_specs=pl.BlockSpec((tm, D), lambda i, j: (i, 0)),
        out_shape=jax.ShapeDtypeStruct((M, D), F32),
        scratch_shapes=[pltpu.VMEM((tm, D), BF16), pltpu.VMEM((tm, D), F32)],
        compiler_params=_cparams(("parallel", "arbitrary")),
        name="ffn",
    )(x, g_pre, w_gate, w_up, w_down, g_post)


def _pad_cols(w, n):
    return jnp.pad(w, ((0, 0), (0, n - w.shape[1])))


def _pad_rows(w, n):
    return jnp.pad(w, ((0, n - w.shape[0]), (0, 0)))


def _kv_groups(z, B, T, pad):
    z = z.reshape(B, T, KV_GROUPS, HEAD_DIM).transpose(0, 2, 1, 3)
    return jnp.pad(z, ((0, 0), (0, 0), (pad, 0), (0, 0)))


def _group_bias(b):
    return b.reshape(KV_GROUPS, REP, b.shape[1], b.shape[2])


def kernel(x, norm_pre_mix, norm_post_mix, norm_pre_ffn, norm_post_ffn, w_in, rwkv_mu, rwkv_w0, rwkv_w2, rwkv_a0, rwkv_a2, rwkv_g2, rwkv_k_k, rwkv_k_a, rwkv_r_k, rwkv_ln_w, rwkv_ln_b, nsa_pe_k, nsa_pe_v, nsa_ck_w1, nsa_ck_w2, nsa_cv_w1, nsa_cv_w2, swa_sinks, rel_bias, w_branch, w_out, ffn_w_gate, ffn_w_up, ffn_w_down):
    B, T, D = x.shape
    depth = w_in.shape[0]
    M = B * T
    NB = T // CMP_STRIDE
    NS = T // SEL_BLOCK
    ffn_hidden = ffn_w_gate.shape[2]

    rw_cols = 3 * WIDTH + W_LORA + A_LORA + G_LORA
    nsa_cols = WIDTH + 6 * KV_WIDTH + 3 * N_HEADS
    swa_cols = WIDTH + 2 * KV_WIDTH
    c_nsa = rw_cols
    c_swa = c_nsa + nsa_cols
    c_gate = c_swa + swa_cols
    c_zw = 3 * WIDTH

    ii = np.arange(Q_BLOCK)[:, None]
    d_swa = SWA_WINDOW + ii - np.arange(SWA_WINDOW + Q_BLOCK)[None, :]
    d_win = NSA_WINDOW + ii - np.arange(NSA_WINDOW + Q_BLOCK)[None, :]
    d_sel = SEL_SPAN + ii - np.arange(SEL_NEAR)[None, :]
    d_cmp = ii + (CMP_PAD * CMP_STRIDE - CMP_LEN + 1) - CMP_STRIDE * np.arange(CMP_NEAR)[None, :]
    bias_swa = bias_tile(rel_bias, d_swa, (d_swa >= 0) & (d_swa < SWA_WINDOW))
    bias_win = bias_tile(rel_bias, d_win, (d_win >= 0) & (d_win < NSA_WINDOW))
    bias_sel = bias_tile(rel_bias, d_sel, d_sel >= 0)
    bias_cmp = bias_tile(rel_bias, d_cmp, d_cmp >= 0)
    bias_swa = _group_bias(bias_swa[N_HEADS:])
    bias_win = _group_bias(bias_win[:N_HEADS])
    bias_sel = _group_bias(bias_sel[:N_HEADS])
    bias_cmp = _group_bias(bias_cmp[:N_HEADS])
    far = rel_bias[REL_BUCKETS - 1, :N_HEADS].reshape(KV_GROUPS, REP, 1, 1)
    bias_sel_t = bias_sel.transpose(0, 3, 1, 2).reshape(KV_GROUPS, SEL_NEAR, REP * Q_BLOCK) * LOG2E
    far2 = far * LOG2E
    far_hi = far2.astype(BF16)
    far_lo = (far2 - far_hi.astype(F32)).astype(BF16)
    far_rows = jnp.concatenate([far_hi, far_lo], axis=2)
    far_rows = jnp.broadcast_to(far_rows, (KV_GROUPS, REP, 2, Q_BLOCK)).transpose(0, 2, 1, 3)
    bfar_sel = jnp.pad(far_rows.reshape(KV_GROUPS, 2, REP * Q_BLOCK), ((0, 0), (0, SEL_TILE_BLOCKS - 2), (0, 0)))
    rows_pad = SEL_PAD + T
    blk_in_tile = (np.arange(rows_pad) // SEL_BLOCK) % SEL_TILE_BLOCKS
    ka = np.zeros((rows_pad, AUG - HEAD_DIM), np.float32)
    ka[np.arange(rows_pad), blk_in_tile] = 1.0
    ka[:, SEL_TILE_BLOCKS:SEL_TILE_BLOCKS + 2] = 1.0
    key_aug = jnp.asarray(ka, BF16)
    bfar_cmp = jnp.broadcast_to(far, (KV_GROUPS, REP, 1, LANE))
    ex = np.zeros((LANE, 3 * WIDTH), np.float32)
    for c in range(3 * N_HEADS):
        ex[c, c * HEAD_DIM:(c + 1) * HEAD_DIM] = 1.0
    expand = jnp.asarray(ex, BF16)

    xf = x.reshape(M, D)
    for l in range(depth):
        w = w_in[l]
        w_rw = jnp.concatenate([w[:, :c_zw], _pad_cols(w[:, c_zw:c_zw + W_LORA], LANE),
                                _pad_cols(w[:, c_zw + W_LORA:c_zw + W_LORA + A_LORA], LANE),
                                w[:, c_zw + W_LORA + A_LORA:rw_cols],
                                _pad_cols(w[:, c_swa - 3 * N_HEADS:c_swa], LANE)], axis=1).astype(BF16)
        w_nsa = w[:, c_nsa:c_nsa + WIDTH + 6 * KV_WIDTH].astype(BF16)
        w_swa = w[:, c_swa:c_gate].astype(BF16)
        w_gate = w[:, c_gate:].astype(BF16)
        g_pre = norm_pre_mix[l][None, :]
        p_rw = norm_matmul(xf, g_pre, w_rw, F32, 1024)
        p_nsa = norm_matmul(xf, g_pre, w_nsa, BF16, 1024)
        p_swa = norm_matmul(xf, g_pre, w_swa, BF16, 1024)
        p_gate = norm_matmul(xf, g_pre, w_gate, F32, 512)

        mu = rwkv_mu[l]
        mu_p = jnp.concatenate([mu[:c_zw], jnp.pad(mu[c_zw:c_zw + W_LORA], (0, LANE - W_LORA)),
                                jnp.pad(mu[c_zw + W_LORA:c_zw + W_LORA + A_LORA], (0, LANE - A_LORA)),
                                mu[c_zw + W_LORA + A_LORA:]])[None, :]
        r, k, v, a, lw, cum, g = rwkv_prep(p_rw.reshape(B, T, RW_COLS + LANE), mu_p, rwkv_w0[l][None, :],
                                      _pad_rows(rwkv_w2[l], LANE), rwkv_a0[l][None, :],
                                      _pad_rows(rwkv_a2[l], LANE), rwkv_g2[l], 512)
        row = lambda z: z.reshape(1, WIDTH)
        o_rwkv = rwkv_recurrence(r, k, v, a, lw, cum, g, row(rwkv_k_k[l]), row(rwkv_k_a[l]), row(rwkv_r_k[l]),
                                 row(rwkv_ln_w[l]), row(rwkv_ln_b[l]), 512).reshape(M, WIDTH)

        pn = p_nsa.reshape(B, T, WIDTH + 6 * KV_WIDTH)
        seg = lambda n: pn[..., WIDTH + n * KV_WIDTH:WIDTH + (n + 1) * KV_WIDTH]

        def strides(z):
            z = z.reshape(B, NB, CMP_STRIDE, KV_GROUPS, HEAD_DIM).transpose(0, 3, 1, 2, 4)
            return z.reshape(B, KV_GROUPS, NB, CMP_STRIDE * HEAD_DIM)

        def pe_rows(pe):
            return jnp.broadcast_to(pe.reshape(1, CMP_LEN * HEAD_DIM), (8, CMP_LEN * HEAD_DIM)).astype(BF16)

        k_cmp = nsa_compress(strides(seg(0)), pe_rows(nsa_pe_k[l]), nsa_ck_w1[l].astype(BF16),
                             nsa_ck_w2[l].astype(BF16))
        v_cmp = nsa_compress(strides(seg(1)), pe_rows(nsa_pe_v[l]), nsa_cv_w1[l].astype(BF16),
                             nsa_cv_w2[l].astype(BF16))

        def panels(z):
            z = z.reshape(B, KV_GROUPS, NS, 4, HEAD_DIM).transpose(0, 1, 3, 2, 4)
            return z.reshape(B, KV_GROUPS, NB, HEAD_DIM).astype(BF16)

        near = lambda z: jnp.pad(z, ((0, 0), (0, 0), (CMP_PAD, 8), (0, 0)))
        o_c, sel_t = nsa_compressed(pn, panels(k_cmp), panels(v_cmp), near(k_cmp), near(v_cmp), bias_cmp, bfar_cmp)
        k_aug = jnp.concatenate([_kv_groups(seg(2), B, T, SEL_PAD),
                                 jnp.broadcast_to(key_aug, (B, KV_GROUPS) + key_aug.shape)], axis=-1)
        v_t = _kv_groups(seg(3), B, T, SEL_PAD).reshape(B, KV_GROUPS, -1, V_CHUNK, HEAD_DIM).swapaxes(-1, -2)
        o_s = nsa_selected(pn, sel_t, k_aug, v_t, bias_sel_t, bfar_sel)
        o_w = band_attention(pn, _kv_groups(seg(4), B, T, NSA_WINDOW), _kv_groups(seg(5), B, T, NSA_WINDOW),
                             bias_win, None, NSA_WINDOW)

        ps = p_swa.reshape(B, T, swa_cols)
        sink = jnp.where(np.arange(LANE) == 0, swa_sinks[l].reshape(KV_GROUPS, REP, 1, 1), NEG_INF)
        sink = jnp.broadcast_to(sink, (KV_GROUPS, REP, Q_BLOCK, LANE))
        o_swa = band_attention(ps,
                               _kv_groups(ps[..., WIDTH:WIDTH + KV_WIDTH], B, T, SWA_WINDOW),
                               _kv_groups(ps[..., WIDTH + KV_WIDTH:], B, T, SWA_WINDOW),
                               bias_swa, sink, SWA_WINDOW)

        flat = lambda o: o.reshape(M, WIDTH)
        xf = merge_out(xf, p_gate, p_rw, o_rwkv, flat(o_c), flat(o_s), flat(o_w),
                       flat(o_swa), w_branch[l].astype(BF16), w_out[l].astype(BF16),
                       norm_post_mix[l][None, :], expand, 256)

        xf = ffn(xf, norm_pre_ffn[l][None, :], ffn_w_gate[l].astype(BF16), ffn_w_up[l].astype(BF16),
                 ffn_w_down[l].astype(BF16), norm_post_ffn[l][None, :], 512, ffn_hidden // 2)
    return xf.reshape(B, T, D)
```

```python
import functools
import math

import numpy as np
import jax
import jax.numpy as jnp
from jax import lax
from jax.experimental import pallas as pl
from jax.experimental.pallas import tpu as pltpu

F32 = jnp.float32
BF16 = jnp.bfloat16
HI = lax.Precision.HIGHEST

D_MODEL = 1024
HEAD_DIM = 64
N_HEADS = 8
KV_GROUPS = 2
REP = N_HEADS // KV_GROUPS
WIDTH = N_HEADS * HEAD_DIM
KV_WIDTH = KV_GROUPS * HEAD_DIM
W_LORA, A_LORA, G_LORA = 64, 64, 128
RWKV_LN_EPS = 64e-5
CMP_STRIDE = 16
CMP_LEN = 32
CMP_HIDDEN = 256
SEL_BLOCK = 64
SEL_TOPK = 16
SEL_FORCE_SCORE = 1e4
NSA_WINDOW = 512
Q_BLOCK = 128
SWA_WINDOW = 128
REL_BUCKETS = 32
REL_MAX_DIST = 1024
NORM_EPS = 1e-6
NEG_INF = -1e30
LOG2E = math.log2(math.e)
LANE = 128
RWKV_CHUNK = 64
SEL_NEAR = 1024
SEL_SPAN = SEL_NEAR - Q_BLOCK
SEL_PAD = SEL_NEAR
SEL_TILE = 1024
SEL_TILE_BLOCKS = SEL_TILE // SEL_BLOCK
SEL_PAD_BLOCKS = SEL_PAD // SEL_BLOCK
AUG = 2 * HEAD_DIM
V_CHUNK = 128
BAND_BLOCKS = 2
CMP_BLOCKS = 2
CMP_NEAR = 64
CMP_PAD = CMP_NEAR - Q_BLOCK // CMP_STRIDE
VMEM_LIMIT = 56 * 1024 * 1024


def _cparams(sem):
    return pltpu.CompilerParams(dimension_semantics=sem, vmem_limit_bytes=VMEM_LIMIT)


def _dot(a, b, precision=None):
    return jnp.dot(a, b, preferred_element_type=F32, precision=precision)


def _dot_nt(a, b, precision=None):
    return lax.dot_general(a, b, (((1,), (1,)), ((), ())), preferred_element_type=F32, precision=precision)


def _rms(x, g):
    return x * lax.rsqrt(jnp.mean(x * x, axis=-1, keepdims=True) + NORM_EPS) * g


def _iota(shape, dim):
    return lax.broadcasted_iota(jnp.int32, shape, dim)


def _norm_mm_body(x_ref, g_ref, w_ref, o_ref):
    h = _rms(x_ref[...], g_ref[...]).astype(BF16)
    o_ref[...] = _dot(h, w_ref[...]).astype(o_ref.dtype)


def norm_matmul(x, g, w, out_dtype, tm):
    M, D = x.shape
    N = w.shape[1]
    return pl.pallas_call(
        _norm_mm_body,
        grid=(M // tm,),
        in_specs=[pl.BlockSpec((tm, D), lambda i: (i, 0)),
                  pl.BlockSpec((1, D), lambda i: (0, 0)),
                  pl.BlockSpec((D, N), lambda i: (0, 0))],
        out_specs=pl.BlockSpec((tm, N), lambda i: (i, 0)),
        out_shape=jax.ShapeDtypeStruct((M, N), out_dtype),
        compiler_params=_cparams(("parallel",)),
        name="norm_matmul",
    )(x, g, w)


RW_COLS = 3 * WIDTH + 3 * LANE


def _rwkv_prep_body(p_ref, mu_ref, w0_ref, w2_ref, a0_ref, a2_ref, g2_ref,
                    r_o, k_o, v_o, a_o, lw_o, cum_o, g_o, carry_ref):
    tm = p_ref.shape[0]

    @pl.when(pl.program_id(1) == 0)
    def _():
        carry_ref[...] = jnp.zeros_like(carry_ref)

    p = p_ref[...]
    row = _iota(p.shape, 0)
    prev = jnp.where(row == 0, carry_ref[0:1, :], pltpu.roll(p, 1, axis=0))
    carry_ref[0:1, :] = p[tm - 1:tm, :]
    ps = p + (prev - p) * mu_ref[...]
    r_o[...] = ps[:, 0:WIDTH]
    k_o[...] = ps[:, WIDTH:2 * WIDTH]
    v_o[...] = ps[:, 2 * WIDTH:3 * WIDTH]
    zw = ps[:, 3 * WIDTH:3 * WIDTH + LANE]
    za = ps[:, 3 * WIDTH + LANE:3 * WIDTH + 2 * LANE]
    zg = ps[:, 3 * WIDTH + 2 * LANE:3 * WIDTH + 3 * LANE]
    z = -(w0_ref[...] + _dot(jnp.tanh(zw), w2_ref[...], HI))
    softplus = jnp.maximum(z, 0.0) + jnp.log(1.0 + jnp.exp(-jnp.abs(z)))
    lw = -jnp.exp(-softplus - 0.5)
    lw_o[...] = lw
    C = RWKV_CHUNK
    tri = (_iota((C, C), 0) >= _iota((C, C), 1)).astype(F32)
    for c in range(tm // C):
        cum_o[c * C:(c + 1) * C, :] = _dot(tri, lw[c * C:(c + 1) * C, :], HI)
    a_o[...] = jax.nn.sigmoid(a0_ref[...] + _dot(za, a2_ref[...], HI))
    g_o[...] = _dot(jax.nn.sigmoid(zg), g2_ref[...], HI)


def rwkv_prep(p, mu, w0, w2, a0, a2, g2, tm):
    B, T, _ = p.shape
    row = lambda b, i: (b, i, 0)
    fixed = lambda b, i: (0, 0)
    out = jax.ShapeDtypeStruct((B, T, WIDTH), F32)
    return pl.pallas_call(
        _rwkv_prep_body,
        grid=(B, T // tm),
        in_specs=[pl.BlockSpec((None, tm, RW_COLS), row),
                  pl.BlockSpec((1, RW_COLS), fixed),
                  pl.BlockSpec((1, WIDTH), fixed), pl.BlockSpec((LANE, WIDTH), fixed),
                  pl.BlockSpec((1, WIDTH), fixed), pl.BlockSpec((LANE, WIDTH), fixed),
                  pl.BlockSpec((LANE, WIDTH), fixed)],
        out_specs=[pl.BlockSpec((None, tm, WIDTH), row)] * 7,
        out_shape=[out] * 7,
        scratch_shapes=[pltpu.VMEM((8, RW_COLS), F32)],
        compiler_params=_cparams(("parallel", "arbitrary")),
        name="rwkv_prep",
    )(p, mu, w0, w2, a0, a2, g2)


RWKV_HEADS_PER_STEP = 4


def _bbmm(a, b):
    return lax.dot_general(a.astype(BF16), b.astype(BF16), (((2,), (1,)), ((0,), (0,))),
                           preferred_element_type=F32)


def _bbmm_nt(a, b):
    return lax.dot_general(a.astype(BF16), b.astype(BF16), (((2,), (2,)), ((0,), (0,))),
                           preferred_element_type=F32)


def _bbmm_tn(a, b):
    return lax.dot_general(a.astype(BF16), b.astype(BF16), (((1,), (1,)), ((0,), (0,))),
                           preferred_element_type=F32)


def _rwkv_rec_body(r_ref, k_ref, v_ref, a_ref, lw_ref, cum_ref, g_ref, kk_ref, ka_ref, rk_ref, lnw_ref, lnb_ref,
                   o_ref, s_ref):
    C = RWKV_CHUNK
    N = HEAD_DIM
    assert C == N
    tc = r_ref.shape[0]
    nc = tc // C
    hb = r_ref.shape[1] // N

    @pl.when(pl.program_id(2) == 0)
    def _():
        s_ref[...] = jnp.zeros_like(s_ref)

    def chunks(ref):
        x = ref[...]
        return jnp.concatenate([x[:, h * N:(h + 1) * N].reshape(nc, C, N) for h in range(hb)], axis=0)

    def per_head(ref):
        x = ref[...]
        return jnp.concatenate([jnp.broadcast_to(x[:, h * N:(h + 1) * N][None], (nc, 1, N))
                                for h in range(hb)], axis=0)

    row = _iota((C, C), 0)
    col = _iota((C, C), 1)
    incl = (row >= col)[None]
    strict = (row > col)[None]
    eye = (row == col)[None]
    nb = hb * nc
    r, k, v, a, lw = chunks(r_ref), chunks(k_ref), chunks(v_ref), chunks(a_ref), chunks(lw_ref)
    ones = jnp.ones((nb, N, N), BF16)

    def lane_sum(x):
        hi = x.astype(BF16)
        return _bbmm(hi, ones) + _bbmm(x - hi.astype(F32), ones)

    kk = k * per_head(kk_ref)
    kk = kk / jnp.maximum(jnp.sqrt(lane_sum(kk * kk)), 1e-12)
    k2 = k * (1.0 + (a - 1.0) * per_head(ka_ref))
    cum = chunks(cum_ref)
    p_incl = jnp.exp(cum)
    p_inv = jnp.exp(-cum)
    a_t = -kk * jnp.exp(cum - lw)
    r_t = r * p_incl
    b_t = kk * a * p_inv
    k_t = k2 * p_inv
    gram = _bbmm_nt(jnp.concatenate([a_t, r_t], axis=1), jnp.concatenate([b_t, k_t], axis=1))
    l_ab = jnp.where(strict, gram[:, 0:C, 0:C], 0.0)
    l_ak = jnp.where(strict, gram[:, 0:C, C:2 * C], 0.0)
    m_rb = jnp.where(incl, gram[:, C:2 * C, 0:C], 0.0)
    m_rk = jnp.where(incl, gram[:, C:2 * C, C:2 * C], 0.0)
    inv = jnp.where(eye, 1.0, l_ab)
    lp = l_ab
    for _ in range(5):
        lp = _bbmm(lp, lp)
        inv = inv + _bbmm(lp, inv)
    t_a = _bbmm(inv, a_t)
    t_v = _bbmm(inv, _bbmm(l_ak, v))
    p_end = p_incl[:, C - 1:C, :]
    bp = b_t * p_end
    kp = k_t * p_end
    ry = (r_t + _bbmm(m_rb, t_a)).reshape(hb, nc, C, N)
    yc = (_bbmm(m_rb, t_v) + _bbmm(m_rk, v)).reshape(hb, nc, C, N)
    am = (jnp.where(eye, jnp.broadcast_to(p_end, (nb, C, C)), 0.0) + _bbmm_tn(bp, t_a)).reshape(hb, nc, C, C)
    gm = (_bbmm_tn(bp, t_v) + _bbmm_tn(kp, v)).reshape(hb, nc, C, N)
    h = s_ref[...]
    ry_am = jnp.concatenate([ry, am], axis=2)
    ys = []
    for c in range(nc):
        both = _bbmm(ry_am[:, c], h)
        ys.append(both[:, 0:C] + yc[:, c])
        h = both[:, C:2 * C] + gm[:, c]
    s_ref[...] = h
    y = jnp.stack(ys, axis=1).reshape(nb, C, N)
    mean = lane_sum(y) * (1.0 / N)
    var = lane_sum(jnp.square(y - mean)) * (1.0 / N)
    yn = (y - mean) * lax.rsqrt(var + RWKV_LN_EPS) * per_head(lnw_ref) + per_head(lnb_ref)
    bonus = lane_sum(r * k2 * per_head(rk_ref)) * v
    out = (yn + bonus) * chunks(g_ref)
    o_ref[...] = jnp.concatenate([out[h * nc:(h + 1) * nc].reshape(tc, N) for h in range(hb)], axis=1)


def rwkv_recurrence(r, k, v, a, lw, cum, g, k_k, k_a, r_k, ln_w, ln_b, tc):
    B, T, W = r.shape
    slab = RWKV_HEADS_PER_STEP * HEAD_DIM
    seq = pl.BlockSpec((None, tc, slab), lambda b, h, i: (b, i, h))
    par = pl.BlockSpec((1, slab), lambda b, h, i: (0, h))
    return pl.pallas_call(
        _rwkv_rec_body,
        grid=(B, W // slab, T // tc),
        in_specs=[seq] * 7 + [par] * 5,
        out_specs=seq,
        out_shape=jax.ShapeDtypeStruct((B, T, W), F32),
        scratch_shapes=[pltpu.VMEM((RWKV_HEADS_PER_STEP, HEAD_DIM, HEAD_DIM), F32)],
        compiler_params=_cparams(("parallel", "parallel", "arbitrary")),
        name="rwkv_recurrence",
    )(r, k, v, a, lw, cum, g, k_k, k_a, r_k, ln_w, ln_b)


def _compress_body(z_ref, pe_ref, w1_ref, w2_ref, o_ref):
    nb = z_ref.shape[0]
    half = z_ref.shape[1]
    z = z_ref[...]
    first = _dot(z, w1_ref[0:half, :])
    second = _dot(z, w1_ref[half:2 * half, :])
    pe_term = _dot(pe_ref[...], w1_ref[...])[0:1, :]
    hidden = first + pltpu.roll(second, nb - 1, axis=0) + pe_term
    out = _dot(jax.nn.gelu(hidden).astype(BF16), w2_ref[...])
    rows = _iota(out.shape, 0)
    o_ref[...] = jnp.where(rows < nb - 1, out, 0.0)


def nsa_compress(z, pe, w1, w2):
    B, G, NB, HALF = z.shape
    return pl.pallas_call(
        _compress_body,
        grid=(B, G),
        in_specs=[pl.BlockSpec((None, None, NB, HALF), lambda b, g: (b, g, 0, 0)),
                  pl.BlockSpec((8, 2 * HALF), lambda b, g: (0, 0)),
                  pl.BlockSpec((2 * HALF, CMP_HIDDEN), lambda b, g: (0, 0)),
                  pl.BlockSpec((CMP_HIDDEN, HEAD_DIM), lambda b, g: (0, 0))],
        out_specs=pl.BlockSpec((None, None, NB, HEAD_DIM), lambda b, g: (b, g, 0, 0)),
        out_shape=jax.ShapeDtypeStruct((B, G, NB, HEAD_DIM), F32),
        compiler_params=_cparams(("parallel", "parallel")),
        name="nsa_compress",
    )(z, pe, w1, w2)


def _t5_bucket_np(dist):
    n = np.maximum(dist, 0)
    max_exact = REL_BUCKETS // 2
    nf = np.maximum(n, 1).astype(np.float64)
    large = max_exact + (np.log(nf / max_exact) / math.log(REL_MAX_DIST / max_exact)
                         * (REL_BUCKETS - max_exact)).astype(np.int32)
    large = np.minimum(large, REL_BUCKETS - 1)
    return np.where(n < max_exact, n, large).astype(np.int32)


def _bias_body(tbl_ref, bk_ref, o_ref):
    h = pl.program_id(0)
    bk = bk_ref[...]
    acc = jnp.full(bk.shape, NEG_INF, F32)
    for b in range(REL_BUCKETS):
        acc = jnp.where(bk == b, tbl_ref[b, h], acc)
    o_ref[...] = acc


def bias_tile(table, dist_np, valid_np):
    P, Q = dist_np.shape
    H = table.shape[1]
    buckets = jnp.asarray(np.where(valid_np, _t5_bucket_np(dist_np), -1).astype(np.int32))
    return pl.pallas_call(
        _bias_body,
        grid=(H,),
        in_specs=[pl.BlockSpec(memory_space=pltpu.SMEM),
                  pl.BlockSpec((P, Q), lambda h: (0, 0))],
        out_specs=pl.BlockSpec((None, P, Q), lambda h: (h, 0, 0)),
        out_shape=jax.ShapeDtypeStruct((H, P, Q), F32),
        compiler_params=_cparams(("arbitrary",)),
        name="bias_tile",
    )(table, buckets)


def _heads_on_rows(qb):
    return jnp.concatenate([qb[:, r * HEAD_DIM:(r + 1) * HEAD_DIM] for r in range(REP)], axis=0)


def _heads_on_lanes(o):
    return jnp.concatenate([o[r * Q_BLOCK:(r + 1) * Q_BLOCK] for r in range(REP)], axis=1)


def _per_head_column(ref):
    return jnp.concatenate([jnp.broadcast_to(ref[r][:, 0:1], (Q_BLOCK, 1)) for r in range(REP)], axis=0)


def _nsa_cmp_body(*refs):
    for u in range(CMP_BLOCKS):
        _nsa_cmp_block(u, *refs)


def _nsa_cmp_block(u, q_ref, kp_ref, vp_ref, kn_ref, vn_ref, bias_ref, bfar_ref, o_ref, sel_ref):
    QB = Q_BLOCK
    NQ = REP * QB
    NB = kp_ref.shape[0]
    NS = NB // 4
    ns_shift = NS.bit_length() - 1
    blk = pl.program_id(2) * CMP_BLOCKS + u
    rows = slice(u * QB, (u + 1) * QB)
    start = pl.multiple_of(blk * (QB // CMP_STRIDE), 8)
    kn = kn_ref[pl.ds(start, CMP_NEAR), :].astype(BF16)
    vn = vn_ref[pl.ds(start, CMP_NEAR), :].astype(BF16)
    first_near = blk * (QB // CMP_STRIDE) - CMP_PAD
    lane = _iota((1, NB), 1)
    c_far = 4 * (lane & (NS - 1)) + (lane >> ns_shift)
    far_row = jnp.where(c_far < first_near, 0.0, NEG_INF)
    near_row = jnp.where(first_near + _iota((1, CMP_NEAR), 1) >= 0, 0.0, NEG_INF)
    q = _heads_on_rows(q_ref[rows, :]) * (HEAD_DIM ** -0.5)
    s_f = _dot_nt(q, kp_ref[...]) + _per_head_column(bfar_ref) + far_row
    s_n = _dot_nt(q, kn) + bias_ref[...].reshape(NQ, CMP_NEAR) + near_row
    m = jnp.maximum(jnp.max(s_f, axis=-1, keepdims=True), jnp.max(s_n, axis=-1, keepdims=True))
    m = jnp.maximum(m, 0.1 * NEG_INF)
    p_f = jnp.exp(s_f - m)
    p_n = jnp.exp(s_n - m)
    den = jnp.sum(p_f, axis=-1, keepdims=True) + jnp.sum(p_n, axis=-1, keepdims=True)
    inv = 1.0 / jnp.where(den > 0, den, 1.0)
    p_f = p_f * inv
    p_n = p_n * inv
    o_ref[rows, :] = _heads_on_lanes(_dot(p_f.astype(BF16), vp_ref[...]) + _dot(p_n.astype(BF16), vn))
    imp_far = p_f[0:QB]
    imp_near = p_n[0:QB]
    for r in range(1, REP):
        imp_far = imp_far + p_f[r * QB:(r + 1) * QB]
        imp_near = imp_near + p_n[r * QB:(r + 1) * QB]
    panel = [imp_far[:, m * NS:(m + 1) * NS] for m in range(4)]
    j = _iota((QB, NS), 1)
    prev3 = jnp.where(j == 0, 0.0, pltpu.roll(panel[3], 1, axis=1))
    imp = prev3 + 2.0 * panel[0] + 2.0 * panel[1] + 2.0 * panel[2] + panel[3]
    c_abs = first_near + _iota((CMP_NEAR, NS), 0)
    off = c_abs + 1 - 4 * _iota((CMP_NEAR, NS), 1)
    w_near = jnp.where((off == 0) | (off == 4), 1.0, jnp.where((off >= 1) & (off <= 3), 2.0, 0.0))
    imp = imp + _dot(imp_near, w_near, HI)
    cur = 2 * blk + (_iota((QB, NS), 0) >= SEL_BLOCK).astype(jnp.int32)
    forced = (j == 0) | (j == cur) | (j == cur - 1)
    score = jnp.where(forced, SEL_FORCE_SCORE, jnp.where(j <= cur, imp, -1.0))
    score = score.T
    jt = _iota((NS, QB), 0)
    sel = jnp.zeros((NS, QB), F32)
    for _ in range(min(SEL_TOPK, NS)):
        m = jnp.max(score, axis=0, keepdims=True)
        idx = jnp.min(jnp.where(score == m, jt, NS), axis=0, keepdims=True)
        hit = jt == idx
        sel = jnp.where(hit, 1.0, sel)
        score = jnp.where(hit, -3.0, score)
    pad = jnp.zeros((SEL_PAD_BLOCKS, QB), BF16)
    sel_ref[u] = jnp.concatenate([pad, sel.astype(BF16), pad], axis=0)


GROUP_LANES = REP * HEAD_DIM


def _q_spec(blocks=1):
    return pl.BlockSpec((None, blocks * Q_BLOCK, GROUP_LANES), lambda b, g, i: (b, i, g))


def nsa_compressed(p_nsa, kperm, vperm, knear, vnear, bias, bias_far):
    B, T, _ = p_nsa.shape
    G, NQ, N = KV_GROUPS, T // Q_BLOCK, HEAD_DIM
    NB = kperm.shape[2]
    nsp = NB // 4 + 2 * SEL_PAD_BLOCKS
    blk4 = lambda b, g, i: (b, g, 0, 0)
    return pl.pallas_call(
        _nsa_cmp_body,
        grid=(B, G, NQ // CMP_BLOCKS),
        in_specs=[_q_spec(CMP_BLOCKS),
                  pl.BlockSpec((None, None, NB, N), blk4), pl.BlockSpec((None, None, NB, N), blk4),
                  pl.BlockSpec((None, None, knear.shape[2], N), blk4),
                  pl.BlockSpec((None, None, knear.shape[2], N), blk4),
                  pl.BlockSpec((None, REP, Q_BLOCK, CMP_NEAR), lambda b, g, i: (g, 0, 0, 0)),
                  pl.BlockSpec((None, REP, 1, LANE), lambda b, g, i: (g, 0, 0, 0))],
        out_specs=[_q_spec(CMP_BLOCKS),
                   pl.BlockSpec((None, None, CMP_BLOCKS, nsp, Q_BLOCK), lambda b, g, i: (b, g, i, 0, 0))],
        out_shape=[jax.ShapeDtypeStruct((B, T, WIDTH), F32),
                   jax.ShapeDtypeStruct((B, G, NQ, nsp, Q_BLOCK), BF16)],
        compiler_params=_cparams(("parallel", "parallel", "parallel")),
        name="nsa_compressed",
    )(p_nsa, kperm, vperm, knear, vnear, bias, bias_far)


def _nsa_sel_body(q_ref, selt_ref, k_ref, vt_ref, biast_ref, bfar_ref, o_ref, sa_ref, sb_ref):
    QB = Q_BLOCK
    NQ = REP * QB
    blk = pl.program_id(2)
    qb = q_ref[...].astype(F32) * (HEAD_DIM ** -0.5 * LOG2E)
    qt = jnp.concatenate([qb[:, r * HEAD_DIM:(r + 1) * HEAD_DIM].T for r in range(REP)], axis=1).astype(BF16)
    near_chunk = blk + (SEL_PAD - SEL_SPAN) // V_CHUNK
    first = near_chunk * (V_CHUNK // SEL_BLOCK)
    pb_u = _iota((SEL_TILE_BLOCKS, QB), 0)
    zeros = jnp.zeros((AUG - HEAD_DIM - 2 * SEL_TILE_BLOCKS, NQ), BF16)
    bfar = bfar_ref[...]

    def far_scores(kt, s_ref):
        b0 = pl.multiple_of(kt * SEL_TILE_BLOCKS, SEL_TILE_BLOCKS)
        picked = selt_ref[pl.ds(b0, SEL_TILE_BLOCKS), :].astype(F32) > 0.5
        neg = jnp.where(picked & (b0 + pb_u < first), 0.0, NEG_INF).astype(BF16)
        q_far = jnp.concatenate([qt, jnp.concatenate([neg] * REP, axis=1), bfar, zeros], axis=0)
        r0 = pl.multiple_of(kt * SEL_TILE, SEL_TILE)
        s_ref[...] = _dot(k_ref[pl.ds(r0, SEL_TILE), :], q_far)

    def weighted_values(chunk0, p):
        out = jnp.zeros((HEAD_DIM, NQ), F32)
        for j in range(0, SEL_TILE // V_CHUNK, 2):
            vt = jnp.concatenate([vt_ref[chunk0 + j], vt_ref[chunk0 + j + 1]], axis=1)
            out = out + _dot(vt, p[j * V_CHUNK:(j + 2) * V_CHUNK])
        return out

    def far_update(kt, s_ref, carry):
        m, l, acc = carry
        s = s_ref[...]
        m_new = jnp.maximum(m, jnp.max(s, axis=0, keepdims=True))
        alpha = jnp.exp2(m - m_new)
        p = jnp.exp2(s - m_new)
        l = alpha * l + jnp.sum(p, axis=0, keepdims=True)
        p = p.astype(BF16)
        return m_new, l, alpha * acc + weighted_values(kt * (SEL_TILE // V_CHUNK), p)

    far_scores(0, sa_ref)

    row0 = pl.multiple_of(near_chunk * V_CHUNK, V_CHUNK)
    base = pl.multiple_of((first // SEL_TILE_BLOCKS) * SEL_TILE_BLOCKS, SEL_TILE_BLOCKS)
    rows = selt_ref[pl.ds(base, 2 * SEL_TILE_BLOCKS), :].astype(F32)
    picked = jnp.where(pb_u >= first - base, rows[0:SEL_TILE_BLOCKS], rows[SEL_TILE_BLOCKS:]) > 0.5
    neg = jnp.where(picked, 0.0, NEG_INF).astype(BF16)
    q_near = jnp.concatenate([qt, jnp.concatenate([neg] * REP, axis=1), jnp.zeros_like(bfar), zeros], axis=0)
    s = _dot(k_ref[pl.ds(row0, SEL_NEAR), :], q_near) + biast_ref[...]
    m = jnp.max(s, axis=0, keepdims=True)
    p = jnp.exp2(s - m)
    l = jnp.sum(p, axis=0, keepdims=True)
    acc = weighted_values(near_chunk, p.astype(BF16))

    def far_pair(j, carry):
        far_scores(2 * j + 1, sb_ref)
        carry = far_update(2 * j, sa_ref, carry)
        far_scores(2 * j + 2, sa_ref)
        return far_update(2 * j + 1, sb_ref, carry)

    n_far = (first + SEL_TILE_BLOCKS - 1) // SEL_TILE_BLOCKS
    carry = lax.fori_loop(0, n_far // 2, far_pair, (m, l, acc))
    m, l, acc = lax.cond(n_far % 2 == 1, lambda c: far_update(n_far - 1, sa_ref, c), lambda c: c, carry)
    out = acc / l
    o_ref[...] = jnp.concatenate([out[:, r * QB:(r + 1) * QB].T for r in range(REP)], axis=1)


def nsa_selected(p_nsa, selt, k_aug, vt, bias_t, bias_far):
    B, T, _ = p_nsa.shape
    G, NQ, N, W = KV_GROUPS, T // Q_BLOCK, HEAD_DIM, REP * Q_BLOCK
    nsp = selt.shape[3]
    rows = k_aug.shape[2]
    return pl.pallas_call(
        _nsa_sel_body,
        grid=(B, G, NQ),
        in_specs=[_q_spec(),
                  pl.BlockSpec((None, None, None, nsp, Q_BLOCK), lambda b, g, i: (b, g, i, 0, 0)),
                  pl.BlockSpec((None, None, rows, AUG), lambda b, g, i: (b, g, 0, 0)),
                  pl.BlockSpec((None, None, rows // V_CHUNK, N, V_CHUNK), lambda b, g, i: (b, g, 0, 0, 0)),
                  pl.BlockSpec((None, SEL_NEAR, W), lambda b, g, i: (g, 0, 0)),
                  pl.BlockSpec((None, SEL_TILE_BLOCKS, W), lambda b, g, i: (g, 0, 0))],
        out_specs=_q_spec(),
        out_shape=jax.ShapeDtypeStruct((B, T, WIDTH), F32),
        scratch_shapes=[pltpu.VMEM((SEL_TILE, W), F32)] * 2,
        compiler_params=_cparams(("parallel", "parallel", "parallel")),
        name="nsa_selected",
    )(p_nsa, selt, k_aug, vt, bias_t, bias_far)


def _band_body(*refs, pad, has_sink):
    if has_sink:
        q_ref, k_ref, v_ref, bias_ref, sink_ref, o_ref = refs
    else:
        q_ref, k_ref, v_ref, bias_ref, o_ref = refs
    QB = Q_BLOCK
    NQ = REP * QB
    width = pad + QB
    bias = bias_ref[...].reshape(NQ, width)
    if has_sink:
        sink = sink_ref[...].reshape(NQ, LANE)
    for u in range(BAND_BLOCKS):
        blk = pl.program_id(2) * BAND_BLOCKS + u
        row0 = pl.multiple_of(blk * QB, QB)
        kb = k_ref[pl.ds(row0, width), :]
        vb = v_ref[pl.ds(row0, width), :]
        before_start = jnp.where(row0 + _iota((1, width), 1) >= pad, 0.0, NEG_INF)
        q = _heads_on_rows(q_ref[u * QB:(u + 1) * QB, :]) * (HEAD_DIM ** -0.5)
        s = _dot_nt(q, kb) + bias + before_start
        m = jnp.max(s, axis=-1, keepdims=True)
        if has_sink:
            m = jnp.maximum(m, jnp.max(sink, axis=-1, keepdims=True))
        p = jnp.exp(s - m)
        den = jnp.sum(p, axis=-1, keepdims=True)
        if has_sink:
            den = den + jnp.sum(jnp.exp(sink - m), axis=-1, keepdims=True)
        p = p * (1.0 / den)
        o_ref[u * QB:(u + 1) * QB, :] = _heads_on_lanes(_dot(p.astype(BF16), vb))


def band_attention(p, k, v, bias, sink, pad):
    B, T, _ = p.shape
    G, NQ, N = KV_GROUPS, T // Q_BLOCK, HEAD_DIM
    rows = k.shape[2]
    width = pad + Q_BLOCK
    kv = pl.BlockSpec((None, None, rows, N), lambda b, g, i: (b, g, 0, 0))
    in_specs = [_q_spec(BAND_BLOCKS), kv, kv,
                pl.BlockSpec((None, REP, Q_BLOCK, width), lambda b, g, i: (g, 0, 0, 0))]
    args = [p, k, v, bias]
    if sink is not None:
        in_specs.append(pl.BlockSpec((None, REP, Q_BLOCK, LANE), lambda b, g, i: (g, 0, 0, 0)))
        args.append(sink)
    return pl.pallas_call(
        functools.partial(_band_body, pad=pad, has_sink=sink is not None),
        grid=(B, G, NQ // BAND_BLOCKS),
        in_specs=in_specs,
        out_specs=_q_spec(BAND_BLOCKS),
        out_shape=jax.ShapeDtypeStruct((B, T, WIDTH), F32),
        compiler_params=_cparams(("parallel", "parallel", "parallel")),
        name="band_attention",
    )(*args)


def _merge_body(x_ref, pg_ref, hg_ref, orw_ref, oc_ref, os_ref, ow_ref, osw_ref, wb_ref, wo_ref, gn_ref, ex_ref,
                o_ref):
    pg = pg_ref[...]
    head_gates = jax.nn.sigmoid(hg_ref[...])
    gates_hi = head_gates.astype(BF16)
    gates_lo = (head_gates - gates_hi.astype(F32)).astype(BF16)
    ge = _dot(gates_hi, ex_ref[...]) + _dot(gates_lo, ex_ref[...])
    o_nsa = (ge[:, 0:WIDTH] * oc_ref[...] + ge[:, WIDTH:2 * WIDTH] * os_ref[...]
             + ge[:, 2 * WIDTH:3 * WIDTH] * ow_ref[...])
    merged = (jax.nn.sigmoid(pg[:, 0:D_MODEL]) * _dot(orw_ref[...].astype(BF16), wb_ref[0])
              + jax.nn.sigmoid(pg[:, D_MODEL:2 * D_MODEL]) * _dot(o_nsa.astype(BF16), wb_ref[1])
              + jax.nn.sigmoid(pg[:, 2 * D_MODEL:3 * D_MODEL]) * _dot(osw_ref[...].astype(BF16), wb_ref[2]))
    y = _dot(merged.astype(BF16), wo_ref[...])
    o_ref[...] = x_ref[...] + _rms(y, gn_ref[...])


def merge_out(x, pg, p_rw, o_rwkv, o_c, o_s, o_w, o_swa, w_branch, w_out, g_post, expand, tm):
    M, D = x.shape
    row = lambda i: (i, 0)
    wide = pl.BlockSpec((tm, WIDTH), row)
    return pl.pallas_call(
        _merge_body,
        grid=(M // tm,),
        in_specs=[pl.BlockSpec((tm, D), row), pl.BlockSpec((tm, 3 * D), row),
                  pl.BlockSpec((tm, LANE), lambda i: (i, RW_COLS // LANE)),
                  wide, wide, wide, wide, wide,
                  pl.BlockSpec((3, WIDTH, D), lambda i: (0, 0, 0)),
                  pl.BlockSpec((D, D), lambda i: (0, 0)),
                  pl.BlockSpec((1, D), lambda i: (0, 0)),
                  pl.BlockSpec((LANE, 3 * WIDTH), lambda i: (0, 0))],
        out_specs=pl.BlockSpec((tm, D), row),
        out_shape=jax.ShapeDtypeStruct((M, D), F32),
        compiler_params=_cparams(("parallel",)),
        name="merge_out",
    )(x, pg, p_rw, o_rwkv, o_c, o_s, o_w, o_swa, w_branch, w_out, g_post, expand)


def _ffn_body(x_ref, gpre_ref, wg_ref, wu_ref, wd_ref, gpost_ref, o_ref, h_ref, acc_ref):
    j = pl.program_id(1)

    @pl.when(j == 0)
    def _():
        h_ref[...] = _rms(x_ref[...], gpre_ref[...]).astype(BF16)
        acc_ref[...] = jnp.zeros_like(acc_ref)

    h = h_ref[...]
    gate = _dot(h, wg_ref[...])
    act = gate * jax.nn.sigmoid(gate) * _dot(h, wu_ref[...])
    acc_ref[...] += _dot(act.astype(BF16), wd_ref[...])

    @pl.when(j == pl.num_programs(1) - 1)
    def _():
        o_ref[...] = x_ref[...] + _rms(acc_ref[...], gpost_ref[...])


def ffn(x, g_pre, w_gate, w_up, w_down, g_post, tm, th):
    M, D = x.shape
    Hd = w_gate.shape[1]
    return pl.pallas_call(
        _ffn_body,
        grid=(M // tm, Hd // th),
        in_specs=[pl.BlockSpec((tm, D), lambda i, j: (i, 0)),
                  pl.BlockSpec((1, D), lambda i, j: (0, 0)),
                  pl.BlockSpec((D, th), lambda i, j: (0, j)),
                  pl.BlockSpec((D, th), lambda i, j: (0, j)),
                  pl.BlockSpec((th, D), lambda i, j: (j, 0)),
                  pl.BlockSpec((1, D), lambda i, j: (0, 0))],
        out_specs=pl.BlockSpec((tm, D), lambda i, j: (i, 0)),
        out_shape=jax.ShapeDtypeStruct((M, D), F32),
        scratch_shapes=[pltpu.VMEM((tm, D), BF16), pltpu.VMEM((tm, D), F32)],
        compiler_params=_cparams(("parallel", "arbitrary")),
        name="ffn",
    )(x, g_pre, w_gate, w_up, w_down, g_post)


def _pad_cols(w, n):
    return jnp.pad(w, ((0, 0), (0, n - w.shape[1])))


def _pad_rows(w, n):
    return jnp.pad(w, ((0, n - w.shape[0]), (0, 0)))


def _kv_groups(z, B, T, pad):
    z = z.reshape(B, T, KV_GROUPS, HEAD_DIM).transpose(0, 2, 1, 3)
    return jnp.pad(z, ((0, 0), (0, 0), (pad, 0), (0, 0)))


def _group_bias(b):
    return b.reshape(KV_GROUPS, REP, b.shape[1], b.shape[2])


def kernel(x, norm_pre_mix, norm_post_mix, norm_pre_ffn, norm_post_ffn, w_in, rwkv_mu, rwkv_w0, rwkv_w2, rwkv_a0, rwkv_a2, rwkv_g2, rwkv_k_k, rwkv_k_a, rwkv_r_k, rwkv_ln_w, rwkv_ln_b, nsa_pe_k, nsa_pe_v, nsa_ck_w1, nsa_ck_w2, nsa_cv_w1, nsa_cv_w2, swa_sinks, rel_bias, w_branch, w_out, ffn_w_gate, ffn_w_up, ffn_w_down):
    B, T, D = x.shape
    depth = w_in.shape[0]
    M = B * T
    NB = T // CMP_STRIDE
    NS = T // SEL_BLOCK
    ffn_hidden = ffn_w_gate.shape[2]

    rw_cols = 3 * WIDTH + W_LORA + A_LORA + G_LORA
    nsa_cols = WIDTH + 6 * KV_WIDTH + 3 * N_HEADS
    swa_cols = WIDTH + 2 * KV_WIDTH
    c_nsa = rw_cols
    c_swa = c_nsa + nsa_cols
    c_gate = c_swa + swa_cols
    c_zw = 3 * WIDTH

    ii = np.arange(Q_BLOCK)[:, None]
    d_swa = SWA_WINDOW + ii - np.arange(SWA_WINDOW + Q_BLOCK)[None, :]
    d_win = NSA_WINDOW + ii - np.arange(NSA_WINDOW + Q_BLOCK)[None, :]
    d_sel = SEL_SPAN + ii - np.arange(SEL_NEAR)[None, :]
    d_cmp = ii + (CMP_PAD * CMP_STRIDE - CMP_LEN + 1) - CMP_STRIDE * np.arange(CMP_NEAR)[None, :]
    bias_swa = bias_tile(rel_bias, d_swa, (d_swa >= 0) & (d_swa < SWA_WINDOW))
    bias_win = bias_tile(rel_bias, d_win, (d_win >= 0) & (d_win < NSA_WINDOW))
    bias_sel = bias_tile(rel_bias, d_sel, d_sel >= 0)
    bias_cmp = bias_tile(rel_bias, d_cmp, d_cmp >= 0)
    bias_swa = _group_bias(bias_swa[N_HEADS:])
    bias_win = _group_bias(bias_win[:N_HEADS])
    bias_sel = _group_bias(bias_sel[:N_HEADS])
    bias_cmp = _group_bias(bias_cmp[:N_HEADS])
    far = rel_bias[REL_BUCKETS - 1, :N_HEADS].reshape(KV_GROUPS, REP, 1, 1)
    bias_sel_t = bias_sel.transpose(0, 3, 1, 2).reshape(KV_GROUPS, SEL_NEAR, REP * Q_BLOCK) * LOG2E
    far2 = far * LOG2E
    far_hi = far2.astype(BF16)
    far_lo = (far2 - far_hi.astype(F32)).astype(BF16)
    far_rows = jnp.concatenate([far_hi, far_lo], axis=2)
    far_rows = jnp.broadcast_to(far_rows, (KV_GROUPS, REP, 2, Q_BLOCK)).transpose(0, 2, 1, 3)
    bfar_sel = jnp.pad(far_rows.reshape(KV_GROUPS, 2, REP * Q_BLOCK), ((0, 0), (0, SEL_TILE_BLOCKS - 2), (0, 0)))
    rows_pad = SEL_PAD + T
    blk_in_tile = (np.arange(rows_pad) // SEL_BLOCK) % SEL_TILE_BLOCKS
    ka = np.zeros((rows_pad, AUG - HEAD_DIM), np.float32)
    ka[np.arange(rows_pad), blk_in_tile] = 1.0
    ka[:, SEL_TILE_BLOCKS:SEL_TILE_BLOCKS + 2] = 1.0
    key_aug = jnp.asarray(ka, BF16)
    bfar_cmp = jnp.broadcast_to(far, (KV_GROUPS, REP, 1, LANE))
    ex = np.zeros((LANE, 3 * WIDTH), np.float32)
    for c in range(3 * N_HEADS):
        ex[c, c * HEAD_DIM:(c + 1) * HEAD_DIM] = 1.0
    expand = jnp.asarray(ex, BF16)

    xf = x.reshape(M, D)
    for l in range(depth):
        w = w_in[l]
        w_rw = jnp.concatenate([w[:, :c_zw], _pad_cols(w[:, c_zw:c_zw + W_LORA], LANE),
                                _pad_cols(w[:, c_zw + W_LORA:c_zw + W_LORA + A_LORA], LANE),
                                w[:, c_zw + W_LORA + A_LORA:rw_cols],
                                _pad_cols(w[:, c_swa - 3 * N_HEADS:c_swa], LANE)], axis=1).astype(BF16)
        w_nsa = w[:, c_nsa:c_nsa + WIDTH + 6 * KV_WIDTH].astype(BF16)
        w_swa = w[:, c_swa:c_gate].astype(BF16)
        w_gate = w[:, c_gate:].astype(BF16)
        g_pre = norm_pre_mix[l][None, :]
        p_rw = norm_matmul(xf, g_pre, w_rw, F32, 1024)
        p_nsa = norm_matmul(xf, g_pre, w_nsa, BF16, 1024)
        p_swa = norm_matmul(xf, g_pre, w_swa, BF16, 1024)
        p_gate = norm_matmul(xf, g_pre, w_gate, F32, 512)

        mu = rwkv_mu[l]
        mu_p = jnp.concatenate([mu[:c_zw], jnp.pad(mu[c_zw:c_zw + W_LORA], (0, LANE - W_LORA)),
                                jnp.pad(mu[c_zw + W_LORA:c_zw + W_LORA + A_LORA], (0, LANE - A_LORA)),
                                mu[c_zw + W_LORA + A_LORA:]])[None, :]
        r, k, v, a, lw, cum, g = rwkv_prep(p_rw.reshape(B, T, RW_COLS + LANE), mu_p, rwkv_w0[l][None, :],
                                      _pad_rows(rwkv_w2[l], LANE), rwkv_a0[l][None, :],
                                      _pad_rows(rwkv_a2[l], LANE), rwkv_g2[l], 512)
        row = lambda z: z.reshape(1, WIDTH)
        o_rwkv = rwkv_recurrence(r, k, v, a, lw, cum, g, row(rwkv_k_k[l]), row(rwkv_k_a[l]), row(rwkv_r_k[l]),
                                 row(rwkv_ln_w[l]), row(rwkv_ln_b[l]), 512).reshape(M, WIDTH)

        pn = p_nsa.reshape(B, T, WIDTH + 6 * KV_WIDTH)
        seg = lambda n: pn[..., WIDTH + n * KV_WIDTH:WIDTH + (n + 1) * KV_WIDTH]

        def strides(z):
            z = z.reshape(B, NB, CMP_STRIDE, KV_GROUPS, HEAD_DIM).transpose(0, 3, 1, 2, 4)
            return z.reshape(B, KV_GROUPS, NB, CMP_STRIDE * HEAD_DIM)

        def pe_rows(pe):
            return jnp.broadcast_to(pe.reshape(1, CMP_LEN * HEAD_DIM), (8, CMP_LEN * HEAD_DIM)).astype(BF16)

        k_cmp = nsa_compress(strides(seg(0)), pe_rows(nsa_pe_k[l]), nsa_ck_w1[l].astype(BF16),
                             nsa_ck_w2[l].astype(BF16))
        v_cmp = nsa_compress(strides(seg(1)), pe_rows(nsa_pe_v[l]), nsa_cv_w1[l].astype(BF16),
                             nsa_cv_w2[l].astype(BF16))

        def panels(z):
            z = z.reshape(B, KV_GROUPS, NS, 4, HEAD_DIM).transpose(0, 1, 3, 2, 4)
            return z.reshape(B, KV_GROUPS, NB, HEAD_DIM).astype(BF16)

        near = lambda z: jnp.pad(z, ((0, 0), (0, 0), (CMP_PAD, 8), (0, 0)))
        o_c, sel_t = nsa_compressed(pn, panels(k_cmp), panels(v_cmp), near(k_cmp), near(v_cmp), bias_cmp, bfar_cmp)
        k_aug = jnp.concatenate([_kv_groups(seg(2), B, T, SEL_PAD),
                                 jnp.broadcast_to(key_aug, (B, KV_GROUPS) + key_aug.shape)], axis=-1)
        v_t = _kv_groups(seg(3), B, T, SEL_PAD).reshape(B, KV_GROUPS, -1, V_CHUNK, HEAD_DIM).swapaxes(-1, -2)
        o_s = nsa_selected(pn, sel_t, k_aug, v_t, bias_sel_t, bfar_sel)
        o_w = band_attention(pn, _kv_groups(seg(4), B, T, NSA_WINDOW), _kv_groups(seg(5), B, T, NSA_WINDOW),
                             bias_win, None, NSA_WINDOW)

        ps = p_swa.reshape(B, T, swa_cols)
        sink = jnp.where(np.arange(LANE) == 0, swa_sinks[l].reshape(KV_GROUPS, REP, 1, 1), NEG_INF)
        sink = jnp.broadcast_to(sink, (KV_GROUPS, REP, Q_BLOCK, LANE))
        o_swa = band_attention(ps,
                               _kv_groups(ps[..., WIDTH:WIDTH + KV_WIDTH], B, T, SWA_WINDOW),
                               _kv_groups(ps[..., WIDTH + KV_WIDTH:], B, T, SWA_WINDOW),
                               bias_swa, sink, SWA_WINDOW)

        flat = lambda o: o.reshape(M, WIDTH)
        xf = merge_out(xf, p_gate, p_rw, o_rwkv, flat(o_c), flat(o_s), flat(o_w),
                       flat(o_swa), w_branch[l].astype(BF16), w_out[l].astype(BF16),
                       norm_post_mix[l][None, :], expand, 256)

        xf = ffn(xf, norm_pre_ffn[l][None, :], ffn_w_gate[l].astype(BF16), ffn_w_up[l].astype(BF16),
                 ffn_w_down[l].astype(BF16), norm_post_ffn[l][None, :], 512, ffn_hidden // 2)
    return xf.reshape(B, T, D)
```

```python
import functools
import math

import numpy as np
import jax
import jax.numpy as jnp
from jax import lax
from jax.experimental import pallas as pl
from jax.experimental.pallas import tpu as pltpu

F32 = jnp.float32
BF16 = jnp.bfloat16
HI = lax.Precision.HIGHEST

D_MODEL = 1024
HEAD_DIM = 64
N_HEADS = 8
KV_GROUPS = 2
REP = N_HEADS // KV_GROUPS
WIDTH = N_HEADS * HEAD_DIM
KV_WIDTH = KV_GROUPS * HEAD_DIM
W_LORA, A_LORA, G_LORA = 64, 64, 128
RWKV_LN_EPS = 64e-5
CMP_STRIDE = 16
CMP_LEN = 32
CMP_HIDDEN = 256
SEL_BLOCK = 64
SEL_TOPK = 16
SEL_FORCE_SCORE = 1e4
NSA_WINDOW = 512
Q_BLOCK = 128
SWA_WINDOW = 128
REL_BUCKETS = 32
REL_MAX_DIST = 1024
NORM_EPS = 1e-6
NEG_INF = -1e30
LOG2E = math.log2(math.e)
LANE = 128
RWKV_CHUNK = 64
SEL_NEAR = 1024
SEL_SPAN = SEL_NEAR - Q_BLOCK
SEL_PAD = SEL_NEAR
SEL_TILE = 1024
SEL_TILE_BLOCKS = SEL_TILE // SEL_BLOCK
SEL_PAD_BLOCKS = SEL_PAD // SEL_BLOCK
AUG = 2 * HEAD_DIM
V_CHUNK = 128
BAND_BLOCKS = 2
CMP_BLOCKS = 2
CMP_NEAR = 64
CMP_PAD = CMP_NEAR - Q_BLOCK // CMP_STRIDE
VMEM_LIMIT = 56 * 1024 * 1024


def _cparams(sem):
    return pltpu.CompilerParams(dimension_semantics=sem, vmem_limit_bytes=VMEM_LIMIT)


def _dot(a, b, precision=None):
    return jnp.dot(a, b, preferred_element_type=F32, precision=precision)


def _dot_nt(a, b, precision=None):
    return lax.dot_general(a, b, (((1,), (1,)), ((), ())), preferred_element_type=F32, precision=precision)


def _rms(x, g):
    return x * lax.rsqrt(jnp.mean(x * x, axis=-1, keepdims=True) + NORM_EPS) * g


def _iota(shape, dim):
    return lax.broadcasted_iota(jnp.int32, shape, dim)


def _norm_mm_body(x_ref, g_ref, w_ref, o_ref):
    h = _rms(x_ref[...], g_ref[...]).astype(BF16)
    o_ref[...] = _dot(h, w_ref[...]).astype(o_ref.dtype)


def norm_matmul(x, g, w, out_dtype, tm):
    M, D = x.shape
    N = w.shape[1]
    return pl.pallas_call(
        _norm_mm_body,
        grid=(M // tm,),
        in_specs=[pl.BlockSpec((tm, D), lambda i: (i, 0)),
                  pl.BlockSpec((1, D), lambda i: (0, 0)),
                  pl.BlockSpec((D, N), lambda i: (0, 0))],
        out_specs=pl.BlockSpec((tm, N), lambda i: (i, 0)),
        out_shape=jax.ShapeDtypeStruct((M, N), out_dtype),
        compiler_params=_cparams(("parallel",)),
        name="norm_matmul",
    )(x, g, w)


RW_COLS = 3 * WIDTH + 3 * LANE


def _rwkv_prep_body(p_ref, mu_ref, w0_ref, w2_ref, a0_ref, a2_ref, g2_ref,
                    r_o, k_o, v_o, a_o, lw_o, cum_o, g_o, carry_ref):
    tm = p_ref.shape[0]

    @pl.when(pl.program_id(1) == 0)
    def _():
        carry_ref[...] = jnp.zeros_like(carry_ref)

    p = p_ref[...]
    row = _iota(p.shape, 0)
    prev = jnp.where(row == 0, carry_ref[0:1, :], pltpu.roll(p, 1, axis=0))
    carry_ref[0:1, :] = p[tm - 1:tm, :]
    ps = p + (prev - p) * mu_ref[...]
    r_o[...] = ps[:, 0:WIDTH]
    k_o[...] = ps[:, WIDTH:2 * WIDTH]
    v_o[...] = ps[:, 2 * WIDTH:3 * WIDTH]
    zw = ps[:, 3 * WIDTH:3 * WIDTH + LANE]
    za = ps[:, 3 * WIDTH + LANE:3 * WIDTH + 2 * LANE]
    zg = ps[:, 3 * WIDTH + 2 * LANE:3 * WIDTH + 3 * LANE]
    z = -(w0_ref[...] + _dot(jnp.tanh(zw), w2_ref[...], HI))
    softplus = jnp.maximum(z, 0.0) + jnp.log(1.0 + jnp.exp(-jnp.abs(z)))
    lw = -jnp.exp(-softplus - 0.5)
    lw_o[...] = lw
    C = RWKV_CHUNK
    tri = (_iota((C, C), 0) >= _iota((C, C), 1)).astype(F32)
    for c in range(tm // C):
        cum_o[c * C:(c + 1) * C, :] = _dot(tri, lw[c * C:(c + 1) * C, :], HI)
    a_o[...] = jax.nn.sigmoid(a0_ref[...] + _dot(za, a2_ref[...], HI))
    g_o[...] = _dot(jax.nn.sigmoid(zg), g2_ref[...], HI)


def rwkv_prep(p, mu, w0, w2, a0, a2, g2, tm):
    B, T, _ = p.shape
    row = lambda b, i: (b, i, 0)
    fixed = lambda b, i: (0, 0)
    out = jax.ShapeDtypeStruct((B, T, WIDTH), F32)
    return pl.pallas_call(
        _rwkv_prep_body,
        grid=(B, T // tm),
        in_specs=[pl.BlockSpec((None, tm, RW_COLS), row),
                  pl.BlockSpec((1, RW_COLS), fixed),
                  pl.BlockSpec((1, WIDTH), fixed), pl.BlockSpec((LANE, WIDTH), fixed),
                  pl.BlockSpec((1, WIDTH), fixed), pl.BlockSpec((LANE, WIDTH), fixed),
                  pl.BlockSpec((LANE, WIDTH), fixed)],
        out_specs=[pl.BlockSpec((None, tm, WIDTH), row)] * 7,
        out_shape=[out] * 7,
        scratch_shapes=[pltpu.VMEM((8, RW_COLS), F32)],
        compiler_params=_cparams(("parallel", "arbitrary")),
        name="rwkv_prep",
    )(p, mu, w0, w2, a0, a2, g2)


RWKV_HEADS_PER_STEP = 4


def _bbmm(a, b):
    return lax.dot_general(a.astype(BF16), b.astype(BF16), (((2,), (1,)), ((0,), (0,))),
                           preferred_element_type=F32)


def _bbmm_nt(a, b):
    return lax.dot_general(a.astype(BF16), b.astype(BF16), (((2,), (2,)), ((0,), (0,))),
                           preferred_element_type=F32)


def _bbmm_tn(a, b):
    return lax.dot_general(a.astype(BF16), b.astype(BF16), (((1,), (1,)), ((0,), (0,))),
                           preferred_element_type=F32)


def _rwkv_rec_body(r_ref, k_ref, v_ref, a_ref, lw_ref, cum_ref, g_ref, kk_ref, ka_ref, rk_ref, lnw_ref, lnb_ref,
                   o_ref, s_ref):
    C = RWKV_CHUNK
    N = HEAD_DIM
    assert C == N
    tc = r_ref.shape[0]
    nc = tc // C
    hb = r_ref.shape[1] // N

    @pl.when(pl.program_id(2) == 0)
    def _():
        s_ref[...] = jnp.zeros_like(s_ref)

    def chunks(ref):
        x = ref[...]
        return jnp.concatenate([x[:, h * N:(h + 1) * N].reshape(nc, C, N) for h in range(hb)], axis=0)

    def per_head(ref):
        x = ref[...]
        return jnp.concatenate([jnp.broadcast_to(x[:, h * N:(h + 1) * N][None], (nc, 1, N))
                                for h in range(hb)], axis=0)

    row = _iota((C, C), 0)
    col = _iota((C, C), 1)
    incl = (row >= col)[None]
    strict = (row > col)[None]
    eye = (row == col)[None]
    nb = hb * nc
    r, k, v, a, lw = chunks(r_ref), chunks(k_ref), chunks(v_ref), chunks(a_ref), chunks(lw_ref)
    ones = jnp.ones((nb, N, N), BF16)

    def lane_sum(x):
        hi = x.astype(BF16)
        return _bbmm(hi, ones) + _bbmm(x - hi.astype(F32), ones)

    kk = k * per_head(kk_ref)
    kk = kk / jnp.maximum(jnp.sqrt(lane_sum(kk * kk)), 1e-12)
    k2 = k * (1.0 + (a - 1.0) * per_head(ka_ref))
    cum = chunks(cum_ref)
    p_incl = jnp.exp(cum)
    p_inv = jnp.exp(-cum)
    a_t = -kk * jnp.exp(cum - lw)
    r_t = r * p_incl
    b_t = kk * a * p_inv
    k_t = k2 * p_inv
    gram = _bbmm_nt(jnp.concatenate([a_t, r_t], axis=1), jnp.concatenate([b_t, k_t], axis=1))
    l_ab = jnp.where(strict, gram[:, 0:C, 0:C], 0.0)
    l_ak = jnp.where(strict, gram[:, 0:C, C:2 * C], 0.0)
    m_rb = jnp.where(incl, gram[:, C:2 * C, 0:C], 0.0)
    m_rk = jnp.where(incl, gram[:, C:2 * C, C:2 * C], 0.0)
    inv = jnp.where(eye, 1.0, l_ab)
    lp = l_ab
    for _ in range(5):
        lp = _bbmm(lp, lp)
        inv = inv + _bbmm(lp, inv)
    t_a = _bbmm(inv, a_t)
    t_v = _bbmm(inv, _bbmm(l_ak, v))
    p_end = p_incl[:, C - 1:C, :]
    bp = b_t * p_end
    kp = k_t * p_end
    ry = (r_t + _bbmm(m_rb, t_a)).reshape(hb, nc, C, N)
    yc = (_bbmm(m_rb, t_v) + _bbmm(m_rk, v)).reshape(hb, nc, C, N)
    am = (jnp.where(eye, jnp.broadcast_to(p_end, (nb, C, C)), 0.0) + _bbmm_tn(bp, t_a)).reshape(hb, nc, C, C)
    gm = (_bbmm_tn(bp, t_v) + _bbmm_tn(kp, v)).reshape(hb, nc, C, N)
    h = s_ref[...]
    ry_am = jnp.concatenate([ry, am], axis=2)
    ys = []
    for c in range(nc):
        both = _bbmm(ry_am[:, c], h)
        ys.append(both[:, 0:C] + yc[:, c])
        h = both[:, C:2 * C] + gm[:, c]
    s_ref[...] = h
    y = jnp.stack(ys, axis=1).reshape(nb, C, N)
    mean = lane_sum(y) * (1.0 / N)
    var = lane_sum(jnp.square(y - mean)) * (1.0 / N)
    yn = (y - mean) * lax.rsqrt(var + RWKV_LN_EPS) * per_head(lnw_ref) + per_head(lnb_ref)
    bonus = lane_sum(r * k2 * per_head(rk_ref)) * v
    out = (yn + bonus) * chunks(g_ref)
    o_ref[...] = jnp.concatenate([out[h * nc:(h + 1) * nc].reshape(tc, N) for h in range(hb)], axis=1)


def rwkv_recurrence(r, k, v, a, lw, cum, g, k_k, k_a, r_k, ln_w, ln_b, tc):
    B, T, W = r.shape
    slab = RWKV_HEADS_PER_STEP * HEAD_DIM
    seq = pl.BlockSpec((None, tc, slab), lambda b, h, i: (b, i, h))
    par = pl.BlockSpec((1, slab), lambda b, h, i: (0, h))
    return pl.pallas_call(
        _rwkv_rec_body,
        grid=(B, W // slab, T // tc),
        in_specs=[seq] * 7 + [par] * 5,
        out_specs=seq,
        out_shape=jax.ShapeDtypeStruct((B, T, W), F32),
        scratch_shapes=[pltpu.VMEM((RWKV_HEADS_PER_STEP, HEAD_DIM, HEAD_DIM), F32)],
        compiler_params=_cparams(("parallel", "parallel", "arbitrary")),
        name="rwkv_recurrence",
    )(r, k, v, a, lw, cum, g, k_k, k_a, r_k, ln_w, ln_b)


def _compress_body(z_ref, pe_ref, w1_ref, w2_ref, o_ref):
    nb = z_ref.shape[0]
    half = z_ref.shape[1]
    z = z_ref[...]
    first = _dot(z, w1_ref[0:half, :])
    second = _dot(z, w1_ref[half:2 * half, :])
    pe_term = _dot(pe_ref[...], w1_ref[...])[0:1, :]
    hidden = first + pltpu.roll(second, nb - 1, axis=0) + pe_term
    out = _dot(jax.nn.gelu(hidden).astype(BF16), w2_ref[...])
    rows = _iota(out.shape, 0)
    o_ref[...] = jnp.where(rows < nb - 1, out, 0.0)


def nsa_compress(z, pe, w1, w2):
    B, G, NB, HALF = z.shape
    return pl.pallas_call(
        _compress_body,
        grid=(B, G),
        in_specs=[pl.BlockSpec((None, None, NB, HALF), lambda b, g: (b, g, 0, 0)),
                  pl.BlockSpec((8, 2 * HALF), lambda b, g: (0, 0)),
                  pl.BlockSpec((2 * HALF, CMP_HIDDEN), lambda b, g: (0, 0)),
                  pl.BlockSpec((CMP_HIDDEN, HEAD_DIM), lambda b, g: (0, 0))],
        out_specs=pl.BlockSpec((None, None, NB, HEAD_DIM), lambda b, g: (b, g, 0, 0)),
        out_shape=jax.ShapeDtypeStruct((B, G, NB, HEAD_DIM), F32),
        compiler_params=_cparams(("parallel", "parallel")),
        name="nsa_compress",
    )(z, pe, w1, w2)


def _t5_bucket_np(dist):
    n = np.maximum(dist, 0)
    max_exact = REL_BUCKETS // 2
    nf = np.maximum(n, 1).astype(np.float64)
    large = max_exact + (np.log(nf / max_exact) / math.log(REL_MAX_DIST / max_exact)
                         * (REL_BUCKETS - max_exact)).astype(np.int32)
    large = np.minimum(large, REL_BUCKETS - 1)
    return np.where(n < max_exact, n, large).astype(np.int32)


def _bias_body(tbl_ref, bk_ref, o_ref):
    h = pl.program_id(0)
    bk = bk_ref[...]
    acc = jnp.full(bk.shape, NEG_INF, F32)
    for b in range(REL_BUCKETS):
        acc = jnp.where(bk == b, tbl_ref[b, h], acc)
    o_ref[...] = acc


def bias_tile(table, dist_np, valid_np):
    P, Q = dist_np.shape
    H = table.shape[1]
    buckets = jnp.asarray(np.where(valid_np, _t5_bucket_np(dist_np), -1).astype(np.int32))
    return pl.pallas_call(
        _bias_body,
        grid=(H,),
        in_specs=[pl.BlockSpec(memory_space=pltpu.SMEM),
                  pl.BlockSpec((P, Q), lambda h: (0, 0))],
        out_specs=pl.BlockSpec((None, P, Q), lambda h: (h, 0, 0)),
        out_shape=jax.ShapeDtypeStruct((H, P, Q), F32),
        compiler_params=_cparams(("arbitrary",)),
        name="bias_tile",
    )(table, buckets)


def _heads_on_rows(qb):
    return jnp.concatenate([qb[:, r * HEAD_DIM:(r + 1) * HEAD_DIM] for r in range(REP)], axis=0)


def _heads_on_lanes(o):
    return jnp.concatenate([o[r * Q_BLOCK:(r + 1) * Q_BLOCK] for r in range(REP)], axis=1)


def _per_head_column(ref):
    return jnp.concatenate([jnp.broadcast_to(ref[r][:, 0:1], (Q_BLOCK, 1)) for r in range(REP)], axis=0)


def _nsa_cmp_body(*refs):
    for u in range(CMP_BLOCKS):
        _nsa_cmp_block(u, *refs)


def _nsa_cmp_block(u, q_ref, kp_ref, vp_ref, kn_ref, vn_ref, bias_ref, bfar_ref, o_ref, sel_ref):
    QB = Q_BLOCK
    NQ = REP * QB
    NB = kp_ref.shape[0]
    NS = NB // 4
    ns_shift = NS.bit_length() - 1
    blk = pl.program_id(2) * CMP_BLOCKS + u
    rows = slice(u * QB, (u + 1) * QB)
    start = pl.multiple_of(blk * (QB // CMP_STRIDE), 8)
    kn = kn_ref[pl.ds(start, CMP_NEAR), :].astype(BF16)
    vn = vn_ref[pl.ds(start, CMP_NEAR), :].astype(BF16)
    first_near = blk * (QB // CMP_STRIDE) - CMP_PAD
    lane = _iota((1, NB), 1)
    c_far = 4 * (lane & (NS - 1)) + (lane >> ns_shift)
    far_row = jnp.where(c_far < first_near, 0.0, NEG_INF)
    near_row = jnp.where(first_near + _iota((1, CMP_NEAR), 1) >= 0, 0.0, NEG_INF)
    q = _heads_on_rows(q_ref[rows, :]) * (HEAD_DIM ** -0.5)
    s_f = _dot_nt(q, kp_ref[...]) + _per_head_column(bfar_ref) + far_row
    s_n = _dot_nt(q, kn) + bias_ref[...].reshape(NQ, CMP_NEAR) + near_row
    m = jnp.maximum(jnp.max(s_f, axis=-1, keepdims=True), jnp.max(s_n, axis=-1, keepdims=True))
    m = jnp.maximum(m, 0.1 * NEG_INF)
    p_f = jnp.exp(s_f - m)
    p_n = jnp.exp(s_n - m)
    den = jnp.sum(p_f, axis=-1, keepdims=True) + jnp.sum(p_n, axis=-1, keepdims=True)
    inv = 1.0 / jnp.where(den > 0, den, 1.0)
    p_f = p_f * inv
    p_n = p_n * inv
    o_ref[rows, :] = _heads_on_lanes(_dot(p_f.astype(BF16), vp_ref[...]) + _dot(p_n.astype(BF16), vn))
    imp_far = p_f[0:QB]
    imp_near = p_n[0:QB]
    for r in range(1, REP):
        imp_far = imp_far + p_f[r * QB:(r + 1) * QB]
        imp_near = imp_near + p_n[r * QB:(r + 1) * QB]
    panel = [imp_far[:, m * NS:(m + 1) * NS] for m in range(4)]
    j = _iota((QB, NS), 1)
    prev3 = jnp.where(j == 0, 0.0, pltpu.roll(panel[3], 1, axis=1))
    imp = prev3 + 2.0 * panel[0] + 2.0 * panel[1] + 2.0 * panel[2] + panel[3]
    c_abs = first_near + _iota((CMP_NEAR, NS), 0)
    off = c_abs + 1 - 4 * _iota((CMP_NEAR, NS), 1)
    w_near = jnp.where((off == 0) | (off == 4), 1.0, jnp.where((off >= 1) & (off <= 3), 2.0, 0.0))
    imp = imp + _dot(imp_near, w_near, HI)
    cur = 2 * blk + (_iota((QB, NS), 0) >= SEL_BLOCK).astype(jnp.int32)
    forced = (j == 0) | (j == cur) | (j == cur - 1)
    score = jnp.where(forced, SEL_FORCE_SCORE, jnp.where(j <= cur, imp, -1.0))
    score = score.T
    jt = _iota((NS, QB), 0)
    sel = jnp.zeros((NS, QB), F32)
    for _ in range(min(SEL_TOPK, NS)):
        m = jnp.max(score, axis=0, keepdims=True)
        idx = jnp.min(jnp.where(score == m, jt, NS), axis=0, keepdims=True)
        hit = jt == idx
        sel = jnp.where(hit, 1.0, sel)
        score = jnp.where(hit, -3.0, score)
    pad = jnp.zeros((SEL_PAD_BLOCKS, QB), BF16)
    sel_ref[u] = jnp.concatenate([pad, sel.astype(BF16), pad], axis=0)


GROUP_LANES = REP * HEAD_DIM


def _q_spec(blocks=1):
    return pl.BlockSpec((None, blocks * Q_BLOCK, GROUP_LANES), lambda b, g, i: (b, i, g))


def nsa_compressed(p_nsa, kperm, vperm, knear, vnear, bias, bias_far):
    B, T, _ = p_nsa.shape
    G, NQ, N = KV_GROUPS, T // Q_BLOCK, HEAD_DIM
    NB = kperm.shape[2]
    nsp = NB // 4 + 2 * SEL_PAD_BLOCKS
    blk4 = lambda b, g, i: (b, g, 0, 0)
    return pl.pallas_call(
        _nsa_cmp_body,
        grid=(B, G, NQ // CMP_BLOCKS),
        in_specs=[_q_spec(CMP_BLOCKS),
                  pl.BlockSpec((None, None, NB, N), blk4), pl.BlockSpec((None, None, NB, N), blk4),
                  pl.BlockSpec((None, None, knear.shape[2], N), blk4),
                  pl.BlockSpec((None, None, knear.shape[2], N), blk4),
                  pl.BlockSpec((None, REP, Q_BLOCK, CMP_NEAR), lambda b, g, i: (g, 0, 0, 0)),
                  pl.BlockSpec((None, REP, 1, LANE), lambda b, g, i: (g, 0, 0, 0))],
        out_specs=[_q_spec(CMP_BLOCKS),
                   pl.BlockSpec((None, None, CMP_BLOCKS, nsp, Q_BLOCK), lambda b, g, i: (b, g, i, 0, 0))],
        out_shape=[jax.ShapeDtypeStruct((B, T, WIDTH), F32),
                   jax.ShapeDtypeStruct((B, G, NQ, nsp, Q_BLOCK), BF16)],
        compiler_params=_cparams(("parallel", "parallel", "parallel")),
        name="nsa_compressed",
    )(p_nsa, kperm, vperm, knear, vnear, bias, bias_far)


def _nsa_sel_body(q_ref, selt_ref, k_ref, vt_ref, biast_ref, bfar_ref, o_ref, sa_ref, sb_ref, ma_ref, mb_ref):
    QB = Q_BLOCK
    NQ = REP * QB
    blk = pl.program_id(2)
    qb = q_ref[...].astype(F32) * (HEAD_DIM ** -0.5 * LOG2E)
    qt = jnp.concatenate([qb[:, r * HEAD_DIM:(r + 1) * HEAD_DIM].T for r in range(REP)], axis=1).astype(BF16)
    near_chunk = blk + (SEL_PAD - SEL_SPAN) // V_CHUNK
    first = near_chunk * (V_CHUNK // SEL_BLOCK)
    pb_u = _iota((SEL_TILE_BLOCKS, QB), 0)
    zeros = jnp.zeros((AUG - HEAD_DIM - 2 * SEL_TILE_BLOCKS, NQ), BF16)
    bfar = bfar_ref[...]

    def far_scores(kt, s_ref, max_ref):
        b0 = pl.multiple_of(kt * SEL_TILE_BLOCKS, SEL_TILE_BLOCKS)
        picked = selt_ref[pl.ds(b0, SEL_TILE_BLOCKS), :].astype(F32) > 0.5
        neg = jnp.where(picked & (b0 + pb_u < first), 0.0, NEG_INF).astype(BF16)
        q_far = jnp.concatenate([qt, jnp.concatenate([neg] * REP, axis=1), bfar, zeros], axis=0)
        r0 = pl.multiple_of(kt * SEL_TILE, SEL_TILE)
        s = _dot(k_ref[pl.ds(r0, SEL_TILE), :], q_far)
        s_ref[...] = s
        max_ref[...] = jnp.max(s, axis=0, keepdims=True)

    def weighted_values(chunk0, p):
        out = jnp.zeros((HEAD_DIM, NQ), F32)
        for j in range(0, SEL_TILE // V_CHUNK, 2):
            vt = jnp.concatenate([vt_ref[chunk0 + j], vt_ref[chunk0 + j + 1]], axis=1)
            out = out + _dot(vt, p[j * V_CHUNK:(j + 2) * V_CHUNK])
        return out

    def far_update(kt, s_ref, max_ref, carry):
        m, l, acc = carry
        s = s_ref[...]
        m_new = jnp.maximum(m, max_ref[...])
        alpha = jnp.exp2(m - m_new)
        p = jnp.exp2(s - m_new)
        l = alpha * l + jnp.sum(p, axis=0, keepdims=True)
        p = p.astype(BF16)
        return m_new, l, alpha * acc + weighted_values(kt * (SEL_TILE // V_CHUNK), p)

    far_scores(0, sa_ref, ma_ref)

    row0 = pl.multiple_of(near_chunk * V_CHUNK, V_CHUNK)
    base = pl.multiple_of((first // SEL_TILE_BLOCKS) * SEL_TILE_BLOCKS, SEL_TILE_BLOCKS)
    rows = selt_ref[pl.ds(base, 2 * SEL_TILE_BLOCKS), :].astype(F32)
    picked = jnp.where(pb_u >= first - base, rows[0:SEL_TILE_BLOCKS], rows[SEL_TILE_BLOCKS:]) > 0.5
    neg = jnp.where(picked, 0.0, NEG_INF).astype(BF16)
    q_near = jnp.concatenate([qt, jnp.concatenate([neg] * REP, axis=1), jnp.zeros_like(bfar), zeros], axis=0)
    s = _dot(k_ref[pl.ds(row0, SEL_NEAR), :], q_near) + biast_ref[...]
    m = jnp.max(s, axis=0, keepdims=True)
    p = jnp.exp2(s - m)
    l = jnp.sum(p, axis=0, keepdims=True)
    acc = weighted_values(near_chunk, p.astype(BF16))

    def far_pair(j, carry):
        far_scores(2 * j + 1, sb_ref, mb_ref)
        carry = far_update(2 * j, sa_ref, ma_ref, carry)
        far_scores(2 * j + 2, sa_ref, ma_ref)
        return far_update(2 * j + 1, sb_ref, mb_ref, carry)

    n_far = (first + SEL_TILE_BLOCKS - 1) // SEL_TILE_BLOCKS
    carry = lax.fori_loop(0, n_far // 2, far_pair, (m, l, acc))
    m, l, acc = lax.cond(n_far % 2 == 1, lambda c: far_update(n_far - 1, sa_ref, ma_ref, c), lambda c: c, carry)
    out = acc / l
    o_ref[...] = jnp.concatenate([out[:, r * QB:(r + 1) * QB].T for r in range(REP)], axis=1)


def nsa_selected(p_nsa, selt, k_aug, vt, bias_t, bias_far):
    B, T, _ = p_nsa.shape
    G, NQ, N, W = KV_GROUPS, T // Q_BLOCK, HEAD_DIM, REP * Q_BLOCK
    nsp = selt.shape[3]
    rows = k_aug.shape[2]
    return pl.pallas_call(
        _nsa_sel_body,
        grid=(B, G, NQ),
        in_specs=[_q_spec(),
                  pl.BlockSpec((None, None, None, nsp, Q_BLOCK), lambda b, g, i: (b, g, i, 0, 0)),
                  pl.BlockSpec((None, None, rows, AUG), lambda b, g, i: (b, g, 0, 0)),
                  pl.BlockSpec((None, None, rows // V_CHUNK, N, V_CHUNK), lambda b, g, i: (b, g, 0, 0, 0)),
                  pl.BlockSpec((None, SEL_NEAR, W), lambda b, g, i: (g, 0, 0)),
                  pl.BlockSpec((None, SEL_TILE_BLOCKS, W), lambda b, g, i: (g, 0, 0))],
        out_specs=_q_spec(),
        out_shape=jax.ShapeDtypeStruct((B, T, WIDTH), F32),
        scratch_shapes=[pltpu.VMEM((SEL_TILE, W), F32)] * 2 + [pltpu.VMEM((1, W), F32)] * 2,
        compiler_params=_cparams(("parallel", "parallel", "parallel")),
        name="nsa_selected",
    )(p_nsa, selt, k_aug, vt, bias_t, bias_far)


def _band_body(*refs, pad, has_sink):
    if has_sink:
        q_ref, k_ref, v_ref, bias_ref, sink_ref, o_ref = refs
    else:
        q_ref, k_ref, v_ref, bias_ref, o_ref = refs
    QB = Q_BLOCK
    NQ = REP * QB
    width = pad + QB
    bias = bias_ref[...].reshape(NQ, width)
    if has_sink:
        sink = sink_ref[...].reshape(NQ, LANE)
    for u in range(BAND_BLOCKS):
        blk = pl.program_id(2) * BAND_BLOCKS + u
        row0 = pl.multiple_of(blk * QB, QB)
        kb = k_ref[pl.ds(row0, width), :]
        vb = v_ref[pl.ds(row0, width), :]
        before_start = jnp.where(row0 + _iota((1, width), 1) >= pad, 0.0, NEG_INF)
        q = _heads_on_rows(q_ref[u * QB:(u + 1) * QB, :]) * (HEAD_DIM ** -0.5)
        s = _dot_nt(q, kb) + bias + before_start
        m = jnp.max(s, axis=-1, keepdims=True)
        if has_sink:
            m = jnp.maximum(m, jnp.max(sink, axis=-1, keepdims=True))
        p = jnp.exp(s - m)
        den = jnp.sum(p, axis=-1, keepdims=True)
        if has_sink:
            den = den + jnp.sum(jnp.exp(sink - m), axis=-1, keepdims=True)
        p = p * (1.0 / den)
        o_ref[u * QB:(u + 1) * QB, :] = _heads_on_lanes(_dot(p.astype(BF16), vb))


def band_attention(p, k, v, bias, sink, pad):
    B, T, _ = p.shape
    G, NQ, N = KV_GROUPS, T // Q_BLOCK, HEAD_DIM
    rows = k.shape[2]
    width = pad + Q_BLOCK
    kv = pl.BlockSpec((None, None, rows, N), lambda b, g, i: (b, g, 0, 0))
    in_specs = [_q_spec(BAND_BLOCKS), kv, kv,
                pl.BlockSpec((None, REP, Q_BLOCK, width), lambda b, g, i: (g, 0, 0, 0))]
    args = [p, k, v, bias]
    if sink is not None:
        in_specs.append(pl.BlockSpec((None, REP, Q_BLOCK, LANE), lambda b, g, i: (g, 0, 0, 0)))
        args.append(sink)
    return pl.pallas_call(
        functools.partial(_band_body, pad=pad, has_sink=sink is not None),
        grid=(B, G, NQ // BAND_BLOCKS),
        in_specs=in_specs,
        out_specs=_q_spec(BAND_BLOCKS),
        out_shape=jax.ShapeDtypeStruct((B, T, WIDTH), F32),
        compiler_params=_cparams(("parallel", "parallel", "parallel")),
        name="band_attention",
    )(*args)


def _merge_body(x_ref, pg_ref, hg_ref, orw_ref, oc_ref, os_ref, ow_ref, osw_ref, wb_ref, wo_ref, gn_ref, ex_ref,
                o_ref):
    pg = pg_ref[...]
    head_gates = jax.nn.sigmoid(hg_ref[...])
    gates_hi = head_gates.astype(BF16)
    gates_lo = (head_gates - gates_hi.astype(F32)).astype(BF16)
    ge = _dot(gates_hi, ex_ref[...]) + _dot(gates_lo, ex_ref[...])
    o_nsa = (ge[:, 0:WIDTH] * oc_ref[...] + ge[:, WIDTH:2 * WIDTH] * os_ref[...]
             + ge[:, 2 * WIDTH:3 * WIDTH] * ow_ref[...])
    merged = (jax.nn.sigmoid(pg[:, 0:D_MODEL]) * _dot(orw_ref[...].astype(BF16), wb_ref[0])
              + jax.nn.sigmoid(pg[:, D_MODEL:2 * D_MODEL]) * _dot(o_nsa.astype(BF16), wb_ref[1])
              + jax.nn.sigmoid(pg[:, 2 * D_MODEL:3 * D_MODEL]) * _dot(osw_ref[...].astype(BF16), wb_ref[2]))
    y = _dot(merged.astype(BF16), wo_ref[...])
    o_ref[...] = x_ref[...] + _rms(y, gn_ref[...])


def merge_out(x, pg, p_rw, o_rwkv, o_c, o_s, o_w, o_swa, w_branch, w_out, g_post, expand, tm):
    M, D = x.shape
    row = lambda i: (i, 0)
    wide = pl.BlockSpec((tm, WIDTH), row)
    return pl.pallas_call(
        _merge_body,
        grid=(M // tm,),
        in_specs=[pl.BlockSpec((tm, D), row), pl.BlockSpec((tm, 3 * D), row),
                  pl.BlockSpec((tm, LANE), lambda i: (i, RW_COLS // LANE)),
                  wide, wide, wide, wide, wide,
                  pl.BlockSpec((3, WIDTH, D), lambda i: (0, 0, 0)),
                  pl.BlockSpec((D, D), lambda i: (0, 0)),
                  pl.BlockSpec((1, D), lambda i: (0, 0)),
                  pl.BlockSpec((LANE, 3 * WIDTH), lambda i: (0, 0))],
        out_specs=pl.BlockSpec((tm, D), row),
        out_shape=jax.ShapeDtypeStruct((M, D), F32),
        compiler_params=_cparams(("parallel",)),
        name="merge_out",
    )(x, pg, p_rw, o_rwkv, o_c, o_s, o_w, o_swa, w_branch, w_out, g_post, expand)


def _ffn_body(x_ref, gpre_ref, wg_ref, wu_ref, wd_ref, gpost_ref, o_ref, h_ref, acc_ref):
    j = pl.program_id(1)

    @pl.when(j == 0)
    def _():
        h_ref[...] = _rms(x_ref[...], gpre_ref[...]).astype(BF16)
        acc_ref[...] = jnp.zeros_like(acc_ref)

    h = h_ref[...]
    gate = _dot(h, wg_ref[...])
    act = gate * jax.nn.sigmoid(gate) * _dot(h, wu_ref[...])
    acc_ref[...] += _dot(act.astype(BF16), wd_ref[...])

    @pl.when(j == pl.num_programs(1) - 1)
    def _():
        o_ref[...] = x_ref[...] + _rms(acc_ref[...], gpost_ref[...])


def ffn(x, g_pre, w_gate, w_up, w_down, g_post, tm, th):
    M, D = x.shape
    Hd = w_gate.shape[1]
    return pl.pallas_call(
        _ffn_body,
        grid=(M // tm, Hd // th),
        in_specs=[pl.BlockSpec((tm, D), lambda i, j: (i, 0)),
                  pl.BlockSpec((1, D), lambda i, j: (0, 0)),
                  pl.BlockSpec((D, th), lambda i, j: (0, j)),
                  pl.BlockSpec((D, th), lambda i, j: (0, j)),
                  pl.BlockSpec((th, D), lambda i, j: (j, 0)),
                  pl.BlockSpec((1, D), lambda i, j: (0, 0))],
        out_specs=pl.BlockSpec((tm, D), lambda i, j: (i, 0)),
        out_shape=jax.ShapeDtypeStruct((M, D), F32),
        scratch_shapes=[pltpu.VMEM((tm, D), BF16), pltpu.VMEM((tm, D), F32)],
        compiler_params=_cparams(("parallel", "arbitrary")),
        name="ffn",
    )(x, g_pre, w_gate, w_up, w_down, g_post)


def _pad_cols(w, n):
    return jnp.pad(w, ((0, 0), (0, n - w.shape[1])))


def _pad_rows(w, n):
    return jnp.pad(w, ((0, n - w.shape[0]), (0, 0)))


def _kv_groups(z, B, T, pad):
    z = z.reshape(B, T, KV_GROUPS, HEAD_DIM).transpose(0, 2, 1, 3)
    return jnp.pad(z, ((0, 0), (0, 0), (pad, 0), (0, 0)))


def _group_bias(b):
    return b.reshape(KV_GROUPS, REP, b.shape[1], b.shape[2])


def kernel(x, norm_pre_mix, norm_post_mix, norm_pre_ffn, norm_post_ffn, w_in, rwkv_mu, rwkv_w0, rwkv_w2, rwkv_a0, rwkv_a2, rwkv_g2, rwkv_k_k, rwkv_k_a, rwkv_r_k, rwkv_ln_w, rwkv_ln_b, nsa_pe_k, nsa_pe_v, nsa_ck_w1, nsa_ck_w2, nsa_cv_w1, nsa_cv_w2, swa_sinks, rel_bias, w_branch, w_out, ffn_w_gate, ffn_w_up, ffn_w_down):
    B, T, D = x.shape
    depth = w_in.shape[0]
    M = B * T
    NB = T // CMP_STRIDE
    NS = T // SEL_BLOCK
    ffn_hidden = ffn_w_gate.shape[2]

    rw_cols = 3 * WIDTH + W_LORA + A_LORA + G_LORA
    nsa_cols = WIDTH + 6 * KV_WIDTH + 3 * N_HEADS
    swa_cols = WIDTH + 2 * KV_WIDTH
    c_nsa = rw_cols
    c_swa = c_nsa + nsa_cols
    c_gate = c_swa + swa_cols
    c_zw = 3 * WIDTH

    ii = np.arange(Q_BLOCK)[:, None]
    d_swa = SWA_WINDOW + ii - np.arange(SWA_WINDOW + Q_BLOCK)[None, :]
    d_win = NSA_WINDOW + ii - np.arange(NSA_WINDOW + Q_BLOCK)[None, :]
    d_sel = SEL_SPAN + ii - np.arange(SEL_NEAR)[None, :]
    d_cmp = ii + (CMP_PAD * CMP_STRIDE - CMP_LEN + 1) - CMP_STRIDE * np.arange(CMP_NEAR)[None, :]
    bias_swa = bias_tile(rel_bias, d_swa, (d_swa >= 0) & (d_swa < SWA_WINDOW))
    bias_win = bias_tile(rel_bias, d_win, (d_win >= 0) & (d_win < NSA_WINDOW))
    bias_sel = bias_tile(rel_bias, d_sel, d_sel >= 0)
    bias_cmp = bias_tile(rel_bias, d_cmp, d_cmp >= 0)
    bias_swa = _group_bias(bias_swa[N_HEADS:])
    bias_win = _group_bias(bias_win[:N_HEADS])
    bias_sel = _group_bias(bias_sel[:N_HEADS])
    bias_cmp = _group_bias(bias_cmp[:N_HEADS])
    far = rel_bias[REL_BUCKETS - 1, :N_HEADS].reshape(KV_GROUPS, REP, 1, 1)
    bias_sel_t = bias_sel.transpose(0, 3, 1, 2).reshape(KV_GROUPS, SEL_NEAR, REP * Q_BLOCK) * LOG2E
    far2 = far * LOG2E
    far_hi = far2.astype(BF16)
    far_lo = (far2 - far_hi.astype(F32)).astype(BF16)
    far_rows = jnp.concatenate([far_hi, far_lo], axis=2)
    far_rows = jnp.broadcast_to(far_rows, (KV_GROUPS, REP, 2, Q_BLOCK)).transpose(0, 2, 1, 3)
    bfar_sel = jnp.pad(far_rows.reshape(KV_GROUPS, 2, REP * Q_BLOCK), ((0, 0), (0, SEL_TILE_BLOCKS - 2), (0, 0)))
    rows_pad = SEL_PAD + T
    blk_in_tile = (np.arange(rows_pad) // SEL_BLOCK) % SEL_TILE_BLOCKS
    ka = np.zeros((rows_pad, AUG - HEAD_DIM), np.float32)
    ka[np.arange(rows_pad), blk_in_tile] = 1.0
    ka[:, SEL_TILE_BLOCKS:SEL_TILE_BLOCKS + 2] = 1.0
    key_aug = jnp.asarray(ka, BF16)
    bfar_cmp = jnp.broadcast_to(far, (KV_GROUPS, REP, 1, LANE))
    ex = np.zeros((LANE, 3 * WIDTH), np.float32)
    for c in range(3 * N_HEADS):
        ex[c, c * HEAD_DIM:(c + 1) * HEAD_DIM] = 1.0
    expand = jnp.asarray(ex, BF16)

    xf = x.reshape(M, D)
    for l in range(depth):
        w = w_in[l]
        w_rw = jnp.concatenate([w[:, :c_zw], _pad_cols(w[:, c_zw:c_zw + W_LORA], LANE),
                                _pad_cols(w[:, c_zw + W_LORA:c_zw + W_LORA + A_LORA], LANE),
                                w[:, c_zw + W_LORA + A_LORA:rw_cols],
                                _pad_cols(w[:, c_swa - 3 * N_HEADS:c_swa], LANE)], axis=1).astype(BF16)
        w_nsa = w[:, c_nsa:c_nsa + WIDTH + 6 * KV_WIDTH].astype(BF16)
        w_swa = w[:, c_swa:c_gate].astype(BF16)
        w_gate = w[:, c_gate:].astype(BF16)
        g_pre = norm_pre_mix[l][None, :]
        p_rw = norm_matmul(xf, g_pre, w_rw, F32, 1024)
        p_nsa = norm_matmul(xf, g_pre, w_nsa, BF16, 1024)
        p_swa = norm_matmul(xf, g_pre, w_swa, BF16, 1024)
        p_gate = norm_matmul(xf, g_pre, w_gate, F32, 512)

        mu = rwkv_mu[l]
        mu_p = jnp.concatenate([mu[:c_zw], jnp.pad(mu[c_zw:c_zw + W_LORA], (0, LANE - W_LORA)),
                                jnp.pad(mu[c_zw + W_LORA:c_zw + W_LORA + A_LORA], (0, LANE - A_LORA)),
                                mu[c_zw + W_LORA + A_LORA:]])[None, :]
        r, k, v, a, lw, cum, g = rwkv_prep(p_rw.reshape(B, T, RW_COLS + LANE), mu_p, rwkv_w0[l][None, :],
                                      _pad_rows(rwkv_w2[l], LANE), rwkv_a0[l][None, :],
                                      _pad_rows(rwkv_a2[l], LANE), rwkv_g2[l], 512)
        row = lambda z: z.reshape(1, WIDTH)
        o_rwkv = rwkv_recurrence(r, k, v, a, lw, cum, g, row(rwkv_k_k[l]), row(rwkv_k_a[l]), row(rwkv_r_k[l]),
                                 row(rwkv_ln_w[l]), row(rwkv_ln_b[l]), 512).reshape(M, WIDTH)

        pn = p_nsa.reshape(B, T, WIDTH + 6 * KV_WIDTH)
        seg = lambda n: pn[..., WIDTH + n * KV_WIDTH:WIDTH + (n + 1) * KV_WIDTH]

        def strides(z):
            z = z.reshape(B, NB, CMP_STRIDE, KV_GROUPS, HEAD_DIM).transpose(0, 3, 1, 2, 4)
            return z.reshape(B, KV_GROUPS, NB, CMP_STRIDE * HEAD_DIM)

        def pe_rows(pe):
            return jnp.broadcast_to(pe.reshape(1, CMP_LEN * HEAD_DIM), (8, CMP_LEN * HEAD_DIM)).astype(BF16)

        k_cmp = nsa_compress(strides(seg(0)), pe_rows(nsa_pe_k[l]), nsa_ck_w1[l].astype(BF16),
                             nsa_ck_w2[l].astype(BF16))
        v_cmp = nsa_compress(strides(seg(1)), pe_rows(nsa_pe_v[l]), nsa_cv_w1[l].astype(BF16),
                             nsa_cv_w2[l].astype(BF16))

        def panels(z):
            z = z.reshape(B, KV_GROUPS, NS, 4, HEAD_DIM).transpose(0, 1, 3, 2, 4)
            return z.reshape(B, KV_GROUPS, NB, HEAD_DIM).astype(BF16)

        near = lambda z: jnp.pad(z, ((0, 0), (0, 0), (CMP_PAD, 8), (0, 0)))
        o_c, sel_t = nsa_compressed(pn, panels(k_cmp), panels(v_cmp), near(k_cmp), near(v_cmp), bias_cmp, bfar_cmp)
        k_aug = jnp.concatenate([_kv_groups(seg(2), B, T, SEL_PAD),
                                 jnp.broadcast_to(key_aug, (B, KV_GROUPS) + key_aug.shape)], axis=-1)
        v_t = _kv_groups(seg(3), B, T, SEL_PAD).reshape(B, KV_GROUPS, -1, V_CHUNK, HEAD_DIM).swapaxes(-1, -2)
        o_s = nsa_selected(pn, sel_t, k_aug, v_t, bias_sel_t, bfar_sel)
        o_w = band_attention(pn, _kv_groups(seg(4), B, T, NSA_WINDOW), _kv_groups(seg(5), B, T, NSA_WINDOW),
                             bias_win, None, NSA_WINDOW)

        ps = p_swa.reshape(B, T, swa_cols)
        sink = jnp.where(np.arange(LANE) == 0, swa_sinks[l].reshape(KV_GROUPS, REP, 1, 1), NEG_INF)
        sink = jnp.broadcast_to(sink, (KV_GROUPS, REP, Q_BLOCK, LANE))
        o_swa = band_attention(ps,
                               _kv_groups(ps[..., WIDTH:WIDTH + KV_WIDTH], B, T, SWA_WINDOW),
                               _kv_groups(ps[..., WIDTH + KV_WIDTH:], B, T, SWA_WINDOW),
                               bias_swa, sink, SWA_WINDOW)

        flat = lambda o: o.reshape(M, WIDTH)
        xf = merge_out(xf, p_gate, p_rw, o_rwkv, flat(o_c), flat(o_s), flat(o_w),
                       flat(o_swa), w_branch[l].astype(BF16), w_out[l].astype(BF16),
                       norm_post_mix[l][None, :], expand, 256)

        xf = ffn(xf, norm_pre_ffn[l][None, :], ffn_w_gate[l].astype(BF16), ffn_w_up[l].astype(BF16),
                 ffn_w_down[l].astype(BF16), norm_post_ffn[l][None, :], 512, ffn_hidden // 2)
    return xf.reshape(B, T, D)
```

```python
import functools
import math

import numpy as np
import jax
import jax.numpy as jnp
from jax import lax
from jax.experimental import pallas as pl
from jax.experimental.pallas import tpu as pltpu

F32 = jnp.float32
BF16 = jnp.bfloat16

D_MODEL = 1024
HEAD_DIM = 64
N_HEADS = 8
KV_GROUPS = 2
REP = N_HEADS // KV_GROUPS
WIDTH = N_HEADS * HEAD_DIM
KV_WIDTH = KV_GROUPS * HEAD_DIM
W_LORA, A_LORA, G_LORA = 64, 64, 128
RWKV_LN_EPS = 64e-5
CMP_STRIDE = 16
CMP_LEN = 32
CMP_HIDDEN = 256
SEL_BLOCK = 64
SEL_TOPK = 16
NSA_WINDOW = 512
Q_BLOCK = 128
SWA_WINDOW = 128
REL_BUCKETS = 32
REL_MAX_DIST = 1024
NORM_EPS = 1e-6
NEG_INF = -1e30
LOG2E = math.log2(math.e)
LANE = 128
RWKV_CHUNK = 64
SEL_NEAR = 1024
SEL_SPAN = SEL_NEAR - Q_BLOCK
SEL_PAD = SEL_NEAR
SEL_TILE = 1024
SEL_TILE_BLOCKS = SEL_TILE // SEL_BLOCK
SEL_PAD_BLOCKS = SEL_PAD // SEL_BLOCK
AUG = 2 * HEAD_DIM
V_CHUNK = 128
BAND_BLOCKS = 4
CMP_BLOCKS = 2
CMP_NEAR = 64
CMP_PAD = CMP_NEAR - Q_BLOCK // CMP_STRIDE
VMEM_LIMIT = 56 * 1024 * 1024


def _cparams(sem):
    return pltpu.CompilerParams(dimension_semantics=sem, vmem_limit_bytes=VMEM_LIMIT)


def _dot(a, b, precision=None):
    return jnp.dot(a, b, preferred_element_type=F32, precision=precision)


def _dot_nt(a, b, precision=None):
    return lax.dot_general(a, b, (((1,), (1,)), ((), ())), preferred_element_type=F32, precision=precision)


def _dot3(a, b):
    a_hi, b_hi = a.astype(BF16), b.astype(BF16)
    a_lo, b_lo = (a - a_hi.astype(F32)).astype(BF16), (b - b_hi.astype(F32)).astype(BF16)
    return _dot(a_hi, b_hi) + (_dot(a_lo, b_hi) + _dot(a_hi, b_lo))


def _rms(x, g):
    return x * lax.rsqrt(jnp.mean(x * x, axis=-1, keepdims=True) + NORM_EPS) * g


def _iota(shape, dim):
    return lax.broadcasted_iota(jnp.int32, shape, dim)


def _norm_mm_body(x_ref, g_ref, w_ref, o_ref):
    h = _rms(x_ref[...], g_ref[...]).astype(BF16)
    o_ref[...] = _dot(h, w_ref[...]).astype(o_ref.dtype)


def norm_matmul(x, g, w, out_dtype, tm):
    M, D = x.shape
    N = w.shape[1]
    return pl.pallas_call(
        _norm_mm_body,
        grid=(M // tm,),
        in_specs=[pl.BlockSpec((tm, D), lambda i: (i, 0)),
                  pl.BlockSpec((1, D), lambda i: (0, 0)),
                  pl.BlockSpec((D, N), lambda i: (0, 0))],
        out_specs=pl.BlockSpec((tm, N), lambda i: (i, 0)),
        out_shape=jax.ShapeDtypeStruct((M, N), out_dtype),
        compiler_params=_cparams(("parallel",)),
        name="norm_matmul",
    )(x, g, w)


RW_COLS = 3 * WIDTH + 3 * LANE


def _rwkv_prep_body(p_ref, mu_ref, w0_ref, w2_ref, a0_ref, a2_ref, g2_ref,
                    r_o, k_o, v_o, a_o, lw_o, cum_o, g_o, carry_ref):
    tm = p_ref.shape[0]

    @pl.when(pl.program_id(1) == 0)
    def _():
        carry_ref[...] = jnp.zeros_like(carry_ref)

    p = p_ref[...]
    row = _iota(p.shape, 0)
    prev = jnp.where(row == 0, carry_ref[0:1, :], pltpu.roll(p, 1, axis=0))
    carry_ref[0:1, :] = p[tm - 1:tm, :]
    ps = p + (prev - p) * mu_ref[...]
    r_o[...] = ps[:, 0:WIDTH]
    k_o[...] = ps[:, WIDTH:2 * WIDTH]
    v_o[...] = ps[:, 2 * WIDTH:3 * WIDTH]
    zw = ps[:, 3 * WIDTH:3 * WIDTH + LANE]
    za = ps[:, 3 * WIDTH + LANE:3 * WIDTH + 2 * LANE]
    zg = ps[:, 3 * WIDTH + 2 * LANE:3 * WIDTH + 3 * LANE]
    z = -(w0_ref[...] + _dot3(jnp.tanh(zw), w2_ref[...]))
    softplus = jnp.maximum(z, 0.0) + jnp.log(1.0 + jnp.exp(-jnp.abs(z)))
    lw = -jnp.exp(-softplus - 0.5)
    lw_o[...] = lw
    C = RWKV_CHUNK
    tri = (_iota((C, C), 0) >= _iota((C, C), 1)).astype(F32)
    for c in range(tm // C):
        cum_o[c * C:(c + 1) * C, :] = _dot3(tri, lw[c * C:(c + 1) * C, :])
    a_o[...] = jax.nn.sigmoid(a0_ref[...] + _dot(za.astype(BF16), a2_ref[...].astype(BF16)))
    g_o[...] = _dot(jax.nn.sigmoid(zg).astype(BF16), g2_ref[...].astype(BF16))


def rwkv_prep(p, mu, w0, w2, a0, a2, g2, tm):
    B, T, _ = p.shape
    row = lambda b, i: (b, i, 0)
    fixed = lambda b, i: (0, 0)
    out = jax.ShapeDtypeStruct((B, T, WIDTH), F32)
    return pl.pallas_call(
        _rwkv_prep_body,
        grid=(B, T // tm),
        in_specs=[pl.BlockSpec((None, tm, RW_COLS), row),
                  pl.BlockSpec((1, RW_COLS), fixed),
                  pl.BlockSpec((1, WIDTH), fixed), pl.BlockSpec((LANE, WIDTH), fixed),
                  pl.BlockSpec((1, WIDTH), fixed), pl.BlockSpec((LANE, WIDTH), fixed),
                  pl.BlockSpec((LANE, WIDTH), fixed)],
        out_specs=[pl.BlockSpec((None, tm, WIDTH), row)] * 7,
        out_shape=[out] * 7,
        scratch_shapes=[pltpu.VMEM((8, RW_COLS), F32)],
        compiler_params=_cparams(("parallel", "arbitrary")),
        name="rwkv_prep",
    )(p, mu, w0, w2, a0, a2, g2)


RWKV_HEADS_PER_STEP = 4


def _bbmm(a, b):
    return lax.dot_general(a.astype(BF16), b.astype(BF16), (((2,), (1,)), ((0,), (0,))),
                           preferred_element_type=F32)


def _bbmm_nt(a, b):
    return lax.dot_general(a.astype(BF16), b.astype(BF16), (((2,), (2,)), ((0,), (0,))),
                           preferred_element_type=F32)


def _bbmm_tn(a, b):
    return lax.dot_general(a.astype(BF16), b.astype(BF16), (((1,), (1,)), ((0,), (0,))),
                           preferred_element_type=F32)


def _rwkv_rec_body(r_ref, k_ref, v_ref, a_ref, lw_ref, cum_ref, g_ref, kk_ref, ka_ref, rk_ref, lnw_ref, lnb_ref,
                   o_ref, s_ref):
    C = RWKV_CHUNK
    N = HEAD_DIM
    assert C == N
    tc = r_ref.shape[0]
    nc = tc // C
    hb = r_ref.shape[1] // N

    @pl.when(pl.program_id(2) == 0)
    def _():
        s_ref[...] = jnp.zeros_like(s_ref)

    def chunks(ref):
        x = ref[...]
        return jnp.concatenate([x[:, h * N:(h + 1) * N].reshape(nc, C, N) for h in range(hb)], axis=0)

    def per_head(ref):
        x = ref[...]
        return jnp.concatenate([jnp.broadcast_to(x[:, h * N:(h + 1) * N][None], (nc, 1, N))
                                for h in range(hb)], axis=0)

    row = _iota((C, C), 0)
    col = _iota((C, C), 1)
    incl = (row >= col)[None]
    strict = (row > col)[None]
    eye = (row == col)[None]
    nb = hb * nc
    r, k, v, a, lw = chunks(r_ref), chunks(k_ref), chunks(v_ref), chunks(a_ref), chunks(lw_ref)
    ones = jnp.ones((nb, N, N), BF16)

    def lane_sum(x):
        hi = x.astype(BF16)
        return _bbmm(hi, ones) + _bbmm(x - hi.astype(F32), ones)

    kk = k * per_head(kk_ref)
    kk = kk / jnp.maximum(jnp.sqrt(lane_sum(kk * kk)), 1e-12)
    k2 = k * (1.0 + (a - 1.0) * per_head(ka_ref))
    cum = chunks(cum_ref)
    p_incl = jnp.exp(cum)
    p_inv = jnp.exp(-cum)
    a_t = -kk * jnp.exp(cum - lw)
    r_t = r * p_incl
    b_t = kk * a * p_inv
    k_t = k2 * p_inv
    gram = _bbmm_nt(jnp.concatenate([a_t, r_t], axis=1), jnp.concatenate([b_t, k_t], axis=1))
    l_ab = jnp.where(strict, gram[:, 0:C, 0:C], 0.0)
    l_ak = jnp.where(strict, gram[:, 0:C, C:2 * C], 0.0)
    m_rb = jnp.where(incl, gram[:, C:2 * C, 0:C], 0.0)
    m_rk = jnp.where(incl, gram[:, C:2 * C, C:2 * C], 0.0)
    inv = jnp.where(eye, 1.0, l_ab)
    lp = l_ab
    for _ in range(5):
        lp = _bbmm(lp, lp)
        inv = inv + _bbmm(lp, inv)
    t_a = _bbmm(inv, a_t)
    t_v = _bbmm(inv, _bbmm(l_ak, v))
    p_end = p_incl[:, C - 1:C, :]
    bp = b_t * p_end
    kp = k_t * p_end
    ry = (r_t + _bbmm(m_rb, t_a)).reshape(hb, nc, C, N)
    yc = (_bbmm(m_rb, t_v) + _bbmm(m_rk, v)).reshape(hb, nc, C, N)
    am = (jnp.where(eye, jnp.broadcast_to(p_end, (nb, C, C)), 0.0) + _bbmm_tn(bp, t_a)).reshape(hb, nc, C, C)
    gm = (_bbmm_tn(bp, t_v) + _bbmm_tn(kp, v)).reshape(hb, nc, C, N)
    h = s_ref[...]
    ry_am = jnp.concatenate([ry, am], axis=2)
    ys = []
    for c in range(nc):
        both = _bbmm(ry_am[:, c], h)
        ys.append(both[:, 0:C] + yc[:, c])
        h = both[:, C:2 * C] + gm[:, c]
    s_ref[...] = h
    y = jnp.stack(ys, axis=1).reshape(nb, C, N)
    mean = lane_sum(y) * (1.0 / N)
    var = lane_sum(jnp.square(y - mean)) * (1.0 / N)
    yn = (y - mean) * lax.rsqrt(var + RWKV_LN_EPS) * per_head(lnw_ref) + per_head(lnb_ref)
    bonus = lane_sum(r * k2 * per_head(rk_ref)) * v
    out = (yn + bonus) * chunks(g_ref)
    o_ref[...] = jnp.concatenate([out[h * nc:(h + 1) * nc].reshape(tc, N) for h in range(hb)], axis=1)


def rwkv_recurrence(r, k, v, a, lw, cum, g, k_k, k_a, r_k, ln_w, ln_b, tc):
    B, T, W = r.shape
    slab = RWKV_HEADS_PER_STEP * HEAD_DIM
    seq = pl.BlockSpec((None, tc, slab), lambda b, h, i: (b, i, h))
    par = pl.BlockSpec((1, slab), lambda b, h, i: (0, h))
    return pl.pallas_call(
        _rwkv_rec_body,
        grid=(B, W // slab, T // tc),
        in_specs=[seq] * 7 + [par] * 5,
        out_specs=seq,
        out_shape=jax.ShapeDtypeStruct((B, T, W), F32),
        scratch_shapes=[pltpu.VMEM((RWKV_HEADS_PER_STEP, HEAD_DIM, HEAD_DIM), F32)],
        compiler_params=_cparams(("parallel", "parallel", "arbitrary")),
        name="rwkv_recurrence",
    )(r, k, v, a, lw, cum, g, k_k, k_a, r_k, ln_w, ln_b)


def _compress_body(z_ref, pe_ref, w1_ref, w2_ref, o_ref):
    nb = z_ref.shape[0]
    half = z_ref.shape[1]
    z = z_ref[...]
    first = _dot(z, w1_ref[0:half, :])
    second = _dot(z, w1_ref[half:2 * half, :])
    pe_term = _dot(pe_ref[...], w1_ref[...])[0:1, :]
    hidden = first + pltpu.roll(second, nb - 1, axis=0) + pe_term
    out = _dot(jax.nn.gelu(hidden).astype(BF16), w2_ref[...])
    rows = _iota(out.shape, 0)
    o_ref[...] = jnp.where(rows < nb - 1, out, 0.0)


def nsa_compress(z, pe, w1, w2):
    B, G, NB, HALF = z.shape
    return pl.pallas_call(
        _compress_body,
        grid=(B, G),
        in_specs=[pl.BlockSpec((None, None, NB, HALF), lambda b, g: (b, g, 0, 0)),
                  pl.BlockSpec((8, 2 * HALF), lambda b, g: (0, 0)),
                  pl.BlockSpec((2 * HALF, CMP_HIDDEN), lambda b, g: (0, 0)),
                  pl.BlockSpec((CMP_HIDDEN, HEAD_DIM), lambda b, g: (0, 0))],
        out_specs=pl.BlockSpec((None, None, NB, HEAD_DIM), lambda b, g: (b, g, 0, 0)),
        out_shape=jax.ShapeDtypeStruct((B, G, NB, HEAD_DIM), F32),
        compiler_params=_cparams(("parallel", "parallel")),
        name="nsa_compress",
    )(z, pe, w1, w2)


def _t5_bucket_np(dist):
    n = np.maximum(dist, 0)
    max_exact = REL_BUCKETS // 2
    nf = np.maximum(n, 1).astype(np.float64)
    large = max_exact + (np.log(nf / max_exact) / math.log(REL_MAX_DIST / max_exact)
                         * (REL_BUCKETS - max_exact)).astype(np.int32)
    large = np.minimum(large, REL_BUCKETS - 1)
    return np.where(n < max_exact, n, large).astype(np.int32)


def _bias_body(tbl_ref, bk_ref, o_ref):
    h = pl.program_id(0)
    bk = bk_ref[...]
    acc = jnp.full(bk.shape, NEG_INF, F32)
    for b in range(REL_BUCKETS):
        acc = jnp.where(bk == b, tbl_ref[b, h], acc)
    o_ref[...] = acc


def bias_tile(table, dist_np, valid_np):
    P, Q = dist_np.shape
    H = table.shape[1]
    buckets = jnp.asarray(np.where(valid_np, _t5_bucket_np(dist_np), -1).astype(np.int32))
    return pl.pallas_call(
        _bias_body,
        grid=(H,),
        in_specs=[pl.BlockSpec(memory_space=pltpu.SMEM),
                  pl.BlockSpec((P, Q), lambda h: (0, 0))],
        out_specs=pl.BlockSpec((None, P, Q), lambda h: (h, 0, 0)),
        out_shape=jax.ShapeDtypeStruct((H, P, Q), F32),
        compiler_params=_cparams(("arbitrary",)),
        name="bias_tile",
    )(table, buckets)


def _heads_on_rows(qb):
    return jnp.concatenate([qb[:, r * HEAD_DIM:(r + 1) * HEAD_DIM] for r in range(REP)], axis=0)


def _heads_on_lanes(o):
    return jnp.concatenate([o[r * Q_BLOCK:(r + 1) * Q_BLOCK] for r in range(REP)], axis=1)


def _per_head_column(ref):
    return jnp.concatenate([jnp.broadcast_to(ref[r][:, 0:1], (Q_BLOCK, 1)) for r in range(REP)], axis=0)


def _nsa_cmp_body(*refs):
    for u in range(CMP_BLOCKS):
        _nsa_cmp_block(u, *refs)


def _nsa_cmp_block(u, q_ref, kp_ref, vp_ref, kn_ref, vn_ref, bias_ref, bfar_ref, o_ref, sel_ref):
    QB = Q_BLOCK
    NQ = REP * QB
    NB = kp_ref.shape[0]
    NS = NB // 4
    ns_shift = NS.bit_length() - 1
    blk = pl.program_id(2) * CMP_BLOCKS + u
    rows = slice(u * QB, (u + 1) * QB)
    start = pl.multiple_of(blk * (QB // CMP_STRIDE), 8)
    kn = kn_ref[pl.ds(start, CMP_NEAR), :].astype(BF16)
    vn = vn_ref[pl.ds(start, CMP_NEAR), :].astype(BF16)
    first_near = blk * (QB // CMP_STRIDE) - CMP_PAD
    lane = _iota((1, NB), 1)
    c_far = 4 * (lane & (NS - 1)) + (lane >> ns_shift)
    far_row = jnp.where(c_far < first_near, 0.0, NEG_INF)
    near_row = jnp.where(first_near + _iota((1, CMP_NEAR), 1) >= 0, 0.0, NEG_INF)
    q = _heads_on_rows(q_ref[rows, :]) * (HEAD_DIM ** -0.5)
    s_f = _dot_nt(q, kp_ref[...]) + _per_head_column(bfar_ref) + far_row
    s_n = _dot_nt(q, kn) + bias_ref[...].reshape(NQ, CMP_NEAR) + near_row
    m = jnp.maximum(jnp.max(s_f, axis=-1, keepdims=True), jnp.max(s_n, axis=-1, keepdims=True))
    m = jnp.maximum(m, 0.1 * NEG_INF)
    p_f = jnp.exp(s_f - m)
    p_n = jnp.exp(s_n - m)
    den = jnp.sum(p_f, axis=-1, keepdims=True) + jnp.sum(p_n, axis=-1, keepdims=True)
    inv = 1.0 / jnp.where(den > 0, den, 1.0)
    p_f = p_f * inv
    p_n = p_n * inv
    o_ref[rows, :] = _heads_on_lanes(_dot(p_f.astype(BF16), vp_ref[...]) + _dot(p_n.astype(BF16), vn))
    imp_far = p_f[0:QB]
    imp_near = p_n[0:QB]
    for r in range(1, REP):
        imp_far = imp_far + p_f[r * QB:(r + 1) * QB]
        imp_near = imp_near + p_n[r * QB:(r + 1) * QB]
    panel = [imp_far[:, m * NS:(m + 1) * NS] for m in range(4)]
    j = _iota((QB, NS), 1)
    prev3 = jnp.where(j == 0, 0.0, pltpu.roll(panel[3], 1, axis=1))
    imp = prev3 + 2.0 * panel[0] + 2.0 * panel[1] + 2.0 * panel[2] + panel[3]
    c_abs = first_near + _iota((CMP_NEAR, NS), 0)
    off = c_abs + 1 - 4 * _iota((CMP_NEAR, NS), 1)
    w_near = jnp.where((off == 0) | (off == 4), 1.0, jnp.where((off >= 1) & (off <= 3), 2.0, 0.0))
    imp = imp + _dot3(imp_near, w_near)
    cur = 2 * blk + (_iota((QB, NS), 0) >= SEL_BLOCK).astype(jnp.int32)
    forced = (j == 0) | (j == cur) | (j == cur - 1)
    score = jnp.where(forced, -3.0, jnp.where(j <= cur, imp, -1.0))
    score = score.T
    jt = _iota((NS, QB), 0)
    cur_t = 2 * blk + (_iota((NS, QB), 1) >= SEL_BLOCK).astype(jnp.int32)
    sel = jnp.where((jt == 0) | (jt == cur_t) | (jt == cur_t - 1), 1.0, 0.0)
    for _ in range(min(SEL_TOPK, NS) - 3):
        m = jnp.max(score, axis=0, keepdims=True)
        idx = jnp.min(jnp.where(score == m, jt, NS), axis=0, keepdims=True)
        hit = jt == idx
        sel = jnp.where(hit, 1.0, sel)
        score = jnp.where(hit, -3.0, score)
    pad = jnp.zeros((SEL_PAD_BLOCKS, QB), BF16)
    sel_ref[u] = jnp.concatenate([pad, sel.astype(BF16), pad], axis=0)


GROUP_LANES = REP * HEAD_DIM


def _q_spec(blocks=1):
    return pl.BlockSpec((None, blocks * Q_BLOCK, GROUP_LANES), lambda b, g, i: (b, i, g))


def nsa_compressed(p_nsa, kperm, vperm, knear, vnear, bias, bias_far):
    B, T, _ = p_nsa.shape
    G, NQ, N = KV_GROUPS, T // Q_BLOCK, HEAD_DIM
    NB = kperm.shape[2]
    nsp = NB // 4 + 2 * SEL_PAD_BLOCKS
    blk4 = lambda b, g, i: (b, g, 0, 0)
    return pl.pallas_call(
        _nsa_cmp_body,
        grid=(B, G, NQ // CMP_BLOCKS),
        in_specs=[_q_spec(CMP_BLOCKS),
                  pl.BlockSpec((None, None, NB, N), blk4), pl.BlockSpec((None, None, NB, N), blk4),
                  pl.BlockSpec((None, None, knear.shape[2], N), blk4),
                  pl.BlockSpec((None, None, knear.shape[2], N), blk4),
                  pl.BlockSpec((None, REP, Q_BLOCK, CMP_NEAR), lambda b, g, i: (g, 0, 0, 0)),
                  pl.BlockSpec((None, REP, 1, LANE), lambda b, g, i: (g, 0, 0, 0))],
        out_specs=[_q_spec(CMP_BLOCKS),
                   pl.BlockSpec((None, None, CMP_BLOCKS, nsp, Q_BLOCK), lambda b, g, i: (b, g, i, 0, 0))],
        out_shape=[jax.ShapeDtypeStruct((B, T, WIDTH), F32),
                   jax.ShapeDtypeStruct((B, G, NQ, nsp, Q_BLOCK), BF16)],
        compiler_params=_cparams(("parallel", "parallel", "parallel")),
        name="nsa_compressed",
    )(p_nsa, kperm, vperm, knear, vnear, bias, bias_far)


def _nsa_sel_body(q_ref, selt_ref, k_ref, vt_ref, biast_ref, bfar_ref, o_ref, sa_ref, sb_ref, ma_ref, mb_ref):
    QB = Q_BLOCK
    NQ = REP * QB
    blk = pl.program_id(2)
    qb = q_ref[...].astype(F32) * (HEAD_DIM ** -0.5 * LOG2E)
    qt = jnp.concatenate([qb[:, r * HEAD_DIM:(r + 1) * HEAD_DIM].T for r in range(REP)], axis=1).astype(BF16)
    near_chunk = blk + (SEL_PAD - SEL_SPAN) // V_CHUNK
    first = near_chunk * (V_CHUNK // SEL_BLOCK)
    pb_u = _iota((SEL_TILE_BLOCKS, QB), 0)
    zeros = jnp.zeros((AUG - HEAD_DIM - 2 * SEL_TILE_BLOCKS, NQ), BF16)
    bfar = bfar_ref[...]

    def far_scores(kt, s_ref, max_ref):
        b0 = pl.multiple_of(kt * SEL_TILE_BLOCKS, SEL_TILE_BLOCKS)
        picked = selt_ref[pl.ds(b0, SEL_TILE_BLOCKS), :].astype(F32) > 0.5
        neg = jnp.where(picked & (b0 + pb_u < first), 0.0, NEG_INF).astype(BF16)
        q_far = jnp.concatenate([qt, jnp.concatenate([neg] * REP, axis=1), bfar, zeros], axis=0)
        r0 = pl.multiple_of(kt * SEL_TILE, SEL_TILE)
        s = _dot(k_ref[pl.ds(r0, SEL_TILE), :], q_far)
        s_ref[...] = s
        max_ref[...] = jnp.max(s, axis=0, keepdims=True)

    def weighted_values(chunk0, p):
        out = jnp.zeros((HEAD_DIM, NQ), F32)
        for j in range(0, SEL_TILE // V_CHUNK, 2):
            vt = jnp.concatenate([vt_ref[chunk0 + j], vt_ref[chunk0 + j + 1]], axis=1)
            out = out + _dot(vt, p[j * V_CHUNK:(j + 2) * V_CHUNK])
        return out

    def far_update(kt, s_ref, max_ref, carry):
        m, l, acc = carry
        s = s_ref[...]
        m_new = jnp.maximum(m, max_ref[...])
        alpha = jnp.exp2(m - m_new)
        p = jnp.exp2(s - m_new)
        l = alpha * l + jnp.sum(p, axis=0, keepdims=True)
        p = p.astype(BF16)
        return m_new, l, alpha * acc + weighted_values(kt * (SEL_TILE // V_CHUNK), p)

    far_scores(0, sa_ref, ma_ref)

    row0 = pl.multiple_of(near_chunk * V_CHUNK, V_CHUNK)
    base = pl.multiple_of((first // SEL_TILE_BLOCKS) * SEL_TILE_BLOCKS, SEL_TILE_BLOCKS)
    rows = selt_ref[pl.ds(base, 2 * SEL_TILE_BLOCKS), :].astype(F32)
    picked = jnp.where(pb_u >= first - base, rows[0:SEL_TILE_BLOCKS], rows[SEL_TILE_BLOCKS:]) > 0.5
    neg = jnp.where(picked, 0.0, NEG_INF).astype(BF16)
    q_near = jnp.concatenate([qt, jnp.concatenate([neg] * REP, axis=1), jnp.zeros_like(bfar), zeros], axis=0)
    s = _dot(k_ref[pl.ds(row0, SEL_NEAR), :], q_near) + biast_ref[...]
    m = jnp.max(s, axis=0, keepdims=True)
    p = jnp.exp2(s - m)
    l = jnp.sum(p, axis=0, keepdims=True)
    acc = weighted_values(near_chunk, p.astype(BF16))

    def far_pair(j, carry):
        far_scores(2 * j + 1, sb_ref, mb_ref)
        carry = far_update(2 * j, sa_ref, ma_ref, carry)
        far_scores(2 * j + 2, sa_ref, ma_ref)
        return far_update(2 * j + 1, sb_ref, mb_ref, carry)

    n_far = (first + SEL_TILE_BLOCKS - 1) // SEL_TILE_BLOCKS
    carry = lax.fori_loop(0, n_far // 2, far_pair, (m, l, acc))
    m, l, acc = lax.cond(n_far % 2 == 1, lambda c: far_update(n_far - 1, sa_ref, ma_ref, c), lambda c: c, carry)
    out = acc / l
    o_ref[...] = jnp.concatenate([out[:, r * QB:(r + 1) * QB].T for r in range(REP)], axis=1)


def nsa_selected(p_nsa, selt, k_aug, vt, bias_t, bias_far):
    B, T, _ = p_nsa.shape
    G, NQ, N, W = KV_GROUPS, T // Q_BLOCK, HEAD_DIM, REP * Q_BLOCK
    nsp = selt.shape[3]
    rows = k_aug.shape[2]
    return pl.pallas_call(
        _nsa_sel_body,
        grid=(B, G, NQ),
        in_specs=[_q_spec(),
                  pl.BlockSpec((None, None, None, nsp, Q_BLOCK), lambda b, g, i: (b, g, i, 0, 0)),
                  pl.BlockSpec((None, None, rows, AUG), lambda b, g, i: (b, g, 0, 0)),
                  pl.BlockSpec((None, None, rows // V_CHUNK, N, V_CHUNK), lambda b, g, i: (b, g, 0, 0, 0)),
                  pl.BlockSpec((None, SEL_NEAR, W), lambda b, g, i: (g, 0, 0)),
                  pl.BlockSpec((None, SEL_TILE_BLOCKS, W), lambda b, g, i: (g, 0, 0))],
        out_specs=_q_spec(),
        out_shape=jax.ShapeDtypeStruct((B, T, WIDTH), F32),
        scratch_shapes=[pltpu.VMEM((SEL_TILE, W), F32)] * 2 + [pltpu.VMEM((1, W), F32)] * 2,
        compiler_params=_cparams(("parallel", "parallel", "parallel")),
        name="nsa_selected",
    )(p_nsa, selt, k_aug, vt, bias_t, bias_far)


def _band_body(*refs, pad, has_sink):
    if has_sink:
        q_ref, k_ref, v_ref, bias_ref, sink_ref, o_ref = refs
    else:
        q_ref, k_ref, v_ref, bias_ref, o_ref = refs
    QB = Q_BLOCK
    NQ = REP * QB
    width = pad + QB
    bias = bias_ref[...].reshape(NQ, width)
    if has_sink:
        sink = sink_ref[...].reshape(NQ, LANE)
    for u in range(BAND_BLOCKS):
        blk = pl.program_id(2) * BAND_BLOCKS + u
        row0 = pl.multiple_of(blk * QB, QB)
        kb = k_ref[pl.ds(row0, width), :]
        vb = v_ref[pl.ds(row0, width), :]
        before_start = jnp.where(row0 + _iota((1, width), 1) >= pad, 0.0, NEG_INF)
        q = _heads_on_rows(q_ref[u * QB:(u + 1) * QB, :]) * (HEAD_DIM ** -0.5)
        s = _dot_nt(q, kb) + bias + before_start
        m = jnp.max(s, axis=-1, keepdims=True)
        if has_sink:
            m = jnp.maximum(m, jnp.max(sink, axis=-1, keepdims=True))
        p = jnp.exp(s - m)
        den = jnp.sum(p, axis=-1, keepdims=True)
        if has_sink:
            den = den + jnp.sum(jnp.exp(sink - m), axis=-1, keepdims=True)
        p = p * (1.0 / den)
        o_ref[u * QB:(u + 1) * QB, :] = _heads_on_lanes(_dot(p.astype(BF16), vb))


def band_attention(p, k, v, bias, sink, pad):
    B, T, _ = p.shape
    G, NQ, N = KV_GROUPS, T // Q_BLOCK, HEAD_DIM
    rows = k.shape[2]
    width = pad + Q_BLOCK
    kv = pl.BlockSpec((None, None, rows, N), lambda b, g, i: (b, g, 0, 0))
    in_specs = [_q_spec(BAND_BLOCKS), kv, kv,
                pl.BlockSpec((None, REP, Q_BLOCK, width), lambda b, g, i: (g, 0, 0, 0))]
    args = [p, k, v, bias]
    if sink is not None:
        in_specs.append(pl.BlockSpec((None, REP, Q_BLOCK, LANE), lambda b, g, i: (g, 0, 0, 0)))
        args.append(sink)
    return pl.pallas_call(
        functools.partial(_band_body, pad=pad, has_sink=sink is not None),
        grid=(B, G, NQ // BAND_BLOCKS),
        in_specs=in_specs,
        out_specs=_q_spec(BAND_BLOCKS),
        out_shape=jax.ShapeDtypeStruct((B, T, WIDTH), F32),
        compiler_params=_cparams(("parallel", "parallel", "parallel")),
        name="band_attention",
    )(*args)


def _merge_body(x_ref, pg_ref, hg_ref, orw_ref, oc_ref, os_ref, ow_ref, osw_ref, wb_ref, wo_ref, gn_ref, ex_ref,
                o_ref):
    pg = pg_ref[...]
    head_gates = jax.nn.sigmoid(hg_ref[...])
    gates_hi = head_gates.astype(BF16)
    gates_lo = (head_gates - gates_hi.astype(F32)).astype(BF16)
    ge = _dot(gates_hi, ex_ref[...]) + _dot(gates_lo, ex_ref[...])
    o_nsa = (ge[:, 0:WIDTH] * oc_ref[...] + ge[:, WIDTH:2 * WIDTH] * os_ref[...]
             + ge[:, 2 * WIDTH:3 * WIDTH] * ow_ref[...])
    merged = (jax.nn.sigmoid(pg[:, 0:D_MODEL]) * _dot(orw_ref[...].astype(BF16), wb_ref[0])
              + jax.nn.sigmoid(pg[:, D_MODEL:2 * D_MODEL]) * _dot(o_nsa.astype(BF16), wb_ref[1])
              + jax.nn.sigmoid(pg[:, 2 * D_MODEL:3 * D_MODEL]) * _dot(osw_ref[...].astype(BF16), wb_ref[2]))
    y = _dot(merged.astype(BF16), wo_ref[...])
    o_ref[...] = x_ref[...] + _rms(y, gn_ref[...])


def merge_out(x, pg, p_rw, o_rwkv, o_c, o_s, o_w, o_swa, w_branch, w_out, g_post, expand, tm):
    M, D = x.shape
    row = lambda i: (i, 0)
    wide = pl.BlockSpec((tm, WIDTH), row)
    return pl.pallas_call(
        _merge_body,
        grid=(M // tm,),
        in_specs=[pl.BlockSpec((tm, D), row), pl.BlockSpec((tm, 3 * D), row),
                  pl.BlockSpec((tm, LANE), lambda i: (i, RW_COLS // LANE)),
                  wide, wide, wide, wide, wide,
                  pl.BlockSpec((3, WIDTH, D), lambda i: (0, 0, 0)),
                  pl.BlockSpec((D, D), lambda i: (0, 0)),
                  pl.BlockSpec((1, D), lambda i: (0, 0)),
                  pl.BlockSpec((LANE, 3 * WIDTH), lambda i: (0, 0))],
        out_specs=pl.BlockSpec((tm, D), row),
        out_shape=jax.ShapeDtypeStruct((M, D), F32),
        compiler_params=_cparams(("parallel",)),
        name="merge_out",
    )(x, pg, p_rw, o_rwkv, o_c, o_s, o_w, o_swa, w_branch, w_out, g_post, expand)


def _ffn_body(x_ref, gpre_ref, wg_ref, wu_ref, wd_ref, gpost_ref, o_ref, h_ref, acc_ref):
    j = pl.program_id(1)

    @pl.when(j == 0)
    def _():
        h_ref[...] = _rms(x_ref[...], gpre_ref[...]).astype(BF16)
        acc_ref[...] = jnp.zeros_like(acc_ref)

    h = h_ref[...]
    gate = _dot(h, wg_ref[...])
    act = gate * jax.nn.sigmoid(gate) * _dot(h, wu_ref[...])
    acc_ref[...] += _dot(act.astype(BF16), wd_ref[...])

    @pl.when(j == pl.num_programs(1) - 1)
    def _():
        o_ref[...] = x_ref[...] + _rms(acc_ref[...], gpost_ref[...])


def ffn(x, g_pre, w_gate, w_up, w_down, g_post, tm, th):
    M, D = x.shape
    Hd = w_gate.shape[1]
    return pl.pallas_call(
        _ffn_body,
        grid=(M // tm, Hd // th),
        in_specs=[pl.BlockSpec((tm, D), lambda i, j: (i, 0)),
                  pl.BlockSpec((1, D), lambda i, j: (0, 0)),
                  pl.BlockSpec((D, th), lambda i, j: (0, j)),
                  pl.BlockSpec((D, th), lambda i, j: (0, j)),
                  pl.BlockSpec((th, D), lambda i, j: (j, 0)),
                  pl.BlockSpec((1, D), lambda i, j: (0, 0))],
        out_specs=pl.BlockSpec((tm, D), lambda i, j: (i, 0)),
        out_shape=jax.ShapeDtypeStruct((M, D), F32),
        scratch_shapes=[pltpu.VMEM((tm, D), BF16), pltpu.VMEM((tm, D), F32)],
        compiler_params=_cparams(("parallel", "arbitrary")),
        name="ffn",
    )(x, g_pre, w_gate, w_up, w_down, g_post)


def _pad_cols(w, n):
    return jnp.pad(w, ((0, 0), (0, n - w.shape[1])))


def _pad_rows(w, n):
    return jnp.pad(w, ((0, n - w.shape[0]), (0, 0)))


def _kv_groups(z, B, T, pad):
    z = z.reshape(B, T, KV_GROUPS, HEAD_DIM).transpose(0, 2, 1, 3)
    return jnp.pad(z, ((0, 0), (0, 0), (pad, 0), (0, 0)))


def _group_bias(b):
    return b.reshape(KV_GROUPS, REP, b.shape[1], b.shape[2])


def kernel(x, norm_pre_mix, norm_post_mix, norm_pre_ffn, norm_post_ffn, w_in, rwkv_mu, rwkv_w0, rwkv_w2, rwkv_a0, rwkv_a2, rwkv_g2, rwkv_k_k, rwkv_k_a, rwkv_r_k, rwkv_ln_w, rwkv_ln_b, nsa_pe_k, nsa_pe_v, nsa_ck_w1, nsa_ck_w2, nsa_cv_w1, nsa_cv_w2, swa_sinks, rel_bias, w_branch, w_out, ffn_w_gate, ffn_w_up, ffn_w_down):
    B, T, D = x.shape
    depth = w_in.shape[0]
    M = B * T
    NB = T // CMP_STRIDE
    NS = T // SEL_BLOCK
    ffn_hidden = ffn_w_gate.shape[2]

    rw_cols = 3 * WIDTH + W_LORA + A_LORA + G_LORA
    nsa_cols = WIDTH + 6 * KV_WIDTH + 3 * N_HEADS
    swa_cols = WIDTH + 2 * KV_WIDTH
    c_nsa = rw_cols
    c_swa = c_nsa + nsa_cols
    c_gate = c_swa + swa_cols
    c_zw = 3 * WIDTH

    ii = np.arange(Q_BLOCK)[:, None]
    d_swa = SWA_WINDOW + ii - np.arange(SWA_WINDOW + Q_BLOCK)[None, :]
    d_win = NSA_WINDOW + ii - np.arange(NSA_WINDOW + Q_BLOCK)[None, :]
    d_sel = SEL_SPAN + ii - np.arange(SEL_NEAR)[None, :]
    d_cmp = ii + (CMP_PAD * CMP_STRIDE - CMP_LEN + 1) - CMP_STRIDE * np.arange(CMP_NEAR)[None, :]
    bias_swa = bias_tile(rel_bias, d_swa, (d_swa >= 0) & (d_swa < SWA_WINDOW))
    bias_win = bias_tile(rel_bias, d_win, (d_win >= 0) & (d_win < NSA_WINDOW))
    bias_sel = bias_tile(rel_bias, d_sel, d_sel >= 0)
    bias_cmp = bias_tile(rel_bias, d_cmp, d_cmp >= 0)
    bias_swa = _group_bias(bias_swa[N_HEADS:])
    bias_win = _group_bias(bias_win[:N_HEADS])
    bias_sel = _group_bias(bias_sel[:N_HEADS])
    bias_cmp = _group_bias(bias_cmp[:N_HEADS])
    far = rel_bias[REL_BUCKETS - 1, :N_HEADS].reshape(KV_GROUPS, REP, 1, 1)
    bias_sel_t = bias_sel.transpose(0, 3, 1, 2).reshape(KV_GROUPS, SEL_NEAR, REP * Q_BLOCK) * LOG2E
    far2 = far * LOG2E
    far_hi = far2.astype(BF16)
    far_lo = (far2 - far_hi.astype(F32)).astype(BF16)
    far_rows = jnp.concatenate([far_hi, far_lo], axis=2)
    far_rows = jnp.broadcast_to(far_rows, (KV_GROUPS, REP, 2, Q_BLOCK)).transpose(0, 2, 1, 3)
    bfar_sel = jnp.pad(far_rows.reshape(KV_GROUPS, 2, REP * Q_BLOCK), ((0, 0), (0, SEL_TILE_BLOCKS - 2), (0, 0)))
    rows_pad = SEL_PAD + T
    blk_in_tile = (np.arange(rows_pad) // SEL_BLOCK) % SEL_TILE_BLOCKS
    ka = np.zeros((rows_pad, AUG - HEAD_DIM), np.float32)
    ka[np.arange(rows_pad), blk_in_tile] = 1.0
    ka[:, SEL_TILE_BLOCKS:SEL_TILE_BLOCKS + 2] = 1.0
    key_aug = jnp.asarray(ka, BF16)
    bfar_cmp = jnp.broadcast_to(far, (KV_GROUPS, REP, 1, LANE))
    ex = np.zeros((LANE, 3 * WIDTH), np.float32)
    for c in range(3 * N_HEADS):
        ex[c, c * HEAD_DIM:(c + 1) * HEAD_DIM] = 1.0
    expand = jnp.asarray(ex, BF16)

    xf = x.reshape(M, D)
    for l in range(depth):
        w = w_in[l]
        w_rw = jnp.concatenate([w[:, :c_zw], _pad_cols(w[:, c_zw:c_zw + W_LORA], LANE),
                                _pad_cols(w[:, c_zw + W_LORA:c_zw + W_LORA + A_LORA], LANE),
                                w[:, c_zw + W_LORA + A_LORA:rw_cols],
                                _pad_cols(w[:, c_swa - 3 * N_HEADS:c_swa], LANE)], axis=1).astype(BF16)
        w_nsa = w[:, c_nsa:c_nsa + WIDTH + 6 * KV_WIDTH].astype(BF16)
        w_swa = w[:, c_swa:c_gate].astype(BF16)
        w_gate = w[:, c_gate:].astype(BF16)
        g_pre = norm_pre_mix[l][None, :]
        p_rw = norm_matmul(xf, g_pre, w_rw, F32, 1024)
        p_nsa = norm_matmul(xf, g_pre, w_nsa, BF16, 1024)
        p_swa = norm_matmul(xf, g_pre, w_swa, BF16, 1024)
        p_gate = norm_matmul(xf, g_pre, w_gate, F32, 512)

        mu = rwkv_mu[l]
        mu_p = jnp.concatenate([mu[:c_zw], jnp.pad(mu[c_zw:c_zw + W_LORA], (0, LANE - W_LORA)),
                                jnp.pad(mu[c_zw + W_LORA:c_zw + W_LORA + A_LORA], (0, LANE - A_LORA)),
                                mu[c_zw + W_LORA + A_LORA:]])[None, :]
        r, k, v, a, lw, cum, g = rwkv_prep(p_rw.reshape(B, T, RW_COLS + LANE), mu_p, rwkv_w0[l][None, :],
                                      _pad_rows(rwkv_w2[l], LANE), rwkv_a0[l][None, :],
                                      _pad_rows(rwkv_a2[l], LANE), rwkv_g2[l], 512)
        row = lambda z: z.reshape(1, WIDTH)
        o_rwkv = rwkv_recurrence(r, k, v, a, lw, cum, g, row(rwkv_k_k[l]), row(rwkv_k_a[l]), row(rwkv_r_k[l]),
                                 row(rwkv_ln_w[l]), row(rwkv_ln_b[l]), 512).reshape(M, WIDTH)

        pn = p_nsa.reshape(B, T, WIDTH + 6 * KV_WIDTH)
        seg = lambda n: pn[..., WIDTH + n * KV_WIDTH:WIDTH + (n + 1) * KV_WIDTH]

        def strides(z):
            z = z.reshape(B, NB, CMP_STRIDE, KV_GROUPS, HEAD_DIM).transpose(0, 3, 1, 2, 4)
            return z.reshape(B, KV_GROUPS, NB, CMP_STRIDE * HEAD_DIM)

        def pe_rows(pe):
            return jnp.broadcast_to(pe.reshape(1, CMP_LEN * HEAD_DIM), (8, CMP_LEN * HEAD_DIM)).astype(BF16)

        k_cmp = nsa_compress(strides(seg(0)), pe_rows(nsa_pe_k[l]), nsa_ck_w1[l].astype(BF16),
                             nsa_ck_w2[l].astype(BF16))
        v_cmp = nsa_compress(strides(seg(1)), pe_rows(nsa_pe_v[l]), nsa_cv_w1[l].astype(BF16),
                             nsa_cv_w2[l].astype(BF16))

        def panels(z):
            z = z.reshape(B, KV_GROUPS, NS, 4, HEAD_DIM).transpose(0, 1, 3, 2, 4)
            return z.reshape(B, KV_GROUPS, NB, HEAD_DIM).astype(BF16)

        near = lambda z: jnp.pad(z, ((0, 0), (0, 0), (CMP_PAD, 8), (0, 0)))
        o_c, sel_t = nsa_compressed(pn, panels(k_cmp), panels(v_cmp), near(k_cmp), near(v_cmp), bias_cmp, bfar_cmp)
        k_aug = jnp.concatenate([_kv_groups(seg(2), B, T, SEL_PAD),
                                 jnp.broadcast_to(key_aug, (B, KV_GROUPS) + key_aug.shape)], axis=-1)
        v_t = _kv_groups(seg(3), B, T, SEL_PAD).reshape(B, KV_GROUPS, -1, V_CHUNK, HEAD_DIM).swapaxes(-1, -2)
        o_s = nsa_selected(pn, sel_t, k_aug, v_t, bias_sel_t, bfar_sel)
        o_w = band_attention(pn, _kv_groups(seg(4), B, T, NSA_WINDOW), _kv_groups(seg(5), B, T, NSA_WINDOW),
                             bias_win, None, NSA_WINDOW)

        ps = p_swa.reshape(B, T, swa_cols)
        sink = jnp.where(np.arange(LANE) == 0, swa_sinks[l].reshape(KV_GROUPS, REP, 1, 1), NEG_INF)
        sink = jnp.broadcast_to(sink, (KV_GROUPS, REP, Q_BLOCK, LANE))
        o_swa = band_attention(ps,
                               _kv_groups(ps[..., WIDTH:WIDTH + KV_WIDTH], B, T, SWA_WINDOW),
                               _kv_groups(ps[..., WIDTH + KV_WIDTH:], B, T, SWA_WINDOW),
                               bias_swa, sink, SWA_WINDOW)

        flat = lambda o: o.reshape(M, WIDTH)
        xf = merge_out(xf, p_gate, p_rw, o_rwkv, flat(o_c), flat(o_s), flat(o_w),
                       flat(o_swa), w_branch[l].astype(BF16), w_out[l].astype(BF16),
                       norm_post_mix[l][None, :], expand, 256)

        xf = ffn(xf, norm_pre_ffn[l][None, :], ffn_w_gate[l].astype(BF16), ffn_w_up[l].astype(BF16),
                 ffn_w_down[l].astype(BF16), norm_post_ffn[l][None, :], 512, ffn_hidden // 2)
    return xf.reshape(B, T, D)
```

```python
import functools
import math

import numpy as np
import jax
import jax.numpy as jnp
from jax import lax
from jax.experimental import pallas as pl
from jax.experimental.pallas import tpu as pltpu

F32 = jnp.float32
BF16 = jnp.bfloat16

D_MODEL = 1024
HEAD_DIM = 64
N_HEADS = 8
KV_GROUPS = 2
REP = N_HEADS // KV_GROUPS
WIDTH = N_HEADS * HEAD_DIM
KV_WIDTH = KV_GROUPS * HEAD_DIM
W_LORA, A_LORA, G_LORA = 64, 64, 128
RWKV_LN_EPS = 64e-5
CMP_STRIDE = 16
CMP_LEN = 32
CMP_HIDDEN = 256
SEL_BLOCK = 64
SEL_TOPK = 16
NSA_WINDOW = 512
Q_BLOCK = 128
SWA_WINDOW = 128
REL_BUCKETS = 32
REL_MAX_DIST = 1024
NORM_EPS = 1e-6
NEG_INF = -1e30
LOG2E = math.log2(math.e)
LANE = 128
RWKV_CHUNK = 64
SEL_NEAR = 1024
SEL_SPAN = SEL_NEAR - Q_BLOCK
SEL_PAD = SEL_NEAR
SEL_TILE = 1024
SEL_TILE_BLOCKS = SEL_TILE // SEL_BLOCK
SEL_PAD_BLOCKS = SEL_PAD // SEL_BLOCK
AUG = 2 * HEAD_DIM
V_CHUNK = 128
BAND_BLOCKS = 4
CMP_BLOCKS = 2
CMP_NEAR = 64
CMP_PAD = CMP_NEAR - Q_BLOCK // CMP_STRIDE
VMEM_LIMIT = 56 * 1024 * 1024


def _cparams(sem):
    return pltpu.CompilerParams(dimension_semantics=sem, vmem_limit_bytes=VMEM_LIMIT)


def _dot(a, b, precision=None):
    return jnp.dot(a, b, preferred_element_type=F32, precision=precision)


def _dot_nt(a, b, precision=None):
    return lax.dot_general(a, b, (((1,), (1,)), ((), ())), preferred_element_type=F32, precision=precision)


def _dot3(a, b):
    a_hi, b_hi = a.astype(BF16), b.astype(BF16)
    a_lo, b_lo = (a - a_hi.astype(F32)).astype(BF16), (b - b_hi.astype(F32)).astype(BF16)
    return _dot(a_hi, b_hi) + (_dot(a_lo, b_hi) + _dot(a_hi, b_lo))


def _rms(x, g):
    return x * lax.rsqrt(jnp.mean(x * x, axis=-1, keepdims=True) + NORM_EPS) * g


def _iota(shape, dim):
    return lax.broadcasted_iota(jnp.int32, shape, dim)


def _norm_mm_body(x_ref, g_ref, w_ref, o_ref):
    h = _rms(x_ref[...], g_ref[...]).astype(BF16)
    o_ref[...] = _dot(h, w_ref[...]).astype(o_ref.dtype)


def norm_matmul(x, g, w, out_dtype, tm):
    M, D = x.shape
    N = w.shape[1]
    return pl.pallas_call(
        _norm_mm_body,
        grid=(M // tm,),
        in_specs=[pl.BlockSpec((tm, D), lambda i: (i, 0)),
                  pl.BlockSpec((1, D), lambda i: (0, 0)),
                  pl.BlockSpec((D, N), lambda i: (0, 0))],
        out_specs=pl.BlockSpec((tm, N), lambda i: (i, 0)),
        out_shape=jax.ShapeDtypeStruct((M, N), out_dtype),
        compiler_params=_cparams(("parallel",)),
        name="norm_matmul",
    )(x, g, w)


RW_COLS = 3 * WIDTH + 3 * LANE


def _rwkv_prep_body(p_ref, mu_ref, w0_ref, w2_ref, a0_ref, a2_ref, g2_ref,
                    r_o, k_o, v_o, a_o, lw_o, cum_o, g_o, carry_ref):
    tm = p_ref.shape[0]

    @pl.when(pl.program_id(1) == 0)
    def _():
        carry_ref[...] = jnp.zeros_like(carry_ref)

    p = p_ref[...]
    row = _iota(p.shape, 0)
    prev = jnp.where(row == 0, carry_ref[0:1, :], pltpu.roll(p, 1, axis=0))
    carry_ref[0:1, :] = p[tm - 1:tm, :]
    ps = p + (prev - p) * mu_ref[...]
    r_o[...] = ps[:, 0:WIDTH]
    k_o[...] = ps[:, WIDTH:2 * WIDTH]
    v_o[...] = ps[:, 2 * WIDTH:3 * WIDTH]
    zw = ps[:, 3 * WIDTH:3 * WIDTH + LANE]
    za = ps[:, 3 * WIDTH + LANE:3 * WIDTH + 2 * LANE]
    zg = ps[:, 3 * WIDTH + 2 * LANE:3 * WIDTH + 3 * LANE]
    z = -(w0_ref[...] + _dot3(jnp.tanh(zw), w2_ref[...]))
    softplus = jnp.maximum(z, 0.0) + jnp.log(1.0 + jnp.exp(-jnp.abs(z)))
    lw = -jnp.exp(-softplus - 0.5)
    lw_o[...] = lw
    C = RWKV_CHUNK
    tri = (_iota((C, C), 0) >= _iota((C, C), 1)).astype(F32)
    for c in range(tm // C):
        cum_o[c * C:(c + 1) * C, :] = _dot3(tri, lw[c * C:(c + 1) * C, :])
    a_o[...] = jax.nn.sigmoid(a0_ref[...] + _dot(za.astype(BF16), a2_ref[...].astype(BF16)))
    g_o[...] = _dot(jax.nn.sigmoid(zg).astype(BF16), g2_ref[...].astype(BF16))


def rwkv_prep(p, mu, w0, w2, a0, a2, g2, tm):
    B, T, _ = p.shape
    row = lambda b, i: (b, i, 0)
    fixed = lambda b, i: (0, 0)
    out = jax.ShapeDtypeStruct((B, T, WIDTH), F32)
    return pl.pallas_call(
        _rwkv_prep_body,
        grid=(B, T // tm),
        in_specs=[pl.BlockSpec((None, tm, RW_COLS), row),
                  pl.BlockSpec((1, RW_COLS), fixed),
                  pl.BlockSpec((1, WIDTH), fixed), pl.BlockSpec((LANE, WIDTH), fixed),
                  pl.BlockSpec((1, WIDTH), fixed), pl.BlockSpec((LANE, WIDTH), fixed),
                  pl.BlockSpec((LANE, WIDTH), fixed)],
        out_specs=[pl.BlockSpec((None, tm, WIDTH), row)] * 7,
        out_shape=[out] * 7,
        scratch_shapes=[pltpu.VMEM((8, RW_COLS), F32)],
        compiler_params=_cparams(("parallel", "arbitrary")),
        name="rwkv_prep",
    )(p, mu, w0, w2, a0, a2, g2)


RWKV_HEADS_PER_STEP = 4


def _bbmm(a, b):
    return lax.dot_general(a.astype(BF16), b.astype(BF16), (((2,), (1,)), ((0,), (0,))),
                           preferred_element_type=F32)


def _bbmm_nt(a, b):
    return lax.dot_general(a.astype(BF16), b.astype(BF16), (((2,), (2,)), ((0,), (0,))),
                           preferred_element_type=F32)


def _bbmm_tn(a, b):
    return lax.dot_general(a.astype(BF16), b.astype(BF16), (((1,), (1,)), ((0,), (0,))),
                           preferred_element_type=F32)


def _rwkv_rec_body(r_ref, k_ref, v_ref, a_ref, lw_ref, cum_ref, g_ref, kk_ref, ka_ref, rk_ref, lnw_ref, lnb_ref,
                   o_ref, s_ref):
    C = RWKV_CHUNK
    N = HEAD_DIM
    assert C == N
    tc = r_ref.shape[0]
    nc = tc // C
    hb = r_ref.shape[1] // N

    @pl.when(pl.program_id(2) == 0)
    def _():
        s_ref[...] = jnp.zeros_like(s_ref)

    def chunks(ref):
        x = ref[...]
        return jnp.concatenate([x[:, h * N:(h + 1) * N].reshape(nc, C, N) for h in range(hb)], axis=0)

    def per_head(ref):
        x = ref[...]
        return jnp.concatenate([jnp.broadcast_to(x[:, h * N:(h + 1) * N][None], (nc, 1, N))
                                for h in range(hb)], axis=0)

    row = _iota((C, C), 0)
    col = _iota((C, C), 1)
    incl = (row >= col)[None]
    strict = (row > col)[None]
    eye = (row == col)[None]
    nb = hb * nc
    r, k, v, a, lw = chunks(r_ref), chunks(k_ref), chunks(v_ref), chunks(a_ref), chunks(lw_ref)
    ones = jnp.ones((nb, N, N), BF16)

    def lane_sum(x):
        hi = x.astype(BF16)
        return _bbmm(hi, ones) + _bbmm(x - hi.astype(F32), ones)

    kk = k * per_head(kk_ref)
    kk = kk / jnp.maximum(jnp.sqrt(lane_sum(kk * kk)), 1e-12)
    k2 = k * (1.0 + (a - 1.0) * per_head(ka_ref))
    cum = chunks(cum_ref)
    p_incl = jnp.exp(cum)
    p_inv = jnp.exp(-cum)
    a_t = -kk * jnp.exp(cum - lw)
    r_t = r * p_incl
    b_t = kk * a * p_inv
    k_t = k2 * p_inv
    gram = _bbmm_nt(jnp.concatenate([a_t, r_t], axis=1), jnp.concatenate([b_t, k_t], axis=1))
    l_ab = jnp.where(strict, gram[:, 0:C, 0:C], 0.0)
    l_ak = jnp.where(strict, gram[:, 0:C, C:2 * C], 0.0)
    m_rb = jnp.where(incl, gram[:, C:2 * C, 0:C], 0.0)
    m_rk = jnp.where(incl, gram[:, C:2 * C, C:2 * C], 0.0)
    inv = jnp.where(eye, 1.0, l_ab)
    lp = l_ab
    for _ in range(5):
        lp = _bbmm(lp, lp)
        inv = inv + _bbmm(lp, inv)
    t_a = _bbmm(inv, a_t)
    t_v = _bbmm(inv, _bbmm(l_ak, v))
    p_end = p_incl[:, C - 1:C, :]
    bp = b_t * p_end
    kp = k_t * p_end
    ry = (r_t + _bbmm(m_rb, t_a)).reshape(hb, nc, C, N)
    yc = (_bbmm(m_rb, t_v) + _bbmm(m_rk, v)).reshape(hb, nc, C, N)
    am = (jnp.where(eye, jnp.broadcast_to(p_end, (nb, C, C)), 0.0) + _bbmm_tn(bp, t_a)).reshape(hb, nc, C, C)
    gm = (_bbmm_tn(bp, t_v) + _bbmm_tn(kp, v)).reshape(hb, nc, C, N)
    h = s_ref[...]
    ry_am = jnp.concatenate([ry, am], axis=2)
    ys = []
    for c in range(nc):
        both = _bbmm(ry_am[:, c], h)
        ys.append(both[:, 0:C] + yc[:, c])
        h = both[:, C:2 * C] + gm[:, c]
    s_ref[...] = h
    y = jnp.stack(ys, axis=1).reshape(nb, C, N)
    mean = lane_sum(y) * (1.0 / N)
    var = lane_sum(jnp.square(y - mean)) * (1.0 / N)
    yn = (y - mean) * lax.rsqrt(var + RWKV_LN_EPS) * per_head(lnw_ref) + per_head(lnb_ref)
    bonus = lane_sum(r * k2 * per_head(rk_ref)) * v
    out = (yn + bonus) * chunks(g_ref)
    o_ref[...] = jnp.concatenate([out[h * nc:(h + 1) * nc].reshape(tc, N) for h in range(hb)], axis=1)


def rwkv_recurrence(r, k, v, a, lw, cum, g, k_k, k_a, r_k, ln_w, ln_b, tc):
    B, T, W = r.shape
    slab = RWKV_HEADS_PER_STEP * HEAD_DIM
    seq = pl.BlockSpec((None, tc, slab), lambda b, h, i: (b, i, h))
    par = pl.BlockSpec((1, slab), lambda b, h, i: (0, h))
    return pl.pallas_call(
        _rwkv_rec_body,
        grid=(B, W // slab, T // tc),
        in_specs=[seq] * 7 + [par] * 5,
        out_specs=seq,
        out_shape=jax.ShapeDtypeStruct((B, T, W), F32),
        scratch_shapes=[pltpu.VMEM((RWKV_HEADS_PER_STEP, HEAD_DIM, HEAD_DIM), F32)],
        compiler_params=_cparams(("parallel", "parallel", "arbitrary")),
        name="rwkv_recurrence",
    )(r, k, v, a, lw, cum, g, k_k, k_a, r_k, ln_w, ln_b)


def _compress_body(z_ref, pe_ref, w1_ref, w2_ref, o_ref):
    nb = z_ref.shape[0]
    half = z_ref.shape[1]
    z = z_ref[...]
    first = _dot(z, w1_ref[0:half, :])
    second = _dot(z, w1_ref[half:2 * half, :])
    pe_term = _dot(pe_ref[...], w1_ref[...])[0:1, :]
    hidden = first + pltpu.roll(second, nb - 1, axis=0) + pe_term
    out = _dot(jax.nn.gelu(hidden).astype(BF16), w2_ref[...])
    rows = _iota(out.shape, 0)
    o_ref[...] = jnp.where(rows < nb - 1, out, 0.0)


def nsa_compress(z, pe, w1, w2):
    B, G, NB, HALF = z.shape
    return pl.pallas_call(
        _compress_body,
        grid=(B, G),
        in_specs=[pl.BlockSpec((None, None, NB, HALF), lambda b, g: (b, g, 0, 0)),
                  pl.BlockSpec((8, 2 * HALF), lambda b, g: (0, 0)),
                  pl.BlockSpec((2 * HALF, CMP_HIDDEN), lambda b, g: (0, 0)),
                  pl.BlockSpec((CMP_HIDDEN, HEAD_DIM), lambda b, g: (0, 0))],
        out_specs=pl.BlockSpec((None, None, NB, HEAD_DIM), lambda b, g: (b, g, 0, 0)),
        out_shape=jax.ShapeDtypeStruct((B, G, NB, HEAD_DIM), F32),
        compiler_params=_cparams(("parallel", "parallel")),
        name="nsa_compress",
    )(z, pe, w1, w2)


def _t5_bucket_np(dist):
    n = np.maximum(dist, 0)
    max_exact = REL_BUCKETS // 2
    nf = np.maximum(n, 1).astype(np.float64)
    large = max_exact + (np.log(nf / max_exact) / math.log(REL_MAX_DIST / max_exact)
                         * (REL_BUCKETS - max_exact)).astype(np.int32)
    large = np.minimum(large, REL_BUCKETS - 1)
    return np.where(n < max_exact, n, large).astype(np.int32)


def _bias_body(tbl_ref, bk_ref, o_ref):
    h = pl.program_id(0)
    bk = bk_ref[...]
    acc = jnp.full(bk.shape, NEG_INF, F32)
    for b in range(REL_BUCKETS):
        acc = jnp.where(bk == b, tbl_ref[b, h], acc)
    o_ref[...] = acc


def bias_tile(table, dist_np, valid_np):
    P, Q = dist_np.shape
    H = table.shape[1]
    buckets = jnp.asarray(np.where(valid_np, _t5_bucket_np(dist_np), -1).astype(np.int32))
    return pl.pallas_call(
        _bias_body,
        grid=(H,),
        in_specs=[pl.BlockSpec(memory_space=pltpu.SMEM),
                  pl.BlockSpec((P, Q), lambda h: (0, 0))],
        out_specs=pl.BlockSpec((None, P, Q), lambda h: (h, 0, 0)),
        out_shape=jax.ShapeDtypeStruct((H, P, Q), F32),
        compiler_params=_cparams(("arbitrary",)),
        name="bias_tile",
    )(table, buckets)


def _heads_on_rows(qb):
    return jnp.concatenate([qb[:, r * HEAD_DIM:(r + 1) * HEAD_DIM] for r in range(REP)], axis=0)


def _heads_on_lanes(o):
    return jnp.concatenate([o[r * Q_BLOCK:(r + 1) * Q_BLOCK] for r in range(REP)], axis=1)


def _per_head_column(ref):
    return jnp.concatenate([jnp.broadcast_to(ref[r][:, 0:1], (Q_BLOCK, 1)) for r in range(REP)], axis=0)


def _nsa_cmp_body(*refs):
    for u in range(CMP_BLOCKS):
        _nsa_cmp_block(u, *refs)


def _nsa_cmp_block(u, q_ref, kp_ref, vp_ref, kn_ref, vn_ref, bias_ref, bfar_ref, o_ref, sel_ref):
    QB = Q_BLOCK
    NQ = REP * QB
    NB = kp_ref.shape[0]
    NS = NB // 4
    ns_shift = NS.bit_length() - 1
    blk = pl.program_id(2) * CMP_BLOCKS + u
    rows = slice(u * QB, (u + 1) * QB)
    start = pl.multiple_of(blk * (QB // CMP_STRIDE), 8)
    kn = kn_ref[pl.ds(start, CMP_NEAR), :].astype(BF16)
    vn = vn_ref[pl.ds(start, CMP_NEAR), :].astype(BF16)
    first_near = blk * (QB // CMP_STRIDE) - CMP_PAD
    lane = _iota((1, NB), 1)
    c_far = 4 * (lane & (NS - 1)) + (lane >> ns_shift)
    far_row = jnp.where(c_far < first_near, 0.0, NEG_INF)
    near_row = jnp.where(first_near + _iota((1, CMP_NEAR), 1) >= 0, 0.0, NEG_INF)
    q = _heads_on_rows(q_ref[rows, :]) * (HEAD_DIM ** -0.5)
    s_f = _dot_nt(q, kp_ref[...]) + _per_head_column(bfar_ref) + far_row
    s_n = _dot_nt(q, kn) + bias_ref[...].reshape(NQ, CMP_NEAR) + near_row
    m = jnp.maximum(jnp.max(s_f, axis=-1, keepdims=True), jnp.max(s_n, axis=-1, keepdims=True))
    m = jnp.maximum(m, 0.1 * NEG_INF)
    p_f = jnp.exp(s_f - m)
    p_n = jnp.exp(s_n - m)
    den = jnp.sum(p_f, axis=-1, keepdims=True) + jnp.sum(p_n, axis=-1, keepdims=True)
    inv = 1.0 / jnp.where(den > 0, den, 1.0)
    p_f = p_f * inv
    p_n = p_n * inv
    o_ref[rows, :] = _heads_on_lanes(_dot(p_f.astype(BF16), vp_ref[...]) + _dot(p_n.astype(BF16), vn))
    imp_far = p_f[0:QB]
    imp_near = p_n[0:QB]
    for r in range(1, REP):
        imp_far = imp_far + p_f[r * QB:(r + 1) * QB]
        imp_near = imp_near + p_n[r * QB:(r + 1) * QB]
    panel = [imp_far[:, m * NS:(m + 1) * NS] for m in range(4)]
    j = _iota((QB, NS), 1)
    prev3 = jnp.where(j == 0, 0.0, pltpu.roll(panel[3], 1, axis=1))
    imp = prev3 + 2.0 * panel[0] + 2.0 * panel[1] + 2.0 * panel[2] + panel[3]
    c_abs = first_near + _iota((CMP_NEAR, NS), 0)
    off = c_abs + 1 - 4 * _iota((CMP_NEAR, NS), 1)
    w_near = jnp.where((off == 0) | (off == 4), 1.0, jnp.where((off >= 1) & (off <= 3), 2.0, 0.0))
    imp = imp + _dot3(imp_near, w_near)
    cur = 2 * blk + (_iota((QB, NS), 0) >= SEL_BLOCK).astype(jnp.int32)
    forced = (j == 0) | (j == cur) | (j == cur - 1)
    score = jnp.where(forced, -3.0, jnp.where(j <= cur, imp, -1.0))
    score = score.T
    jt = _iota((NS, QB), 0)
    cur_t = 2 * blk + (_iota((NS, QB), 1) >= SEL_BLOCK).astype(jnp.int32)
    sel = jnp.where((jt == 0) | (jt == cur_t) | (jt == cur_t - 1), 1.0, 0.0)
    for _ in range(min(SEL_TOPK, NS) - 3):
        m = jnp.max(score, axis=0, keepdims=True)
        idx = jnp.min(jnp.where(score == m, jt, NS), axis=0, keepdims=True)
        hit = jt == idx
        sel = jnp.where(hit, 1.0, sel)
        score = jnp.where(hit, -3.0, score)
    pad = jnp.zeros((SEL_PAD_BLOCKS, QB), BF16)
    sel_ref[u] = jnp.concatenate([pad, sel.astype(BF16), pad], axis=0)


GROUP_LANES = REP * HEAD_DIM


def _q_spec(blocks=1):
    return pl.BlockSpec((None, blocks * Q_BLOCK, GROUP_LANES), lambda b, g, i: (b, i, g))


def nsa_compressed(p_nsa, kperm, vperm, knear, vnear, bias, bias_far):
    B, T, _ = p_nsa.shape
    G, NQ, N = KV_GROUPS, T // Q_BLOCK, HEAD_DIM
    NB = kperm.shape[2]
    nsp = NB // 4 + 2 * SEL_PAD_BLOCKS
    blk4 = lambda b, g, i: (b, g, 0, 0)
    return pl.pallas_call(
        _nsa_cmp_body,
        grid=(B, G, NQ // CMP_BLOCKS),
        in_specs=[_q_spec(CMP_BLOCKS),
                  pl.BlockSpec((None, None, NB, N), blk4), pl.BlockSpec((None, None, NB, N), blk4),
                  pl.BlockSpec((None, None, knear.shape[2], N), blk4),
                  pl.BlockSpec((None, None, knear.shape[2], N), blk4),
                  pl.BlockSpec((None, REP, Q_BLOCK, CMP_NEAR), lambda b, g, i: (g, 0, 0, 0)),
                  pl.BlockSpec((None, REP, 1, LANE), lambda b, g, i: (g, 0, 0, 0))],
        out_specs=[_q_spec(CMP_BLOCKS),
                   pl.BlockSpec((None, None, CMP_BLOCKS, nsp, Q_BLOCK), lambda b, g, i: (b, g, i, 0, 0))],
        out_shape=[jax.ShapeDtypeStruct((B, T, WIDTH), F32),
                   jax.ShapeDtypeStruct((B, G, NQ, nsp, Q_BLOCK), BF16)],
        compiler_params=_cparams(("parallel", "parallel", "parallel")),
        name="nsa_compressed",
    )(p_nsa, kperm, vperm, knear, vnear, bias, bias_far)


def _nsa_sel_body(q_ref, selt_ref, k_ref, vt_ref, biast_ref, bfar_ref, o_ref, sa_ref, sb_ref, ma_ref, mb_ref):
    QB = Q_BLOCK
    NQ = REP * QB
    blk = pl.program_id(2)
    qb = q_ref[...].astype(F32) * (HEAD_DIM ** -0.5 * LOG2E)
    qt = jnp.concatenate([qb[:, r * HEAD_DIM:(r + 1) * HEAD_DIM].T for r in range(REP)], axis=1).astype(BF16)
    near_chunk = blk + (SEL_PAD - SEL_SPAN) // V_CHUNK
    first = near_chunk * (V_CHUNK // SEL_BLOCK)
    pb_u = _iota((SEL_TILE_BLOCKS, QB), 0)
    zeros = jnp.zeros((AUG - HEAD_DIM - 2 * SEL_TILE_BLOCKS, NQ), BF16)
    bfar = bfar_ref[...]

    def far_scores(kt, s_ref, max_ref):
        b0 = pl.multiple_of(kt * SEL_TILE_BLOCKS, SEL_TILE_BLOCKS)
        picked = selt_ref[pl.ds(b0, SEL_TILE_BLOCKS), :].astype(F32) > 0.5
        neg = jnp.where(picked & (b0 + pb_u < first), 0.0, NEG_INF).astype(BF16)
        q_far = jnp.concatenate([qt, jnp.concatenate([neg] * REP, axis=1), bfar, zeros], axis=0)
        r0 = pl.multiple_of(kt * SEL_TILE, SEL_TILE)
        s = _dot(k_ref[pl.ds(r0, SEL_TILE), :], q_far)
        s_ref[...] = s
        max_ref[...] = jnp.max(s, axis=0, keepdims=True)

    def weighted_values(chunk0, p):
        out = jnp.zeros((HEAD_DIM, NQ), F32)
        for j in range(0, SEL_TILE // V_CHUNK, 2):
            vt = jnp.concatenate([vt_ref[chunk0 + j], vt_ref[chunk0 + j + 1]], axis=1)
            out = out + _dot(vt, p[j * V_CHUNK:(j + 2) * V_CHUNK])
        return out

    def far_update(kt, s_ref, max_ref, carry):
        m, l, acc = carry
        s = s_ref[...]
        m_new = jnp.maximum(m, max_ref[...])
        alpha = jnp.exp2(m - m_new)
        p = jnp.exp2(s - m_new)
        l = alpha * l + jnp.sum(p, axis=0, keepdims=True)
        p = p.astype(BF16)
        return m_new, l, alpha * acc + weighted_values(kt * (SEL_TILE // V_CHUNK), p)

    row0 = pl.multiple_of(near_chunk * V_CHUNK, V_CHUNK)
    base = pl.multiple_of((first // SEL_TILE_BLOCKS) * SEL_TILE_BLOCKS, SEL_TILE_BLOCKS)
    rows = selt_ref[pl.ds(base, 2 * SEL_TILE_BLOCKS), :].astype(F32)
    picked = jnp.where(pb_u >= first - base, rows[0:SEL_TILE_BLOCKS], rows[SEL_TILE_BLOCKS:]) > 0.5
    neg = jnp.where(picked, 0.0, NEG_INF).astype(BF16)
    q_near = jnp.concatenate([qt, jnp.concatenate([neg] * REP, axis=1), jnp.zeros_like(bfar), zeros], axis=0)
    s = _dot(k_ref[pl.ds(row0, SEL_NEAR), :], q_near) + biast_ref[...]
    far_scores(0, sa_ref, ma_ref)
    m = jnp.max(s, axis=0, keepdims=True)
    p = jnp.exp2(s - m)
    l = jnp.sum(p, axis=0, keepdims=True)
    acc = weighted_values(near_chunk, p.astype(BF16))

    def far_pair(j, carry):
        far_scores(2 * j + 1, sb_ref, mb_ref)
        carry = far_update(2 * j, sa_ref, ma_ref, carry)
        far_scores(2 * j + 2, sa_ref, ma_ref)
        return far_update(2 * j + 1, sb_ref, mb_ref, carry)

    n_far = (first + SEL_TILE_BLOCKS - 1) // SEL_TILE_BLOCKS
    carry = lax.fori_loop(0, n_far // 2, far_pair, (m, l, acc))
    m, l, acc = lax.cond(n_far % 2 == 1, lambda c: far_update(n_far - 1, sa_ref, ma_ref, c), lambda c: c, carry)
    out = acc / l
    o_ref[...] = jnp.concatenate([out[:, r * QB:(r + 1) * QB].T for r in range(REP)], axis=1)


def nsa_selected(p_nsa, selt, k_aug, vt, bias_t, bias_far):
    B, T, _ = p_nsa.shape
    G, NQ, N, W = KV_GROUPS, T // Q_BLOCK, HEAD_DIM, REP * Q_BLOCK
    nsp = selt.shape[3]
    rows = k_aug.shape[2]
    return pl.pallas_call(
        _nsa_sel_body,
        grid=(B, G, NQ),
        in_specs=[_q_spec(),
                  pl.BlockSpec((None, None, None, nsp, Q_BLOCK), lambda b, g, i: (b, g, i, 0, 0)),
                  pl.BlockSpec((None, None, rows, AUG), lambda b, g, i: (b, g, 0, 0)),
                  pl.BlockSpec((None, None, rows // V_CHUNK, N, V_CHUNK), lambda b, g, i: (b, g, 0, 0, 0)),
                  pl.BlockSpec((None, SEL_NEAR, W), lambda b, g, i: (g, 0, 0)),
                  pl.BlockSpec((None, SEL_TILE_BLOCKS, W), lambda b, g, i: (g, 0, 0))],
        out_specs=_q_spec(),
        out_shape=jax.ShapeDtypeStruct((B, T, WIDTH), F32),
        scratch_shapes=[pltpu.VMEM((SEL_TILE, W), F32)] * 2 + [pltpu.VMEM((1, W), F32)] * 2,
        compiler_params=_cparams(("parallel", "parallel", "parallel")),
        name="nsa_selected",
    )(p_nsa, selt, k_aug, vt, bias_t, bias_far)


def _band_body(*refs, pad, has_sink):
    if has_sink:
        q_ref, k_ref, v_ref, bias_ref, sink_ref, o_ref = refs
    else:
        q_ref, k_ref, v_ref, bias_ref, o_ref = refs
    QB = Q_BLOCK
    NQ = REP * QB
    width = pad + QB
    bias = bias_ref[...].reshape(NQ, width)
    if has_sink:
        sink = sink_ref[...].reshape(NQ, LANE)
    for u in range(BAND_BLOCKS):
        blk = pl.program_id(2) * BAND_BLOCKS + u
        row0 = pl.multiple_of(blk * QB, QB)
        kb = k_ref[pl.ds(row0, width), :]
        vb = v_ref[pl.ds(row0, width), :]
        before_start = jnp.where(row0 + _iota((1, width), 1) >= pad, 0.0, NEG_INF)
        q = _heads_on_rows(q_ref[u * QB:(u + 1) * QB, :]) * (HEAD_DIM ** -0.5)
        s = _dot_nt(q, kb) + bias + before_start
        m = jnp.max(s, axis=-1, keepdims=True)
        if has_sink:
            m = jnp.maximum(m, jnp.max(sink, axis=-1, keepdims=True))
        p = jnp.exp(s - m)
        den = jnp.sum(p, axis=-1, keepdims=True)
        if has_sink:
            den = den + jnp.sum(jnp.exp(sink - m), axis=-1, keepdims=True)
        p = p * (1.0 / den)
        o_ref[u * QB:(u + 1) * QB, :] = _heads_on_lanes(_dot(p.astype(BF16), vb))


def band_attention(p, k, v, bias, sink, pad):
    B, T, _ = p.shape
    G, NQ, N = KV_GROUPS, T // Q_BLOCK, HEAD_DIM
    rows = k.shape[2]
    width = pad + Q_BLOCK
    kv = pl.BlockSpec((None, None, rows, N), lambda b, g, i: (b, g, 0, 0))
    in_specs = [_q_spec(BAND_BLOCKS), kv, kv,
                pl.BlockSpec((None, REP, Q_BLOCK, width), lambda b, g, i: (g, 0, 0, 0))]
    args = [p, k, v, bias]
    if sink is not None:
        in_specs.append(pl.BlockSpec((None, REP, Q_BLOCK, LANE), lambda b, g, i: (g, 0, 0, 0)))
        args.append(sink)
    return pl.pallas_call(
        functools.partial(_band_body, pad=pad, has_sink=sink is not None),
        grid=(B, G, NQ // BAND_BLOCKS),
        in_specs=in_specs,
        out_specs=_q_spec(BAND_BLOCKS),
        out_shape=jax.ShapeDtypeStruct((B, T, WIDTH), F32),
        compiler_params=_cparams(("parallel", "parallel", "parallel")),
        name="band_attention",
    )(*args)


def _merge_body(x_ref, pg_ref, hg_ref, orw_ref, oc_ref, os_ref, ow_ref, osw_ref, wb_ref, wo_ref, gn_ref, ex_ref,
                o_ref):
    pg = pg_ref[...]
    head_gates = jax.nn.sigmoid(hg_ref[...])
    gates_hi = head_gates.astype(BF16)
    gates_lo = (head_gates - gates_hi.astype(F32)).astype(BF16)
    ge = _dot(gates_hi, ex_ref[...]) + _dot(gates_lo, ex_ref[...])
    o_nsa = (ge[:, 0:WIDTH] * oc_ref[...] + ge[:, WIDTH:2 * WIDTH] * os_ref[...]
             + ge[:, 2 * WIDTH:3 * WIDTH] * ow_ref[...])
    merged = (jax.nn.sigmoid(pg[:, 0:D_MODEL]) * _dot(orw_ref[...].astype(BF16), wb_ref[0])
              + jax.nn.sigmoid(pg[:, D_MODEL:2 * D_MODEL]) * _dot(o_nsa.astype(BF16), wb_ref[1])
              + jax.nn.sigmoid(pg[:, 2 * D_MODEL:3 * D_MODEL]) * _dot(osw_ref[...].astype(BF16), wb_ref[2]))
    y = _dot(merged.astype(BF16), wo_ref[...])
    o_ref[...] = x_ref[...] + _rms(y, gn_ref[...])


def merge_out(x, pg, p_rw, o_rwkv, o_c, o_s, o_w, o_swa, w_branch, w_out, g_post, expand, tm):
    M, D = x.shape
    row = lambda i: (i, 0)
    wide = pl.BlockSpec((tm, WIDTH), row)
    return pl.pallas_call(
        _merge_body,
        grid=(M // tm,),
        in_specs=[pl.BlockSpec((tm, D), row), pl.BlockSpec((tm, 3 * D), row),
                  pl.BlockSpec((tm, LANE), lambda i: (i, RW_COLS // LANE)),
                  wide, wide, wide, wide, wide,
                  pl.BlockSpec((3, WIDTH, D), lambda i: (0, 0, 0)),
                  pl.BlockSpec((D, D), lambda i: (0, 0)),
                  pl.BlockSpec((1, D), lambda i: (0, 0)),
                  pl.BlockSpec((LANE, 3 * WIDTH), lambda i: (0, 0))],
        out_specs=pl.BlockSpec((tm, D), row),
        out_shape=jax.ShapeDtypeStruct((M, D), F32),
        compiler_params=_cparams(("parallel",)),
        name="merge_out",
    )(x, pg, p_rw, o_rwkv, o_c, o_s, o_w, o_swa, w_branch, w_out, g_post, expand)


FFN_CHUNKS = 2


def _ffn_body(x_ref, gpre_ref, wg_ref, wu_ref, wd_ref, gpost_ref, o_ref):
    x = x_ref[...]
    h = _rms(x, gpre_ref[...]).astype(BF16)
    th = wg_ref.shape[1] // FFN_CHUNKS
    acc = jnp.zeros(x.shape, F32)
    for c in range(FFN_CHUNKS):
        cols = slice(c * th, (c + 1) * th)
        gate = _dot(h, wg_ref[:, cols])
        act = gate * jax.nn.sigmoid(gate) * _dot(h, wu_ref[:, cols])
        acc = acc + _dot(act.astype(BF16), wd_ref[cols, :])
    o_ref[...] = x + _rms(acc, gpost_ref[...])


def ffn(x, g_pre, w_gate, w_up, w_down, g_post, tm):
    M, D = x.shape
    Hd = w_gate.shape[1]
    whole = lambda shape: pl.BlockSpec(shape, lambda i: (0, 0), pipeline_mode=pl.Buffered(1))
    return pl.pallas_call(
        _ffn_body,
        grid=(M // tm,),
        in_specs=[pl.BlockSpec((tm, D), lambda i: (i, 0)),
                  pl.BlockSpec((1, D), lambda i: (0, 0)),
                  whole((D, Hd)), whole((D, Hd)), whole((Hd, D)),
                  pl.BlockSpec((1, D), lambda i: (0, 0))],
        out_specs=pl.BlockSpec((tm, D), lambda i: (i, 0)),
        out_shape=jax.ShapeDtypeStruct((M, D), F32),
        compiler_params=_cparams(("parallel",)),
        name="ffn",
    )(x, g_pre, w_gate, w_up, w_down, g_post)


def _pad_cols(w, n):
    return jnp.pad(w, ((0, 0), (0, n - w.shape[1])))


def _pad_rows(w, n):
    return jnp.pad(w, ((0, n - w.shape[0]), (0, 0)))


def _kv_groups(z, B, T, pad):
    z = z.reshape(B, T, KV_GROUPS, HEAD_DIM).transpose(0, 2, 1, 3)
    return jnp.pad(z, ((0, 0), (0, 0), (pad, 0), (0, 0)))


def _group_bias(b):
    return b.reshape(KV_GROUPS, REP, b.shape[1], b.shape[2])


def kernel(x, norm_pre_mix, norm_post_mix, norm_pre_ffn, norm_post_ffn, w_in, rwkv_mu, rwkv_w0, rwkv_w2, rwkv_a0, rwkv_a2, rwkv_g2, rwkv_k_k, rwkv_k_a, rwkv_r_k, rwkv_ln_w, rwkv_ln_b, nsa_pe_k, nsa_pe_v, nsa_ck_w1, nsa_ck_w2, nsa_cv_w1, nsa_cv_w2, swa_sinks, rel_bias, w_branch, w_out, ffn_w_gate, ffn_w_up, ffn_w_down):
    B, T, D = x.shape
    depth = w_in.shape[0]
    M = B * T
    NB = T // CMP_STRIDE
    NS = T // SEL_BLOCK

    rw_cols = 3 * WIDTH + W_LORA + A_LORA + G_LORA
    nsa_cols = WIDTH + 6 * KV_WIDTH + 3 * N_HEADS
    swa_cols = WIDTH + 2 * KV_WIDTH
    c_nsa = rw_cols
    c_swa = c_nsa + nsa_cols
    c_gate = c_swa + swa_cols
    c_zw = 3 * WIDTH

    ii = np.arange(Q_BLOCK)[:, None]
    d_swa = SWA_WINDOW + ii - np.arange(SWA_WINDOW + Q_BLOCK)[None, :]
    d_win = NSA_WINDOW + ii - np.arange(NSA_WINDOW + Q_BLOCK)[None, :]
    d_sel = SEL_SPAN + ii - np.arange(SEL_NEAR)[None, :]
    d_cmp = ii + (CMP_PAD * CMP_STRIDE - CMP_LEN + 1) - CMP_STRIDE * np.arange(CMP_NEAR)[None, :]
    bias_swa = bias_tile(rel_bias, d_swa, (d_swa >= 0) & (d_swa < SWA_WINDOW))
    bias_win = bias_tile(rel_bias, d_win, (d_win >= 0) & (d_win < NSA_WINDOW))
    bias_sel = bias_tile(rel_bias, d_sel, d_sel >= 0)
    bias_cmp = bias_tile(rel_bias, d_cmp, d_cmp >= 0)
    bias_swa = _group_bias(bias_swa[N_HEADS:])
    bias_win = _group_bias(bias_win[:N_HEADS])
    bias_sel = _group_bias(bias_sel[:N_HEADS])
    bias_cmp = _group_bias(bias_cmp[:N_HEADS])
    far = rel_bias[REL_BUCKETS - 1, :N_HEADS].reshape(KV_GROUPS, REP, 1, 1)
    bias_sel_t = bias_sel.transpose(0, 3, 1, 2).reshape(KV_GROUPS, SEL_NEAR, REP * Q_BLOCK) * LOG2E
    far2 = far * LOG2E
    far_hi = far2.astype(BF16)
    far_lo = (far2 - far_hi.astype(F32)).astype(BF16)
    far_rows = jnp.concatenate([far_hi, far_lo], axis=2)
    far_rows = jnp.broadcast_to(far_rows, (KV_GROUPS, REP, 2, Q_BLOCK)).transpose(0, 2, 1, 3)
    bfar_sel = jnp.pad(far_rows.reshape(KV_GROUPS, 2, REP * Q_BLOCK), ((0, 0), (0, SEL_TILE_BLOCKS - 2), (0, 0)))
    rows_pad = SEL_PAD + T
    blk_in_tile = (np.arange(rows_pad) // SEL_BLOCK) % SEL_TILE_BLOCKS
    ka = np.zeros((rows_pad, AUG - HEAD_DIM), np.float32)
    ka[np.arange(rows_pad), blk_in_tile] = 1.0
    ka[:, SEL_TILE_BLOCKS:SEL_TILE_BLOCKS + 2] = 1.0
    key_aug = jnp.asarray(ka, BF16)
    bfar_cmp = jnp.broadcast_to(far, (KV_GROUPS, REP, 1, LANE))
    ex = np.zeros((LANE, 3 * WIDTH), np.float32)
    for c in range(3 * N_HEADS):
        ex[c, c * HEAD_DIM:(c + 1) * HEAD_DIM] = 1.0
    expand = jnp.asarray(ex, BF16)

    xf = x.reshape(M, D)
    for l in range(depth):
        w = w_in[l]
        w_rw = jnp.concatenate([w[:, :c_zw], _pad_cols(w[:, c_zw:c_zw + W_LORA], LANE),
                                _pad_cols(w[:, c_zw + W_LORA:c_zw + W_LORA + A_LORA], LANE),
                                w[:, c_zw + W_LORA + A_LORA:rw_cols],
                                _pad_cols(w[:, c_swa - 3 * N_HEADS:c_swa], LANE)], axis=1).astype(BF16)
        w_nsa = w[:, c_nsa:c_nsa + WIDTH + 6 * KV_WIDTH].astype(BF16)
        w_swa = w[:, c_swa:c_gate].astype(BF16)
        w_gate = w[:, c_gate:].astype(BF16)
        g_pre = norm_pre_mix[l][None, :]
        p_rw = norm_matmul(xf, g_pre, w_rw, F32, 1024)
        p_nsa = norm_matmul(xf, g_pre, w_nsa, BF16, 1024)
        p_swa = norm_matmul(xf, g_pre, w_swa, BF16, 1024)
        p_gate = norm_matmul(xf, g_pre, w_gate, F32, 512)

        mu = rwkv_mu[l]
        mu_p = jnp.concatenate([mu[:c_zw], jnp.pad(mu[c_zw:c_zw + W_LORA], (0, LANE - W_LORA)),
                                jnp.pad(mu[c_zw + W_LORA:c_zw + W_LORA + A_LORA], (0, LANE - A_LORA)),
                                mu[c_zw + W_LORA + A_LORA:]])[None, :]
        r, k, v, a, lw, cum, g = rwkv_prep(p_rw.reshape(B, T, RW_COLS + LANE), mu_p, rwkv_w0[l][None, :],
                                      _pad_rows(rwkv_w2[l], LANE), rwkv_a0[l][None, :],
                                      _pad_rows(rwkv_a2[l], LANE), rwkv_g2[l], 512)
        row = lambda z: z.reshape(1, WIDTH)
        o_rwkv = rwkv_recurrence(r, k, v, a, lw, cum, g, row(rwkv_k_k[l]), row(rwkv_k_a[l]), row(rwkv_r_k[l]),
                                 row(rwkv_ln_w[l]), row(rwkv_ln_b[l]), 512).reshape(M, WIDTH)

        pn = p_nsa.reshape(B, T, WIDTH + 6 * KV_WIDTH)
        seg = lambda n: pn[..., WIDTH + n * KV_WIDTH:WIDTH + (n + 1) * KV_WIDTH]

        def strides(z):
            z = z.reshape(B, NB, CMP_STRIDE, KV_GROUPS, HEAD_DIM).transpose(0, 3, 1, 2, 4)
            return z.reshape(B, KV_GROUPS, NB, CMP_STRIDE * HEAD_DIM)

        def pe_rows(pe):
            return jnp.broadcast_to(pe.reshape(1, CMP_LEN * HEAD_DIM), (8, CMP_LEN * HEAD_DIM)).astype(BF16)

        k_cmp = nsa_compress(strides(seg(0)), pe_rows(nsa_pe_k[l]), nsa_ck_w1[l].astype(BF16),
                             nsa_ck_w2[l].astype(BF16))
        v_cmp = nsa_compress(strides(seg(1)), pe_rows(nsa_pe_v[l]), nsa_cv_w1[l].astype(BF16),
                             nsa_cv_w2[l].astype(BF16))

        def panels(z):
            z = z.reshape(B, KV_GROUPS, NS, 4, HEAD_DIM).transpose(0, 1, 3, 2, 4)
            return z.reshape(B, KV_GROUPS, NB, HEAD_DIM).astype(BF16)

        near = lambda z: jnp.pad(z, ((0, 0), (0, 0), (CMP_PAD, 8), (0, 0)))
        o_c, sel_t = nsa_compressed(pn, panels(k_cmp), panels(v_cmp), near(k_cmp), near(v_cmp), bias_cmp, bfar_cmp)
        k_aug = jnp.concatenate([_kv_groups(seg(2), B, T, SEL_PAD),
                                 jnp.broadcast_to(key_aug, (B, KV_GROUPS) + key_aug.shape)], axis=-1)
        v_t = _kv_groups(seg(3), B, T, SEL_PAD).reshape(B, KV_GROUPS, -1, V_CHUNK, HEAD_DIM).swapaxes(-1, -2)
        o_s = nsa_selected(pn, sel_t, k_aug, v_t, bias_sel_t, bfar_sel)
        o_w = band_attention(pn, _kv_groups(seg(4), B, T, NSA_WINDOW), _kv_groups(seg(5), B, T, NSA_WINDOW),
                             bias_win, None, NSA_WINDOW)

        ps = p_swa.reshape(B, T, swa_cols)
        sink = jnp.where(np.arange(LANE) == 0, swa_sinks[l].reshape(KV_GROUPS, REP, 1, 1), NEG_INF)
        sink = jnp.broadcast_to(sink, (KV_GROUPS, REP, Q_BLOCK, LANE))
        o_swa = band_attention(ps,
                               _kv_groups(ps[..., WIDTH:WIDTH + KV_WIDTH], B, T, SWA_WINDOW),
                               _kv_groups(ps[..., WIDTH + KV_WIDTH:], B, T, SWA_WINDOW),
                               bias_swa, sink, SWA_WINDOW)

        flat = lambda o: o.reshape(M, WIDTH)
        xf = merge_out(xf, p_gate, p_rw, o_rwkv, flat(o_c), flat(o_s), flat(o_w),
                       flat(o_swa), w_branch[l].astype(BF16), w_out[l].astype(BF16),
                       norm_post_mix[l][None, :], expand, 256)

        xf = ffn(xf, norm_pre_ffn[l][None, :], ffn_w_gate[l].astype(BF16), ffn_w_up[l].astype(BF16),
                 ffn_w_down[l].astype(BF16), norm_post_ffn[l][None, :], 512)
    return xf.reshape(B, T, D)
```

```python
import functools
import math

import numpy as np
import jax
import jax.numpy as jnp
from jax import lax
from jax.experimental import pallas as pl
from jax.experimental.pallas import tpu as pltpu

F32 = jnp.float32
BF16 = jnp.bfloat16

D_MODEL = 1024
HEAD_DIM = 64
N_HEADS = 8
KV_GROUPS = 2
REP = N_HEADS // KV_GROUPS
WIDTH = N_HEADS * HEAD_DIM
KV_WIDTH = KV_GROUPS * HEAD_DIM
W_LORA, A_LORA, G_LORA = 64, 64, 128
RWKV_LN_EPS = 64e-5
CMP_STRIDE = 16
CMP_LEN = 32
CMP_HIDDEN = 256
SEL_BLOCK = 64
SEL_TOPK = 16
NSA_WINDOW = 512
Q_BLOCK = 128
SWA_WINDOW = 128
REL_BUCKETS = 32
REL_MAX_DIST = 1024
NORM_EPS = 1e-6
NEG_INF = -1e30
LOG2E = math.log2(math.e)
LANE = 128
RWKV_CHUNK = 64
SEL_NEAR = 1024
SEL_SPAN = SEL_NEAR - Q_BLOCK
SEL_PAD = SEL_NEAR
SEL_TILE = 1024
SEL_TILE_BLOCKS = SEL_TILE // SEL_BLOCK
SEL_PAD_BLOCKS = SEL_PAD // SEL_BLOCK
AUG = 2 * HEAD_DIM
V_CHUNK = 128
BAND_BLOCKS = 4
CMP_BLOCKS = 2
CMP_NEAR = 64
CMP_PAD = CMP_NEAR - Q_BLOCK // CMP_STRIDE
VMEM_LIMIT = 56 * 1024 * 1024


def _cparams(sem):
    return pltpu.CompilerParams(dimension_semantics=sem, vmem_limit_bytes=VMEM_LIMIT)


def _dot(a, b, precision=None):
    return jnp.dot(a, b, preferred_element_type=F32, precision=precision)


def _dot_nt(a, b, precision=None):
    return lax.dot_general(a, b, (((1,), (1,)), ((), ())), preferred_element_type=F32, precision=precision)


def _dot3(a, b):
    a_hi, b_hi = a.astype(BF16), b.astype(BF16)
    a_lo, b_lo = (a - a_hi.astype(F32)).astype(BF16), (b - b_hi.astype(F32)).astype(BF16)
    return _dot(a_hi, b_hi) + (_dot(a_lo, b_hi) + _dot(a_hi, b_lo))


def _rms(x, g):
    return x * lax.rsqrt(jnp.mean(x * x, axis=-1, keepdims=True) + NORM_EPS) * g


def _iota(shape, dim):
    return lax.broadcasted_iota(jnp.int32, shape, dim)


def _norm_mm_body(x_ref, g_ref, w_ref, o_ref):
    h = _rms(x_ref[...], g_ref[...]).astype(BF16)
    o_ref[...] = _dot(h, w_ref[...]).astype(o_ref.dtype)


def norm_matmul(x, g, w, out_dtype, tm):
    M, D = x.shape
    N = w.shape[1]
    return pl.pallas_call(
        _norm_mm_body,
        grid=(M // tm,),
        in_specs=[pl.BlockSpec((tm, D), lambda i: (i, 0)),
                  pl.BlockSpec((1, D), lambda i: (0, 0)),
                  pl.BlockSpec((D, N), lambda i: (0, 0))],
        out_specs=pl.BlockSpec((tm, N), lambda i: (i, 0)),
        out_shape=jax.ShapeDtypeStruct((M, N), out_dtype),
        compiler_params=_cparams(("parallel",)),
        name="norm_matmul",
    )(x, g, w)


RW_COLS = 3 * WIDTH + 3 * LANE


def _rwkv_prep_body(p_ref, mu_ref, w0_ref, w2_ref, a0_ref, a2_ref, g2_ref,
                    r_o, k_o, v_o, a_o, lw_o, cum_o, g_o, carry_ref):
    tm = p_ref.shape[0]

    @pl.when(pl.program_id(1) == 0)
    def _():
        carry_ref[...] = jnp.zeros_like(carry_ref)

    p = p_ref[...]
    row = _iota(p.shape, 0)
    prev = jnp.where(row == 0, carry_ref[0:1, :], pltpu.roll(p, 1, axis=0))
    carry_ref[0:1, :] = p[tm - 1:tm, :]
    ps = p + (prev - p) * mu_ref[...]
    r_o[...] = ps[:, 0:WIDTH]
    k_o[...] = ps[:, WIDTH:2 * WIDTH]
    v_o[...] = ps[:, 2 * WIDTH:3 * WIDTH]
    zw = ps[:, 3 * WIDTH:3 * WIDTH + LANE]
    za = ps[:, 3 * WIDTH + LANE:3 * WIDTH + 2 * LANE]
    zg = ps[:, 3 * WIDTH + 2 * LANE:3 * WIDTH + 3 * LANE]
    z = -(w0_ref[...] + _dot3(jnp.tanh(zw), w2_ref[...]))
    softplus = jnp.maximum(z, 0.0) + jnp.log(1.0 + jnp.exp(-jnp.abs(z)))
    lw = -jnp.exp(-softplus - 0.5)
    lw_o[...] = lw
    C = RWKV_CHUNK
    tri = (_iota((C, C), 0) >= _iota((C, C), 1)).astype(F32)
    for c in range(tm // C):
        cum_o[c * C:(c + 1) * C, :] = _dot3(tri, lw[c * C:(c + 1) * C, :])
    a_o[...] = jax.nn.sigmoid(a0_ref[...] + _dot(za.astype(BF16), a2_ref[...].astype(BF16)))
    g_o[...] = _dot(jax.nn.sigmoid(zg).astype(BF16), g2_ref[...].astype(BF16))


def rwkv_prep(p, mu, w0, w2, a0, a2, g2, tm):
    B, T, _ = p.shape
    row = lambda b, i: (b, i, 0)
    fixed = lambda b, i: (0, 0)
    out = jax.ShapeDtypeStruct((B, T, WIDTH), F32)
    return pl.pallas_call(
        _rwkv_prep_body,
        grid=(B, T // tm),
        in_specs=[pl.BlockSpec((None, tm, RW_COLS), row),
                  pl.BlockSpec((1, RW_COLS), fixed),
                  pl.BlockSpec((1, WIDTH), fixed), pl.BlockSpec((LANE, WIDTH), fixed),
                  pl.BlockSpec((1, WIDTH), fixed), pl.BlockSpec((LANE, WIDTH), fixed),
                  pl.BlockSpec((LANE, WIDTH), fixed)],
        out_specs=[pl.BlockSpec((None, tm, WIDTH), row)] * 7,
        out_shape=[out] * 7,
        scratch_shapes=[pltpu.VMEM((8, RW_COLS), F32)],
        compiler_params=_cparams(("parallel", "arbitrary")),
        name="rwkv_prep",
    )(p, mu, w0, w2, a0, a2, g2)


RWKV_HEADS_PER_STEP = 4


def _bbmm(a, b):
    return lax.dot_general(a.astype(BF16), b.astype(BF16), (((2,), (1,)), ((0,), (0,))),
                           preferred_element_type=F32)


def _bbmm_nt(a, b):
    return lax.dot_general(a.astype(BF16), b.astype(BF16), (((2,), (2,)), ((0,), (0,))),
                           preferred_element_type=F32)


def _bbmm_tn(a, b):
    return lax.dot_general(a.astype(BF16), b.astype(BF16), (((1,), (1,)), ((0,), (0,))),
                           preferred_element_type=F32)


def _rwkv_rec_body(r_ref, k_ref, v_ref, a_ref, lw_ref, cum_ref, g_ref, kk_ref, ka_ref, rk_ref, lnw_ref, lnb_ref,
                   o_ref, s_ref):
    C = RWKV_CHUNK
    N = HEAD_DIM
    assert C == N
    tc = r_ref.shape[0]
    nc = tc // C
    hb = r_ref.shape[1] // N

    @pl.when(pl.program_id(2) == 0)
    def _():
        s_ref[...] = jnp.zeros_like(s_ref)

    def chunks(ref):
        x = ref[...]
        return jnp.concatenate([x[:, h * N:(h + 1) * N].reshape(nc, C, N) for h in range(hb)], axis=0)

    def per_head(ref):
        x = ref[...]
        return jnp.concatenate([jnp.broadcast_to(x[:, h * N:(h + 1) * N][None], (nc, 1, N))
                                for h in range(hb)], axis=0)

    row = _iota((C, C), 0)
    col = _iota((C, C), 1)
    incl = (row >= col)[None]
    strict = (row > col)[None]
    eye = (row == col)[None]
    nb = hb * nc
    r, k, v, a, lw = chunks(r_ref), chunks(k_ref), chunks(v_ref), chunks(a_ref), chunks(lw_ref)
    ones = jnp.ones((nb, N, N), BF16)

    def lane_sum(x):
        hi = x.astype(BF16)
        return _bbmm(hi, ones) + _bbmm(x - hi.astype(F32), ones)

    kk = k * per_head(kk_ref)
    kk = kk / jnp.maximum(jnp.sqrt(lane_sum(kk * kk)), 1e-12)
    k2 = k * (1.0 + (a - 1.0) * per_head(ka_ref))
    cum = chunks(cum_ref)
    p_incl = jnp.exp(cum)
    p_inv = jnp.exp(-cum)
    a_t = -kk * jnp.exp(cum - lw)
    r_t = r * p_incl
    b_t = kk * a * p_inv
    k_t = k2 * p_inv
    gram = _bbmm_nt(jnp.concatenate([a_t, r_t], axis=1), jnp.concatenate([b_t, k_t], axis=1))
    l_ab = jnp.where(strict, gram[:, 0:C, 0:C], 0.0)
    l_ak = jnp.where(strict, gram[:, 0:C, C:2 * C], 0.0)
    m_rb = jnp.where(incl, gram[:, C:2 * C, 0:C], 0.0)
    m_rk = jnp.where(incl, gram[:, C:2 * C, C:2 * C], 0.0)
    inv = jnp.where(eye, 1.0, l_ab)
    lp = l_ab
    for _ in range(5):
        lp = _bbmm(lp, lp)
        inv = inv + _bbmm(lp, inv)
    t_a = _bbmm(inv, a_t)
    t_v = _bbmm(inv, _bbmm(l_ak, v))
    p_end = p_incl[:, C - 1:C, :]
    bp = b_t * p_end
    kp = k_t * p_end
    ry = (r_t + _bbmm(m_rb, t_a)).reshape(hb, nc, C, N)
    yc = (_bbmm(m_rb, t_v) + _bbmm(m_rk, v)).reshape(hb, nc, C, N)
    am = (jnp.where(eye, jnp.broadcast_to(p_end, (nb, C, C)), 0.0) + _bbmm_tn(bp, t_a)).reshape(hb, nc, C, C)
    gm = (_bbmm_tn(bp, t_v) + _bbmm_tn(kp, v)).reshape(hb, nc, C, N)
    h = s_ref[...]
    ry_am = jnp.concatenate([ry, am], axis=2)
    ys = []
    for c in range(nc):
        both = _bbmm(ry_am[:, c], h)
        ys.append(both[:, 0:C] + yc[:, c])
        h = both[:, C:2 * C] + gm[:, c]
    s_ref[...] = h
    y = jnp.stack(ys, axis=1).reshape(nb, C, N)
    mean = lane_sum(y) * (1.0 / N)
    var = lane_sum(jnp.square(y - mean)) * (1.0 / N)
    yn = (y - mean) * lax.rsqrt(var + RWKV_LN_EPS) * per_head(lnw_ref) + per_head(lnb_ref)
    bonus = lane_sum(r * k2 * per_head(rk_ref)) * v
    out = (yn + bonus) * chunks(g_ref)
    o_ref[...] = jnp.concatenate([out[h * nc:(h + 1) * nc].reshape(tc, N) for h in range(hb)], axis=1)


def rwkv_recurrence(r, k, v, a, lw, cum, g, k_k, k_a, r_k, ln_w, ln_b, tc):
    B, T, W = r.shape
    slab = RWKV_HEADS_PER_STEP * HEAD_DIM
    seq = pl.BlockSpec((None, tc, slab), lambda b, h, i: (b, i, h))
    par = pl.BlockSpec((1, slab), lambda b, h, i: (0, h))
    return pl.pallas_call(
        _rwkv_rec_body,
        grid=(B, W // slab, T // tc),
        in_specs=[seq] * 7 + [par] * 5,
        out_specs=seq,
        out_shape=jax.ShapeDtypeStruct((B, T, W), F32),
        scratch_shapes=[pltpu.VMEM((RWKV_HEADS_PER_STEP, HEAD_DIM, HEAD_DIM), F32)],
        compiler_params=_cparams(("parallel", "parallel", "arbitrary")),
        name="rwkv_recurrence",
    )(r, k, v, a, lw, cum, g, k_k, k_a, r_k, ln_w, ln_b)


def _compress_body(z_ref, pe_ref, w1_ref, w2_ref, o_ref):
    nb = z_ref.shape[0]
    wide = z_ref.shape[1]
    z = z_ref[...]
    first = _dot(z, w1_ref[0])
    second = _dot(z, w1_ref[1])
    pe_term = (_dot(pe_ref[:, 0:wide], w1_ref[0]) + _dot(pe_ref[:, wide:2 * wide], w1_ref[1]))[0:1, :]
    hidden = first + pltpu.roll(second, nb - 1, axis=0) + pe_term
    out = _dot(jax.nn.gelu(hidden).astype(BF16), w2_ref[...])
    rows = _iota(out.shape, 0)
    o_ref[...] = jnp.where(rows < nb - 1, out, 0.0)


def nsa_compress(z, pe, w1, w2):
    B, NB, WIDE = z.shape
    G = w1.shape[0]
    return pl.pallas_call(
        _compress_body,
        grid=(B, G),
        in_specs=[pl.BlockSpec((None, NB, WIDE), lambda b, g: (b, 0, 0)),
                  pl.BlockSpec((8, 2 * WIDE), lambda b, g: (0, 0)),
                  pl.BlockSpec((None, 2, WIDE, CMP_HIDDEN), lambda b, g: (g, 0, 0, 0)),
                  pl.BlockSpec((CMP_HIDDEN, HEAD_DIM), lambda b, g: (0, 0))],
        out_specs=pl.BlockSpec((None, None, NB, HEAD_DIM), lambda b, g: (b, g, 0, 0)),
        out_shape=jax.ShapeDtypeStruct((B, G, NB, HEAD_DIM), F32),
        compiler_params=_cparams(("parallel", "parallel")),
        name="nsa_compress",
    )(z, pe, w1, w2)


def _t5_bucket_np(dist):
    n = np.maximum(dist, 0)
    max_exact = REL_BUCKETS // 2
    nf = np.maximum(n, 1).astype(np.float64)
    large = max_exact + (np.log(nf / max_exact) / math.log(REL_MAX_DIST / max_exact)
                         * (REL_BUCKETS - max_exact)).astype(np.int32)
    large = np.minimum(large, REL_BUCKETS - 1)
    return np.where(n < max_exact, n, large).astype(np.int32)


def _bias_body(tbl_ref, bk_ref, o_ref):
    h = pl.program_id(0)
    bk = bk_ref[...]
    acc = jnp.full(bk.shape, NEG_INF, F32)
    for b in range(REL_BUCKETS):
        acc = jnp.where(bk == b, tbl_ref[b, h], acc)
    o_ref[...] = acc


def bias_tile(table, dist_np, valid_np):
    P, Q = dist_np.shape
    H = table.shape[1]
    buckets = jnp.asarray(np.where(valid_np, _t5_bucket_np(dist_np), -1).astype(np.int32))
    return pl.pallas_call(
        _bias_body,
        grid=(H,),
        in_specs=[pl.BlockSpec(memory_space=pltpu.SMEM),
                  pl.BlockSpec((P, Q), lambda h: (0, 0))],
        out_specs=pl.BlockSpec((None, P, Q), lambda h: (h, 0, 0)),
        out_shape=jax.ShapeDtypeStruct((H, P, Q), F32),
        compiler_params=_cparams(("arbitrary",)),
        name="bias_tile",
    )(table, buckets)


def _heads_on_rows(qb):
    return jnp.concatenate([qb[:, r * HEAD_DIM:(r + 1) * HEAD_DIM] for r in range(REP)], axis=0)


def _heads_on_lanes(o):
    return jnp.concatenate([o[r * Q_BLOCK:(r + 1) * Q_BLOCK] for r in range(REP)], axis=1)


def _per_head_column(ref):
    return jnp.concatenate([jnp.broadcast_to(ref[r][:, 0:1], (Q_BLOCK, 1)) for r in range(REP)], axis=0)


def _nsa_cmp_body(*refs):
    for u in range(CMP_BLOCKS):
        _nsa_cmp_block(u, *refs)


def _nsa_cmp_block(u, q_ref, kp_ref, vp_ref, kn_ref, vn_ref, bias_ref, bfar_ref, o_ref, sel_ref):
    QB = Q_BLOCK
    NQ = REP * QB
    NB = kp_ref.shape[0]
    NS = NB // 4
    ns_shift = NS.bit_length() - 1
    blk = pl.program_id(2) * CMP_BLOCKS + u
    rows = slice(u * QB, (u + 1) * QB)
    start = pl.multiple_of(blk * (QB // CMP_STRIDE), 8)
    kn = kn_ref[pl.ds(start, CMP_NEAR), :].astype(BF16)
    vn = vn_ref[pl.ds(start, CMP_NEAR), :].astype(BF16)
    first_near = blk * (QB // CMP_STRIDE) - CMP_PAD
    lane = _iota((1, NB), 1)
    c_far = 4 * (lane & (NS - 1)) + (lane >> ns_shift)
    far_row = jnp.where(c_far < first_near, 0.0, NEG_INF)
    near_row = jnp.where(first_near + _iota((1, CMP_NEAR), 1) >= 0, 0.0, NEG_INF)
    q = _heads_on_rows(q_ref[rows, :]) * (HEAD_DIM ** -0.5)
    s_f = _dot_nt(q, kp_ref[...]) + _per_head_column(bfar_ref) + far_row
    s_n = _dot_nt(q, kn) + bias_ref[...].reshape(NQ, CMP_NEAR) + near_row
    m = jnp.maximum(jnp.max(s_f, axis=-1, keepdims=True), jnp.max(s_n, axis=-1, keepdims=True))
    m = jnp.maximum(m, 0.1 * NEG_INF)
    p_f = jnp.exp(s_f - m)
    p_n = jnp.exp(s_n - m)
    den = jnp.sum(p_f, axis=-1, keepdims=True) + jnp.sum(p_n, axis=-1, keepdims=True)
    inv = 1.0 / jnp.where(den > 0, den, 1.0)
    p_f = p_f * inv
    p_n = p_n * inv
    o_ref[rows, :] = _heads_on_lanes(_dot(p_f.astype(BF16), vp_ref[...]) + _dot(p_n.astype(BF16), vn))
    imp_far = p_f[0:QB]
    imp_near = p_n[0:QB]
    for r in range(1, REP):
        imp_far = imp_far + p_f[r * QB:(r + 1) * QB]
        imp_near = imp_near + p_n[r * QB:(r + 1) * QB]
    panel = [imp_far[:, m * NS:(m + 1) * NS] for m in range(4)]
    j = _iota((QB, NS), 1)
    prev3 = jnp.where(j == 0, 0.0, pltpu.roll(panel[3], 1, axis=1))
    imp = prev3 + 2.0 * panel[0] + 2.0 * panel[1] + 2.0 * panel[2] + panel[3]
    c_abs = first_near + _iota((CMP_NEAR, NS), 0)
    off = c_abs + 1 - 4 * _iota((CMP_NEAR, NS), 1)
    w_near = jnp.where((off == 0) | (off == 4), 1.0, jnp.where((off >= 1) & (off <= 3), 2.0, 0.0))
    imp = imp + _dot3(imp_near, w_near)
    cur = 2 * blk + (_iota((QB, NS), 0) >= SEL_BLOCK).astype(jnp.int32)
    forced = (j == 0) | (j == cur) | (j == cur - 1)
    score = jnp.where(forced, -3.0, jnp.where(j <= cur, imp, -1.0))
    score = score.T
    jt = _iota((NS, QB), 0)
    cur_t = 2 * blk + (_iota((NS, QB), 1) >= SEL_BLOCK).astype(jnp.int32)
    sel = jnp.where((jt == 0) | (jt == cur_t) | (jt == cur_t - 1), 1.0, 0.0)
    for _ in range(min(SEL_TOPK, NS) - 3):
        m = jnp.max(score, axis=0, keepdims=True)
        idx = jnp.min(jnp.where(score == m, jt, NS), axis=0, keepdims=True)
        hit = jt == idx
        sel = jnp.where(hit, 1.0, sel)
        score = jnp.where(hit, -3.0, score)
    pad = jnp.zeros((SEL_PAD_BLOCKS, QB), BF16)
    sel_ref[u] = jnp.concatenate([pad, sel.astype(BF16), pad], axis=0)


GROUP_LANES = REP * HEAD_DIM


def _q_spec(blocks=1):
    return pl.BlockSpec((None, blocks * Q_BLOCK, GROUP_LANES), lambda b, g, i: (b, i, g))


def nsa_compressed(p_nsa, kperm, vperm, knear, vnear, bias, bias_far):
    B, T, _ = p_nsa.shape
    G, NQ, N = KV_GROUPS, T // Q_BLOCK, HEAD_DIM
    NB = kperm.shape[2]
    nsp = NB // 4 + 2 * SEL_PAD_BLOCKS
    blk4 = lambda b, g, i: (b, g, 0, 0)
    return pl.pallas_call(
        _nsa_cmp_body,
        grid=(B, G, NQ // CMP_BLOCKS),
        in_specs=[_q_spec(CMP_BLOCKS),
                  pl.BlockSpec((None, None, NB, N), blk4), pl.BlockSpec((None, None, NB, N), blk4),
                  pl.BlockSpec((None, None, knear.shape[2], N), blk4),
                  pl.BlockSpec((None, None, knear.shape[2], N), blk4),
                  pl.BlockSpec((None, REP, Q_BLOCK, CMP_NEAR), lambda b, g, i: (g, 0, 0, 0)),
                  pl.BlockSpec((None, REP, 1, LANE), lambda b, g, i: (g, 0, 0, 0))],
        out_specs=[_q_spec(CMP_BLOCKS),
                   pl.BlockSpec((None, None, CMP_BLOCKS, nsp, Q_BLOCK), lambda b, g, i: (b, g, i, 0, 0))],
        out_shape=[jax.ShapeDtypeStruct((B, T, WIDTH), F32),
                   jax.ShapeDtypeStruct((B, G, NQ, nsp, Q_BLOCK), BF16)],
        compiler_params=_cparams(("parallel", "parallel", "parallel")),
        name="nsa_compressed",
    )(p_nsa, kperm, vperm, knear, vnear, bias, bias_far)


def _nsa_sel_body(q_ref, selt_ref, k_ref, vt_ref, biast_ref, bfar_ref, o_ref, sa_ref, sb_ref, ma_ref, mb_ref):
    QB = Q_BLOCK
    NQ = REP * QB
    blk = pl.program_id(2)
    qb = q_ref[...].astype(F32) * (HEAD_DIM ** -0.5 * LOG2E)
    qt = jnp.concatenate([qb[:, r * HEAD_DIM:(r + 1) * HEAD_DIM].T for r in range(REP)], axis=1).astype(BF16)
    near_chunk = blk + (SEL_PAD - SEL_SPAN) // V_CHUNK
    first = near_chunk * (V_CHUNK // SEL_BLOCK)
    pb_u = _iota((SEL_TILE_BLOCKS, QB), 0)
    zeros = jnp.zeros((AUG - HEAD_DIM - 2 * SEL_TILE_BLOCKS, NQ), BF16)
    bfar = bfar_ref[...]

    def far_scores(kt, s_ref, max_ref):
        b0 = pl.multiple_of(kt * SEL_TILE_BLOCKS, SEL_TILE_BLOCKS)
        picked = selt_ref[pl.ds(b0, SEL_TILE_BLOCKS), :].astype(F32) > 0.5
        neg = jnp.where(picked & (b0 + pb_u < first), 0.0, NEG_INF).astype(BF16)
        q_far = jnp.concatenate([qt, jnp.concatenate([neg] * REP, axis=1), bfar, zeros], axis=0)
        r0 = pl.multiple_of(kt * SEL_TILE, SEL_TILE)
        s = _dot(k_ref[pl.ds(r0, SEL_TILE), :], q_far)
        s_ref[...] = s
        max_ref[...] = jnp.max(s, axis=0, keepdims=True)

    def weighted_values(chunk0, p):
        out = jnp.zeros((HEAD_DIM, NQ), F32)
        for j in range(0, SEL_TILE // V_CHUNK, 2):
            vt = jnp.concatenate([vt_ref[chunk0 + j], vt_ref[chunk0 + j + 1]], axis=1)
            out = out + _dot(vt, p[j * V_CHUNK:(j + 2) * V_CHUNK])
        return out

    def far_update(kt, s_ref, max_ref, carry):
        m, l, acc = carry
        s = s_ref[...]
        m_new = jnp.maximum(m, max_ref[...])
        alpha = jnp.exp2(m - m_new)
        p = jnp.exp2(s - m_new)
        l = alpha * l + jnp.sum(p, axis=0, keepdims=True)
        p = p.astype(BF16)
        return m_new, l, alpha * acc + weighted_values(kt * (SEL_TILE // V_CHUNK), p)

    row0 = pl.multiple_of(near_chunk * V_CHUNK, V_CHUNK)
    base = pl.multiple_of((first // SEL_TILE_BLOCKS) * SEL_TILE_BLOCKS, SEL_TILE_BLOCKS)
    rows = selt_ref[pl.ds(base, 2 * SEL_TILE_BLOCKS), :].astype(F32)
    picked = jnp.where(pb_u >= first - base, rows[0:SEL_TILE_BLOCKS], rows[SEL_TILE_BLOCKS:]) > 0.5
    neg = jnp.where(picked, 0.0, NEG_INF).astype(BF16)
    q_near = jnp.concatenate([qt, jnp.concatenate([neg] * REP, axis=1), jnp.zeros_like(bfar), zeros], axis=0)
    s = _dot(k_ref[pl.ds(row0, SEL_NEAR), :], q_near) + biast_ref[...]
    far_scores(0, sa_ref, ma_ref)
    m = jnp.max(s, axis=0, keepdims=True)
    p = jnp.exp2(s - m)
    l = jnp.sum(p, axis=0, keepdims=True)
    acc = weighted_values(near_chunk, p.astype(BF16))

    def far_pair(j, carry):
        far_scores(2 * j + 1, sb_ref, mb_ref)
        carry = far_update(2 * j, sa_ref, ma_ref, carry)
        far_scores(2 * j + 2, sa_ref, ma_ref)
        return far_update(2 * j + 1, sb_ref, mb_ref, carry)

    n_far = (first + SEL_TILE_BLOCKS - 1) // SEL_TILE_BLOCKS
    carry = lax.fori_loop(0, n_far // 2, far_pair, (m, l, acc))
    m, l, acc = lax.cond(n_far % 2 == 1, lambda c: far_update(n_far - 1, sa_ref, ma_ref, c), lambda c: c, carry)
    out = acc / l
    o_ref[...] = jnp.concatenate([out[:, r * QB:(r + 1) * QB].T for r in range(REP)], axis=1)


def nsa_selected(p_nsa, selt, k_aug, vt, bias_t, bias_far):
    B, T, _ = p_nsa.shape
    G, NQ, N, W = KV_GROUPS, T // Q_BLOCK, HEAD_DIM, REP * Q_BLOCK
    nsp = selt.shape[3]
    rows = k_aug.shape[2]
    return pl.pallas_call(
        _nsa_sel_body,
        grid=(B, G, NQ),
        in_specs=[_q_spec(),
                  pl.BlockSpec((None, None, None, nsp, Q_BLOCK), lambda b, g, i: (b, g, i, 0, 0)),
                  pl.BlockSpec((None, None, rows, AUG), lambda b, g, i: (b, g, 0, 0)),
                  pl.BlockSpec((None, None, rows // V_CHUNK, N, V_CHUNK), lambda b, g, i: (b, g, 0, 0, 0)),
                  pl.BlockSpec((None, SEL_NEAR, W), lambda b, g, i: (g, 0, 0)),
                  pl.BlockSpec((None, SEL_TILE_BLOCKS, W), lambda b, g, i: (g, 0, 0))],
        out_specs=_q_spec(),
        out_shape=jax.ShapeDtypeStruct((B, T, WIDTH), F32),
        scratch_shapes=[pltpu.VMEM((SEL_TILE, W), F32)] * 2 + [pltpu.VMEM((1, W), F32)] * 2,
        compiler_params=_cparams(("parallel", "parallel", "parallel")),
        name="nsa_selected",
    )(p_nsa, selt, k_aug, vt, bias_t, bias_far)


def _band_body(*refs, pad, has_sink):
    if has_sink:
        q_ref, k_ref, v_ref, bias_ref, sink_ref, o_ref = refs
    else:
        q_ref, k_ref, v_ref, bias_ref, o_ref = refs
    QB = Q_BLOCK
    NQ = REP * QB
    width = pad + QB
    bias = bias_ref[...].reshape(NQ, width)
    if has_sink:
        sink = sink_ref[...].reshape(NQ, LANE)
    for u in range(BAND_BLOCKS):
        blk = pl.program_id(2) * BAND_BLOCKS + u
        row0 = pl.multiple_of(blk * QB, QB)
        kb = k_ref[pl.ds(row0, width), :]
        vb = v_ref[pl.ds(row0, width), :]
        before_start = jnp.where(row0 + _iota((1, width), 1) >= pad, 0.0, NEG_INF)
        q = _heads_on_rows(q_ref[u * QB:(u + 1) * QB, :]) * (HEAD_DIM ** -0.5)
        s = _dot_nt(q, kb) + bias + before_start
        m = jnp.max(s, axis=-1, keepdims=True)
        if has_sink:
            m = jnp.maximum(m, jnp.max(sink, axis=-1, keepdims=True))
        p = jnp.exp(s - m)
        den = jnp.sum(p, axis=-1, keepdims=True)
        if has_sink:
            den = den + jnp.sum(jnp.exp(sink - m), axis=-1, keepdims=True)
        p = p * (1.0 / den)
        o_ref[u * QB:(u + 1) * QB, :] = _heads_on_lanes(_dot(p.astype(BF16), vb))


def band_attention(p, k, v, bias, sink, pad):
    B, T, _ = p.shape
    G, NQ, N = KV_GROUPS, T // Q_BLOCK, HEAD_DIM
    rows = k.shape[2]
    width = pad + Q_BLOCK
    kv = pl.BlockSpec((None, None, rows, N), lambda b, g, i: (b, g, 0, 0))
    in_specs = [_q_spec(BAND_BLOCKS), kv, kv,
                pl.BlockSpec((None, REP, Q_BLOCK, width), lambda b, g, i: (g, 0, 0, 0))]
    args = [p, k, v, bias]
    if sink is not None:
        in_specs.append(pl.BlockSpec((None, REP, Q_BLOCK, LANE), lambda b, g, i: (g, 0, 0, 0)))
        args.append(sink)
    return pl.pallas_call(
        functools.partial(_band_body, pad=pad, has_sink=sink is not None),
        grid=(B, G, NQ // BAND_BLOCKS),
        in_specs=in_specs,
        out_specs=_q_spec(BAND_BLOCKS),
        out_shape=jax.ShapeDtypeStruct((B, T, WIDTH), F32),
        compiler_params=_cparams(("parallel", "parallel", "parallel")),
        name="band_attention",
    )(*args)


def _merge_body(x_ref, pg_ref, hg_ref, orw_ref, oc_ref, os_ref, ow_ref, osw_ref, wb_ref, wo_ref, gn_ref, ex_ref,
                o_ref):
    pg = pg_ref[...]
    head_gates = jax.nn.sigmoid(hg_ref[...])
    gates_hi = head_gates.astype(BF16)
    gates_lo = (head_gates - gates_hi.astype(F32)).astype(BF16)
    ge = _dot(gates_hi, ex_ref[...]) + _dot(gates_lo, ex_ref[...])
    o_nsa = (ge[:, 0:WIDTH] * oc_ref[...] + ge[:, WIDTH:2 * WIDTH] * os_ref[...]
             + ge[:, 2 * WIDTH:3 * WIDTH] * ow_ref[...])
    merged = (jax.nn.sigmoid(pg[:, 0:D_MODEL]) * _dot(orw_ref[...].astype(BF16), wb_ref[0])
              + jax.nn.sigmoid(pg[:, D_MODEL:2 * D_MODEL]) * _dot(o_nsa.astype(BF16), wb_ref[1])
              + jax.nn.sigmoid(pg[:, 2 * D_MODEL:3 * D_MODEL]) * _dot(osw_ref[...].astype(BF16), wb_ref[2]))
    y = _dot(merged.astype(BF16), wo_ref[...])
    o_ref[...] = x_ref[...] + _rms(y, gn_ref[...])


def merge_out(x, pg, p_rw, o_rwkv, o_c, o_s, o_w, o_swa, w_branch, w_out, g_post, expand, tm):
    M, D = x.shape
    row = lambda i: (i, 0)
    wide = pl.BlockSpec((tm, WIDTH), row)
    return pl.pallas_call(
        _merge_body,
        grid=(M // tm,),
        in_specs=[pl.BlockSpec((tm, D), row), pl.BlockSpec((tm, 3 * D), row),
                  pl.BlockSpec((tm, LANE), lambda i: (i, RW_COLS // LANE)),
                  wide, wide, wide, wide, wide,
                  pl.BlockSpec((3, WIDTH, D), lambda i: (0, 0, 0)),
                  pl.BlockSpec((D, D), lambda i: (0, 0)),
                  pl.BlockSpec((1, D), lambda i: (0, 0)),
                  pl.BlockSpec((LANE, 3 * WIDTH), lambda i: (0, 0))],
        out_specs=pl.BlockSpec((tm, D), row),
        out_shape=jax.ShapeDtypeStruct((M, D), F32),
        compiler_params=_cparams(("parallel",)),
        name="merge_out",
    )(x, pg, p_rw, o_rwkv, o_c, o_s, o_w, o_swa, w_branch, w_out, g_post, expand)


FFN_CHUNKS = 2


def _ffn_body(x_ref, gpre_ref, wg_ref, wu_ref, wd_ref, gpost_ref, o_ref):
    x = x_ref[...]
    h = _rms(x, gpre_ref[...]).astype(BF16)
    th = wg_ref.shape[1] // FFN_CHUNKS
    acc = jnp.zeros(x.shape, F32)
    for c in range(FFN_CHUNKS):
        cols = slice(c * th, (c + 1) * th)
        gate = _dot(h, wg_ref[:, cols])
        act = gate * jax.nn.sigmoid(gate) * _dot(h, wu_ref[:, cols])
        acc = acc + _dot(act.astype(BF16), wd_ref[cols, :])
    o_ref[...] = x + _rms(acc, gpost_ref[...])


def ffn(x, g_pre, w_gate, w_up, w_down, g_post, tm):
    M, D = x.shape
    Hd = w_gate.shape[1]
    whole = lambda shape: pl.BlockSpec(shape, lambda i: (0, 0), pipeline_mode=pl.Buffered(1))
    return pl.pallas_call(
        _ffn_body,
        grid=(M // tm,),
        in_specs=[pl.BlockSpec((tm, D), lambda i: (i, 0)),
                  pl.BlockSpec((1, D), lambda i: (0, 0)),
                  whole((D, Hd)), whole((D, Hd)), whole((Hd, D)),
                  pl.BlockSpec((1, D), lambda i: (0, 0))],
        out_specs=pl.BlockSpec((tm, D), lambda i: (i, 0)),
        out_shape=jax.ShapeDtypeStruct((M, D), F32),
        compiler_params=_cparams(("parallel",)),
        name="ffn",
    )(x, g_pre, w_gate, w_up, w_down, g_post)


def _pad_cols(w, n):
    return jnp.pad(w, ((0, 0), (0, n - w.shape[1])))


def _pad_rows(w, n):
    return jnp.pad(w, ((0, n - w.shape[0]), (0, 0)))


def _kv_groups(z, B, T, pad):
    z = z.reshape(B, T, KV_GROUPS, HEAD_DIM).transpose(0, 2, 1, 3)
    return jnp.pad(z, ((0, 0), (0, 0), (pad, 0), (0, 0)))


def _group_bias(b):
    return b.reshape(KV_GROUPS, REP, b.shape[1], b.shape[2])


def kernel(x, norm_pre_mix, norm_post_mix, norm_pre_ffn, norm_post_ffn, w_in, rwkv_mu, rwkv_w0, rwkv_w2, rwkv_a0, rwkv_a2, rwkv_g2, rwkv_k_k, rwkv_k_a, rwkv_r_k, rwkv_ln_w, rwkv_ln_b, nsa_pe_k, nsa_pe_v, nsa_ck_w1, nsa_ck_w2, nsa_cv_w1, nsa_cv_w2, swa_sinks, rel_bias, w_branch, w_out, ffn_w_gate, ffn_w_up, ffn_w_down):
    B, T, D = x.shape
    depth = w_in.shape[0]
    M = B * T
    NB = T // CMP_STRIDE
    NS = T // SEL_BLOCK

    rw_cols = 3 * WIDTH + W_LORA + A_LORA + G_LORA
    nsa_cols = WIDTH + 6 * KV_WIDTH + 3 * N_HEADS
    swa_cols = WIDTH + 2 * KV_WIDTH
    c_nsa = rw_cols
    c_swa = c_nsa + nsa_cols
    c_gate = c_swa + swa_cols
    c_zw = 3 * WIDTH

    ii = np.arange(Q_BLOCK)[:, None]
    d_swa = SWA_WINDOW + ii - np.arange(SWA_WINDOW + Q_BLOCK)[None, :]
    d_win = NSA_WINDOW + ii - np.arange(NSA_WINDOW + Q_BLOCK)[None, :]
    d_sel = SEL_SPAN + ii - np.arange(SEL_NEAR)[None, :]
    d_cmp = ii + (CMP_PAD * CMP_STRIDE - CMP_LEN + 1) - CMP_STRIDE * np.arange(CMP_NEAR)[None, :]
    bias_swa = bias_tile(rel_bias, d_swa, (d_swa >= 0) & (d_swa < SWA_WINDOW))
    bias_win = bias_tile(rel_bias, d_win, (d_win >= 0) & (d_win < NSA_WINDOW))
    bias_sel = bias_tile(rel_bias, d_sel, d_sel >= 0)
    bias_cmp = bias_tile(rel_bias, d_cmp, d_cmp >= 0)
    bias_swa = _group_bias(bias_swa[N_HEADS:])
    bias_win = _group_bias(bias_win[:N_HEADS])
    bias_sel = _group_bias(bias_sel[:N_HEADS])
    bias_cmp = _group_bias(bias_cmp[:N_HEADS])
    far = rel_bias[REL_BUCKETS - 1, :N_HEADS].reshape(KV_GROUPS, REP, 1, 1)
    bias_sel_t = bias_sel.transpose(0, 3, 1, 2).reshape(KV_GROUPS, SEL_NEAR, REP * Q_BLOCK) * LOG2E
    far2 = far * LOG2E
    far_hi = far2.astype(BF16)
    far_lo = (far2 - far_hi.astype(F32)).astype(BF16)
    far_rows = jnp.concatenate([far_hi, far_lo], axis=2)
    far_rows = jnp.broadcast_to(far_rows, (KV_GROUPS, REP, 2, Q_BLOCK)).transpose(0, 2, 1, 3)
    bfar_sel = jnp.pad(far_rows.reshape(KV_GROUPS, 2, REP * Q_BLOCK), ((0, 0), (0, SEL_TILE_BLOCKS - 2), (0, 0)))
    rows_pad = SEL_PAD + T
    blk_in_tile = (np.arange(rows_pad) // SEL_BLOCK) % SEL_TILE_BLOCKS
    ka = np.zeros((rows_pad, AUG - HEAD_DIM), np.float32)
    ka[np.arange(rows_pad), blk_in_tile] = 1.0
    ka[:, SEL_TILE_BLOCKS:SEL_TILE_BLOCKS + 2] = 1.0
    key_aug = jnp.asarray(ka, BF16)
    bfar_cmp = jnp.broadcast_to(far, (KV_GROUPS, REP, 1, LANE))
    ex = np.zeros((LANE, 3 * WIDTH), np.float32)
    for c in range(3 * N_HEADS):
        ex[c, c * HEAD_DIM:(c + 1) * HEAD_DIM] = 1.0
    expand = jnp.asarray(ex, BF16)

    xf = x.reshape(M, D)
    for l in range(depth):
        w = w_in[l]
        w_rw = jnp.concatenate([w[:, :c_zw], _pad_cols(w[:, c_zw:c_zw + W_LORA], LANE),
                                _pad_cols(w[:, c_zw + W_LORA:c_zw + W_LORA + A_LORA], LANE),
                                w[:, c_zw + W_LORA + A_LORA:rw_cols],
                                _pad_cols(w[:, c_swa - 3 * N_HEADS:c_swa], LANE)], axis=1).astype(BF16)
        w_nsa = w[:, c_nsa:c_nsa + WIDTH + 6 * KV_WIDTH].astype(BF16)
        w_swa = w[:, c_swa:c_gate].astype(BF16)
        w_gate = w[:, c_gate:].astype(BF16)
        g_pre = norm_pre_mix[l][None, :]
        p_rw = norm_matmul(xf, g_pre, w_rw, F32, 1024)
        p_nsa = norm_matmul(xf, g_pre, w_nsa, BF16, 1024)
        p_swa = norm_matmul(xf, g_pre, w_swa, BF16, 1024)
        p_gate = norm_matmul(xf, g_pre, w_gate, F32, 512)

        mu = rwkv_mu[l]
        mu_p = jnp.concatenate([mu[:c_zw], jnp.pad(mu[c_zw:c_zw + W_LORA], (0, LANE - W_LORA)),
                                jnp.pad(mu[c_zw + W_LORA:c_zw + W_LORA + A_LORA], (0, LANE - A_LORA)),
                                mu[c_zw + W_LORA + A_LORA:]])[None, :]
        r, k, v, a, lw, cum, g = rwkv_prep(p_rw.reshape(B, T, RW_COLS + LANE), mu_p, rwkv_w0[l][None, :],
                                      _pad_rows(rwkv_w2[l], LANE), rwkv_a0[l][None, :],
                                      _pad_rows(rwkv_a2[l], LANE), rwkv_g2[l], 512)
        row = lambda z: z.reshape(1, WIDTH)
        o_rwkv = rwkv_recurrence(r, k, v, a, lw, cum, g, row(rwkv_k_k[l]), row(rwkv_k_a[l]), row(rwkv_r_k[l]),
                                 row(rwkv_ln_w[l]), row(rwkv_ln_b[l]), 512).reshape(M, WIDTH)

        pn = p_nsa.reshape(B, T, WIDTH + 6 * KV_WIDTH)
        seg = lambda n: pn[..., WIDTH + n * KV_WIDTH:WIDTH + (n + 1) * KV_WIDTH]

        def strides(z):
            return z.reshape(B, NB, CMP_STRIDE * KV_WIDTH)

        def pe_rows(pe):
            pe = jnp.broadcast_to(pe.reshape(2, CMP_STRIDE, 1, HEAD_DIM), (2, CMP_STRIDE, KV_GROUPS, HEAD_DIM))
            return jnp.broadcast_to(pe.reshape(1, -1), (8, 2 * CMP_STRIDE * KV_WIDTH)).astype(BF16)

        def group_rows(w1):
            w = w1.reshape(1, 2, CMP_STRIDE, 1, HEAD_DIM, CMP_HIDDEN)
            own = jnp.eye(KV_GROUPS, dtype=w1.dtype).reshape(KV_GROUPS, 1, 1, KV_GROUPS, 1, 1)
            return (w * own).reshape(KV_GROUPS, 2, CMP_STRIDE * KV_WIDTH, CMP_HIDDEN).astype(BF16)

        k_cmp = nsa_compress(strides(seg(0)), pe_rows(nsa_pe_k[l]), group_rows(nsa_ck_w1[l]),
                             nsa_ck_w2[l].astype(BF16))
        v_cmp = nsa_compress(strides(seg(1)), pe_rows(nsa_pe_v[l]), group_rows(nsa_cv_w1[l]),
                             nsa_cv_w2[l].astype(BF16))

        def panels(z):
            z = z.reshape(B, KV_GROUPS, NS, 4, HEAD_DIM).transpose(0, 1, 3, 2, 4)
            return z.reshape(B, KV_GROUPS, NB, HEAD_DIM).astype(BF16)

        near = lambda z: jnp.pad(z, ((0, 0), (0, 0), (CMP_PAD, 8), (0, 0)))
        o_c, sel_t = nsa_compressed(pn, panels(k_cmp), panels(v_cmp), near(k_cmp), near(v_cmp), bias_cmp, bfar_cmp)
        k_aug = jnp.concatenate([_kv_groups(seg(2), B, T, SEL_PAD),
                                 jnp.broadcast_to(key_aug, (B, KV_GROUPS) + key_aug.shape)], axis=-1)
        v_t = _kv_groups(seg(3), B, T, SEL_PAD).reshape(B, KV_GROUPS, -1, V_CHUNK, HEAD_DIM).swapaxes(-1, -2)
        o_s = nsa_selected(pn, sel_t, k_aug, v_t, bias_sel_t, bfar_sel)
        o_w = band_attention(pn, _kv_groups(seg(4), B, T, NSA_WINDOW), _kv_groups(seg(5), B, T, NSA_WINDOW),
                             bias_win, None, NSA_WINDOW)

        ps = p_swa.reshape(B, T, swa_cols)
        sink = jnp.where(np.arange(LANE) == 0, swa_sinks[l].reshape(KV_GROUPS, REP, 1, 1), NEG_INF)
        sink = jnp.broadcast_to(sink, (KV_GROUPS, REP, Q_BLOCK, LANE))
        o_swa = band_attention(ps,
                               _kv_groups(ps[..., WIDTH:WIDTH + KV_WIDTH], B, T, SWA_WINDOW),
                               _kv_groups(ps[..., WIDTH + KV_WIDTH:], B, T, SWA_WINDOW),
                               bias_swa, sink, SWA_WINDOW)

        flat = lambda o: o.reshape(M, WIDTH)
        xf = merge_out(xf, p_gate, p_rw, o_rwkv, flat(o_c), flat(o_s), flat(o_w),
                       flat(o_swa), w_branch[l].astype(BF16), w_out[l].astype(BF16),
                       norm_post_mix[l][None, :], expand, 256)

        xf = ffn(xf, norm_pre_ffn[l][None, :], ffn_w_gate[l].astype(BF16), ffn_w_up[l].astype(BF16),
                 ffn_w_down[l].astype(BF16), norm_post_ffn[l][None, :], 512)
    return xf.reshape(B, T, D)
```

```python
import functools
import math

import numpy as np
import jax
import jax.numpy as jnp
from jax import lax
from jax.experimental import pallas as pl
from jax.experimental.pallas import tpu as pltpu

F32 = jnp.float32
BF16 = jnp.bfloat16

D_MODEL = 1024
HEAD_DIM = 64
N_HEADS = 8
KV_GROUPS = 2
REP = N_HEADS // KV_GROUPS
WIDTH = N_HEADS * HEAD_DIM
KV_WIDTH = KV_GROUPS * HEAD_DIM
W_LORA, A_LORA, G_LORA = 64, 64, 128
RWKV_LN_EPS = 64e-5
CMP_STRIDE = 16
CMP_LEN = 32
CMP_HIDDEN = 256
SEL_BLOCK = 64
SEL_TOPK = 16
NSA_WINDOW = 512
Q_BLOCK = 128
SWA_WINDOW = 128
REL_BUCKETS = 32
REL_MAX_DIST = 1024
NORM_EPS = 1e-6
NEG_INF = -1e30
LOG2E = math.log2(math.e)
LANE = 128
RWKV_CHUNK = 64
SEL_NEAR = 1024
SEL_SPAN = SEL_NEAR - Q_BLOCK
SEL_PAD = SEL_NEAR
SEL_TILE = 1024
SEL_TILE_BLOCKS = SEL_TILE // SEL_BLOCK
SEL_PAD_BLOCKS = SEL_PAD // SEL_BLOCK
AUG = 2 * HEAD_DIM
V_CHUNK = 128
WIN_BLOCKS = 8
SWA_BLOCKS = 4
CMP_BLOCKS = 4
CMP_NEAR = 64
CMP_PAD = CMP_NEAR - Q_BLOCK // CMP_STRIDE
VMEM_LIMIT = 56 * 1024 * 1024


def _cparams(sem):
    return pltpu.CompilerParams(dimension_semantics=sem, vmem_limit_bytes=VMEM_LIMIT)


def _dot(a, b, precision=None):
    return jnp.dot(a, b, preferred_element_type=F32, precision=precision)


def _dot_nt(a, b, precision=None):
    return lax.dot_general(a, b, (((1,), (1,)), ((), ())), preferred_element_type=F32, precision=precision)


def _dot3(a, b):
    a_hi, b_hi = a.astype(BF16), b.astype(BF16)
    a_lo, b_lo = (a - a_hi.astype(F32)).astype(BF16), (b - b_hi.astype(F32)).astype(BF16)
    return _dot(a_hi, b_hi) + (_dot(a_lo, b_hi) + _dot(a_hi, b_lo))


def _rms(x, g):
    return x * lax.rsqrt(jnp.mean(x * x, axis=-1, keepdims=True) + NORM_EPS) * g


def _iota(shape, dim):
    return lax.broadcasted_iota(jnp.int32, shape, dim)


def _norm_mm_body(x_ref, g_ref, w_ref, o_ref):
    h = _rms(x_ref[...], g_ref[...]).astype(BF16)
    o_ref[...] = _dot(h, w_ref[...]).astype(o_ref.dtype)


def norm_matmul(x, g, w, out_dtype, tm):
    M, D = x.shape
    N = w.shape[1]
    return pl.pallas_call(
        _norm_mm_body,
        grid=(M // tm,),
        in_specs=[pl.BlockSpec((tm, D), lambda i: (i, 0)),
                  pl.BlockSpec((1, D), lambda i: (0, 0)),
                  pl.BlockSpec((D, N), lambda i: (0, 0))],
        out_specs=pl.BlockSpec((tm, N), lambda i: (i, 0)),
        out_shape=jax.ShapeDtypeStruct((M, N), out_dtype),
        compiler_params=_cparams(("parallel",)),
        name="norm_matmul",
    )(x, g, w)


RW_COLS = 3 * WIDTH + 3 * LANE


def _rwkv_prep_body(p_ref, mu_ref, w0_ref, w2_ref, a0_ref, a2_ref, g2_ref,
                    r_o, k_o, v_o, a_o, lw_o, cum_o, g_o, carry_ref):
    tm = p_ref.shape[0]

    @pl.when(pl.program_id(1) == 0)
    def _():
        carry_ref[...] = jnp.zeros_like(carry_ref)

    p = p_ref[...]
    row = _iota(p.shape, 0)
    prev = jnp.where(row == 0, carry_ref[0:1, :], pltpu.roll(p, 1, axis=0))
    carry_ref[0:1, :] = p[tm - 1:tm, :]
    ps = p + (prev - p) * mu_ref[...]
    r_o[...] = ps[:, 0:WIDTH]
    k_o[...] = ps[:, WIDTH:2 * WIDTH]
    v_o[...] = ps[:, 2 * WIDTH:3 * WIDTH]
    zw = ps[:, 3 * WIDTH:3 * WIDTH + LANE]
    za = ps[:, 3 * WIDTH + LANE:3 * WIDTH + 2 * LANE]
    zg = ps[:, 3 * WIDTH + 2 * LANE:3 * WIDTH + 3 * LANE]
    z = -(w0_ref[...] + _dot3(jnp.tanh(zw), w2_ref[...]))
    softplus = jnp.maximum(z, 0.0) + jnp.log(1.0 + jnp.exp(-jnp.abs(z)))
    lw = -jnp.exp(-softplus - 0.5)
    lw_o[...] = lw
    C = RWKV_CHUNK
    tri = (_iota((C, C), 0) >= _iota((C, C), 1)).astype(F32)
    for c in range(tm // C):
        cum_o[c * C:(c + 1) * C, :] = _dot3(tri, lw[c * C:(c + 1) * C, :])
    a_o[...] = jax.nn.sigmoid(a0_ref[...] + _dot(za.astype(BF16), a2_ref[...].astype(BF16)))
    g_o[...] = _dot(jax.nn.sigmoid(zg).astype(BF16), g2_ref[...].astype(BF16))


def rwkv_prep(p, mu, w0, w2, a0, a2, g2, tm):
    B, T, _ = p.shape
    row = lambda b, i: (b, i, 0)
    fixed = lambda b, i: (0, 0)
    out = jax.ShapeDtypeStruct((B, T, WIDTH), F32)
    return pl.pallas_call(
        _rwkv_prep_body,
        grid=(B, T // tm),
        in_specs=[pl.BlockSpec((None, tm, RW_COLS), row),
                  pl.BlockSpec((1, RW_COLS), fixed),
                  pl.BlockSpec((1, WIDTH), fixed), pl.BlockSpec((LANE, WIDTH), fixed),
                  pl.BlockSpec((1, WIDTH), fixed), pl.BlockSpec((LANE, WIDTH), fixed),
                  pl.BlockSpec((LANE, WIDTH), fixed)],
        out_specs=[pl.BlockSpec((None, tm, WIDTH), row)] * 7,
        out_shape=[out] * 7,
        scratch_shapes=[pltpu.VMEM((8, RW_COLS), F32)],
        compiler_params=_cparams(("parallel", "arbitrary")),
        name="rwkv_prep",
    )(p, mu, w0, w2, a0, a2, g2)


RWKV_HEADS_PER_STEP = 4


def _bbmm(a, b):
    return lax.dot_general(a.astype(BF16), b.astype(BF16), (((2,), (1,)), ((0,), (0,))),
                           preferred_element_type=F32)


def _bbmm_nt(a, b):
    return lax.dot_general(a.astype(BF16), b.astype(BF16), (((2,), (2,)), ((0,), (0,))),
                           preferred_element_type=F32)


def _bbmm_tn(a, b):
    return lax.dot_general(a.astype(BF16), b.astype(BF16), (((1,), (1,)), ((0,), (0,))),
                           preferred_element_type=F32)


def _rwkv_rec_body(r_ref, k_ref, v_ref, a_ref, lw_ref, cum_ref, g_ref, kk_ref, ka_ref, rk_ref, lnw_ref, lnb_ref,
                   o_ref, s_ref):
    C = RWKV_CHUNK
    N = HEAD_DIM
    assert C == N
    tc = r_ref.shape[0]
    nc = tc // C
    hb = r_ref.shape[1] // N

    @pl.when(pl.program_id(2) == 0)
    def _():
        s_ref[...] = jnp.zeros_like(s_ref)

    def chunks(ref):
        x = ref[...]
        return jnp.concatenate([x[:, h * N:(h + 1) * N].reshape(nc, C, N) for h in range(hb)], axis=0)

    def per_head(ref):
        x = ref[...]
        return jnp.concatenate([jnp.broadcast_to(x[:, h * N:(h + 1) * N][None], (nc, 1, N))
                                for h in range(hb)], axis=0)

    row = _iota((C, C), 0)
    col = _iota((C, C), 1)
    incl = (row >= col)[None]
    strict = (row > col)[None]
    eye = (row == col)[None]
    nb = hb * nc
    r, k, v, a, lw = chunks(r_ref), chunks(k_ref), chunks(v_ref), chunks(a_ref), chunks(lw_ref)
    ones = jnp.ones((nb, N, N), BF16)

    def lane_sum(x):
        hi = x.astype(BF16)
        return _bbmm(hi, ones) + _bbmm(x - hi.astype(F32), ones)

    kk = k * per_head(kk_ref)
    kk = kk / jnp.maximum(jnp.sqrt(lane_sum(kk * kk)), 1e-12)
    k2 = k * (1.0 + (a - 1.0) * per_head(ka_ref))
    cum = chunks(cum_ref)
    p_incl = jnp.exp(cum)
    p_inv = jnp.exp(-cum)
    a_t = -kk * jnp.exp(cum - lw)
    r_t = r * p_incl
    b_t = kk * a * p_inv
    k_t = k2 * p_inv
    ar = jnp.concatenate([a_t, r_t], axis=1)
    gram_b = _bbmm_nt(ar, b_t)
    gram_k = _bbmm_nt(ar, k_t)
    l_ab = jnp.where(strict, gram_b[:, 0:C], 0.0)
    l_ak = jnp.where(strict, gram_k[:, 0:C], 0.0)
    m_rb = jnp.where(incl, gram_b[:, C:2 * C], 0.0)
    m_rk = jnp.where(incl, gram_k[:, C:2 * C], 0.0)
    inv = jnp.where(eye, 1.0, l_ab)
    lp = l_ab
    for _ in range(5):
        lp = _bbmm(lp, lp)
        inv = inv + _bbmm(lp, inv)
    t_a = _bbmm(inv, a_t)
    t_v = _bbmm(inv, _bbmm(l_ak, v))
    p_end = p_incl[:, C - 1:C, :]
    bp = b_t * p_end
    kp = k_t * p_end
    ry = (r_t + _bbmm(m_rb, t_a)).reshape(hb, nc, C, N)
    yc = (_bbmm(m_rb, t_v) + _bbmm(m_rk, v)).reshape(hb, nc, C, N)
    am = (jnp.where(eye, jnp.broadcast_to(p_end, (nb, C, C)), 0.0) + _bbmm_tn(bp, t_a)).reshape(hb, nc, C, C)
    gm = (_bbmm_tn(bp, t_v) + _bbmm_tn(kp, v)).reshape(hb, nc, C, N)
    h = s_ref[...]
    ry_am = jnp.concatenate([ry, am], axis=2)
    ys = []
    for c in range(nc):
        both = _bbmm(ry_am[:, c], h)
        ys.append(both[:, 0:C] + yc[:, c])
        h = both[:, C:2 * C] + gm[:, c]
    s_ref[...] = h
    y = jnp.stack(ys, axis=1).reshape(nb, C, N)
    mean = lane_sum(y) * (1.0 / N)
    var = lane_sum(jnp.square(y - mean)) * (1.0 / N)
    yn = (y - mean) * lax.rsqrt(var + RWKV_LN_EPS) * per_head(lnw_ref) + per_head(lnb_ref)
    bonus = lane_sum(r * k2 * per_head(rk_ref)) * v
    out = (yn + bonus) * chunks(g_ref)
    o_ref[...] = jnp.concatenate([out[h * nc:(h + 1) * nc].reshape(tc, N) for h in range(hb)], axis=1)


def rwkv_recurrence(r, k, v, a, lw, cum, g, k_k, k_a, r_k, ln_w, ln_b, tc):
    B, T, W = r.shape
    slab = RWKV_HEADS_PER_STEP * HEAD_DIM
    seq = pl.BlockSpec((None, tc, slab), lambda b, h, i: (b, i, h))
    par = pl.BlockSpec((1, slab), lambda b, h, i: (0, h))
    return pl.pallas_call(
        _rwkv_rec_body,
        grid=(B, W // slab, T // tc),
        in_specs=[seq] * 7 + [par] * 5,
        out_specs=seq,
        out_shape=jax.ShapeDtypeStruct((B, T, W), F32),
        scratch_shapes=[pltpu.VMEM((RWKV_HEADS_PER_STEP, HEAD_DIM, HEAD_DIM), F32)],
        compiler_params=_cparams(("parallel", "parallel", "arbitrary")),
        name="rwkv_recurrence",
    )(r, k, v, a, lw, cum, g, k_k, k_a, r_k, ln_w, ln_b)


def _compress_body(z_ref, pe_ref, w1_ref, w2_ref, o_ref):
    nb = z_ref.shape[0]
    wide = z_ref.shape[1]
    z = z_ref[...]
    first = _dot(z, w1_ref[0])
    second = _dot(z, w1_ref[1])
    pe_term = (_dot(pe_ref[:, 0:wide], w1_ref[0]) + _dot(pe_ref[:, wide:2 * wide], w1_ref[1]))[0:1, :]
    hidden = first + pltpu.roll(second, nb - 1, axis=0) + pe_term
    out = _dot(jax.nn.gelu(hidden).astype(BF16), w2_ref[...])
    rows = _iota(out.shape, 0)
    o_ref[...] = jnp.where(rows < nb - 1, out, 0.0)


def nsa_compress(z, pe, w1, w2):
    B, NB, WIDE = z.shape
    G = w1.shape[0]
    return pl.pallas_call(
        _compress_body,
        grid=(B, G),
        in_specs=[pl.BlockSpec((None, NB, WIDE), lambda b, g: (b, 0, 0)),
                  pl.BlockSpec((8, 2 * WIDE), lambda b, g: (0, 0)),
                  pl.BlockSpec((None, 2, WIDE, CMP_HIDDEN), lambda b, g: (g, 0, 0, 0)),
                  pl.BlockSpec((CMP_HIDDEN, HEAD_DIM), lambda b, g: (0, 0))],
        out_specs=pl.BlockSpec((None, None, NB, HEAD_DIM), lambda b, g: (b, g, 0, 0)),
        out_shape=jax.ShapeDtypeStruct((B, G, NB, HEAD_DIM), F32),
        compiler_params=_cparams(("parallel", "parallel")),
        name="nsa_compress",
    )(z, pe, w1, w2)


def _t5_bucket_np(dist):
    n = np.maximum(dist, 0)
    max_exact = REL_BUCKETS // 2
    nf = np.maximum(n, 1).astype(np.float64)
    large = max_exact + (np.log(nf / max_exact) / math.log(REL_MAX_DIST / max_exact)
                         * (REL_BUCKETS - max_exact)).astype(np.int32)
    large = np.minimum(large, REL_BUCKETS - 1)
    return np.where(n < max_exact, n, large).astype(np.int32)


def _bias_body(tbl_ref, bk_ref, o_ref):
    h = pl.program_id(0)
    bk = bk_ref[...]
    acc = jnp.full(bk.shape, NEG_INF, F32)
    for b in range(REL_BUCKETS):
        acc = jnp.where(bk == b, tbl_ref[b, h], acc)
    o_ref[...] = acc


def bias_tile(table, dist_np, valid_np):
    P, Q = dist_np.shape
    H = table.shape[1]
    buckets = jnp.asarray(np.where(valid_np, _t5_bucket_np(dist_np), -1).astype(np.int32))
    return pl.pallas_call(
        _bias_body,
        grid=(H,),
        in_specs=[pl.BlockSpec(memory_space=pltpu.SMEM),
                  pl.BlockSpec((P, Q), lambda h: (0, 0))],
        out_specs=pl.BlockSpec((None, P, Q), lambda h: (h, 0, 0)),
        out_shape=jax.ShapeDtypeStruct((H, P, Q), F32),
        compiler_params=_cparams(("arbitrary",)),
        name="bias_tile",
    )(table, buckets)


def _heads_on_rows(qb):
    return jnp.concatenate([qb[:, r * HEAD_DIM:(r + 1) * HEAD_DIM] for r in range(REP)], axis=0)


def _heads_on_lanes(o):
    return jnp.concatenate([o[r * Q_BLOCK:(r + 1) * Q_BLOCK] for r in range(REP)], axis=1)


def _per_head_column(ref):
    return jnp.concatenate([jnp.broadcast_to(ref[r][:, 0:1], (Q_BLOCK, 1)) for r in range(REP)], axis=0)


def _nsa_cmp_body(*refs):
    for u in range(CMP_BLOCKS):
        _nsa_cmp_block(u, *refs)


def _nsa_cmp_block(u, q_ref, kp_ref, vp_ref, kn_ref, vn_ref, bias_ref, bfar_ref, o_ref, sel_ref):
    QB = Q_BLOCK
    NQ = REP * QB
    NB = kp_ref.shape[0]
    NS = NB // 4
    ns_shift = NS.bit_length() - 1
    blk = pl.program_id(2) * CMP_BLOCKS + u
    rows = slice(u * QB, (u + 1) * QB)
    start = pl.multiple_of(blk * (QB // CMP_STRIDE), 8)
    kn = kn_ref[pl.ds(start, CMP_NEAR), :].astype(BF16)
    vn = vn_ref[pl.ds(start, CMP_NEAR), :].astype(BF16)
    first_near = blk * (QB // CMP_STRIDE) - CMP_PAD
    lane = _iota((1, NB), 1)
    c_far = 4 * (lane & (NS - 1)) + (lane >> ns_shift)
    far_row = jnp.where(c_far < first_near, 0.0, NEG_INF)
    near_row = jnp.where(first_near + _iota((1, CMP_NEAR), 1) >= 0, 0.0, NEG_INF)
    q = _heads_on_rows(q_ref[rows, :]) * (HEAD_DIM ** -0.5)
    s_f = _dot_nt(q, kp_ref[...]) + _per_head_column(bfar_ref) + far_row
    s_n = _dot_nt(q, kn) + bias_ref[...].reshape(NQ, CMP_NEAR) + near_row
    m = jnp.maximum(jnp.max(s_f, axis=-1, keepdims=True), jnp.max(s_n, axis=-1, keepdims=True))
    m = jnp.maximum(m, 0.1 * NEG_INF)
    p_f = jnp.exp(s_f - m)
    p_n = jnp.exp(s_n - m)
    den = jnp.sum(p_f, axis=-1, keepdims=True) + jnp.sum(p_n, axis=-1, keepdims=True)
    inv = 1.0 / jnp.where(den > 0, den, 1.0)
    p_f = p_f * inv
    p_n = p_n * inv
    o_ref[rows, :] = _heads_on_lanes(_dot(p_f.astype(BF16), vp_ref[...]) + _dot(p_n.astype(BF16), vn))
    imp_far = p_f[0:QB]
    imp_near = p_n[0:QB]
    for r in range(1, REP):
        imp_far = imp_far + p_f[r * QB:(r + 1) * QB]
        imp_near = imp_near + p_n[r * QB:(r + 1) * QB]
    panel = [imp_far[:, m * NS:(m + 1) * NS] for m in range(4)]
    j = _iota((QB, NS), 1)
    prev3 = jnp.where(j == 0, 0.0, pltpu.roll(panel[3], 1, axis=1))
    imp = prev3 + 2.0 * panel[0] + 2.0 * panel[1] + 2.0 * panel[2] + panel[3]
    c_abs = first_near + _iota((CMP_NEAR, NS), 0)
    off = c_abs + 1 - 4 * _iota((CMP_NEAR, NS), 1)
    w_near = jnp.where((off == 0) | (off == 4), 1.0, jnp.where((off >= 1) & (off <= 3), 2.0, 0.0))
    imp = imp + _dot3(imp_near, w_near)
    cur = 2 * blk + (_iota((QB, NS), 0) >= SEL_BLOCK).astype(jnp.int32)
    forced = (j == 0) | (j == cur) | (j == cur - 1)
    score = jnp.where(forced, -3.0, jnp.where(j <= cur, imp, -1.0))
    score = score.T
    jt = _iota((NS, QB), 0)
    cur_t = 2 * blk + (_iota((NS, QB), 1) >= SEL_BLOCK).astype(jnp.int32)
    sel = jnp.where((jt == 0) | (jt == cur_t) | (jt == cur_t - 1), 1.0, 0.0)
    for _ in range(min(SEL_TOPK, NS) - 3):
        m = jnp.max(score, axis=0, keepdims=True)
        idx = jnp.min(jnp.where(score == m, jt, NS), axis=0, keepdims=True)
        hit = jt == idx
        sel = jnp.where(hit, 1.0, sel)
        score = jnp.where(hit, -3.0, score)
    pad = jnp.zeros((SEL_PAD_BLOCKS, QB), BF16)
    sel_ref[u] = jnp.concatenate([pad, sel.astype(BF16), pad], axis=0)


GROUP_LANES = REP * HEAD_DIM


def _q_spec(blocks=1):
    return pl.BlockSpec((None, blocks * Q_BLOCK, GROUP_LANES), lambda b, g, i: (b, i, g))


def nsa_compressed(p_nsa, kperm, vperm, knear, vnear, bias, bias_far):
    B, T, _ = p_nsa.shape
    G, NQ, N = KV_GROUPS, T // Q_BLOCK, HEAD_DIM
    NB = kperm.shape[2]
    nsp = NB // 4 + 2 * SEL_PAD_BLOCKS
    blk4 = lambda b, g, i: (b, g, 0, 0)
    return pl.pallas_call(
        _nsa_cmp_body,
        grid=(B, G, NQ // CMP_BLOCKS),
        in_specs=[_q_spec(CMP_BLOCKS),
                  pl.BlockSpec((None, None, NB, N), blk4), pl.BlockSpec((None, None, NB, N), blk4),
                  pl.BlockSpec((None, None, knear.shape[2], N), blk4),
                  pl.BlockSpec((None, None, knear.shape[2], N), blk4),
                  pl.BlockSpec((None, REP, Q_BLOCK, CMP_NEAR), lambda b, g, i: (g, 0, 0, 0)),
                  pl.BlockSpec((None, REP, 1, LANE), lambda b, g, i: (g, 0, 0, 0))],
        out_specs=[_q_spec(CMP_BLOCKS),
                   pl.BlockSpec((None, None, CMP_BLOCKS, nsp, Q_BLOCK), lambda b, g, i: (b, g, i, 0, 0))],
        out_shape=[jax.ShapeDtypeStruct((B, T, WIDTH), F32),
                   jax.ShapeDtypeStruct((B, G, NQ, nsp, Q_BLOCK), BF16)],
        compiler_params=_cparams(("parallel", "parallel", "parallel")),
        name="nsa_compressed",
    )(p_nsa, kperm, vperm, knear, vnear, bias, bias_far)


def _nsa_sel_body(q_ref, selt_ref, k_ref, vt_ref, biast_ref, bfar_ref, o_ref, sa_ref, sb_ref, ma_ref, mb_ref):
    QB = Q_BLOCK
    NQ = REP * QB
    blk = pl.program_id(2)
    qb = q_ref[...].astype(F32) * (HEAD_DIM ** -0.5 * LOG2E)
    qt = jnp.concatenate([qb[:, r * HEAD_DIM:(r + 1) * HEAD_DIM].T for r in range(REP)], axis=1).astype(BF16)
    near_chunk = blk + (SEL_PAD - SEL_SPAN) // V_CHUNK
    first = near_chunk * (V_CHUNK // SEL_BLOCK)
    pb_u = _iota((SEL_TILE_BLOCKS, QB), 0)
    zeros = jnp.zeros((AUG - HEAD_DIM - 2 * SEL_TILE_BLOCKS, NQ), BF16)
    bfar = bfar_ref[...]

    def far_scores(kt, s_ref, max_ref):
        b0 = pl.multiple_of(kt * SEL_TILE_BLOCKS, SEL_TILE_BLOCKS)
        picked = selt_ref[pl.ds(b0, SEL_TILE_BLOCKS), :].astype(F32) > 0.5
        neg = jnp.where(picked & (b0 + pb_u < first), 0.0, NEG_INF).astype(BF16)
        q_far = jnp.concatenate([qt, jnp.concatenate([neg] * REP, axis=1), bfar, zeros], axis=0)
        r0 = pl.multiple_of(kt * SEL_TILE, SEL_TILE)
        s = _dot(k_ref[pl.ds(r0, SEL_TILE), :], q_far)
        s_ref[...] = s
        max_ref[...] = jnp.max(s, axis=0, keepdims=True)

    def weighted_values(chunk0, p):
        out = jnp.zeros((HEAD_DIM, NQ), F32)
        for j in range(0, SEL_TILE // V_CHUNK, 2):
            vt = jnp.concatenate([vt_ref[chunk0 + j], vt_ref[chunk0 + j + 1]], axis=1)
            out = out + _dot(vt, p[j * V_CHUNK:(j + 2) * V_CHUNK])
        return out

    def far_update(kt, s_ref, max_ref, carry):
        m, l, acc = carry
        s = s_ref[...]
        m_new = jnp.maximum(m, max_ref[...])
        alpha = jnp.exp2(m - m_new)
        p = jnp.exp2(s - m_new)
        l = alpha * l + jnp.sum(p, axis=0, keepdims=True)
        p = p.astype(BF16)
        return m_new, l, alpha * acc + weighted_values(kt * (SEL_TILE // V_CHUNK), p)

    row0 = pl.multiple_of(near_chunk * V_CHUNK, V_CHUNK)
    base = pl.multiple_of((first // SEL_TILE_BLOCKS) * SEL_TILE_BLOCKS, SEL_TILE_BLOCKS)
    rows = selt_ref[pl.ds(base, 2 * SEL_TILE_BLOCKS), :].astype(F32)
    picked = jnp.where(pb_u >= first - base, rows[0:SEL_TILE_BLOCKS], rows[SEL_TILE_BLOCKS:]) > 0.5
    neg = jnp.where(picked, 0.0, NEG_INF).astype(BF16)
    q_near = jnp.concatenate([qt, jnp.concatenate([neg] * REP, axis=1), jnp.zeros_like(bfar), zeros], axis=0)
    s = _dot(k_ref[pl.ds(row0, SEL_NEAR), :], q_near) + biast_ref[...]
    far_scores(0, sa_ref, ma_ref)
    m = jnp.max(s, axis=0, keepdims=True)
    p = jnp.exp2(s - m)
    l = jnp.sum(p, axis=0, keepdims=True)
    acc = weighted_values(near_chunk, p.astype(BF16))

    def far_pair(j, carry):
        far_scores(2 * j + 1, sb_ref, mb_ref)
        carry = far_update(2 * j, sa_ref, ma_ref, carry)
        far_scores(2 * j + 2, sa_ref, ma_ref)
        return far_update(2 * j + 1, sb_ref, mb_ref, carry)

    n_far = (first + SEL_TILE_BLOCKS - 1) // SEL_TILE_BLOCKS
    carry = lax.fori_loop(0, n_far // 2, far_pair, (m, l, acc))
    m, l, acc = lax.cond(n_far % 2 == 1, lambda c: far_update(n_far - 1, sa_ref, ma_ref, c), lambda c: c, carry)
    out = acc / l
    o_ref[...] = jnp.concatenate([out[:, r * QB:(r + 1) * QB].T for r in range(REP)], axis=1)


def nsa_selected(p_nsa, selt, k_aug, vt, bias_t, bias_far):
    B, T, _ = p_nsa.shape
    G, NQ, N, W = KV_GROUPS, T // Q_BLOCK, HEAD_DIM, REP * Q_BLOCK
    nsp = selt.shape[3]
    rows = k_aug.shape[2]
    return pl.pallas_call(
        _nsa_sel_body,
        grid=(B, G, NQ),
        in_specs=[_q_spec(),
                  pl.BlockSpec((None, None, None, nsp, Q_BLOCK), lambda b, g, i: (b, g, i, 0, 0)),
                  pl.BlockSpec((None, None, rows, AUG), lambda b, g, i: (b, g, 0, 0)),
                  pl.BlockSpec((None, None, rows // V_CHUNK, N, V_CHUNK), lambda b, g, i: (b, g, 0, 0, 0)),
                  pl.BlockSpec((None, SEL_NEAR, W), lambda b, g, i: (g, 0, 0)),
                  pl.BlockSpec((None, SEL_TILE_BLOCKS, W), lambda b, g, i: (g, 0, 0))],
        out_specs=_q_spec(),
        out_shape=jax.ShapeDtypeStruct((B, T, WIDTH), F32),
        scratch_shapes=[pltpu.VMEM((SEL_TILE, W), F32)] * 2 + [pltpu.VMEM((1, W), F32)] * 2,
        compiler_params=_cparams(("parallel", "parallel", "parallel")),
        name="nsa_selected",
    )(p_nsa, selt, k_aug, vt, bias_t, bias_far)


def _band_body(*refs, pad, blocks, has_sink):
    if has_sink:
        q_ref, k_ref, v_ref, bias_ref, sink_ref, o_ref = refs
    else:
        q_ref, k_ref, v_ref, bias_ref, o_ref = refs
    QB = Q_BLOCK
    NQ = REP * QB
    width = pad + QB
    bias = bias_ref[...].reshape(NQ, width)
    if has_sink:
        sink = sink_ref[...].reshape(NQ, LANE)
    for u in range(blocks):
        blk = pl.program_id(2) * blocks + u
        row0 = pl.multiple_of(blk * QB, QB)
        kb = k_ref[pl.ds(row0, width), :]
        vb = v_ref[pl.ds(row0, width), :]
        before_start = jnp.where(row0 + _iota((1, width), 1) >= pad, 0.0, NEG_INF)
        q = _heads_on_rows(q_ref[u * QB:(u + 1) * QB, :]) * (HEAD_DIM ** -0.5)
        s = _dot_nt(q, kb) + bias + before_start
        m = jnp.max(s, axis=-1, keepdims=True)
        if has_sink:
            m = jnp.maximum(m, jnp.max(sink, axis=-1, keepdims=True))
        p = jnp.exp(s - m)
        den = jnp.sum(p, axis=-1, keepdims=True)
        if has_sink:
            den = den + jnp.sum(jnp.exp(sink - m), axis=-1, keepdims=True)
        p = p * (1.0 / den)
        o_ref[u * QB:(u + 1) * QB, :] = _heads_on_lanes(_dot(p.astype(BF16), vb))


def band_attention(p, k, v, bias, sink, pad, blocks):
    B, T, _ = p.shape
    G, NQ, N = KV_GROUPS, T // Q_BLOCK, HEAD_DIM
    rows = k.shape[2]
    width = pad + Q_BLOCK
    kv = pl.BlockSpec((None, None, rows, N), lambda b, g, i: (b, g, 0, 0))
    in_specs = [_q_spec(blocks), kv, kv,
                pl.BlockSpec((None, REP, Q_BLOCK, width), lambda b, g, i: (g, 0, 0, 0))]
    args = [p, k, v, bias]
    if sink is not None:
        in_specs.append(pl.BlockSpec((None, REP, Q_BLOCK, LANE), lambda b, g, i: (g, 0, 0, 0)))
        args.append(sink)
    return pl.pallas_call(
        functools.partial(_band_body, pad=pad, blocks=blocks, has_sink=sink is not None),
        grid=(B, G, NQ // blocks),
        in_specs=in_specs,
        out_specs=_q_spec(blocks),
        out_shape=jax.ShapeDtypeStruct((B, T, WIDTH), F32),
        compiler_params=_cparams(("parallel", "parallel", "parallel")),
        name="band_attention",
    )(*args)


def _merge_body(x_ref, pg_ref, hg_ref, orw_ref, oc_ref, os_ref, ow_ref, osw_ref, wb_ref, wo_ref, gn_ref, ex_ref,
                o_ref):
    pg = pg_ref[...]
    head_gates = jax.nn.sigmoid(hg_ref[...])
    gates_hi = head_gates.astype(BF16)
    gates_lo = (head_gates - gates_hi.astype(F32)).astype(BF16)
    ge = _dot(gates_hi, ex_ref[...]) + _dot(gates_lo, ex_ref[...])
    o_nsa = (ge[:, 0:WIDTH] * oc_ref[...] + ge[:, WIDTH:2 * WIDTH] * os_ref[...]
             + ge[:, 2 * WIDTH:3 * WIDTH] * ow_ref[...])
    merged = (jax.nn.sigmoid(pg[:, 0:D_MODEL]) * _dot(orw_ref[...].astype(BF16), wb_ref[0])
              + jax.nn.sigmoid(pg[:, D_MODEL:2 * D_MODEL]) * _dot(o_nsa.astype(BF16), wb_ref[1])
              + jax.nn.sigmoid(pg[:, 2 * D_MODEL:3 * D_MODEL]) * _dot(osw_ref[...].astype(BF16), wb_ref[2]))
    y = _dot(merged.astype(BF16), wo_ref[...])
    o_ref[...] = x_ref[...] + _rms(y, gn_ref[...])


def merge_out(x, pg, p_rw, o_rwkv, o_c, o_s, o_w, o_swa, w_branch, w_out, g_post, expand, tm):
    M, D = x.shape
    row = lambda i: (i, 0)
    wide = pl.BlockSpec((tm, WIDTH), row)
    return pl.pallas_call(
        _merge_body,
        grid=(M // tm,),
        in_specs=[pl.BlockSpec((tm, D), row), pl.BlockSpec((tm, 3 * D), row),
                  pl.BlockSpec((tm, LANE), lambda i: (i, RW_COLS // LANE)),
                  wide, wide, wide, wide, wide,
                  pl.BlockSpec((3, WIDTH, D), lambda i: (0, 0, 0)),
                  pl.BlockSpec((D, D), lambda i: (0, 0)),
                  pl.BlockSpec((1, D), lambda i: (0, 0)),
                  pl.BlockSpec((LANE, 3 * WIDTH), lambda i: (0, 0))],
        out_specs=pl.BlockSpec((tm, D), row),
        out_shape=jax.ShapeDtypeStruct((M, D), F32),
        compiler_params=_cparams(("parallel",)),
        name="merge_out",
    )(x, pg, p_rw, o_rwkv, o_c, o_s, o_w, o_swa, w_branch, w_out, g_post, expand)


FFN_CHUNKS = 2


def _ffn_body(x_ref, gpre_ref, wg_ref, wu_ref, wd_ref, gpost_ref, o_ref):
    x = x_ref[...]
    h = _rms(x, gpre_ref[...]).astype(BF16)
    th = wg_ref.shape[1] // FFN_CHUNKS
    acc = jnp.zeros(x.shape, F32)
    for c in range(FFN_CHUNKS):
        cols = slice(c * th, (c + 1) * th)
        gate = _dot(h, wg_ref[:, cols])
        act = gate * jax.nn.sigmoid(gate) * _dot(h, wu_ref[:, cols])
        acc = acc + _dot(act.astype(BF16), wd_ref[cols, :])
    o_ref[...] = x + _rms(acc, gpost_ref[...])


def ffn(x, g_pre, w_gate, w_up, w_down, g_post, tm):
    M, D = x.shape
    Hd = w_gate.shape[1]
    whole = lambda shape: pl.BlockSpec(shape, lambda i: (0, 0), pipeline_mode=pl.Buffered(1))
    return pl.pallas_call(
        _ffn_body,
        grid=(M // tm,),
        in_specs=[pl.BlockSpec((tm, D), lambda i: (i, 0)),
                  pl.BlockSpec((1, D), lambda i: (0, 0)),
                  whole((D, Hd)), whole((D, Hd)), whole((Hd, D)),
                  pl.BlockSpec((1, D), lambda i: (0, 0))],
        out_specs=pl.BlockSpec((tm, D), lambda i: (i, 0)),
        out_shape=jax.ShapeDtypeStruct((M, D), F32),
        compiler_params=_cparams(("parallel",)),
        name="ffn",
    )(x, g_pre, w_gate, w_up, w_down, g_post)


def _pad_cols(w, n):
    return jnp.pad(w, ((0, 0), (0, n - w.shape[1])))


def _pad_rows(w, n):
    return jnp.pad(w, ((0, n - w.shape[0]), (0, 0)))


def _kv_groups(z, B, T, pad):
    z = z.reshape(B, T, KV_GROUPS, HEAD_DIM).transpose(0, 2, 1, 3)
    return jnp.pad(z, ((0, 0), (0, 0), (pad, 0), (0, 0)))


def _group_bias(b):
    return b.reshape(KV_GROUPS, REP, b.shape[1], b.shape[2])


def kernel(x, norm_pre_mix, norm_post_mix, norm_pre_ffn, norm_post_ffn, w_in, rwkv_mu, rwkv_w0, rwkv_w2, rwkv_a0, rwkv_a2, rwkv_g2, rwkv_k_k, rwkv_k_a, rwkv_r_k, rwkv_ln_w, rwkv_ln_b, nsa_pe_k, nsa_pe_v, nsa_ck_w1, nsa_ck_w2, nsa_cv_w1, nsa_cv_w2, swa_sinks, rel_bias, w_branch, w_out, ffn_w_gate, ffn_w_up, ffn_w_down):
    B, T, D = x.shape
    depth = w_in.shape[0]
    M = B * T
    NB = T // CMP_STRIDE
    NS = T // SEL_BLOCK

    rw_cols = 3 * WIDTH + W_LORA + A_LORA + G_LORA
    nsa_cols = WIDTH + 6 * KV_WIDTH + 3 * N_HEADS
    swa_cols = WIDTH + 2 * KV_WIDTH
    c_nsa = rw_cols
    c_swa = c_nsa + nsa_cols
    c_gate = c_swa + swa_cols
    c_zw = 3 * WIDTH

    ii = np.arange(Q_BLOCK)[:, None]
    d_swa = SWA_WINDOW + ii - np.arange(SWA_WINDOW + Q_BLOCK)[None, :]
    d_win = NSA_WINDOW + ii - np.arange(NSA_WINDOW + Q_BLOCK)[None, :]
    d_sel = SEL_SPAN + ii - np.arange(SEL_NEAR)[None, :]
    d_cmp = ii + (CMP_PAD * CMP_STRIDE - CMP_LEN + 1) - CMP_STRIDE * np.arange(CMP_NEAR)[None, :]
    bias_swa = bias_tile(rel_bias, d_swa, (d_swa >= 0) & (d_swa < SWA_WINDOW))
    bias_win = bias_tile(rel_bias, d_win, (d_win >= 0) & (d_win < NSA_WINDOW))
    bias_sel = bias_tile(rel_bias, d_sel, d_sel >= 0)
    bias_cmp = bias_tile(rel_bias, d_cmp, d_cmp >= 0)
    bias_swa = _group_bias(bias_swa[N_HEADS:])
    bias_win = _group_bias(bias_win[:N_HEADS])
    bias_sel = _group_bias(bias_sel[:N_HEADS])
    bias_cmp = _group_bias(bias_cmp[:N_HEADS])
    far = rel_bias[REL_BUCKETS - 1, :N_HEADS].reshape(KV_GROUPS, REP, 1, 1)
    bias_sel_t = bias_sel.transpose(0, 3, 1, 2).reshape(KV_GROUPS, SEL_NEAR, REP * Q_BLOCK) * LOG2E
    far2 = far * LOG2E
    far_hi = far2.astype(BF16)
    far_lo = (far2 - far_hi.astype(F32)).astype(BF16)
    far_rows = jnp.concatenate([far_hi, far_lo], axis=2)
    far_rows = jnp.broadcast_to(far_rows, (KV_GROUPS, REP, 2, Q_BLOCK)).transpose(0, 2, 1, 3)
    bfar_sel = jnp.pad(far_rows.reshape(KV_GROUPS, 2, REP * Q_BLOCK), ((0, 0), (0, SEL_TILE_BLOCKS - 2), (0, 0)))
    rows_pad = SEL_PAD + T
    blk_in_tile = (np.arange(rows_pad) // SEL_BLOCK) % SEL_TILE_BLOCKS
    ka = np.zeros((rows_pad, AUG - HEAD_DIM), np.float32)
    ka[np.arange(rows_pad), blk_in_tile] = 1.0
    ka[:, SEL_TILE_BLOCKS:SEL_TILE_BLOCKS + 2] = 1.0
    key_aug = jnp.asarray(ka, BF16)
    bfar_cmp = jnp.broadcast_to(far, (KV_GROUPS, REP, 1, LANE))
    ex = np.zeros((LANE, 3 * WIDTH), np.float32)
    for c in range(3 * N_HEADS):
        ex[c, c * HEAD_DIM:(c + 1) * HEAD_DIM] = 1.0
    expand = jnp.asarray(ex, BF16)

    xf = x.reshape(M, D)
    for l in range(depth):
        w = w_in[l]
        w_rw = jnp.concatenate([w[:, :c_zw], _pad_cols(w[:, c_zw:c_zw + W_LORA], LANE),
                                _pad_cols(w[:, c_zw + W_LORA:c_zw + W_LORA + A_LORA], LANE),
                                w[:, c_zw + W_LORA + A_LORA:rw_cols],
                                _pad_cols(w[:, c_swa - 3 * N_HEADS:c_swa], LANE)], axis=1).astype(BF16)
        w_nsa = w[:, c_nsa:c_nsa + WIDTH + 6 * KV_WIDTH].astype(BF16)
        w_swa = w[:, c_swa:c_gate].astype(BF16)
        w_gate = w[:, c_gate:].astype(BF16)
        g_pre = norm_pre_mix[l][None, :]
        p_rw = norm_matmul(xf, g_pre, w_rw, F32, 1024)
        p_nsa = norm_matmul(xf, g_pre, w_nsa, BF16, 1024)
        p_swa = norm_matmul(xf, g_pre, w_swa, BF16, 1024)
        p_gate = norm_matmul(xf, g_pre, w_gate, F32, 512)

        mu = rwkv_mu[l]
        mu_p = jnp.concatenate([mu[:c_zw], jnp.pad(mu[c_zw:c_zw + W_LORA], (0, LANE - W_LORA)),
                                jnp.pad(mu[c_zw + W_LORA:c_zw + W_LORA + A_LORA], (0, LANE - A_LORA)),
                                mu[c_zw + W_LORA + A_LORA:]])[None, :]
        r, k, v, a, lw, cum, g = rwkv_prep(p_rw.reshape(B, T, RW_COLS + LANE), mu_p, rwkv_w0[l][None, :],
                                      _pad_rows(rwkv_w2[l], LANE), rwkv_a0[l][None, :],
                                      _pad_rows(rwkv_a2[l], LANE), rwkv_g2[l], 512)
        row = lambda z: z.reshape(1, WIDTH)
        o_rwkv = rwkv_recurrence(r, k, v, a, lw, cum, g, row(rwkv_k_k[l]), row(rwkv_k_a[l]), row(rwkv_r_k[l]),
                                 row(rwkv_ln_w[l]), row(rwkv_ln_b[l]), 512).reshape(M, WIDTH)

        pn = p_nsa.reshape(B, T, WIDTH + 6 * KV_WIDTH)
        seg = lambda n: pn[..., WIDTH + n * KV_WIDTH:WIDTH + (n + 1) * KV_WIDTH]

        def strides(z):
            return z.reshape(B, NB, CMP_STRIDE * KV_WIDTH)

        def pe_rows(pe):
            pe = jnp.broadcast_to(pe.reshape(2, CMP_STRIDE, 1, HEAD_DIM), (2, CMP_STRIDE, KV_GROUPS, HEAD_DIM))
            return jnp.broadcast_to(pe.reshape(1, -1), (8, 2 * CMP_STRIDE * KV_WIDTH)).astype(BF16)

        def group_rows(w1):
            w = w1.reshape(1, 2, CMP_STRIDE, 1, HEAD_DIM, CMP_HIDDEN)
            own = jnp.eye(KV_GROUPS, dtype=w1.dtype).reshape(KV_GROUPS, 1, 1, KV_GROUPS, 1, 1)
            return (w * own).reshape(KV_GROUPS, 2, CMP_STRIDE * KV_WIDTH, CMP_HIDDEN).astype(BF16)

        k_cmp = nsa_compress(strides(seg(0)), pe_rows(nsa_pe_k[l]), group_rows(nsa_ck_w1[l]),
                             nsa_ck_w2[l].astype(BF16))
        v_cmp = nsa_compress(strides(seg(1)), pe_rows(nsa_pe_v[l]), group_rows(nsa_cv_w1[l]),
                             nsa_cv_w2[l].astype(BF16))

        def panels(z):
            z = z.reshape(B, KV_GROUPS, NS, 4, HEAD_DIM).transpose(0, 1, 3, 2, 4)
            return z.reshape(B, KV_GROUPS, NB, HEAD_DIM).astype(BF16)

        near = lambda z: jnp.pad(z, ((0, 0), (0, 0), (CMP_PAD, 8), (0, 0)))
        o_c, sel_t = nsa_compressed(pn, panels(k_cmp), panels(v_cmp), near(k_cmp), near(v_cmp), bias_cmp, bfar_cmp)
        k_aug = jnp.concatenate([_kv_groups(seg(2), B, T, SEL_PAD),
                                 jnp.broadcast_to(key_aug, (B, KV_GROUPS) + key_aug.shape)], axis=-1)
        v_t = _kv_groups(seg(3), B, T, SEL_PAD).reshape(B, KV_GROUPS, -1, V_CHUNK, HEAD_DIM).swapaxes(-1, -2)
        o_s = nsa_selected(pn, sel_t, k_aug, v_t, bias_sel_t, bfar_sel)
        o_w = band_attention(pn, _kv_groups(seg(4), B, T, NSA_WINDOW), _kv_groups(seg(5), B, T, NSA_WINDOW),
                             bias_win, None, NSA_WINDOW, WIN_BLOCKS)

        ps = p_swa.reshape(B, T, swa_cols)
        sink = jnp.where(np.arange(LANE) == 0, swa_sinks[l].reshape(KV_GROUPS, REP, 1, 1), NEG_INF)
        sink = jnp.broadcast_to(sink, (KV_GROUPS, REP, Q_BLOCK, LANE))
        o_swa = band_attention(ps,
                               _kv_groups(ps[..., WIDTH:WIDTH + KV_WIDTH], B, T, SWA_WINDOW),
                               _kv_groups(ps[..., WIDTH + KV_WIDTH:], B, T, SWA_WINDOW),
                               bias_swa, sink, SWA_WINDOW, SWA_BLOCKS)

        flat = lambda o: o.reshape(M, WIDTH)
        xf = merge_out(xf, p_gate, p_rw, o_rwkv, flat(o_c), flat(o_s), flat(o_w),
                       flat(o_swa), w_branch[l].astype(BF16), w_out[l].astype(BF16),
                       norm_post_mix[l][None, :], expand, 256)

        xf = ffn(xf, norm_pre_ffn[l][None, :], ffn_w_gate[l].astype(BF16), ffn_w_up[l].astype(BF16),
                 ffn_w_down[l].astype(BF16), norm_post_ffn[l][None, :], 512)
    return xf.reshape(B, T, D)
```
